```python
import jax, jax.numpy as jnp
from jax import lax
import numpy as np

D_MODEL = 1024
BATCH = 8
SEQ = 2048
DEPTH = 1

GLA_HEADS = 4
GLA_DK = D_MODEL // 2 // GLA_HEADS
GLA_DV = D_MODEL // GLA_HEADS
GLA_RANK = 16
GLA_TAU = 16.0
GLA_CHUNK = 64
CONV_WIDTH = D_MODEL
CONV_K = 3
N_EXPERTS = 32
TOP_K = 4
D_FF = D_MODEL
SWIGLU_LIMIT = 7.0
SWIGLU_ALPHA = 1.702
EPS = 1e-6

QK_W = GLA_HEADS * GLA_DK
V_W = GLA_HEADS * GLA_DV
SPLITS = (QK_W, QK_W, V_W, V_W, GLA_RANK, CONV_WIDTH, CONV_WIDTH, CONV_WIDTH, D_MODEL, D_MODEL)
IN_PROJ_W = sum(SPLITS)

kernel_name = "hybrid_gla_shortconv_moe_block"


def rms_norm(x, g):
    xf = x.astype(jnp.float32)
    y = xf * lax.rsqrt(jnp.mean(xf * xf, axis=-1, keepdims=True) + EPS)
    return (y * g.astype(jnp.float32)).astype(x.dtype)


def split_columns(z):
    offsets = [int(o) for o in np.cumsum(SPLITS)[:-1]]
    return jnp.split(z, offsets, axis=-1)


def gla_chunked(q, k, v, log_a):
    B, S, H, DK = q.shape
    DV = v.shape[-1]
    C = GLA_CHUNK
    N = S // C
    f32 = jnp.float32
    qf = q.astype(f32).reshape(B, N, C, H, DK) * (DK ** -0.5)
    kf = k.astype(f32).reshape(B, N, C, H, DK)
    vf = v.astype(f32).reshape(B, N, C, H, DV)
    b = jnp.cumsum(log_a.astype(f32).reshape(B, N, C, H, DK), axis=2)
    b_last = b[:, :, -1]
    q_dec = qf * jnp.exp(b)
    k_inv = kf * jnp.exp(-b)
    k_tail = kf * jnp.exp(b_last[:, :, None] - b)
    scores = jnp.einsum("bnihd,bnjhd->bnhij", q_dec, k_inv)
    causal = jnp.tril(jnp.ones((C, C), dtype=bool))
    scores = jnp.where(causal, scores, 0.0)
    o_intra = jnp.einsum("bnhij,bnjhv->bnihv", scores, vf)
    chunk_update = jnp.einsum("bnjhd,bnjhv->nbhdv", k_tail, vf)
    chunk_decay = jnp.exp(b_last).transpose(1, 0, 2, 3)

    def step(state, inp):
        dec, upd = inp
        return dec[..., None] * state + upd, state

    state0 = jnp.zeros((B, H, DK, DV), f32)
    _, state_in = lax.scan(step, state0, (chunk_decay, chunk_update))
    o_inter = jnp.einsum("bnihd,nbhdv->bnihv", q_dec, state_in)
    return (o_intra + o_inter).reshape(B, S, H, DV).astype(v.dtype)


def causal_dwconv(u, w):
    S = u.shape[1]
    up = jnp.pad(u, ((0, 0), (CONV_K - 1, 0), (0, 0)))
    y = w[0] * up[:, 0:S]
    for i in range(1, CONV_K):
        y = y + w[i] * up[:, i:i + S]
    return y


def moe_ffn(h, w_router, b_router, w_gate_up, b_gate_up, w_down, b_down):
    B, S, D = h.shape
    T = B * S
    xt = h.reshape(T, D)
    logits = jnp.matmul(xt, w_router).astype(jnp.float32) + b_router.astype(jnp.float32)
    top_logit, top_idx = lax.top_k(logits, TOP_K)
    gates = jax.nn.softmax(top_logit, axis=-1)
    flat_e = top_idx.reshape(-1)
    order = jnp.argsort(flat_e)
    e_sorted = flat_e[order]
    tok_sorted = order // TOP_K
    group_sizes = jnp.bincount(flat_e, length=N_EXPERTS).astype(jnp.int32)
    xs = xt[tok_sorted]
    gu = lax.ragged_dot(xs, w_gate_up, group_sizes) + b_gate_up[e_sorted]
    gate, lin = jnp.split(gu, 2, axis=-1)
    gate = jnp.minimum(gate, SWIGLU_LIMIT)
    lin = jnp.clip(lin, -SWIGLU_LIMIT, SWIGLU_LIMIT)
    act = (lin + 1.0) * (gate * jax.nn.sigmoid(SWIGLU_ALPHA * gate))
    out = lax.ragged_dot(act, w_down, group_sizes) + b_down[e_sorted]
    out_pairs = jnp.zeros_like(out).at[order].set(out).reshape(T, TOP_K, D)
    y = jnp.einsum("tkd,tk->td", out_pairs, gates.astype(out.dtype))
    return y.reshape(B, S, D)


def setup_inputs(seed: int = 0) -> dict:
    key = jax.random.key(seed)
    ks = jax.random.split(key, 20)
    f32 = jnp.float32
    L, D, E, F = DEPTH, D_MODEL, N_EXPERTS, D_FF

    def nrm(k, shape, scale):
        return jax.random.normal(k, shape, f32) * scale

    return {
        "x": nrm(ks[0], (BATCH, SEQ, D), 1.0),
        "g_mix": 1.0 + nrm(ks[1], (L, D), 0.02),
        "w_in": nrm(ks[2], (L, D, IN_PROJ_W), D ** -0.5),
        "w_alpha_up": nrm(ks[3], (L, GLA_RANK, QK_W), GLA_RANK ** -0.5),
        "b_alpha": nrm(ks[4], (L, QK_W), 0.1),
        "g_gla": 1.0 + nrm(ks[5], (L, GLA_HEADS, GLA_DV), 0.02),
        "w_conv": nrm(ks[6], (L, CONV_K, CONV_WIDTH), CONV_K ** -0.5),
        "w_branch_a": nrm(ks[7], (L, V_W, D), V_W ** -0.5),
        "w_branch_b": nrm(ks[8], (L, CONV_WIDTH, D), CONV_WIDTH ** -0.5),
        "w_out": nrm(ks[9], (L, D, D), D ** -0.5),
        "g_ffn": 1.0 + nrm(ks[10], (L, D), 0.02),
        "w_router": nrm(ks[11], (L, D, E), D ** -0.5),
        "b_router": nrm(ks[12], (L, E), 0.01),
        "w_gate_up": nrm(ks[13], (L, E, D, 2 * F), D ** -0.5),
        "b_gate_up": nrm(ks[14], (L, E, 2 * F), 0.02),
        "w_down": nrm(ks[15], (L, E, F, D), F ** -0.5),
        "b_down": nrm(ks[16], (L, E, D), 0.02),
        "g_final": 1.0 + nrm(ks[17], (D,), 0.02),
    }


def reference(x, g_mix, w_in, w_alpha_up, b_alpha, g_gla, w_conv, w_branch_a, w_branch_b, w_out,
              g_ffn, w_router, b_router, w_gate_up, b_gate_up, w_down, b_down, g_final):
    B, S, _ = x.shape
    for l in range(DEPTH):
        h = rms_norm(x, g_mix[l])
        z = jnp.matmul(h, w_in[l])
        q, k, v, r, a_lr, c_b, c_c, c_x, g_a, g_b = split_columns(z)
        log_a = jax.nn.log_sigmoid(jnp.matmul(a_lr, w_alpha_up[l]) + b_alpha[l]) / GLA_TAU
        o = gla_chunked(q.reshape(B, S, GLA_HEADS, GLA_DK),
                        k.reshape(B, S, GLA_HEADS, GLA_DK),
                        v.reshape(B, S, GLA_HEADS, GLA_DV),
                        log_a.reshape(B, S, GLA_HEADS, GLA_DK))
        o = rms_norm(o, g_gla[l]) * jax.nn.silu(r).reshape(B, S, GLA_HEADS, GLA_DV)
        y_a = jnp.matmul(o.reshape(B, S, V_W), w_branch_a[l])
        y_conv = causal_dwconv(c_c * c_x, w_conv[l])
        y_b = jnp.matmul(c_b * y_conv, w_branch_b[l])
        mixed = jax.nn.sigmoid(g_a) * y_a + jax.nn.sigmoid(g_b) * y_b
        x = x + jnp.matmul(mixed, w_out[l])
        h = rms_norm(x, g_ffn[l])
        x = x + moe_ffn(h, w_router[l], b_router[l], w_gate_up[l], b_gate_up[l], w_down[l], b_down[l])
    return rms_norm(x, g_final)
```

```python
import functools

import jax
import jax.numpy as jnp
from jax import lax
from jax.experimental import pallas as pl
from jax.experimental.pallas import tpu as pltpu

D_MODEL = 1024
GLA_HEADS = 4
GLA_DK = 128
GLA_DV = 256
GLA_RANK = 16
GLA_TAU = 16.0
GLA_CHUNK = 64
_CHUNK_SHIFT = GLA_CHUNK.bit_length() - 1
CONV_K = 3
N_EXPERTS = 32
TOP_K = 4
D_FF = 1024
SWIGLU_LIMIT = 7.0
SWIGLU_ALPHA = 1.702
EPS = 1e-6

QK_W = GLA_HEADS * GLA_DK
V_W = GLA_HEADS * GLA_DV

LANES = 128
BF16_ROWS = 16
VMEM_LIMIT = 56 * 1024 * 1024

MIX_TS = 256
MOE_TM = 512
FFN_TM = 256

_C_Q, _C_K, _C_V, _C_R = 0, QK_W, 2 * QK_W, 2 * QK_W + V_W
_C_CB = _C_R + V_W
_C_CC = _C_CB + D_MODEL
_C_CX = _C_CC + D_MODEL
_C_GA = _C_CX + D_MODEL
_C_GB = _C_GA + D_MODEL
_W_MAIN = _C_GB + D_MODEL

f32 = jnp.float32
bf16 = jnp.bfloat16


def _rms(x, g):
    return x * lax.rsqrt(jnp.mean(x * x, axis=-1, keepdims=True) + EPS) * g


def _dot(a, b):
    return jnp.dot(a, b, preferred_element_type=f32)


def _split_bf16(a):
    hi = a.astype(bf16)
    lo = (a - hi.astype(f32)).astype(bf16)
    return hi, lo


def _mixer_body(x_ref, gmix_ref, wmain_ref, walr_ref, wup_ref, balpha_ref, ggla_ref, wconv_ref,
                wa_ref, wb_ref, wo_ref, gffn_ref, wrh_ref, wrl_ref, br_ref,
                x1_ref, h2_ref, logit_ref,
                state_ref, ubuf_ref, obuf_ref):
    ts = x_ref.shape[0]
    nchunk = ts // GLA_CHUNK

    @pl.when(pl.program_id(1) == 0)
    def _():
        state_ref[...] = jnp.zeros_like(state_ref)
        ubuf_ref[0:8, :] = jnp.zeros((8, D_MODEL), f32)

    x = x_ref[...]
    hb = _rms(x, gmix_ref[...]).astype(bf16)

    def proj(c0, width):
        return _dot(hb, wmain_ref[:, c0:c0 + width])

    a_lr = _dot(hb, walr_ref[...])
    pre = _dot(a_lr.astype(bf16), wup_ref[...]) + balpha_ref[...]
    log_a = -(jnp.maximum(-pre, 0.0) + jnp.log1p(jnp.exp(-jnp.abs(pre)))) * (1.0 / GLA_TAU)
    ri = lax.broadcasted_iota(jnp.int32, (ts, ts), 0)
    ci = lax.broadcasted_iota(jnp.int32, (ts, ts), 1)
    same_chunk = (ri >> _CHUNK_SHIFT) == (ci >> _CHUNK_SHIFT)
    causal = jnp.logical_and(same_chunk, ci <= ri)
    tri = jnp.where(causal, 1.0, 0.0).astype(bf16)
    blk = jnp.where(same_chunk, 1.0, 0.0).astype(bf16)
    la_hi, la_lo = _split_bf16(log_a)
    b = _dot(tri, la_hi) + _dot(tri, la_lo)
    b_tot = _dot(blk, la_hi) + _dot(blk, la_lo)
    e_pos = jnp.exp(b)
    e_neg = jnp.exp(-b)
    e_tail = jnp.exp(b_tot - b)
    e_tot = jnp.exp(b_tot)

    q = proj(_C_Q, QK_W)
    k = proj(_C_K, QK_W)
    q_dec = (q * (GLA_DK ** -0.5) * e_pos).astype(bf16)
    k_inv = (k * e_neg).astype(bf16)
    k_tail = (k * e_tail).astype(bf16)
    vb = proj(_C_V, V_W).astype(bf16)

    for hh in range(GLA_HEADS):
        qs = slice(hh * GLA_DK, (hh + 1) * GLA_DK)
        vs = slice(hh * GLA_DV, (hh + 1) * GLA_DV)
        sc = lax.dot_general(q_dec[:, qs], k_inv[:, qs], (((1,), (1,)), ((), ())),
                             preferred_element_type=f32)
        sc = jnp.where(causal, sc, 0.0).astype(bf16)
        obuf_ref[:, vs] = _dot(sc, vb[:, vs])
        st = state_ref[hh]
        for c in range(nchunk):
            rs = slice(c * GLA_CHUNK, (c + 1) * GLA_CHUNK)
            o_inter = lax.dot_general(q_dec[rs, qs], st.astype(bf16), (((1,), (1,)), ((), ())),
                                      preferred_element_type=f32)
            obuf_ref[rs, vs] += o_inter
            upd = lax.dot_general(vb[rs, vs], k_tail[rs, qs], (((0,), (0,)), ((), ())),
                                  preferred_element_type=f32)
            st = st * e_tot[c * GLA_CHUNK:c * GLA_CHUNK + 1, qs] + upd
        state_ref[hh] = st

    r = proj(_C_R, V_W)
    ggla = ggla_ref[...]
    o_parts = []
    for hh in range(GLA_HEADS):
        vs = slice(hh * GLA_DV, (hh + 1) * GLA_DV)
        o_h = _rms(obuf_ref[:, vs], ggla[:, vs])
        r_h = r[:, vs]
        o_parts.append((o_h * (r_h * jax.nn.sigmoid(r_h))).astype(bf16))
    y_a = _dot(jnp.concatenate(o_parts, axis=1), wa_ref[...])

    u = proj(_C_CC, D_MODEL) * proj(_C_CX, D_MODEL)
    ubuf_ref[8:8 + ts, :] = u
    wc = wconv_ref[...]
    y_conv = (wc[0:1, :] * ubuf_ref[6:6 + ts, :] + wc[1:2, :] * ubuf_ref[7:7 + ts, :]
              + wc[2:3, :] * u)
    ubuf_ref[0:8, :] = ubuf_ref[ts:ts + 8, :]
    y_b = _dot((proj(_C_CB, D_MODEL) * y_conv).astype(bf16), wb_ref[...])

    mixed = (jax.nn.sigmoid(proj(_C_GA, D_MODEL)) * y_a
             + jax.nn.sigmoid(proj(_C_GB, D_MODEL)) * y_b)
    x1 = x + _dot(mixed.astype(bf16), wo_ref[...])
    x1_ref[...] = x1

    h2 = _rms(x1, gffn_ref[...])
    h2_hi, h2_lo = _split_bf16(h2)
    h2_ref[...] = h2_hi
    wrh = wrh_ref[...]
    logit_ref[...] = (_dot(h2_hi, wrh) + _dot(h2_lo, wrh) + _dot(h2_hi, wrl_ref[...])
                      + br_ref[...])


def _mixer(x2, batch, seq, gmix, wmain, walr, wup, balpha, ggla, wconv, wa, wb, wo, gffn, wrh, wrl, br):
    ts = min(MIX_TS, seq)
    ns = seq // ts
    tokens = batch * seq

    def const(shape):
        return pl.BlockSpec(shape, lambda b, s: (0,) * len(shape), pipeline_mode=pl.Buffered(1))

    def row(width, dtype):
        return pl.BlockSpec((ts, width), lambda b, s: (b * ns + s, 0))

    return pl.pallas_call(
        _mixer_body,
        grid=(batch, ns),
        in_specs=[row(D_MODEL, f32), const(gmix.shape), const(wmain.shape), const(walr.shape),
                  const(wup.shape), const(balpha.shape), const(ggla.shape), const(wconv.shape),
                  const(wa.shape), const(wb.shape), const(wo.shape), const(gffn.shape),
                  const(wrh.shape), const(wrl.shape), const(br.shape)],
        out_specs=[row(D_MODEL, f32), row(D_MODEL, bf16), row(LANES, f32)],
        out_shape=[jax.ShapeDtypeStruct((tokens, D_MODEL), f32),
                   jax.ShapeDtypeStruct((tokens, D_MODEL), bf16),
                   jax.ShapeDtypeStruct((tokens, LANES), f32)],
        scratch_shapes=[pltpu.VMEM((GLA_HEADS, GLA_DV, GLA_DK), f32),
                        pltpu.VMEM((ts + 8, D_MODEL), f32),
                        pltpu.VMEM((ts, V_W), f32)],
        compiler_params=pltpu.CompilerParams(
            dimension_semantics=("parallel", "arbitrary"), vmem_limit_bytes=VMEM_LIMIT),
        name="mixer",
    )(x2, gmix, wmain, walr, wup, balpha, ggla, wconv, wa, wb, wo, gffn, wrh, wrl, br)


def _route_body(logit_ref, meta_ref, metat_ref, cnt_ref):
    tm = logit_ref.shape[0]
    lane = lax.broadcasted_iota(jnp.int32, (tm, LANES), 1)
    lane_f = lane.astype(f32)
    lg = jnp.where(lane < N_EXPERTS, logit_ref[...], -jnp.inf)
    sels, tops = [], []
    for _ in range(TOP_K):
        m = jnp.max(lg, axis=-1, keepdims=True)
        first = jnp.min(jnp.where(lg == m, lane_f, float(LANES)), axis=-1, keepdims=True)
        sel = lane_f == first
        sels.append(sel)
        tops.append(m)
        lg = jnp.where(sel, -jnp.inf, lg)
    ps = [jnp.exp(t - tops[0]) for t in tops]
    denom = ps[0] + ps[1] + ps[2] + ps[3]
    gates = [p / denom for p in ps]

    onehot = jnp.zeros((tm, LANES), f32)
    for sel in sels:
        onehot = onehot + jnp.where(sel, 1.0, 0.0)
    onehot_b = onehot.astype(bf16)
    ri = lax.broadcasted_iota(jnp.int32, (tm, tm), 0)
    ci = lax.broadcasted_iota(jnp.int32, (tm, tm), 1)
    strict_lower = jnp.where(ci < ri, 1.0, 0.0).astype(bf16)
    rank = _dot(strict_lower, onehot_b)
    counts = _dot(jnp.ones((8, tm), bf16), onehot_b)
    blocks = jnp.floor((counts + (BF16_ROWS - 1)) * (1.0 / BF16_ROWS))
    ui = lax.broadcasted_iota(jnp.int32, (LANES, LANES), 0)
    uj = lax.broadcasted_iota(jnp.int32, (LANES, LANES), 1)
    strict_upper = jnp.where(ui < uj, 1.0, 0.0).astype(bf16)
    voff = _dot(blocks.astype(bf16), strict_upper) * float(BF16_ROWS)
    slot = rank + voff[0:1, :]

    meta = jnp.zeros((tm, LANES), f32)
    for kk in range(TOP_K):
        lpos = jnp.sum(jnp.where(sels[kk], slot, 0.0), axis=-1, keepdims=True)
        meta = jnp.where(lane == kk, lpos, meta)
        meta = jnp.where(lane == TOP_K + kk, gates[kk], meta)
    meta_ref[...] = meta
    metat_ref[...] = meta.T[0:8, :]
    cnt_ref[...] = counts.astype(jnp.int32)


def _route(logits):
    tokens = logits.shape[0]
    tm = min(MOE_TM, tokens)
    nt = tokens // tm
    return pl.pallas_call(
        _route_body,
        grid=(nt,),
        in_specs=[pl.BlockSpec((tm, LANES), lambda i: (i, 0))],
        out_specs=[pl.BlockSpec((tm, LANES), lambda i: (i, 0)),
                   pl.BlockSpec((8, tm), lambda i: (i, 0)),
                   pl.BlockSpec((8, LANES), lambda i: (i, 0))],
        out_shape=[jax.ShapeDtypeStruct((tokens, LANES), f32),
                   jax.ShapeDtypeStruct((nt * 8, tm), f32),
                   jax.ShapeDtypeStruct((nt * 8, LANES), jnp.int32)],
        compiler_params=pltpu.CompilerParams(
            dimension_semantics=("parallel",), vmem_limit_bytes=VMEM_LIMIT),
        name="route",
    )(logits)


def _lmax(tm):
    full = tm * TOP_K + N_EXPERTS * BF16_ROWS
    return -(-full // 256) * 256


def _segment_copy(src, dst, sem, src_off, dst_off, rows):
    src_off = pl.multiple_of(src_off, BF16_ROWS)
    dst_off = pl.multiple_of(dst_off, BF16_ROWS)
    rows = pl.multiple_of(rows, BF16_ROWS)
    return pltpu.make_async_copy(src.at[pl.ds(src_off, rows)], dst.at[pl.ds(dst_off, rows)], sem)


def _dispatch_body(goff_ref, pc_ref, voff_ref, tail_off_ref, tail_rows_ref, nu_ref,
                   h2_ref, metat_ref, xs_ref, xc_ref, zero_ref, sem, zsem):
    i = pl.program_id(0)
    nt = pl.num_programs(0)
    tm = h2_ref.shape[0]
    lmax = xc_ref.shape[0]

    slots = metat_ref[0:TOP_K, :].astype(jnp.int32)
    row = lax.broadcasted_iota(jnp.int32, (lmax, tm), 0)
    hit = row == slots[0:1, :]
    for kk in range(1, TOP_K):
        hit = jnp.logical_or(hit, row == slots[kk:kk + 1, :])
    pt = jnp.where(hit, 1.0, 0.0).astype(bf16)
    xc_ref[...] = _dot(pt, h2_ref[...]).astype(bf16)

    def seg(e):
        j = i * N_EXPERTS + e
        return _segment_copy(xc_ref, xs_ref, sem, voff_ref[j], goff_ref[j], pc_ref[j])

    for e in range(N_EXPERTS):
        @pl.when(pc_ref[i * N_EXPERTS + e] > 0)
        def _():
            seg(e).start()

    @pl.when(i == nt - 1)
    def _():
        zero_ref[...] = jnp.zeros_like(zero_ref)

        def tail(e):
            return _segment_copy(zero_ref, xs_ref, zsem, 0, tail_off_ref[e], tail_rows_ref[e])

        for e in range(N_EXPERTS):
            @pl.when(tail_rows_ref[e] > 0)
            def _():
                tail(e).start()
        def unused(j):
            return pltpu.make_async_copy(
                zero_ref, xs_ref.at[pl.ds(pl.multiple_of(j * FFN_TM, FFN_TM), FFN_TM)], zsem)

        n_used = nu_ref[0]
        n_tiles = xs_ref.shape[0] // FFN_TM
        lax.fori_loop(n_used, n_tiles, lambda j, c: (unused(j).start(), c)[1], 0)
        lax.fori_loop(n_used, n_tiles, lambda j, c: (unused(j).wait(), c)[1], 0)

        for e in range(N_EXPERTS):
            @pl.when(tail_rows_ref[e] > 0)
            def _():
                tail(e).wait()

    for e in range(N_EXPERTS):
        @pl.when(pc_ref[i * N_EXPERTS + e] > 0)
        def _():
            seg(e).wait()


def _dispatch(goff, pc, voff, tail_off, tail_rows, n_used, h2, metat, rows_total):
    tokens = h2.shape[0]
    tm = min(MOE_TM, tokens)
    nt = tokens // tm
    return pl.pallas_call(
        _dispatch_body,
        grid_spec=pltpu.PrefetchScalarGridSpec(
            num_scalar_prefetch=6,
            grid=(nt,),
            in_specs=[pl.BlockSpec((tm, D_MODEL), lambda i, *_: (i, 0)),
                      pl.BlockSpec((8, tm), lambda i, *_: (i, 0))],
            out_specs=pl.BlockSpec(memory_space=pl.ANY),
            scratch_shapes=[pltpu.VMEM((_lmax(tm), D_MODEL), bf16),
                            pltpu.VMEM((FFN_TM, D_MODEL), bf16),
                            pltpu.SemaphoreType.DMA, pltpu.SemaphoreType.DMA],
        ),
        out_shape=jax.ShapeDtypeStruct((rows_total, D_MODEL), bf16),
        compiler_params=pltpu.CompilerParams(
            dimension_semantics=("arbitrary",), vmem_limit_bytes=VMEM_LIMIT),
        name="dispatch",
    )(goff, pc, voff, tail_off, tail_rows, n_used, h2, metat)


def _ffn_body(te_ref, nu_ref, xs_ref, wgu_ref, bgu_ref, wd_ref, bd_ref, os_ref):
    j = pl.program_id(0)

    @pl.when(j < nu_ref[0])
    def _():
        gu = _dot(xs_ref[...], wgu_ref[...].astype(bf16)) + bgu_ref[...]
        gate = jnp.minimum(gu[:, :D_FF], SWIGLU_LIMIT)
        lin = jnp.clip(gu[:, D_FF:], -SWIGLU_LIMIT, SWIGLU_LIMIT)
        act = (lin + 1.0) * (gate * jax.nn.sigmoid(SWIGLU_ALPHA * gate))
        out = _dot(act.astype(bf16), wd_ref[...].astype(bf16)) + bd_ref[...]
        os_ref[...] = out.astype(bf16)

    @pl.when(j >= nu_ref[0])
    def _():
        os_ref[...] = jnp.zeros_like(os_ref)


def _ffn(tile_expert, n_used, xs, wgu, bgu, wd, bd):
    rows_total = xs.shape[0]
    ntiles = rows_total // FFN_TM

    def xrow(j, te, nu):
        return (jnp.minimum(j, nu[0] - 1), 0)

    return pl.pallas_call(
        _ffn_body,
        grid_spec=pltpu.PrefetchScalarGridSpec(
            num_scalar_prefetch=2,
            grid=(ntiles,),
            in_specs=[pl.BlockSpec((FFN_TM, D_MODEL), xrow),
                      pl.BlockSpec((None, D_MODEL, 2 * D_FF), lambda j, te, nu: (te[j], 0, 0)),
                      pl.BlockSpec((None, 1, 2 * D_FF), lambda j, te, nu: (te[j], 0, 0)),
                      pl.BlockSpec((None, D_FF, D_MODEL), lambda j, te, nu: (te[j], 0, 0)),
                      pl.BlockSpec((None, 1, D_MODEL), lambda j, te, nu: (te[j], 0, 0))],
            out_specs=pl.BlockSpec((FFN_TM, D_MODEL), lambda j, te, nu: (j, 0)),
        ),
        out_shape=jax.ShapeDtypeStruct((rows_total, D_MODEL), bf16),
        compiler_params=pltpu.CompilerParams(
            dimension_semantics=("arbitrary",), vmem_limit_bytes=VMEM_LIMIT),
        name="experts",
    )(tile_expert, n_used, xs, wgu, bgu, wd, bd)


def _combine_body(final_norm, goff_ref, pc_ref, voff_ref, os_ref, meta_ref, x1_ref, gfin_ref, out_ref,
                  oc_ref, sem):
    i = pl.program_id(0)
    tm = x1_ref.shape[0]
    lmax = oc_ref.shape[0]

    @pl.when(i == 0)
    def _():
        oc_ref[...] = jnp.zeros_like(oc_ref)

    def seg(e):
        j = i * N_EXPERTS + e
        return _segment_copy(os_ref, oc_ref, sem, goff_ref[j], voff_ref[j], pc_ref[j])

    for e in range(N_EXPERTS):
        @pl.when(pc_ref[i * N_EXPERTS + e] > 0)
        def _():
            seg(e).start()

    meta = meta_ref[...]
    col = lax.broadcasted_iota(jnp.int32, (tm, lmax), 1)
    p = jnp.zeros((tm, lmax), f32)
    for kk in range(TOP_K):
        slot = meta[:, kk:kk + 1].astype(jnp.int32)
        p = jnp.where(col == slot, meta[:, TOP_K + kk:TOP_K + kk + 1], p)
    pb = p.astype(bf16)

    for e in range(N_EXPERTS):
        @pl.when(pc_ref[i * N_EXPERTS + e] > 0)
        def _():
            seg(e).wait()

    xo = x1_ref[...] + _dot(pb, oc_ref[...])
    out_ref[...] = _rms(xo, gfin_ref[...]) if final_norm else xo


def _combine(goff, pc, voff, os, meta, x1, gfin, final_norm):
    tokens = x1.shape[0]
    tm = min(MOE_TM, tokens)
    nt = tokens // tm
    return pl.pallas_call(
        functools.partial(_combine_body, final_norm),
        grid_spec=pltpu.PrefetchScalarGridSpec(
            num_scalar_prefetch=3,
            grid=(nt,),
            in_specs=[pl.BlockSpec(memory_space=pl.ANY),
                      pl.BlockSpec((tm, LANES), lambda i, *_: (i, 0)),
                      pl.BlockSpec((tm, D_MODEL), lambda i, *_: (i, 0)),
                      pl.BlockSpec((1, D_MODEL), lambda i, *_: (0, 0))],
            out_specs=pl.BlockSpec((tm, D_MODEL), lambda i, *_: (i, 0)),
            scratch_shapes=[pltpu.VMEM((_lmax(tm), D_MODEL), bf16), pltpu.SemaphoreType.DMA],
        ),
        out_shape=jax.ShapeDtypeStruct((tokens, D_MODEL), f32),
        compiler_params=pltpu.CompilerParams(
            dimension_semantics=("arbitrary",), vmem_limit_bytes=VMEM_LIMIT),
        name="combine",
    )(goff, pc, voff, os, meta, x1, gfin)


def _plan(counts, nt):
    c = counts.reshape(nt, 8, LANES)[:, 0, :N_EXPERTS]
    pc = (c + (BF16_ROWS - 1)) // BF16_ROWS * BF16_ROWS
    voff = jnp.cumsum(pc, axis=1) - pc
    gsize = jnp.sum(pc, axis=0)
    gpad = (gsize + (FFN_TM - 1)) // FFN_TM * FFN_TM
    gend = jnp.cumsum(gpad)
    gstart = gend - gpad
    goff = gstart[None, :] + jnp.cumsum(pc, axis=0) - pc
    n_used = gend[-1] // FFN_TM
    return pc, voff, goff, gstart + gsize, gpad - gsize, gend, n_used


def kernel(x, g_mix, w_in, w_alpha_up, b_alpha, g_gla, w_conv, w_branch_a, w_branch_b, w_out, g_ffn, w_router, b_router, w_gate_up, b_gate_up, w_down, b_down, g_final):
    batch, seq, _ = x.shape
    tokens = batch * seq
    depth = w_in.shape[0]
    x2 = x.reshape(tokens, D_MODEL)
    tm = min(MOE_TM, tokens)
    nt = tokens // tm
    rows_total = tokens * TOP_K + nt * N_EXPERTS * (BF16_ROWS - 1) + N_EXPERTS * (FFN_TM - 1)
    rows_total = -(-rows_total // FFN_TM) * FFN_TM
    ntiles = rows_total // FFN_TM

    for l in range(depth):
        wi = w_in[l]
        a0 = 2 * QK_W + 2 * V_W
        wmain = jnp.concatenate([wi[:, :a0], wi[:, a0 + GLA_RANK:]], axis=1).astype(bf16)
        walr = jnp.pad(wi[:, a0:a0 + GLA_RANK], ((0, 0), (0, LANES - GLA_RANK))).astype(bf16)
        wup = jnp.pad(w_alpha_up[l], ((0, LANES - GLA_RANK), (0, 0))).astype(bf16)
        wr = jnp.pad(w_router[l], ((0, 0), (0, LANES - N_EXPERTS)))
        wrh = wr.astype(bf16)
        wrl = (wr - wrh.astype(f32)).astype(bf16)
        br = jnp.pad(b_router[l], (0, LANES - N_EXPERTS)).reshape(1, LANES)
        wconv = jnp.pad(w_conv[l], ((0, 8 - CONV_K), (0, 0)))

        x1, h2, logits = _mixer(
            x2, batch, seq, g_mix[l].reshape(1, D_MODEL), wmain, walr, wup,
            b_alpha[l].reshape(1, QK_W), g_gla[l].reshape(1, V_W), wconv,
            w_branch_a[l].astype(bf16), w_branch_b[l].astype(bf16), w_out[l].astype(bf16),
            g_ffn[l].reshape(1, D_MODEL), wrh, wrl, br)

        meta, metat, counts = _route(logits)
        pc, voff, goff, tail_off, tail_rows, gend, n_used = _plan(counts, nt)
        tile_expert = jnp.minimum(
            jnp.searchsorted(gend, jnp.arange(ntiles, dtype=jnp.int32) * FFN_TM, side="right"),
            N_EXPERTS - 1).astype(jnp.int32)
        goff_f = goff.reshape(-1).astype(jnp.int32)
        pc_f = pc.reshape(-1).astype(jnp.int32)
        voff_f = voff.reshape(-1).astype(jnp.int32)

        n_used = n_used.reshape(1).astype(jnp.int32)
        xs = _dispatch(goff_f, pc_f, voff_f, tail_off.astype(jnp.int32), tail_rows.astype(jnp.int32),
                       n_used, h2, metat, rows_total)
        os = _ffn(tile_expert, n_used, xs,
                  w_gate_up[l], b_gate_up[l].reshape(N_EXPERTS, 1, 2 * D_FF),
                  w_down[l], b_down[l].reshape(N_EXPERTS, 1, D_MODEL))
        x2 = _combine(goff_f, pc_f, voff_f, os, meta, x1, g_final.reshape(1, D_MODEL), l == depth - 1)
    return x2.reshape(batch, seq, D_MODEL)
```

```python
import functools

import jax
import jax.numpy as jnp
from jax import lax
from jax.experimental import pallas as pl
from jax.experimental.pallas import tpu as pltpu

D_MODEL = 1024
GLA_HEADS = 4
GLA_DK = 128
GLA_DV = 256
GLA_RANK = 16
GLA_TAU = 16.0
GLA_CHUNK = 64
_CHUNK_SHIFT = GLA_CHUNK.bit_length() - 1
CONV_K = 3
N_EXPERTS = 32
TOP_K = 4
D_FF = 1024
SWIGLU_LIMIT = 7.0
SWIGLU_ALPHA = 1.702
EPS = 1e-6

QK_W = GLA_HEADS * GLA_DK
V_W = GLA_HEADS * GLA_DV

LANES = 128
BF16_ROWS = 16
VMEM_LIMIT = 56 * 1024 * 1024

MIX_TS = 256
MOE_TM = 512
FFN_TM = 256

_C_Q, _C_K, _C_V, _C_R = 0, QK_W, 2 * QK_W, 2 * QK_W + V_W
_C_CB = _C_R + V_W
_C_CC = _C_CB + D_MODEL
_C_CX = _C_CC + D_MODEL
_C_GA = _C_CX + D_MODEL
_C_GB = _C_GA + D_MODEL
_W_MAIN = _C_GB + D_MODEL

f32 = jnp.float32
bf16 = jnp.bfloat16


def _rms(x, g):
    return x * lax.rsqrt(jnp.mean(x * x, axis=-1, keepdims=True) + EPS) * g


def _dot(a, b):
    return jnp.dot(a, b, preferred_element_type=f32)


def _split_bf16(a):
    hi = a.astype(bf16)
    lo = (a - hi.astype(f32)).astype(bf16)
    return hi, lo


def _mixer_body(x_ref, gmix_ref, wmain_ref, walr_ref, wup_ref, balpha_ref, ggla_ref, wconv_ref,
                wa_ref, wb_ref, wo_ref, gffn_ref, wrh_ref, wrl_ref, br_ref,
                x1_ref, h2_ref, logit_ref,
                state_ref, ubuf_ref, obuf_ref):
    ts = x_ref.shape[0]
    nchunk = ts // GLA_CHUNK

    @pl.when(pl.program_id(1) == 0)
    def _():
        state_ref[...] = jnp.zeros_like(state_ref)
        ubuf_ref[0:8, :] = jnp.zeros((8, D_MODEL), f32)

    x = x_ref[...]
    hb = _rms(x, gmix_ref[...]).astype(bf16)

    def proj(c0, width):
        return _dot(hb, wmain_ref[:, c0:c0 + width])

    a_lr = _dot(hb, walr_ref[...])
    pre = _dot(a_lr.astype(bf16), wup_ref[...]) + balpha_ref[...]
    log_a = -(jnp.maximum(-pre, 0.0) + jnp.log1p(jnp.exp(-jnp.abs(pre)))) * (1.0 / GLA_TAU)
    ri = lax.broadcasted_iota(jnp.int32, (ts, ts), 0)
    ci = lax.broadcasted_iota(jnp.int32, (ts, ts), 1)
    same_chunk = (ri >> _CHUNK_SHIFT) == (ci >> _CHUNK_SHIFT)
    causal = jnp.logical_and(same_chunk, ci <= ri)
    tri = jnp.where(causal, 1.0, 0.0).astype(bf16)
    blk = jnp.where(same_chunk, 1.0, 0.0).astype(bf16)
    la_hi, la_lo = _split_bf16(log_a)
    b = _dot(tri, la_hi) + _dot(tri, la_lo)
    b_tot = _dot(blk, la_hi) + _dot(blk, la_lo)
    e_pos = jnp.exp(b)
    e_neg = jnp.exp(-b)
    e_tail = jnp.exp(b_tot - b)
    e_tot = jnp.exp(b_tot)

    q = proj(_C_Q, QK_W)
    k = proj(_C_K, QK_W)
    q_dec = (q * (GLA_DK ** -0.5) * e_pos).astype(bf16)
    k_inv = (k * e_neg).astype(bf16)
    k_tail = (k * e_tail).astype(bf16)
    vb = proj(_C_V, V_W).astype(bf16)

    for hh in range(GLA_HEADS):
        qs = slice(hh * GLA_DK, (hh + 1) * GLA_DK)
        vs = slice(hh * GLA_DV, (hh + 1) * GLA_DV)
        sc = lax.dot_general(q_dec[:, qs], k_inv[:, qs], (((1,), (1,)), ((), ())),
                             preferred_element_type=f32)
        sc = jnp.where(causal, sc, 0.0).astype(bf16)
        obuf_ref[:, vs] = _dot(sc, vb[:, vs])
        st = state_ref[hh]
        for c in range(nchunk):
            rs = slice(c * GLA_CHUNK, (c + 1) * GLA_CHUNK)
            o_inter = lax.dot_general(q_dec[rs, qs], st.astype(bf16), (((1,), (1,)), ((), ())),
                                      preferred_element_type=f32)
            obuf_ref[rs, vs] += o_inter
            upd = lax.dot_general(vb[rs, vs], k_tail[rs, qs], (((0,), (0,)), ((), ())),
                                  preferred_element_type=f32)
            st = st * e_tot[c * GLA_CHUNK:c * GLA_CHUNK + 1, qs] + upd
        state_ref[hh] = st

    r = proj(_C_R, V_W)
    ggla = ggla_ref[...]
    o_parts = []
    for hh in range(GLA_HEADS):
        vs = slice(hh * GLA_DV, (hh + 1) * GLA_DV)
        o_h = _rms(obuf_ref[:, vs], ggla[:, vs])
        r_h = r[:, vs]
        o_parts.append((o_h * (r_h * jax.nn.sigmoid(r_h))).astype(bf16))
    y_a = _dot(jnp.concatenate(o_parts, axis=1), wa_ref[...])

    u = proj(_C_CC, D_MODEL) * proj(_C_CX, D_MODEL)
    ubuf_ref[8:8 + ts, :] = u
    wc = wconv_ref[...]
    y_conv = (wc[0:1, :] * ubuf_ref[6:6 + ts, :] + wc[1:2, :] * ubuf_ref[7:7 + ts, :]
              + wc[2:3, :] * u)
    ubuf_ref[0:8, :] = ubuf_ref[ts:ts + 8, :]
    y_b = _dot((proj(_C_CB, D_MODEL) * y_conv).astype(bf16), wb_ref[...])

    mixed = (jax.nn.sigmoid(proj(_C_GA, D_MODEL)) * y_a
             + jax.nn.sigmoid(proj(_C_GB, D_MODEL)) * y_b)
    x1 = x + _dot(mixed.astype(bf16), wo_ref[...])
    x1_ref[...] = x1

    h2 = _rms(x1, gffn_ref[...])
    h2_hi, h2_lo = _split_bf16(h2)
    h2_ref[...] = h2_hi
    wrh = wrh_ref[...]
    logit_ref[...] = (_dot(h2_hi, wrh) + _dot(h2_lo, wrh) + _dot(h2_hi, wrl_ref[...])
                      + br_ref[...])


def _mixer(x2, batch, seq, gmix, wmain, walr, wup, balpha, ggla, wconv, wa, wb, wo, gffn, wrh, wrl, br):
    ts = min(MIX_TS, seq)
    ns = seq // ts
    tokens = batch * seq

    def const(shape):
        return pl.BlockSpec(shape, lambda b, s: (0,) * len(shape), pipeline_mode=pl.Buffered(1))

    def row(width, dtype):
        return pl.BlockSpec((ts, width), lambda b, s: (b * ns + s, 0))

    return pl.pallas_call(
        _mixer_body,
        grid=(batch, ns),
        in_specs=[row(D_MODEL, f32), const(gmix.shape), const(wmain.shape), const(walr.shape),
                  const(wup.shape), const(balpha.shape), const(ggla.shape), const(wconv.shape),
                  const(wa.shape), const(wb.shape), const(wo.shape), const(gffn.shape),
                  const(wrh.shape), const(wrl.shape), const(br.shape)],
        out_specs=[row(D_MODEL, f32), row(D_MODEL, bf16), row(LANES, f32)],
        out_shape=[jax.ShapeDtypeStruct((tokens, D_MODEL), f32),
                   jax.ShapeDtypeStruct((tokens, D_MODEL), bf16),
                   jax.ShapeDtypeStruct((tokens, LANES), f32)],
        scratch_shapes=[pltpu.VMEM((GLA_HEADS, GLA_DV, GLA_DK), f32),
                        pltpu.VMEM((ts + 8, D_MODEL), f32),
                        pltpu.VMEM((ts, V_W), f32)],
        compiler_params=pltpu.CompilerParams(
            dimension_semantics=("parallel", "arbitrary"), vmem_limit_bytes=VMEM_LIMIT),
        name="mixer",
    )(x2, gmix, wmain, walr, wup, balpha, ggla, wconv, wa, wb, wo, gffn, wrh, wrl, br)


def _route_body(logit_ref, meta_ref, metat_ref, cnt_ref):
    tm = logit_ref.shape[0]
    lane = lax.broadcasted_iota(jnp.int32, (tm, LANES), 1)
    lane_f = lane.astype(f32)
    lg = jnp.where(lane < N_EXPERTS, logit_ref[...], -jnp.inf)
    sels, tops = [], []
    for _ in range(TOP_K):
        m = jnp.max(lg, axis=-1, keepdims=True)
        first = jnp.min(jnp.where(lg == m, lane_f, float(LANES)), axis=-1, keepdims=True)
        sel = lane_f == first
        sels.append(sel)
        tops.append(m)
        lg = jnp.where(sel, -jnp.inf, lg)
    ps = [jnp.exp(t - tops[0]) for t in tops]
    denom = ps[0] + ps[1] + ps[2] + ps[3]
    gates = [p / denom for p in ps]

    onehot = jnp.zeros((tm, LANES), f32)
    for sel in sels:
        onehot = onehot + jnp.where(sel, 1.0, 0.0)
    onehot_b = onehot.astype(bf16)
    ri = lax.broadcasted_iota(jnp.int32, (tm, tm), 0)
    ci = lax.broadcasted_iota(jnp.int32, (tm, tm), 1)
    strict_lower = jnp.where(ci < ri, 1.0, 0.0).astype(bf16)
    rank = _dot(strict_lower, onehot_b)
    counts = _dot(jnp.ones((8, tm), bf16), onehot_b)
    blocks = jnp.floor((counts + (BF16_ROWS - 1)) * (1.0 / BF16_ROWS))
    ui = lax.broadcasted_iota(jnp.int32, (LANES, LANES), 0)
    uj = lax.broadcasted_iota(jnp.int32, (LANES, LANES), 1)
    strict_upper = jnp.where(ui < uj, 1.0, 0.0).astype(bf16)
    voff = _dot(blocks.astype(bf16), strict_upper) * float(BF16_ROWS)
    slot = rank + voff[0:1, :]

    meta = jnp.zeros((tm, LANES), f32)
    for kk in range(TOP_K):
        lpos = jnp.sum(jnp.where(sels[kk], slot, 0.0), axis=-1, keepdims=True)
        meta = jnp.where(lane == kk, lpos, meta)
        meta = jnp.where(lane == TOP_K + kk, gates[kk], meta)
    meta_ref[...] = meta
    metat_ref[...] = meta.T[0:8, :]
    cnt_ref[...] = counts.astype(jnp.int32)


def _route(logits):
    tokens = logits.shape[0]
    tm = min(MOE_TM, tokens)
    nt = tokens // tm
    return pl.pallas_call(
        _route_body,
        grid=(nt,),
        in_specs=[pl.BlockSpec((tm, LANES), lambda i: (i, 0))],
        out_specs=[pl.BlockSpec((tm, LANES), lambda i: (i, 0)),
                   pl.BlockSpec((8, tm), lambda i: (i, 0)),
                   pl.BlockSpec((8, LANES), lambda i: (i, 0))],
        out_shape=[jax.ShapeDtypeStruct((tokens, LANES), f32),
                   jax.ShapeDtypeStruct((nt * 8, tm), f32),
                   jax.ShapeDtypeStruct((nt * 8, LANES), jnp.int32)],
        compiler_params=pltpu.CompilerParams(
            dimension_semantics=("parallel",), vmem_limit_bytes=VMEM_LIMIT),
        name="route",
    )(logits)


def _lmax(tm):
    full = tm * TOP_K + N_EXPERTS * BF16_ROWS
    return -(-full // 256) * 256


def _segment_copy(src, dst, sem, src_off, dst_off, rows):
    src_off = pl.multiple_of(src_off, BF16_ROWS)
    dst_off = pl.multiple_of(dst_off, BF16_ROWS)
    rows = pl.multiple_of(rows, BF16_ROWS)
    return pltpu.make_async_copy(src.at[pl.ds(src_off, rows)], dst.at[pl.ds(dst_off, rows)], sem)


def _zero_rows(zero_ref, dst_ref, sem, first, start):
    tile = zero_ref.shape[0]
    n = dst_ref.shape[0] - first
    n_full = n // tile

    def full(j):
        return _segment_copy(zero_ref, dst_ref, sem, 0, first + j * tile, tile)

    def rest():
        return _segment_copy(zero_ref, dst_ref, sem, 0, first + n_full * tile, n - n_full * tile)

    def step(j, carry):
        full(j).start() if start else full(j).wait()
        return carry

    lax.fori_loop(0, n_full, step, 0)

    @pl.when(n - n_full * tile > 0)
    def _():
        rest().start() if start else rest().wait()


def _dispatch_body(goff_ref, pc_ref, voff_ref, used_ref,
                   h2_ref, metat_ref, xs_ref, xc_ref, zero_ref, sem, zsem):
    i = pl.program_id(0)
    nt = pl.num_programs(0)
    tm = h2_ref.shape[0]
    lmax = xc_ref.shape[0]

    slots = metat_ref[0:TOP_K, :].astype(jnp.int32)
    row = lax.broadcasted_iota(jnp.int32, (lmax, tm), 0)
    pt = jnp.zeros((lmax, tm), f32)
    for kk in range(TOP_K):
        pt = jnp.where(row == slots[kk:kk + 1, :], 1.0, pt)
    xc_ref[...] = _dot(pt.astype(bf16), h2_ref[...]).astype(bf16)

    def seg(e):
        j = i * N_EXPERTS + e
        return _segment_copy(xc_ref, xs_ref, sem, voff_ref[j], goff_ref[j], pc_ref[j])

    for e in range(N_EXPERTS):
        @pl.when(pc_ref[i * N_EXPERTS + e] > 0)
        def _():
            seg(e).start()

    @pl.when(i == 0)
    def _():
        zero_ref[...] = jnp.zeros_like(zero_ref)
        _zero_rows(zero_ref, xs_ref, zsem, used_ref[0], start=True)

    @pl.when(i == nt - 1)
    def _():
        _zero_rows(zero_ref, xs_ref, zsem, used_ref[0], start=False)

    for e in range(N_EXPERTS):
        @pl.when(pc_ref[i * N_EXPERTS + e] > 0)
        def _():
            seg(e).wait()


def _dispatch(goff, pc, voff, used, h2, metat, rows_total):
    tokens = h2.shape[0]
    tm = min(MOE_TM, tokens)
    nt = tokens // tm
    return pl.pallas_call(
        _dispatch_body,
        grid_spec=pltpu.PrefetchScalarGridSpec(
            num_scalar_prefetch=4,
            grid=(nt,),
            in_specs=[pl.BlockSpec((tm, D_MODEL), lambda i, *_: (i, 0)),
                      pl.BlockSpec((8, tm), lambda i, *_: (i, 0))],
            out_specs=pl.BlockSpec(memory_space=pl.ANY),
            scratch_shapes=[pltpu.VMEM((_lmax(tm), D_MODEL), bf16),
                            pltpu.VMEM((FFN_TM, D_MODEL), bf16),
                            pltpu.SemaphoreType.DMA, pltpu.SemaphoreType.DMA],
        ),
        out_shape=jax.ShapeDtypeStruct((rows_total, D_MODEL), bf16),
        compiler_params=pltpu.CompilerParams(
            dimension_semantics=("arbitrary",), vmem_limit_bytes=VMEM_LIMIT),
        name="dispatch",
    )(goff, pc, voff, used, h2, metat)


def _ffn_body(gstart_ref, gsize_ref, used_ref, xs_ref, wgu_ref, bgu_ref, wd_ref, bd_ref, os_ref,
              xbuf_ref, obuf_ref, zero_ref, state_ref, isem, osem, zsem):
    e = pl.program_id(0)
    ne = pl.num_programs(0)
    tile = xbuf_ref.shape[1]
    base = gstart_ref[e]
    n = gsize_ref[e]
    n_tiles = (n + tile - 1) // tile

    def rows_of(size, t):
        return jnp.minimum(tile, size - t * tile)

    def in_copy(start, size, t, slot):
        rows = pl.multiple_of(rows_of(size, t), BF16_ROWS)
        src = pl.multiple_of(start + t * tile, BF16_ROWS)
        return pltpu.make_async_copy(xs_ref.at[pl.ds(src, rows)],
                                     xbuf_ref.at[slot, pl.ds(0, rows)], isem.at[slot])

    def out_copy(t, slot):
        rows = pl.multiple_of(rows_of(n, t), BF16_ROWS)
        dst = pl.multiple_of(base + t * tile, BF16_ROWS)
        return pltpu.make_async_copy(obuf_ref.at[slot, pl.ds(0, rows)],
                                     os_ref.at[pl.ds(dst, rows)], osem.at[slot])

    @pl.when(e == 0)
    def _():
        xbuf_ref[...] = jnp.zeros_like(xbuf_ref)
        zero_ref[...] = jnp.zeros_like(zero_ref)
        _zero_rows(zero_ref, os_ref, zsem, used_ref[0], start=True)
        state_ref[0] = 0
        state_ref[1] = 0

    slot0 = state_ref[0]

    @pl.when(jnp.logical_and(n_tiles > 0, state_ref[1] == 0))
    def _():
        in_copy(base, n, 0, slot0).start()

    has_next = jnp.logical_and(e + 1 < ne, gsize_ref[jnp.minimum(e + 1, ne - 1)] > 0)

    def step(t, carry):
        slot = (slot0 + t) & 1

        @pl.when(t + 1 < n_tiles)
        def _():
            in_copy(base, n, t + 1, 1 - slot).start()

        @pl.when(jnp.logical_and(t + 1 == n_tiles, has_next))
        def _():
            nxt = jnp.minimum(e + 1, ne - 1)
            in_copy(gstart_ref[nxt], gsize_ref[nxt], 0, 1 - slot).start()

        in_copy(base, n, t, slot).wait()

        @pl.when(t >= 2)
        def _():
            out_copy(t - 2, slot).wait()

        gu = _dot(xbuf_ref[slot], wgu_ref[...].astype(bf16)) + bgu_ref[...]
        gate = jnp.minimum(gu[:, :D_FF], SWIGLU_LIMIT)
        lin = jnp.clip(gu[:, D_FF:], -SWIGLU_LIMIT, SWIGLU_LIMIT)
        act = (lin + 1.0) * (gate * jax.nn.sigmoid(SWIGLU_ALPHA * gate))
        out = _dot(act.astype(bf16), wd_ref[...].astype(bf16)) + bd_ref[...]
        obuf_ref[slot] = out.astype(bf16)
        out_copy(t, slot).start()
        return carry

    lax.fori_loop(0, n_tiles, step, 0)

    for back in (2, 1):
        @pl.when(n_tiles >= back)
        def _():
            t = n_tiles - back
            out_copy(t, (slot0 + t) & 1).wait()

    @pl.when(n_tiles > 0)
    def _():
        state_ref[0] = (slot0 + n_tiles) & 1
        state_ref[1] = has_next.astype(jnp.int32)

    @pl.when(e == ne - 1)
    def _():
        _zero_rows(zero_ref, os_ref, zsem, used_ref[0], start=False)


def _ffn(gstart, gsize, used, xs, wgu, bgu, wd, bd):
    rows_total = xs.shape[0]
    return pl.pallas_call(
        _ffn_body,
        grid_spec=pltpu.PrefetchScalarGridSpec(
            num_scalar_prefetch=3,
            grid=(N_EXPERTS,),
            in_specs=[pl.BlockSpec(memory_space=pl.ANY),
                      pl.BlockSpec((None, D_MODEL, 2 * D_FF), lambda e, *_: (e, 0, 0)),
                      pl.BlockSpec((None, 1, 2 * D_FF), lambda e, *_: (e, 0, 0)),
                      pl.BlockSpec((None, D_FF, D_MODEL), lambda e, *_: (e, 0, 0)),
                      pl.BlockSpec((None, 1, D_MODEL), lambda e, *_: (e, 0, 0))],
            out_specs=pl.BlockSpec(memory_space=pl.ANY),
            scratch_shapes=[pltpu.VMEM((2, FFN_TM, D_MODEL), bf16),
                            pltpu.VMEM((2, FFN_TM, D_MODEL), bf16),
                            pltpu.VMEM((FFN_TM, D_MODEL), bf16),
                            pltpu.SMEM((2,), jnp.int32),
                            pltpu.SemaphoreType.DMA((2,)), pltpu.SemaphoreType.DMA((2,)),
                            pltpu.SemaphoreType.DMA],
        ),
        out_shape=jax.ShapeDtypeStruct((rows_total, D_MODEL), bf16),
        compiler_params=pltpu.CompilerParams(
            dimension_semantics=("arbitrary",), vmem_limit_bytes=VMEM_LIMIT),
        name="experts",
    )(gstart, gsize, used, xs, wgu, bgu, wd, bd)


def _combine_body(final_norm, goff_ref, pc_ref, voff_ref, os_ref, meta_ref, x1_ref, gfin_ref, out_ref,
                  oc_ref, sem):
    i = pl.program_id(0)
    tm = x1_ref.shape[0]
    lmax = oc_ref.shape[0]

    @pl.when(i == 0)
    def _():
        oc_ref[...] = jnp.zeros_like(oc_ref)

    def seg(e):
        j = i * N_EXPERTS + e
        return _segment_copy(os_ref, oc_ref, sem, goff_ref[j], voff_ref[j], pc_ref[j])

    for e in range(N_EXPERTS):
        @pl.when(pc_ref[i * N_EXPERTS + e] > 0)
        def _():
            seg(e).start()

    meta = meta_ref[...]
    col = lax.broadcasted_iota(jnp.int32, (tm, lmax), 1)
    p = jnp.zeros((tm, lmax), f32)
    for kk in range(TOP_K):
        slot = meta[:, kk:kk + 1].astype(jnp.int32)
        p = jnp.where(col == slot, meta[:, TOP_K + kk:TOP_K + kk + 1], p)
    pb = p.astype(bf16)

    for e in range(N_EXPERTS):
        @pl.when(pc_ref[i * N_EXPERTS + e] > 0)
        def _():
            seg(e).wait()

    xo = x1_ref[...] + _dot(pb, oc_ref[...])
    out_ref[...] = _rms(xo, gfin_ref[...]) if final_norm else xo


def _combine(goff, pc, voff, os, meta, x1, gfin, final_norm):
    tokens = x1.shape[0]
    tm = min(MOE_TM, tokens)
    nt = tokens // tm
    return pl.pallas_call(
        functools.partial(_combine_body, final_norm),
        grid_spec=pltpu.PrefetchScalarGridSpec(
            num_scalar_prefetch=3,
            grid=(nt,),
            in_specs=[pl.BlockSpec(memory_space=pl.ANY),
                      pl.BlockSpec((tm, LANES), lambda i, *_: (i, 0)),
                      pl.BlockSpec((tm, D_MODEL), lambda i, *_: (i, 0)),
                      pl.BlockSpec((1, D_MODEL), lambda i, *_: (0, 0))],
            out_specs=pl.BlockSpec((tm, D_MODEL), lambda i, *_: (i, 0)),
            scratch_shapes=[pltpu.VMEM((_lmax(tm), D_MODEL), bf16), pltpu.SemaphoreType.DMA],
        ),
        out_shape=jax.ShapeDtypeStruct((tokens, D_MODEL), f32),
        compiler_params=pltpu.CompilerParams(
            dimension_semantics=("arbitrary",), vmem_limit_bytes=VMEM_LIMIT),
        name="combine",
    )(goff, pc, voff, os, meta, x1, gfin)


def _plan(counts, nt):
    c = counts.reshape(nt, 8, LANES)[:, 0, :N_EXPERTS]
    pc = (c + (BF16_ROWS - 1)) // BF16_ROWS * BF16_ROWS
    voff = jnp.cumsum(pc, axis=1) - pc
    gsize = jnp.sum(pc, axis=0)
    gend = jnp.cumsum(gsize)
    gstart = gend - gsize
    goff = gstart[None, :] + jnp.cumsum(pc, axis=0) - pc
    i32 = lambda a: a.reshape(-1).astype(jnp.int32)
    return i32(pc), i32(voff), i32(goff), i32(gstart), i32(gsize), i32(gend[-1:])


def kernel(x, g_mix, w_in, w_alpha_up, b_alpha, g_gla, w_conv, w_branch_a, w_branch_b, w_out, g_ffn, w_router, b_router, w_gate_up, b_gate_up, w_down, b_down, g_final):
    batch, seq, _ = x.shape
    tokens = batch * seq
    depth = w_in.shape[0]
    x2 = x.reshape(tokens, D_MODEL)
    tm = min(MOE_TM, tokens)
    nt = tokens // tm
    rows_total = tokens * TOP_K + nt * N_EXPERTS * (BF16_ROWS - 1)
    rows_total = -(-rows_total // BF16_ROWS) * BF16_ROWS

    for l in range(depth):
        wi = w_in[l]
        a0 = 2 * QK_W + 2 * V_W
        wmain = jnp.concatenate([wi[:, :a0], wi[:, a0 + GLA_RANK:]], axis=1).astype(bf16)
        walr = jnp.pad(wi[:, a0:a0 + GLA_RANK], ((0, 0), (0, LANES - GLA_RANK))).astype(bf16)
        wup = jnp.pad(w_alpha_up[l], ((0, LANES - GLA_RANK), (0, 0))).astype(bf16)
        wr = jnp.pad(w_router[l], ((0, 0), (0, LANES - N_EXPERTS)))
        wrh = wr.astype(bf16)
        wrl = (wr - wrh.astype(f32)).astype(bf16)
        br = jnp.pad(b_router[l], (0, LANES - N_EXPERTS)).reshape(1, LANES)
        wconv = jnp.pad(w_conv[l], ((0, 8 - CONV_K), (0, 0)))

        x1, h2, logits = _mixer(
            x2, batch, seq, g_mix[l].reshape(1, D_MODEL), wmain, walr, wup,
            b_alpha[l].reshape(1, QK_W), g_gla[l].reshape(1, V_W), wconv,
            w_branch_a[l].astype(bf16), w_branch_b[l].astype(bf16), w_out[l].astype(bf16),
            g_ffn[l].reshape(1, D_MODEL), wrh, wrl, br)

        meta, metat, counts = _route(logits)
        pc, voff, goff, gstart, gsize, used = _plan(counts, nt)
        xs = _dispatch(goff, pc, voff, used, h2, metat, rows_total)
        os = _ffn(gstart, gsize, used, xs,
                  w_gate_up[l], b_gate_up[l].reshape(N_EXPERTS, 1, 2 * D_FF),
                  w_down[l], b_down[l].reshape(N_EXPERTS, 1, D_MODEL))
        x2 = _combine(goff, pc, voff, os, meta, x1, g_final.reshape(1, D_MODEL), l == depth - 1)
    return x2.reshape(batch, seq, D_MODEL)
```

```python
import functools

import jax
import jax.numpy as jnp
from jax import lax
from jax.experimental import pallas as pl
from jax.experimental.pallas import tpu as pltpu

D_MODEL = 1024
GLA_HEADS = 4
GLA_DK = 128
GLA_DV = 256
GLA_RANK = 16
GLA_TAU = 16.0
GLA_CHUNK = 64
_CHUNK_SHIFT = GLA_CHUNK.bit_length() - 1
CONV_K = 3
N_EXPERTS = 32
TOP_K = 4
D_FF = 1024
SWIGLU_LIMIT = 7.0
SWIGLU_ALPHA = 1.702
EPS = 1e-6

QK_W = GLA_HEADS * GLA_DK
V_W = GLA_HEADS * GLA_DV

LANES = 128
BF16_ROWS = 16
VMEM_LIMIT = 56 * 1024 * 1024

MIX_TS = 256
MOE_TM = 512
FFN_TM = 256
_TILE_DMA_PRIORITY = 1

_C_Q, _C_K, _C_V, _C_R = 0, QK_W, 2 * QK_W, 2 * QK_W + V_W
_C_CB = _C_R + V_W
_C_CC = _C_CB + D_MODEL
_C_CX = _C_CC + D_MODEL
_C_GA = _C_CX + D_MODEL
_C_GB = _C_GA + D_MODEL
_W_MAIN = _C_GB + D_MODEL

f32 = jnp.float32
bf16 = jnp.bfloat16


def _rms(x, g):
    return x * lax.rsqrt(jnp.mean(x * x, axis=-1, keepdims=True) + EPS) * g


def _dot(a, b):
    return jnp.dot(a, b, preferred_element_type=f32)


def _split_bf16(a):
    hi = a.astype(bf16)
    lo = (a - hi.astype(f32)).astype(bf16)
    return hi, lo


def _mixer_body(x_ref, gmix_ref, wmain_ref, walr_ref, wup_ref, balpha_ref, ggla_ref, wconv_ref,
                wa_ref, wb_ref, wo_ref, gffn_ref, wrh_ref, wrl_ref, br_ref,
                x1_ref, h2_ref, logit_ref,
                state_ref, ubuf_ref, obuf_ref):
    ts = x_ref.shape[0]
    nchunk = ts // GLA_CHUNK

    @pl.when(pl.program_id(1) == 0)
    def _():
        state_ref[...] = jnp.zeros_like(state_ref)
        ubuf_ref[0:8, :] = jnp.zeros((8, D_MODEL), f32)

    x = x_ref[...]
    hb = _rms(x, gmix_ref[...]).astype(bf16)

    def proj(c0, width):
        return _dot(hb, wmain_ref[:, c0:c0 + width])

    a_lr = _dot(hb, walr_ref[...])
    pre = _dot(a_lr.astype(bf16), wup_ref[...]) + balpha_ref[...]
    log_a = -(jnp.maximum(-pre, 0.0) + jnp.log1p(jnp.exp(-jnp.abs(pre)))) * (1.0 / GLA_TAU)
    ri = lax.broadcasted_iota(jnp.int32, (ts, ts), 0)
    ci = lax.broadcasted_iota(jnp.int32, (ts, ts), 1)
    same_chunk = (ri >> _CHUNK_SHIFT) == (ci >> _CHUNK_SHIFT)
    causal = jnp.logical_and(same_chunk, ci <= ri)
    tri = jnp.where(causal, 1.0, 0.0).astype(bf16)
    blk = jnp.where(same_chunk, 1.0, 0.0).astype(bf16)
    la_hi, la_lo = _split_bf16(log_a)
    b = _dot(tri, la_hi) + _dot(tri, la_lo)
    b_tot = _dot(blk, la_hi) + _dot(blk, la_lo)
    e_pos = jnp.exp(b)
    e_neg = jnp.exp(-b)
    e_tail = jnp.exp(b_tot - b)
    e_tot = jnp.exp(b_tot)

    q = proj(_C_Q, QK_W)
    k = proj(_C_K, QK_W)
    q_dec = (q * (GLA_DK ** -0.5) * e_pos).astype(bf16)
    k_inv = (k * e_neg).astype(bf16)
    k_tail = (k * e_tail).astype(bf16)
    vb = proj(_C_V, V_W).astype(bf16)

    for hh in range(GLA_HEADS):
        qs = slice(hh * GLA_DK, (hh + 1) * GLA_DK)
        vs = slice(hh * GLA_DV, (hh + 1) * GLA_DV)
        sc = lax.dot_general(q_dec[:, qs], k_inv[:, qs], (((1,), (1,)), ((), ())),
                             preferred_element_type=f32)
        sc = jnp.where(causal, sc, 0.0).astype(bf16)
        obuf_ref[:, vs] = _dot(sc, vb[:, vs])
        st = state_ref[hh]
        for c in range(nchunk):
            rs = slice(c * GLA_CHUNK, (c + 1) * GLA_CHUNK)
            o_inter = lax.dot_general(q_dec[rs, qs], st.astype(bf16), (((1,), (1,)), ((), ())),
                                      preferred_element_type=f32)
            obuf_ref[rs, vs] += o_inter
            upd = lax.dot_general(vb[rs, vs], k_tail[rs, qs], (((0,), (0,)), ((), ())),
                                  preferred_element_type=f32)
            st = st * e_tot[c * GLA_CHUNK:c * GLA_CHUNK + 1, qs] + upd
        state_ref[hh] = st

    r = proj(_C_R, V_W)
    ggla = ggla_ref[...]
    o_parts = []
    for hh in range(GLA_HEADS):
        vs = slice(hh * GLA_DV, (hh + 1) * GLA_DV)
        o_h = _rms(obuf_ref[:, vs], ggla[:, vs])
        r_h = r[:, vs]
        o_parts.append((o_h * (r_h * jax.nn.sigmoid(r_h))).astype(bf16))
    y_a = _dot(jnp.concatenate(o_parts, axis=1), wa_ref[...])

    u = proj(_C_CC, D_MODEL) * proj(_C_CX, D_MODEL)
    ubuf_ref[8:8 + ts, :] = u
    wc = wconv_ref[...]
    y_conv = (wc[0:1, :] * ubuf_ref[6:6 + ts, :] + wc[1:2, :] * ubuf_ref[7:7 + ts, :]
              + wc[2:3, :] * u)
    ubuf_ref[0:8, :] = ubuf_ref[ts:ts + 8, :]
    y_b = _dot((proj(_C_CB, D_MODEL) * y_conv).astype(bf16), wb_ref[...])

    mixed = (jax.nn.sigmoid(proj(_C_GA, D_MODEL)) * y_a
             + jax.nn.sigmoid(proj(_C_GB, D_MODEL)) * y_b)
    x1 = x + _dot(mixed.astype(bf16), wo_ref[...])
    x1_ref[...] = x1

    h2 = _rms(x1, gffn_ref[...])
    h2_hi, h2_lo = _split_bf16(h2)
    h2_ref[...] = h2_hi
    wrh = wrh_ref[...]
    logit_ref[...] = (_dot(h2_hi, wrh) + _dot(h2_lo, wrh) + _dot(h2_hi, wrl_ref[...])
                      + br_ref[...])


def _mixer(x2, batch, seq, gmix, wmain, walr, wup, balpha, ggla, wconv, wa, wb, wo, gffn, wrh, wrl, br):
    ts = min(MIX_TS, seq)
    ns = seq // ts
    tokens = batch * seq

    def const(shape):
        return pl.BlockSpec(shape, lambda b, s: (0,) * len(shape), pipeline_mode=pl.Buffered(1))

    def row(width, dtype):
        return pl.BlockSpec((ts, width), lambda b, s: (b * ns + s, 0))

    return pl.pallas_call(
        _mixer_body,
        grid=(batch, ns),
        in_specs=[row(D_MODEL, f32), const(gmix.shape), const(wmain.shape), const(walr.shape),
                  const(wup.shape), const(balpha.shape), const(ggla.shape), const(wconv.shape),
                  const(wa.shape), const(wb.shape), const(wo.shape), const(gffn.shape),
                  const(wrh.shape), const(wrl.shape), const(br.shape)],
        out_specs=[row(D_MODEL, f32), row(D_MODEL, bf16), row(LANES, f32)],
        out_shape=[jax.ShapeDtypeStruct((tokens, D_MODEL), f32),
                   jax.ShapeDtypeStruct((tokens, D_MODEL), bf16),
                   jax.ShapeDtypeStruct((tokens, LANES), f32)],
        scratch_shapes=[pltpu.VMEM((GLA_HEADS, GLA_DV, GLA_DK), f32),
                        pltpu.VMEM((ts + 8, D_MODEL), f32),
                        pltpu.VMEM((ts, V_W), f32)],
        compiler_params=pltpu.CompilerParams(
            dimension_semantics=("parallel", "arbitrary"), vmem_limit_bytes=VMEM_LIMIT),
        name="mixer",
    )(x2, gmix, wmain, walr, wup, balpha, ggla, wconv, wa, wb, wo, gffn, wrh, wrl, br)


def _route_body(logit_ref, meta_ref, metat_ref, cnt_ref):
    tm = logit_ref.shape[0]
    lane = lax.broadcasted_iota(jnp.int32, (tm, LANES), 1)
    lane_f = lane.astype(f32)
    lg = jnp.where(lane < N_EXPERTS, logit_ref[...], -jnp.inf)
    sels, tops = [], []
    for _ in range(TOP_K):
        m = jnp.max(lg, axis=-1, keepdims=True)
        first = jnp.min(jnp.where(lg == m, lane_f, float(LANES)), axis=-1, keepdims=True)
        sel = lane_f == first
        sels.append(sel)
        tops.append(m)
        lg = jnp.where(sel, -jnp.inf, lg)
    ps = [jnp.exp(t - tops[0]) for t in tops]
    denom = ps[0] + ps[1] + ps[2] + ps[3]
    gates = [p / denom for p in ps]

    onehot = jnp.zeros((tm, LANES), f32)
    for sel in sels:
        onehot = onehot + jnp.where(sel, 1.0, 0.0)
    onehot_b = onehot.astype(bf16)
    ri = lax.broadcasted_iota(jnp.int32, (tm, tm), 0)
    ci = lax.broadcasted_iota(jnp.int32, (tm, tm), 1)
    strict_lower = jnp.where(ci < ri, 1.0, 0.0).astype(bf16)
    rank = _dot(strict_lower, onehot_b)
    counts = _dot(jnp.ones((8, tm), bf16), onehot_b)
    blocks = jnp.floor((counts + (BF16_ROWS - 1)) * (1.0 / BF16_ROWS))
    ui = lax.broadcasted_iota(jnp.int32, (LANES, LANES), 0)
    uj = lax.broadcasted_iota(jnp.int32, (LANES, LANES), 1)
    strict_upper = jnp.where(ui < uj, 1.0, 0.0).astype(bf16)
    voff = _dot(blocks.astype(bf16), strict_upper) * float(BF16_ROWS)
    slot = rank + voff[0:1, :]

    meta = jnp.zeros((tm, LANES), f32)
    for kk in range(TOP_K):
        lpos = jnp.sum(jnp.where(sels[kk], slot, 0.0), axis=-1, keepdims=True)
        meta = jnp.where(lane == kk, lpos, meta)
        meta = jnp.where(lane == TOP_K + kk, gates[kk], meta)
    meta_ref[...] = meta
    metat_ref[...] = meta.T[0:8, :]
    cnt_ref[...] = counts.astype(jnp.int32)


def _route(logits):
    tokens = logits.shape[0]
    tm = min(MOE_TM, tokens)
    nt = tokens // tm
    return pl.pallas_call(
        _route_body,
        grid=(nt,),
        in_specs=[pl.BlockSpec((tm, LANES), lambda i: (i, 0))],
        out_specs=[pl.BlockSpec((tm, LANES), lambda i: (i, 0)),
                   pl.BlockSpec((8, tm), lambda i: (i, 0)),
                   pl.BlockSpec((8, LANES), lambda i: (i, 0))],
        out_shape=[jax.ShapeDtypeStruct((tokens, LANES), f32),
                   jax.ShapeDtypeStruct((nt * 8, tm), f32),
                   jax.ShapeDtypeStruct((nt * 8, LANES), jnp.int32)],
        compiler_params=pltpu.CompilerParams(
            dimension_semantics=("parallel",), vmem_limit_bytes=VMEM_LIMIT),
        name="route",
    )(logits)


def _lmax(tm):
    full = tm * TOP_K + N_EXPERTS * BF16_ROWS
    return -(-full // 256) * 256


def _segment_copy(src, dst, sem, src_off, dst_off, rows):
    src_off = pl.multiple_of(src_off, BF16_ROWS)
    dst_off = pl.multiple_of(dst_off, BF16_ROWS)
    rows = pl.multiple_of(rows, BF16_ROWS)
    return pltpu.make_async_copy(src.at[pl.ds(src_off, rows)], dst.at[pl.ds(dst_off, rows)], sem)


def _zero_rows(zero_ref, dst_ref, sem, first, start):
    tile = zero_ref.shape[0]
    n = dst_ref.shape[0] - first
    n_full = n // tile

    def full(j):
        return _segment_copy(zero_ref, dst_ref, sem, 0, first + j * tile, tile)

    def rest():
        return _segment_copy(zero_ref, dst_ref, sem, 0, first + n_full * tile, n - n_full * tile)

    def step(j, carry):
        full(j).start() if start else full(j).wait()
        return carry

    lax.fori_loop(0, n_full, step, 0)

    @pl.when(n - n_full * tile > 0)
    def _():
        rest().start() if start else rest().wait()


def _dispatch_body(goff_ref, pc_ref, voff_ref, used_ref,
                   h2_ref, metat_ref, xs_ref, xc_ref, zero_ref, sem, zsem):
    i = pl.program_id(0)
    nt = pl.num_programs(0)
    tm = h2_ref.shape[0]
    lmax = xc_ref.shape[0]

    slots = metat_ref[0:TOP_K, :].astype(jnp.int32)
    row = lax.broadcasted_iota(jnp.int32, (lmax, tm), 0)
    pt = jnp.zeros((lmax, tm), f32)
    for kk in range(TOP_K):
        pt = jnp.where(row == slots[kk:kk + 1, :], 1.0, pt)
    xc_ref[...] = _dot(pt.astype(bf16), h2_ref[...]).astype(bf16)

    def seg(e):
        j = i * N_EXPERTS + e
        return _segment_copy(xc_ref, xs_ref, sem, voff_ref[j], goff_ref[j], pc_ref[j])

    for e in range(N_EXPERTS):
        @pl.when(pc_ref[i * N_EXPERTS + e] > 0)
        def _():
            seg(e).start()

    @pl.when(i == 0)
    def _():
        zero_ref[...] = jnp.zeros_like(zero_ref)
        _zero_rows(zero_ref, xs_ref, zsem, used_ref[0], start=True)

    @pl.when(i == nt - 1)
    def _():
        _zero_rows(zero_ref, xs_ref, zsem, used_ref[0], start=False)

    for e in range(N_EXPERTS):
        @pl.when(pc_ref[i * N_EXPERTS + e] > 0)
        def _():
            seg(e).wait()


def _dispatch(goff, pc, voff, used, h2, metat, rows_total):
    tokens = h2.shape[0]
    tm = min(MOE_TM, tokens)
    nt = tokens // tm
    return pl.pallas_call(
        _dispatch_body,
        grid_spec=pltpu.PrefetchScalarGridSpec(
            num_scalar_prefetch=4,
            grid=(nt,),
            in_specs=[pl.BlockSpec((tm, D_MODEL), lambda i, *_: (i, 0)),
                      pl.BlockSpec((8, tm), lambda i, *_: (i, 0))],
            out_specs=pl.BlockSpec(memory_space=pl.ANY),
            scratch_shapes=[pltpu.VMEM((_lmax(tm), D_MODEL), bf16),
                            pltpu.VMEM((FFN_TM, D_MODEL), bf16),
                            pltpu.SemaphoreType.DMA, pltpu.SemaphoreType.DMA],
        ),
        out_shape=jax.ShapeDtypeStruct((rows_total, D_MODEL), bf16),
        compiler_params=pltpu.CompilerParams(
            dimension_semantics=("arbitrary",), vmem_limit_bytes=VMEM_LIMIT),
        name="dispatch",
    )(goff, pc, voff, used, h2, metat)


def _ffn_body(gstart_ref, gsize_ref, used_ref, xs_ref, wgu_ref, bgu_ref, wd_ref, bd_ref, os_ref,
              xbuf_ref, obuf_ref, zero_ref, state_ref, isem, osem, zsem):
    e = pl.program_id(0)
    ne = pl.num_programs(0)
    tile = xbuf_ref.shape[1]
    base = gstart_ref[e]
    n = gsize_ref[e]
    n_tiles = (n + tile - 1) // tile

    def rows_of(size, t):
        return jnp.minimum(tile, size - t * tile)

    def in_copy(start, size, t, slot):
        rows = pl.multiple_of(rows_of(size, t), BF16_ROWS)
        src = pl.multiple_of(start + t * tile, BF16_ROWS)
        return pltpu.make_async_copy(xs_ref.at[pl.ds(src, rows)],
                                     xbuf_ref.at[slot, pl.ds(0, rows)], isem.at[slot])

    def out_copy(t, slot):
        rows = pl.multiple_of(rows_of(n, t), BF16_ROWS)
        dst = pl.multiple_of(base + t * tile, BF16_ROWS)
        return pltpu.make_async_copy(obuf_ref.at[slot, pl.ds(0, rows)],
                                     os_ref.at[pl.ds(dst, rows)], osem.at[slot])

    @pl.when(e == 0)
    def _():
        xbuf_ref[...] = jnp.zeros_like(xbuf_ref)
        zero_ref[...] = jnp.zeros_like(zero_ref)
        _zero_rows(zero_ref, os_ref, zsem, used_ref[0], start=True)
        state_ref[0] = 0
        state_ref[1] = 0

    slot0 = state_ref[0]

    @pl.when(jnp.logical_and(n_tiles > 0, state_ref[1] == 0))
    def _():
        in_copy(base, n, 0, slot0).start(priority=_TILE_DMA_PRIORITY)

    has_next = jnp.logical_and(e + 1 < ne, gsize_ref[jnp.minimum(e + 1, ne - 1)] > 0)

    def step(t, carry):
        slot = (slot0 + t) & 1

        @pl.when(t + 1 < n_tiles)
        def _():
            in_copy(base, n, t + 1, 1 - slot).start(priority=_TILE_DMA_PRIORITY)

        @pl.when(jnp.logical_and(t + 1 == n_tiles, has_next))
        def _():
            nxt = jnp.minimum(e + 1, ne - 1)
            in_copy(gstart_ref[nxt], gsize_ref[nxt], 0, 1 - slot).start(priority=_TILE_DMA_PRIORITY)

        in_copy(base, n, t, slot).wait()

        @pl.when(t >= 2)
        def _():
            out_copy(t - 2, slot).wait()

        gu = _dot(xbuf_ref[slot], wgu_ref[...].astype(bf16)) + bgu_ref[...]
        gate = jnp.minimum(gu[:, :D_FF], SWIGLU_LIMIT)
        lin = jnp.clip(gu[:, D_FF:], -SWIGLU_LIMIT, SWIGLU_LIMIT)
        act = (lin + 1.0) * (gate * jax.nn.sigmoid(SWIGLU_ALPHA * gate))
        out = _dot(act.astype(bf16), wd_ref[...].astype(bf16)) + bd_ref[...]
        obuf_ref[slot] = out.astype(bf16)
        out_copy(t, slot).start(priority=_TILE_DMA_PRIORITY)
        return carry

    lax.fori_loop(0, n_tiles, step, 0)

    for back in (2, 1):
        @pl.when(n_tiles >= back)
        def _():
            t = n_tiles - back
            out_copy(t, (slot0 + t) & 1).wait()

    @pl.when(n_tiles > 0)
    def _():
        state_ref[0] = (slot0 + n_tiles) & 1
        state_ref[1] = has_next.astype(jnp.int32)

    @pl.when(e == ne - 1)
    def _():
        _zero_rows(zero_ref, os_ref, zsem, used_ref[0], start=False)


def _ffn(gstart, gsize, used, xs, wgu, bgu, wd, bd):
    rows_total = xs.shape[0]
    return pl.pallas_call(
        _ffn_body,
        grid_spec=pltpu.PrefetchScalarGridSpec(
            num_scalar_prefetch=3,
            grid=(N_EXPERTS,),
            in_specs=[pl.BlockSpec(memory_space=pl.ANY),
                      pl.BlockSpec((None, D_MODEL, 2 * D_FF), lambda e, *_: (e, 0, 0)),
                      pl.BlockSpec((None, 1, 2 * D_FF), lambda e, *_: (e, 0, 0)),
                      pl.BlockSpec((None, D_FF, D_MODEL), lambda e, *_: (e, 0, 0)),
                      pl.BlockSpec((None, 1, D_MODEL), lambda e, *_: (e, 0, 0))],
            out_specs=pl.BlockSpec(memory_space=pl.ANY),
            scratch_shapes=[pltpu.VMEM((2, FFN_TM, D_MODEL), bf16),
                            pltpu.VMEM((2, FFN_TM, D_MODEL), bf16),
                            pltpu.VMEM((FFN_TM, D_MODEL), bf16),
                            pltpu.SMEM((2,), jnp.int32),
                            pltpu.SemaphoreType.DMA((2,)), pltpu.SemaphoreType.DMA((2,)),
                            pltpu.SemaphoreType.DMA],
        ),
        out_shape=jax.ShapeDtypeStruct((rows_total, D_MODEL), bf16),
        compiler_params=pltpu.CompilerParams(
            dimension_semantics=("arbitrary",), vmem_limit_bytes=VMEM_LIMIT),
        name="experts",
    )(gstart, gsize, used, xs, wgu, bgu, wd, bd)


def _combine_body(final_norm, goff_ref, pc_ref, voff_ref, os_ref, meta_ref, x1_ref, gfin_ref, out_ref,
                  oc_ref, sem):
    i = pl.program_id(0)
    tm = x1_ref.shape[0]
    lmax = oc_ref.shape[0]

    @pl.when(i == 0)
    def _():
        oc_ref[...] = jnp.zeros_like(oc_ref)

    def seg(e):
        j = i * N_EXPERTS + e
        return _segment_copy(os_ref, oc_ref, sem, goff_ref[j], voff_ref[j], pc_ref[j])

    for e in range(N_EXPERTS):
        @pl.when(pc_ref[i * N_EXPERTS + e] > 0)
        def _():
            seg(e).start()

    meta = meta_ref[...]
    col = lax.broadcasted_iota(jnp.int32, (tm, lmax), 1)
    p = jnp.zeros((tm, lmax), f32)
    for kk in range(TOP_K):
        slot = meta[:, kk:kk + 1].astype(jnp.int32)
        p = jnp.where(col == slot, meta[:, TOP_K + kk:TOP_K + kk + 1], p)
    pb = p.astype(bf16)

    for e in range(N_EXPERTS):
        @pl.when(pc_ref[i * N_EXPERTS + e] > 0)
        def _():
            seg(e).wait()

    xo = x1_ref[...] + _dot(pb, oc_ref[...])
    out_ref[...] = _rms(xo, gfin_ref[...]) if final_norm else xo


def _combine(goff, pc, voff, os, meta, x1, gfin, final_norm):
    tokens = x1.shape[0]
    tm = min(MOE_TM, tokens)
    nt = tokens // tm
    return pl.pallas_call(
        functools.partial(_combine_body, final_norm),
        grid_spec=pltpu.PrefetchScalarGridSpec(
            num_scalar_prefetch=3,
            grid=(nt,),
            in_specs=[pl.BlockSpec(memory_space=pl.ANY),
                      pl.BlockSpec((tm, LANES), lambda i, *_: (i, 0)),
                      pl.BlockSpec((tm, D_MODEL), lambda i, *_: (i, 0)),
                      pl.BlockSpec((1, D_MODEL), lambda i, *_: (0, 0))],
            out_specs=pl.BlockSpec((tm, D_MODEL), lambda i, *_: (i, 0)),
            scratch_shapes=[pltpu.VMEM((_lmax(tm), D_MODEL), bf16), pltpu.SemaphoreType.DMA],
        ),
        out_shape=jax.ShapeDtypeStruct((tokens, D_MODEL), f32),
        compiler_params=pltpu.CompilerParams(
            dimension_semantics=("arbitrary",), vmem_limit_bytes=VMEM_LIMIT),
        name="combine",
    )(goff, pc, voff, os, meta, x1, gfin)


def _plan(counts, nt):
    c = counts.reshape(nt, 8, LANES)[:, 0, :N_EXPERTS]
    pc = (c + (BF16_ROWS - 1)) // BF16_ROWS * BF16_ROWS
    voff = jnp.cumsum(pc, axis=1) - pc
    gsize = jnp.sum(pc, axis=0)
    gend = jnp.cumsum(gsize)
    gstart = gend - gsize
    goff = gstart[None, :] + jnp.cumsum(pc, axis=0) - pc
    i32 = lambda a: a.reshape(-1).astype(jnp.int32)
    return i32(pc), i32(voff), i32(goff), i32(gstart), i32(gsize), i32(gend[-1:])


def kernel(x, g_mix, w_in, w_alpha_up, b_alpha, g_gla, w_conv, w_branch_a, w_branch_b, w_out, g_ffn, w_router, b_router, w_gate_up, b_gate_up, w_down, b_down, g_final):
    batch, seq, _ = x.shape
    tokens = batch * seq
    depth = w_in.shape[0]
    x2 = x.reshape(tokens, D_MODEL)
    tm = min(MOE_TM, tokens)
    nt = tokens // tm
    rows_total = tokens * TOP_K + nt * N_EXPERTS * (BF16_ROWS - 1)
    rows_total = -(-rows_total // BF16_ROWS) * BF16_ROWS

    for l in range(depth):
        wi = w_in[l]
        a0 = 2 * QK_W + 2 * V_W
        wmain = jnp.concatenate([wi[:, :a0], wi[:, a0 + GLA_RANK:]], axis=1).astype(bf16)
        walr = jnp.pad(wi[:, a0:a0 + GLA_RANK], ((0, 0), (0, LANES - GLA_RANK))).astype(bf16)
        wup = jnp.pad(w_alpha_up[l], ((0, LANES - GLA_RANK), (0, 0))).astype(bf16)
        wr = jnp.pad(w_router[l], ((0, 0), (0, LANES - N_EXPERTS)))
        wrh = wr.astype(bf16)
        wrl = (wr - wrh.astype(f32)).astype(bf16)
        br = jnp.pad(b_router[l], (0, LANES - N_EXPERTS)).reshape(1, LANES)
        wconv = jnp.pad(w_conv[l], ((0, 8 - CONV_K), (0, 0)))

        x1, h2, logits = _mixer(
            x2, batch, seq, g_mix[l].reshape(1, D_MODEL), wmain, walr, wup,
            b_alpha[l].reshape(1, QK_W), g_gla[l].reshape(1, V_W), wconv,
            w_branch_a[l].astype(bf16), w_branch_b[l].astype(bf16), w_out[l].astype(bf16),
            g_ffn[l].reshape(1, D_MODEL), wrh, wrl, br)

        meta, metat, counts = _route(logits)
        pc, voff, goff, gstart, gsize, used = _plan(counts, nt)
        xs = _dispatch(goff, pc, voff, used, h2, metat, rows_total)
        os = _ffn(gstart, gsize, used, xs,
                  w_gate_up[l], b_gate_up[l].reshape(N_EXPERTS, 1, 2 * D_FF),
                  w_down[l], b_down[l].reshape(N_EXPERTS, 1, D_MODEL))
        x2 = _combine(goff, pc, voff, os, meta, x1, g_final.reshape(1, D_MODEL), l == depth - 1)
    return x2.reshape(batch, seq, D_MODEL)
```

```python
import functools

import jax
import jax.numpy as jnp
from jax import lax
from jax.experimental import pallas as pl
from jax.experimental.pallas import tpu as pltpu

D_MODEL = 1024
GLA_HEADS = 4
GLA_DK = 128
GLA_DV = 256
GLA_RANK = 16
GLA_TAU = 16.0
GLA_CHUNK = 64
_CHUNK_SHIFT = GLA_CHUNK.bit_length() - 1
CONV_K = 3
N_EXPERTS = 32
TOP_K = 4
D_FF = 1024
SWIGLU_LIMIT = 7.0
SWIGLU_ALPHA = 1.702
EPS = 1e-6

QK_W = GLA_HEADS * GLA_DK
V_W = GLA_HEADS * GLA_DV

LANES = 128
BF16_ROWS = 16
VMEM_LIMIT = 56 * 1024 * 1024

MIX_TS = 256
MOE_TM = 512
FFN_TM = 256
_TILE_DMA_PRIORITY = 1

_C_Q, _C_K, _C_V, _C_R = 0, QK_W, 2 * QK_W, 2 * QK_W + V_W
_C_CB = _C_R + V_W
_C_CC = _C_CB + D_MODEL
_C_CX = _C_CC + D_MODEL
_C_GA = _C_CX + D_MODEL
_C_GB = _C_GA + D_MODEL
_W_MAIN = _C_GB + D_MODEL

f32 = jnp.float32
bf16 = jnp.bfloat16


def _rms(x, g):
    return x * lax.rsqrt(jnp.mean(x * x, axis=-1, keepdims=True) + EPS) * g


def _dot(a, b):
    return jnp.dot(a, b, preferred_element_type=f32)


def _split_bf16(a):
    hi = a.astype(bf16)
    lo = (a - hi.astype(f32)).astype(bf16)
    return hi, lo


def _mixer_body(x_ref, gmix_ref, wmain_ref, walr_ref, wup_ref, balpha_ref, ggla_ref, wconv_ref,
                wa_ref, wb_ref, wo_ref, gffn_ref, wrh_ref, wrl_ref, br_ref,
                x1_ref, h2_ref, logit_ref,
                state_ref, ubuf_ref, obuf_ref):
    ts = x_ref.shape[0]
    nchunk = ts // GLA_CHUNK

    @pl.when(pl.program_id(1) == 0)
    def _():
        state_ref[...] = jnp.zeros_like(state_ref)
        ubuf_ref[0:8, :] = jnp.zeros((8, D_MODEL), f32)

    x = x_ref[...]
    hb = _rms(x, gmix_ref[...]).astype(bf16)

    def proj(c0, width):
        return _dot(hb, wmain_ref[:, c0:c0 + width])

    a_lr = _dot(hb, walr_ref[...])
    pre = _dot(a_lr.astype(bf16), wup_ref[...]) + balpha_ref[...]
    log_a = -(jnp.maximum(-pre, 0.0) + jnp.log1p(jnp.exp(-jnp.abs(pre)))) * (1.0 / GLA_TAU)
    ri = lax.broadcasted_iota(jnp.int32, (ts, ts), 0)
    ci = lax.broadcasted_iota(jnp.int32, (ts, ts), 1)
    same_chunk = (ri >> _CHUNK_SHIFT) == (ci >> _CHUNK_SHIFT)
    causal = jnp.logical_and(same_chunk, ci <= ri)
    tri = jnp.where(causal, 1.0, 0.0).astype(bf16)
    blk = jnp.where(same_chunk, 1.0, 0.0).astype(bf16)
    la_hi, la_lo = _split_bf16(log_a)
    b = _dot(tri, la_hi) + _dot(tri, la_lo)
    b_tot = _dot(blk, la_hi) + _dot(blk, la_lo)
    e_pos = jnp.exp(b)
    e_neg = jnp.exp(-b)
    e_tail = jnp.exp(b_tot - b)
    e_tot = jnp.exp(b_tot)

    q = proj(_C_Q, QK_W)
    k = proj(_C_K, QK_W)
    q_dec = (q * (GLA_DK ** -0.5) * e_pos).astype(bf16)
    k_inv = (k * e_neg).astype(bf16)
    k_tail = (k * e_tail).astype(bf16)
    vb = proj(_C_V, V_W).astype(bf16)

    for hh in range(GLA_HEADS):
        qs = slice(hh * GLA_DK, (hh + 1) * GLA_DK)
        vs = slice(hh * GLA_DV, (hh + 1) * GLA_DV)
        sc = lax.dot_general(q_dec[:, qs], k_inv[:, qs], (((1,), (1,)), ((), ())),
                             preferred_element_type=f32)
        sc = jnp.where(causal, sc, 0.0).astype(bf16)
        obuf_ref[:, vs] = _dot(sc, vb[:, vs])
        st = state_ref[hh]
        for c in range(nchunk):
            rs = slice(c * GLA_CHUNK, (c + 1) * GLA_CHUNK)
            o_inter = lax.dot_general(q_dec[rs, qs], st.astype(bf16), (((1,), (1,)), ((), ())),
                                      preferred_element_type=f32)
            obuf_ref[rs, vs] += o_inter
            upd = lax.dot_general(vb[rs, vs], k_tail[rs, qs], (((0,), (0,)), ((), ())),
                                  preferred_element_type=f32)
            st = st * e_tot[c * GLA_CHUNK:c * GLA_CHUNK + 1, qs] + upd
        state_ref[hh] = st

    r = proj(_C_R, V_W)
    ggla = ggla_ref[...]
    o_parts = []
    for hh in range(GLA_HEADS):
        vs = slice(hh * GLA_DV, (hh + 1) * GLA_DV)
        o_h = _rms(obuf_ref[:, vs], ggla[:, vs])
        r_h = r[:, vs]
        o_parts.append((o_h * (r_h * jax.nn.sigmoid(r_h))).astype(bf16))
    y_a = _dot(jnp.concatenate(o_parts, axis=1), wa_ref[...])

    u = proj(_C_CC, D_MODEL) * proj(_C_CX, D_MODEL)
    ubuf_ref[8:8 + ts, :] = u
    wc = wconv_ref[...]
    y_conv = (wc[0:1, :] * ubuf_ref[6:6 + ts, :] + wc[1:2, :] * ubuf_ref[7:7 + ts, :]
              + wc[2:3, :] * u)
    ubuf_ref[0:8, :] = ubuf_ref[ts:ts + 8, :]
    y_b = _dot((proj(_C_CB, D_MODEL) * y_conv).astype(bf16), wb_ref[...])

    mixed = (jax.nn.sigmoid(proj(_C_GA, D_MODEL)) * y_a
             + jax.nn.sigmoid(proj(_C_GB, D_MODEL)) * y_b)
    x1 = x + _dot(mixed.astype(bf16), wo_ref[...])
    x1_ref[...] = x1

    h2 = _rms(x1, gffn_ref[...])
    h2_hi, h2_lo = _split_bf16(h2)
    h2_ref[...] = h2_hi
    wrh = wrh_ref[...]
    logit_ref[...] = (_dot(h2_hi, wrh) + _dot(h2_lo, wrh) + _dot(h2_hi, wrl_ref[...])
                      + br_ref[...])


def _mixer(x2, batch, seq, gmix, wmain, walr, wup, balpha, ggla, wconv, wa, wb, wo, gffn, wrh, wrl, br):
    ts = min(MIX_TS, seq)
    ns = seq // ts
    tokens = batch * seq

    def const(shape):
        return pl.BlockSpec(shape, lambda b, s: (0,) * len(shape), pipeline_mode=pl.Buffered(1))

    def row(width, dtype):
        return pl.BlockSpec((ts, width), lambda b, s: (b * ns + s, 0))

    return pl.pallas_call(
        _mixer_body,
        grid=(batch, ns),
        in_specs=[row(D_MODEL, f32), const(gmix.shape), const(wmain.shape), const(walr.shape),
                  const(wup.shape), const(balpha.shape), const(ggla.shape), const(wconv.shape),
                  const(wa.shape), const(wb.shape), const(wo.shape), const(gffn.shape),
                  const(wrh.shape), const(wrl.shape), const(br.shape)],
        out_specs=[row(D_MODEL, f32), row(D_MODEL, bf16), row(LANES, f32)],
        out_shape=[jax.ShapeDtypeStruct((tokens, D_MODEL), f32),
                   jax.ShapeDtypeStruct((tokens, D_MODEL), bf16),
                   jax.ShapeDtypeStruct((tokens, LANES), f32)],
        scratch_shapes=[pltpu.VMEM((GLA_HEADS, GLA_DV, GLA_DK), f32),
                        pltpu.VMEM((ts + 8, D_MODEL), f32),
                        pltpu.VMEM((ts, V_W), f32)],
        compiler_params=pltpu.CompilerParams(
            dimension_semantics=("parallel", "arbitrary"), vmem_limit_bytes=VMEM_LIMIT),
        name="mixer",
    )(x2, gmix, wmain, walr, wup, balpha, ggla, wconv, wa, wb, wo, gffn, wrh, wrl, br)


def _route_body(logit_ref, meta_ref, metat_ref, cnt_ref):
    tm = logit_ref.shape[0]
    lane = lax.broadcasted_iota(jnp.int32, (tm, LANES), 1)
    lane_f = lane.astype(f32)
    lg = jnp.where(lane < N_EXPERTS, logit_ref[...], -jnp.inf)
    sels, tops = [], []
    for _ in range(TOP_K):
        m = jnp.max(lg, axis=-1, keepdims=True)
        first = jnp.min(jnp.where(lg == m, lane_f, float(LANES)), axis=-1, keepdims=True)
        sel = lane_f == first
        sels.append(sel)
        tops.append(m)
        lg = jnp.where(sel, -jnp.inf, lg)
    ps = [jnp.exp(t - tops[0]) for t in tops]
    denom = ps[0] + ps[1] + ps[2] + ps[3]
    gates = [p / denom for p in ps]

    onehot = jnp.zeros((tm, LANES), f32)
    for sel in sels:
        onehot = onehot + jnp.where(sel, 1.0, 0.0)
    onehot_b = onehot.astype(bf16)
    ri = lax.broadcasted_iota(jnp.int32, (tm, tm), 0)
    ci = lax.broadcasted_iota(jnp.int32, (tm, tm), 1)
    strict_lower = jnp.where(ci < ri, 1.0, 0.0).astype(bf16)
    rank = _dot(strict_lower, onehot_b)
    counts = _dot(jnp.ones((8, tm), bf16), onehot_b)
    blocks = jnp.floor((counts + (BF16_ROWS - 1)) * (1.0 / BF16_ROWS))
    ui = lax.broadcasted_iota(jnp.int32, (LANES, LANES), 0)
    uj = lax.broadcasted_iota(jnp.int32, (LANES, LANES), 1)
    strict_upper = jnp.where(ui < uj, 1.0, 0.0).astype(bf16)
    voff = _dot(blocks.astype(bf16), strict_upper) * float(BF16_ROWS)
    slot = rank + voff[0:1, :]

    meta = jnp.zeros((tm, LANES), f32)
    for kk in range(TOP_K):
        lpos = jnp.sum(jnp.where(sels[kk], slot, 0.0), axis=-1, keepdims=True)
        meta = jnp.where(lane == kk, lpos, meta)
        meta = jnp.where(lane == TOP_K + kk, gates[kk], meta)
    meta_ref[...] = meta
    metat_ref[...] = meta.T[0:8, :]
    cnt_ref[...] = counts.astype(jnp.int32)


def _route(logits):
    tokens = logits.shape[0]
    tm = min(MOE_TM, tokens)
    nt = tokens // tm
    return pl.pallas_call(
        _route_body,
        grid=(nt,),
        in_specs=[pl.BlockSpec((tm, LANES), lambda i: (i, 0))],
        out_specs=[pl.BlockSpec((tm, LANES), lambda i: (i, 0)),
                   pl.BlockSpec((8, tm), lambda i: (i, 0)),
                   pl.BlockSpec((8, LANES), lambda i: (i, 0))],
        out_shape=[jax.ShapeDtypeStruct((tokens, LANES), f32),
                   jax.ShapeDtypeStruct((nt * 8, tm), f32),
                   jax.ShapeDtypeStruct((nt * 8, LANES), jnp.int32)],
        compiler_params=pltpu.CompilerParams(
            dimension_semantics=("parallel",), vmem_limit_bytes=VMEM_LIMIT),
        name="route",
    )(logits)


def _lmax(tm):
    full = tm * TOP_K + N_EXPERTS * BF16_ROWS
    return -(-full // 256) * 256


def _segment_copy(src, dst, sem, src_off, dst_off, rows):
    src_off = pl.multiple_of(src_off, BF16_ROWS)
    dst_off = pl.multiple_of(dst_off, BF16_ROWS)
    rows = pl.multiple_of(rows, BF16_ROWS)
    return pltpu.make_async_copy(src.at[pl.ds(src_off, rows)], dst.at[pl.ds(dst_off, rows)], sem)


def _zero_rows(zero_ref, dst_ref, sem, first, start):
    tile = zero_ref.shape[0]
    n = dst_ref.shape[0] - first
    n_full = n // tile

    def full(j):
        return _segment_copy(zero_ref, dst_ref, sem, 0, first + j * tile, tile)

    def rest():
        return _segment_copy(zero_ref, dst_ref, sem, 0, first + n_full * tile, n - n_full * tile)

    def step(j, carry):
        full(j).start() if start else full(j).wait()
        return carry

    lax.fori_loop(0, n_full, step, 0)

    @pl.when(n - n_full * tile > 0)
    def _():
        rest().start() if start else rest().wait()


def _dispatch_body(goff_ref, pc_ref, voff_ref, used_ref,
                   h2_ref, metat_ref, xs_ref, xc_ref, zero_ref, sem, zsem):
    i = pl.program_id(0)
    nt = pl.num_programs(0)
    tm = h2_ref.shape[0]
    lmax = xc_ref.shape[0]

    slots = metat_ref[0:TOP_K, :].astype(jnp.int32)
    row = lax.broadcasted_iota(jnp.int32, (lmax, tm), 0)
    pt = jnp.zeros((lmax, tm), f32)
    for kk in range(TOP_K):
        pt = jnp.where(row == slots[kk:kk + 1, :], 1.0, pt)
    xc_ref[...] = _dot(pt.astype(bf16), h2_ref[...]).astype(bf16)

    def seg(e):
        j = i * N_EXPERTS + e
        return _segment_copy(xc_ref, xs_ref, sem, voff_ref[j], goff_ref[j], pc_ref[j])

    for e in range(N_EXPERTS):
        @pl.when(pc_ref[i * N_EXPERTS + e] > 0)
        def _():
            seg(e).start()

    @pl.when(i == 0)
    def _():
        zero_ref[...] = jnp.zeros_like(zero_ref)
        _zero_rows(zero_ref, xs_ref, zsem, used_ref[0], start=True)

    @pl.when(i == nt - 1)
    def _():
        _zero_rows(zero_ref, xs_ref, zsem, used_ref[0], start=False)

    for e in range(N_EXPERTS):
        @pl.when(pc_ref[i * N_EXPERTS + e] > 0)
        def _():
            seg(e).wait()


def _dispatch(goff, pc, voff, used, h2, metat, rows_total):
    tokens = h2.shape[0]
    tm = min(MOE_TM, tokens)
    nt = tokens // tm
    return pl.pallas_call(
        _dispatch_body,
        grid_spec=pltpu.PrefetchScalarGridSpec(
            num_scalar_prefetch=4,
            grid=(nt,),
            in_specs=[pl.BlockSpec((tm, D_MODEL), lambda i, *_: (i, 0)),
                      pl.BlockSpec((8, tm), lambda i, *_: (i, 0))],
            out_specs=pl.BlockSpec(memory_space=pl.ANY),
            scratch_shapes=[pltpu.VMEM((_lmax(tm), D_MODEL), bf16),
                            pltpu.VMEM((FFN_TM, D_MODEL), bf16),
                            pltpu.SemaphoreType.DMA, pltpu.SemaphoreType.DMA],
        ),
        out_shape=jax.ShapeDtypeStruct((rows_total, D_MODEL), bf16),
        compiler_params=pltpu.CompilerParams(
            dimension_semantics=("arbitrary",), vmem_limit_bytes=VMEM_LIMIT),
        name="dispatch",
    )(goff, pc, voff, used, h2, metat)


def _ffn_body(gstart_ref, gsize_ref, used_ref, xs_ref, wgu_ref, bgu_ref, wd_ref, bd_ref, os_ref,
              xbuf_ref, obuf_ref, zero_ref, state_ref, isem, osem, zsem):
    e = pl.program_id(0)
    ne = pl.num_programs(0)
    tile = xbuf_ref.shape[1]
    base = gstart_ref[e]
    n = gsize_ref[e]
    n_tiles = (n + tile - 1) // tile

    def rows_of(size, t):
        return jnp.minimum(tile, size - t * tile)

    def in_copy(row0, rows, slot):
        return pltpu.make_async_copy(xs_ref.at[pl.ds(row0, rows)],
                                     xbuf_ref.at[slot, pl.ds(0, rows)], isem.at[slot])

    def out_copy(row0, rows, slot):
        return pltpu.make_async_copy(obuf_ref.at[slot, pl.ds(0, rows)],
                                     os_ref.at[pl.ds(row0, rows)], osem.at[slot])

    def start_tile(copy, row0, rows, slot):
        row0 = pl.multiple_of(row0, BF16_ROWS)

        @pl.when(rows == tile)
        def _():
            copy(row0, tile, slot).start(priority=_TILE_DMA_PRIORITY)

        @pl.when(rows < tile)
        def _():
            copy(row0, pl.multiple_of(rows, BF16_ROWS), slot).start()

    def wait_tile(copy, rows, slot):
        copy(0, pl.multiple_of(rows, BF16_ROWS), slot).wait()

    @pl.when(e == 0)
    def _():
        xbuf_ref[...] = jnp.zeros_like(xbuf_ref)
        zero_ref[...] = jnp.zeros_like(zero_ref)
        _zero_rows(zero_ref, os_ref, zsem, used_ref[0], start=True)
        state_ref[0] = 0
        state_ref[1] = 0
        state_ref[2] = 0
        state_ref[3] = 0

    done = state_ref[0]

    @pl.when(jnp.logical_and(n_tiles > 0, state_ref[1] == 0))
    def _():
        start_tile(in_copy, base, rows_of(n, 0), done & 1)

    nxt = jnp.minimum(e + 1, ne - 1)
    has_next = jnp.logical_and(e + 1 < ne, gsize_ref[nxt] > 0)

    def step(t, carry):
        slot = (done + t) & 1

        @pl.when(t + 1 < n_tiles)
        def _():
            start_tile(in_copy, base + (t + 1) * tile, rows_of(n, t + 1), 1 - slot)

        @pl.when(jnp.logical_and(t + 1 == n_tiles, has_next))
        def _():
            start_tile(in_copy, gstart_ref[nxt], rows_of(gsize_ref[nxt], 0), 1 - slot)

        rows = rows_of(n, t)
        wait_tile(in_copy, rows, slot)

        @pl.when(state_ref[2 + slot] > 0)
        def _():
            wait_tile(out_copy, state_ref[2 + slot], slot)

        gu = _dot(xbuf_ref[slot], wgu_ref[...].astype(bf16)) + bgu_ref[...]
        gate = jnp.minimum(gu[:, :D_FF], SWIGLU_LIMIT)
        lin = jnp.clip(gu[:, D_FF:], -SWIGLU_LIMIT, SWIGLU_LIMIT)
        act = (lin + 1.0) * (gate * jax.nn.sigmoid(SWIGLU_ALPHA * gate))
        out = _dot(act.astype(bf16), wd_ref[...].astype(bf16)) + bd_ref[...]
        obuf_ref[slot] = out.astype(bf16)
        start_tile(out_copy, base + t * tile, rows, slot)
        state_ref[2 + slot] = rows
        return carry

    lax.fori_loop(0, n_tiles, step, 0)

    @pl.when(n_tiles > 0)
    def _():
        state_ref[0] = done + n_tiles
        state_ref[1] = has_next.astype(jnp.int32)

    @pl.when(e == ne - 1)
    def _():
        for slot in range(2):
            @pl.when(state_ref[2 + slot] > 0)
            def _():
                wait_tile(out_copy, state_ref[2 + slot], slot)
        _zero_rows(zero_ref, os_ref, zsem, used_ref[0], start=False)


def _ffn(gstart, gsize, used, xs, wgu, bgu, wd, bd):
    rows_total = xs.shape[0]
    return pl.pallas_call(
        _ffn_body,
        grid_spec=pltpu.PrefetchScalarGridSpec(
            num_scalar_prefetch=3,
            grid=(N_EXPERTS,),
            in_specs=[pl.BlockSpec(memory_space=pl.ANY),
                      pl.BlockSpec((None, D_MODEL, 2 * D_FF), lambda e, *_: (e, 0, 0)),
                      pl.BlockSpec((None, 1, 2 * D_FF), lambda e, *_: (e, 0, 0)),
                      pl.BlockSpec((None, D_FF, D_MODEL), lambda e, *_: (e, 0, 0)),
                      pl.BlockSpec((None, 1, D_MODEL), lambda e, *_: (e, 0, 0))],
            out_specs=pl.BlockSpec(memory_space=pl.ANY),
            scratch_shapes=[pltpu.VMEM((2, FFN_TM, D_MODEL), bf16),
                            pltpu.VMEM((2, FFN_TM, D_MODEL), bf16),
                            pltpu.VMEM((FFN_TM, D_MODEL), bf16),
                            pltpu.SMEM((4,), jnp.int32),
                            pltpu.SemaphoreType.DMA((2,)), pltpu.SemaphoreType.DMA((2,)),
                            pltpu.SemaphoreType.DMA],
        ),
        out_shape=jax.ShapeDtypeStruct((rows_total, D_MODEL), bf16),
        compiler_params=pltpu.CompilerParams(
            dimension_semantics=("arbitrary",), vmem_limit_bytes=VMEM_LIMIT),
        name="experts",
    )(gstart, gsize, used, xs, wgu, bgu, wd, bd)


def _combine_body(final_norm, goff_ref, pc_ref, voff_ref, os_ref, meta_ref, x1_ref, gfin_ref, out_ref,
                  oc_ref, sem):
    i = pl.program_id(0)
    tm = x1_ref.shape[0]
    lmax = oc_ref.shape[0]

    @pl.when(i == 0)
    def _():
        oc_ref[...] = jnp.zeros_like(oc_ref)

    def seg(e):
        j = i * N_EXPERTS + e
        return _segment_copy(os_ref, oc_ref, sem, goff_ref[j], voff_ref[j], pc_ref[j])

    for e in range(N_EXPERTS):
        @pl.when(pc_ref[i * N_EXPERTS + e] > 0)
        def _():
            seg(e).start()

    meta = meta_ref[...]
    col = lax.broadcasted_iota(jnp.int32, (tm, lmax), 1)
    p = jnp.zeros((tm, lmax), f32)
    for kk in range(TOP_K):
        slot = meta[:, kk:kk + 1].astype(jnp.int32)
        p = jnp.where(col == slot, meta[:, TOP_K + kk:TOP_K + kk + 1], p)
    pb = p.astype(bf16)

    for e in range(N_EXPERTS):
        @pl.when(pc_ref[i * N_EXPERTS + e] > 0)
        def _():
            seg(e).wait()

    xo = x1_ref[...] + _dot(pb, oc_ref[...])
    out_ref[...] = _rms(xo, gfin_ref[...]) if final_norm else xo


def _combine(goff, pc, voff, os, meta, x1, gfin, final_norm):
    tokens = x1.shape[0]
    tm = min(MOE_TM, tokens)
    nt = tokens // tm
    return pl.pallas_call(
        functools.partial(_combine_body, final_norm),
        grid_spec=pltpu.PrefetchScalarGridSpec(
            num_scalar_prefetch=3,
            grid=(nt,),
            in_specs=[pl.BlockSpec(memory_space=pl.ANY),
                      pl.BlockSpec((tm, LANES), lambda i, *_: (i, 0)),
                      pl.BlockSpec((tm, D_MODEL), lambda i, *_: (i, 0)),
                      pl.BlockSpec((1, D_MODEL), lambda i, *_: (0, 0))],
            out_specs=pl.BlockSpec((tm, D_MODEL), lambda i, *_: (i, 0)),
            scratch_shapes=[pltpu.VMEM((_lmax(tm), D_MODEL), bf16), pltpu.SemaphoreType.DMA],
        ),
        out_shape=jax.ShapeDtypeStruct((tokens, D_MODEL), f32),
        compiler_params=pltpu.CompilerParams(
            dimension_semantics=("arbitrary",), vmem_limit_bytes=VMEM_LIMIT),
        name="combine",
    )(goff, pc, voff, os, meta, x1, gfin)


def _plan(counts, nt):
    c = counts.reshape(nt, 8, LANES)[:, 0, :N_EXPERTS]
    pc = (c + (BF16_ROWS - 1)) // BF16_ROWS * BF16_ROWS
    voff = jnp.cumsum(pc, axis=1) - pc
    gsize = jnp.sum(pc, axis=0)
    gend = jnp.cumsum(gsize)
    gstart = gend - gsize
    goff = gstart[None, :] + jnp.cumsum(pc, axis=0) - pc
    i32 = lambda a: a.reshape(-1).astype(jnp.int32)
    return i32(pc), i32(voff), i32(goff), i32(gstart), i32(gsize), i32(gend[-1:])


def kernel(x, g_mix, w_in, w_alpha_up, b_alpha, g_gla, w_conv, w_branch_a, w_branch_b, w_out, g_ffn, w_router, b_router, w_gate_up, b_gate_up, w_down, b_down, g_final):
    batch, seq, _ = x.shape
    tokens = batch * seq
    depth = w_in.shape[0]
    x2 = x.reshape(tokens, D_MODEL)
    tm = min(MOE_TM, tokens)
    nt = tokens // tm
    rows_total = tokens * TOP_K + nt * N_EXPERTS * (BF16_ROWS - 1)
    rows_total = -(-rows_total // BF16_ROWS) * BF16_ROWS

    for l in range(depth):
        wi = w_in[l]
        a0 = 2 * QK_W + 2 * V_W
        wmain = jnp.concatenate([wi[:, :a0], wi[:, a0 + GLA_RANK:]], axis=1).astype(bf16)
        walr = jnp.pad(wi[:, a0:a0 + GLA_RANK], ((0, 0), (0, LANES - GLA_RANK))).astype(bf16)
        wup = jnp.pad(w_alpha_up[l], ((0, LANES - GLA_RANK), (0, 0))).astype(bf16)
        wr = jnp.pad(w_router[l], ((0, 0), (0, LANES - N_EXPERTS)))
        wrh = wr.astype(bf16)
        wrl = (wr - wrh.astype(f32)).astype(bf16)
        br = jnp.pad(b_router[l], (0, LANES - N_EXPERTS)).reshape(1, LANES)
        wconv = jnp.pad(w_conv[l], ((0, 8 - CONV_K), (0, 0)))

        x1, h2, logits = _mixer(
            x2, batch, seq, g_mix[l].reshape(1, D_MODEL), wmain, walr, wup,
            b_alpha[l].reshape(1, QK_W), g_gla[l].reshape(1, V_W), wconv,
            w_branch_a[l].astype(bf16), w_branch_b[l].astype(bf16), w_out[l].astype(bf16),
            g_ffn[l].reshape(1, D_MODEL), wrh, wrl, br)

        meta, metat, counts = _route(logits)
        pc, voff, goff, gstart, gsize, used = _plan(counts, nt)
        xs = _dispatch(goff, pc, voff, used, h2, metat, rows_total)
        os = _ffn(gstart, gsize, used, xs,
                  w_gate_up[l], b_gate_up[l].reshape(N_EXPERTS, 1, 2 * D_FF),
                  w_down[l], b_down[l].reshape(N_EXPERTS, 1, D_MODEL))
        x2 = _combine(goff, pc, voff, os, meta, x1, g_final.reshape(1, D_MODEL), l == depth - 1)
    return x2.reshape(batch, seq, D_MODEL)
```

```python
import functools

import jax
import jax.numpy as jnp
from jax import lax
from jax.experimental import pallas as pl
from jax.experimental.pallas import tpu as pltpu

D_MODEL = 1024
GLA_HEADS = 4
GLA_DK = 128
GLA_DV = 256
GLA_RANK = 16
GLA_TAU = 16.0
GLA_CHUNK = 64
_CHUNK_SHIFT = GLA_CHUNK.bit_length() - 1
CONV_K = 3
N_EXPERTS = 32
TOP_K = 4
D_FF = 1024
SWIGLU_LIMIT = 7.0
SWIGLU_ALPHA = 1.702
EPS = 1e-6

QK_W = GLA_HEADS * GLA_DK
V_W = GLA_HEADS * GLA_DV

LANES = 128
BF16_ROWS = 16
VMEM_LIMIT = 56 * 1024 * 1024

MIX_TS = 256
MOE_TM = 512
FFN_TM = 256
_TILE_DMA_PRIORITY = 1

_C_Q, _C_K, _C_V, _C_R = 0, QK_W, 2 * QK_W, 2 * QK_W + V_W
_C_CB = _C_R + V_W
_C_CC = _C_CB + D_MODEL
_C_CX = _C_CC + D_MODEL
_C_GA = _C_CX + D_MODEL
_C_GB = _C_GA + D_MODEL
_W_MAIN = _C_GB + D_MODEL

f32 = jnp.float32
bf16 = jnp.bfloat16


def _rms(x, g):
    return x * lax.rsqrt(jnp.mean(x * x, axis=-1, keepdims=True) + EPS) * g


def _dot(a, b):
    return jnp.dot(a, b, preferred_element_type=f32)


def _split_bf16(a):
    hi = a.astype(bf16)
    lo = (a - hi.astype(f32)).astype(bf16)
    return hi, lo


def _mixer_body(x_ref, gmix_ref, wmain_ref, walr_ref, wup_ref, balpha_ref, ggla_ref, wconv_ref,
                wa_ref, wb_ref, wo_ref, gffn_ref, wrh_ref, wrl_ref, br_ref,
                x1_ref, h2_ref, logit_ref,
                state_ref, ubuf_ref, obuf_ref):
    ts = x_ref.shape[0]
    nchunk = ts // GLA_CHUNK

    @pl.when(pl.program_id(1) == 0)
    def _():
        state_ref[...] = jnp.zeros_like(state_ref)
        ubuf_ref[0:8, :] = jnp.zeros((8, D_MODEL), f32)

    x = x_ref[...]
    hb = _rms(x, gmix_ref[...]).astype(bf16)

    def proj(c0, width):
        return _dot(hb, wmain_ref[:, c0:c0 + width])

    a_lr = _dot(hb, walr_ref[...])
    pre = _dot(a_lr.astype(bf16), wup_ref[...]) + balpha_ref[...]
    log_a = -(jnp.maximum(-pre, 0.0) + jnp.log1p(jnp.exp(-jnp.abs(pre)))) * (1.0 / GLA_TAU)
    ri = lax.broadcasted_iota(jnp.int32, (ts, ts), 0)
    ci = lax.broadcasted_iota(jnp.int32, (ts, ts), 1)
    same_chunk = (ri >> _CHUNK_SHIFT) == (ci >> _CHUNK_SHIFT)
    causal = jnp.logical_and(same_chunk, ci <= ri)
    tri = jnp.where(causal, 1.0, 0.0).astype(bf16)
    blk = jnp.where(same_chunk, 1.0, 0.0).astype(bf16)
    la_hi, la_lo = _split_bf16(log_a)
    b = _dot(tri, la_hi) + _dot(tri, la_lo)
    b_tot = _dot(blk, la_hi) + _dot(blk, la_lo)
    e_pos = jnp.exp(b)
    e_neg = jnp.exp(-b)
    e_tail = jnp.exp(b_tot - b)
    e_tot = jnp.exp(b_tot)

    q = proj(_C_Q, QK_W)
    k = proj(_C_K, QK_W)
    q_dec = (q * (GLA_DK ** -0.5) * e_pos).astype(bf16)
    k_inv = (k * e_neg).astype(bf16)
    k_tail = (k * e_tail).astype(bf16)
    vb = proj(_C_V, V_W).astype(bf16)

    for hh in range(GLA_HEADS):
        qs = slice(hh * GLA_DK, (hh + 1) * GLA_DK)
        vs = slice(hh * GLA_DV, (hh + 1) * GLA_DV)
        sc = lax.dot_general(q_dec[:, qs], k_inv[:, qs], (((1,), (1,)), ((), ())),
                             preferred_element_type=f32)
        sc = jnp.where(causal, sc, 0.0).astype(bf16)
        obuf_ref[:, vs] = _dot(sc, vb[:, vs])
        st = state_ref[hh]
        for c in range(nchunk):
            rs = slice(c * GLA_CHUNK, (c + 1) * GLA_CHUNK)
            o_inter = lax.dot_general(q_dec[rs, qs], st.astype(bf16), (((1,), (1,)), ((), ())),
                                      preferred_element_type=f32)
            obuf_ref[rs, vs] += o_inter
            upd = lax.dot_general(vb[rs, vs], k_tail[rs, qs], (((0,), (0,)), ((), ())),
                                  preferred_element_type=f32)
            st = st * e_tot[c * GLA_CHUNK:c * GLA_CHUNK + 1, qs] + upd
        state_ref[hh] = st

    r = proj(_C_R, V_W)
    ggla = ggla_ref[...]
    o_parts = []
    for hh in range(GLA_HEADS):
        vs = slice(hh * GLA_DV, (hh + 1) * GLA_DV)
        o_h = _rms(obuf_ref[:, vs], ggla[:, vs])
        r_h = r[:, vs]
        o_parts.append((o_h * (r_h * jax.nn.sigmoid(r_h))).astype(bf16))
    y_a = _dot(jnp.concatenate(o_parts, axis=1), wa_ref[...])

    u = proj(_C_CC, D_MODEL) * proj(_C_CX, D_MODEL)
    ubuf_ref[8:8 + ts, :] = u
    wc = wconv_ref[...]
    y_conv = (wc[0:1, :] * ubuf_ref[6:6 + ts, :] + wc[1:2, :] * ubuf_ref[7:7 + ts, :]
              + wc[2:3, :] * u)
    ubuf_ref[0:8, :] = ubuf_ref[ts:ts + 8, :]
    y_b = _dot((proj(_C_CB, D_MODEL) * y_conv).astype(bf16), wb_ref[...])

    mixed = (jax.nn.sigmoid(proj(_C_GA, D_MODEL)) * y_a
             + jax.nn.sigmoid(proj(_C_GB, D_MODEL)) * y_b)
    x1 = x + _dot(mixed.astype(bf16), wo_ref[...])
    x1_ref[...] = x1

    h2 = _rms(x1, gffn_ref[...])
    h2_hi, h2_lo = _split_bf16(h2)
    h2_ref[...] = h2_hi
    wrh = wrh_ref[...]
    logit_ref[...] = (_dot(h2_hi, wrh) + _dot(h2_lo, wrh) + _dot(h2_hi, wrl_ref[...])
                      + br_ref[...])


def _mixer(x2, batch, seq, gmix, wmain, walr, wup, balpha, ggla, wconv, wa, wb, wo, gffn, wrh, wrl, br):
    ts = min(MIX_TS, seq)
    ns = seq // ts
    tokens = batch * seq

    def const(shape):
        return pl.BlockSpec(shape, lambda b, s: (0,) * len(shape), pipeline_mode=pl.Buffered(1))

    def row(width, dtype):
        return pl.BlockSpec((ts, width), lambda b, s: (b * ns + s, 0))

    return pl.pallas_call(
        _mixer_body,
        grid=(batch, ns),
        in_specs=[row(D_MODEL, f32), const(gmix.shape), const(wmain.shape), const(walr.shape),
                  const(wup.shape), const(balpha.shape), const(ggla.shape), const(wconv.shape),
                  const(wa.shape), const(wb.shape), const(wo.shape), const(gffn.shape),
                  const(wrh.shape), const(wrl.shape), const(br.shape)],
        out_specs=[row(D_MODEL, f32), row(D_MODEL, bf16), row(LANES, f32)],
        out_shape=[jax.ShapeDtypeStruct((tokens, D_MODEL), f32),
                   jax.ShapeDtypeStruct((tokens, D_MODEL), bf16),
                   jax.ShapeDtypeStruct((tokens, LANES), f32)],
        scratch_shapes=[pltpu.VMEM((GLA_HEADS, GLA_DV, GLA_DK), f32),
                        pltpu.VMEM((ts + 8, D_MODEL), f32),
                        pltpu.VMEM((ts, V_W), f32)],
        compiler_params=pltpu.CompilerParams(
            dimension_semantics=("parallel", "arbitrary"), vmem_limit_bytes=VMEM_LIMIT),
        name="mixer",
    )(x2, gmix, wmain, walr, wup, balpha, ggla, wconv, wa, wb, wo, gffn, wrh, wrl, br)


def _route_body(logit_ref, meta_ref, metat_ref, cnt_ref):
    tm = logit_ref.shape[0]
    lane = lax.broadcasted_iota(jnp.int32, (tm, LANES), 1)
    lane_f = lane.astype(f32)
    lg = jnp.where(lane < N_EXPERTS, logit_ref[...], -jnp.inf)
    sels, tops = [], []
    for _ in range(TOP_K):
        m = jnp.max(lg, axis=-1, keepdims=True)
        first = jnp.min(jnp.where(lg == m, lane_f, float(LANES)), axis=-1, keepdims=True)
        sel = lane_f == first
        sels.append(sel)
        tops.append(m)
        lg = jnp.where(sel, -jnp.inf, lg)
    ps = [jnp.exp(t - tops[0]) for t in tops]
    denom = ps[0] + ps[1] + ps[2] + ps[3]
    gates = [p / denom for p in ps]

    onehot = jnp.zeros((tm, LANES), f32)
    for sel in sels:
        onehot = onehot + jnp.where(sel, 1.0, 0.0)
    onehot_b = onehot.astype(bf16)
    ri = lax.broadcasted_iota(jnp.int32, (tm, tm), 0)
    ci = lax.broadcasted_iota(jnp.int32, (tm, tm), 1)
    strict_lower = jnp.where(ci < ri, 1.0, 0.0).astype(bf16)
    rank = _dot(strict_lower, onehot_b)
    counts = _dot(jnp.ones((8, tm), bf16), onehot_b)
    blocks = jnp.floor((counts + (BF16_ROWS - 1)) * (1.0 / BF16_ROWS))
    ui = lax.broadcasted_iota(jnp.int32, (LANES, LANES), 0)
    uj = lax.broadcasted_iota(jnp.int32, (LANES, LANES), 1)
    strict_upper = jnp.where(ui < uj, 1.0, 0.0).astype(bf16)
    voff = _dot(blocks.astype(bf16), strict_upper) * float(BF16_ROWS)
    slot = rank + voff[0:1, :]

    meta = jnp.zeros((tm, LANES), f32)
    for kk in range(TOP_K):
        lpos = jnp.sum(jnp.where(sels[kk], slot, 0.0), axis=-1, keepdims=True)
        meta = jnp.where(lane == kk, lpos, meta)
        meta = jnp.where(lane == TOP_K + kk, gates[kk], meta)
    meta_ref[...] = meta
    metat_ref[...] = meta.T[0:8, :]
    cnt_ref[...] = counts.astype(jnp.int32)


def _route(logits):
    tokens = logits.shape[0]
    tm = min(MOE_TM, tokens)
    nt = tokens // tm
    return pl.pallas_call(
        _route_body,
        grid=(nt,),
        in_specs=[pl.BlockSpec((tm, LANES), lambda i: (i, 0))],
        out_specs=[pl.BlockSpec((tm, LANES), lambda i: (i, 0)),
                   pl.BlockSpec((8, tm), lambda i: (i, 0)),
                   pl.BlockSpec((8, LANES), lambda i: (i, 0))],
        out_shape=[jax.ShapeDtypeStruct((tokens, LANES), f32),
                   jax.ShapeDtypeStruct((nt * 8, tm), f32),
                   jax.ShapeDtypeStruct((nt * 8, LANES), jnp.int32)],
        compiler_params=pltpu.CompilerParams(
            dimension_semantics=("parallel",), vmem_limit_bytes=VMEM_LIMIT),
        name="route",
    )(logits)


def _lmax(tm):
    full = tm * TOP_K + N_EXPERTS * BF16_ROWS
    return -(-full // 256) * 256


def _segment_copy(src, dst, sem, src_off, dst_off, rows):
    src_off = pl.multiple_of(src_off, BF16_ROWS)
    dst_off = pl.multiple_of(dst_off, BF16_ROWS)
    rows = pl.multiple_of(rows, BF16_ROWS)
    return pltpu.make_async_copy(src.at[pl.ds(src_off, rows)], dst.at[pl.ds(dst_off, rows)], sem)


def _zero_rows(zero_ref, dst_ref, sem, first, start):
    tile = zero_ref.shape[0]
    n = dst_ref.shape[0] - first
    n_full = n // tile

    def full(j):
        return _segment_copy(zero_ref, dst_ref, sem, 0, first + j * tile, tile)

    def rest():
        return _segment_copy(zero_ref, dst_ref, sem, 0, first + n_full * tile, n - n_full * tile)

    def step(j, carry):
        full(j).start() if start else full(j).wait()
        return carry

    lax.fori_loop(0, n_full, step, 0)

    @pl.when(n - n_full * tile > 0)
    def _():
        rest().start() if start else rest().wait()


def _dispatch_body(goff_ref, pc_ref, voff_ref, used_ref,
                   h2_ref, metat_ref, xs_ref, xc_ref, zero_ref, sem, zsem):
    i = pl.program_id(0)
    nt = pl.num_programs(0)
    tm = h2_ref.shape[0]
    lmax = xc_ref.shape[0]

    slots = metat_ref[0:TOP_K, :].astype(jnp.int32)
    row = lax.broadcasted_iota(jnp.int32, (lmax, tm), 0)
    pt = jnp.zeros((lmax, tm), f32)
    for kk in range(TOP_K):
        pt = jnp.where(row == slots[kk:kk + 1, :], 1.0, pt)
    xc_ref[...] = _dot(pt.astype(bf16), h2_ref[...]).astype(bf16)

    def seg(e):
        j = i * N_EXPERTS + e
        return _segment_copy(xc_ref, xs_ref, sem, voff_ref[j], goff_ref[j], pc_ref[j])

    for e in range(N_EXPERTS):
        @pl.when(pc_ref[i * N_EXPERTS + e] > 0)
        def _():
            seg(e).start()

    @pl.when(i == 0)
    def _():
        zero_ref[...] = jnp.zeros_like(zero_ref)
        _zero_rows(zero_ref, xs_ref, zsem, used_ref[0], start=True)

    @pl.when(i == nt - 1)
    def _():
        _zero_rows(zero_ref, xs_ref, zsem, used_ref[0], start=False)

    for e in range(N_EXPERTS):
        @pl.when(pc_ref[i * N_EXPERTS + e] > 0)
        def _():
            seg(e).wait()


def _dispatch(goff, pc, voff, used, h2, metat, rows_total):
    tokens = h2.shape[0]
    tm = min(MOE_TM, tokens)
    nt = tokens // tm
    return pl.pallas_call(
        _dispatch_body,
        grid_spec=pltpu.PrefetchScalarGridSpec(
            num_scalar_prefetch=4,
            grid=(nt,),
            in_specs=[pl.BlockSpec((tm, D_MODEL), lambda i, *_: (i, 0)),
                      pl.BlockSpec((8, tm), lambda i, *_: (i, 0))],
            out_specs=pl.BlockSpec(memory_space=pl.ANY),
            scratch_shapes=[pltpu.VMEM((_lmax(tm), D_MODEL), bf16),
                            pltpu.VMEM((FFN_TM, D_MODEL), bf16),
                            pltpu.SemaphoreType.DMA, pltpu.SemaphoreType.DMA],
        ),
        out_shape=jax.ShapeDtypeStruct((rows_total, D_MODEL), bf16),
        compiler_params=pltpu.CompilerParams(
            dimension_semantics=("arbitrary",), vmem_limit_bytes=VMEM_LIMIT),
        name="dispatch",
    )(goff, pc, voff, used, h2, metat)


def _ffn_body(gstart_ref, gsize_ref, used_ref, xs_ref, wgu_ref, bgu_ref, wd_ref, bd_ref, os_ref,
              xbuf_ref, obuf_ref, zero_ref, wgub_ref, wdb_ref, state_ref, isem, osem, zsem):
    e = pl.program_id(0)
    ne = pl.num_programs(0)
    tile = xbuf_ref.shape[1]
    base = gstart_ref[e]
    n = gsize_ref[e]
    n_tiles = (n + tile - 1) // tile

    def rows_of(size, t):
        return jnp.minimum(tile, size - t * tile)

    def in_copy(row0, rows, slot):
        return pltpu.make_async_copy(xs_ref.at[pl.ds(row0, rows)],
                                     xbuf_ref.at[slot, pl.ds(0, rows)], isem.at[slot])

    def out_copy(row0, rows, slot):
        return pltpu.make_async_copy(obuf_ref.at[slot, pl.ds(0, rows)],
                                     os_ref.at[pl.ds(row0, rows)], osem.at[slot])

    def start_tile(copy, row0, rows, slot):
        row0 = pl.multiple_of(row0, BF16_ROWS)

        @pl.when(rows == tile)
        def _():
            copy(row0, tile, slot).start(priority=_TILE_DMA_PRIORITY)

        @pl.when(rows < tile)
        def _():
            copy(row0, pl.multiple_of(rows, BF16_ROWS), slot).start()

    def wait_tile(copy, rows, slot):
        copy(0, pl.multiple_of(rows, BF16_ROWS), slot).wait()

    @pl.when(e == 0)
    def _():
        xbuf_ref[...] = jnp.zeros_like(xbuf_ref)
        zero_ref[...] = jnp.zeros_like(zero_ref)
        _zero_rows(zero_ref, os_ref, zsem, used_ref[0], start=True)
        state_ref[0] = 0
        state_ref[1] = 0
        state_ref[2] = 0
        state_ref[3] = 0

    done = state_ref[0]

    @pl.when(jnp.logical_and(n_tiles > 0, state_ref[1] == 0))
    def _():
        start_tile(in_copy, base, rows_of(n, 0), done & 1)

    nxt = jnp.minimum(e + 1, ne - 1)
    has_next = jnp.logical_and(e + 1 < ne, gsize_ref[nxt] > 0)

    @pl.when(n_tiles > 0)
    def _():
        chunk = 128
        for r in range(0, D_MODEL, chunk):
            wgub_ref[r:r + chunk, :] = wgu_ref[r:r + chunk, :].astype(bf16)
        for r in range(0, D_FF, chunk):
            wdb_ref[r:r + chunk, :] = wd_ref[r:r + chunk, :].astype(bf16)

    def step(t, carry):
        slot = (done + t) & 1

        @pl.when(t + 1 < n_tiles)
        def _():
            start_tile(in_copy, base + (t + 1) * tile, rows_of(n, t + 1), 1 - slot)

        @pl.when(jnp.logical_and(t + 1 == n_tiles, has_next))
        def _():
            start_tile(in_copy, gstart_ref[nxt], rows_of(gsize_ref[nxt], 0), 1 - slot)

        rows = rows_of(n, t)
        wait_tile(in_copy, rows, slot)

        @pl.when(state_ref[2 + slot] > 0)
        def _():
            wait_tile(out_copy, state_ref[2 + slot], slot)

        gu = _dot(xbuf_ref[slot], wgub_ref[...]) + bgu_ref[...]
        gate = jnp.minimum(gu[:, :D_FF], SWIGLU_LIMIT)
        lin = jnp.clip(gu[:, D_FF:], -SWIGLU_LIMIT, SWIGLU_LIMIT)
        act = (lin + 1.0) * (gate * jax.nn.sigmoid(SWIGLU_ALPHA * gate))
        out = _dot(act.astype(bf16), wdb_ref[...]) + bd_ref[...]
        obuf_ref[slot] = out.astype(bf16)
        start_tile(out_copy, base + t * tile, rows, slot)
        state_ref[2 + slot] = rows
        return carry

    lax.fori_loop(0, n_tiles, step, 0)

    @pl.when(n_tiles > 0)
    def _():
        state_ref[0] = done + n_tiles
        state_ref[1] = has_next.astype(jnp.int32)

    @pl.when(e == ne - 1)
    def _():
        for slot in range(2):
            @pl.when(state_ref[2 + slot] > 0)
            def _():
                wait_tile(out_copy, state_ref[2 + slot], slot)
        _zero_rows(zero_ref, os_ref, zsem, used_ref[0], start=False)


def _ffn(gstart, gsize, used, xs, wgu, bgu, wd, bd):
    rows_total = xs.shape[0]
    return pl.pallas_call(
        _ffn_body,
        grid_spec=pltpu.PrefetchScalarGridSpec(
            num_scalar_prefetch=3,
            grid=(N_EXPERTS,),
            in_specs=[pl.BlockSpec(memory_space=pl.ANY),
                      pl.BlockSpec((None, D_MODEL, 2 * D_FF), lambda e, *_: (e, 0, 0)),
                      pl.BlockSpec((None, 1, 2 * D_FF), lambda e, *_: (e, 0, 0)),
                      pl.BlockSpec((None, D_FF, D_MODEL), lambda e, *_: (e, 0, 0)),
                      pl.BlockSpec((None, 1, D_MODEL), lambda e, *_: (e, 0, 0))],
            out_specs=pl.BlockSpec(memory_space=pl.ANY),
            scratch_shapes=[pltpu.VMEM((2, FFN_TM, D_MODEL), bf16),
                            pltpu.VMEM((2, FFN_TM, D_MODEL), bf16),
                            pltpu.VMEM((FFN_TM, D_MODEL), bf16),
                            pltpu.VMEM((D_MODEL, 2 * D_FF), bf16),
                            pltpu.VMEM((D_FF, D_MODEL), bf16),
                            pltpu.SMEM((4,), jnp.int32),
                            pltpu.SemaphoreType.DMA((2,)), pltpu.SemaphoreType.DMA((2,)),
                            pltpu.SemaphoreType.DMA],
        ),
        out_shape=jax.ShapeDtypeStruct((rows_total, D_MODEL), bf16),
        compiler_params=pltpu.CompilerParams(
            dimension_semantics=("arbitrary",), vmem_limit_bytes=VMEM_LIMIT),
        name="experts",
    )(gstart, gsize, used, xs, wgu, bgu, wd, bd)


def _combine_body(final_norm, goff_ref, pc_ref, voff_ref, os_ref, meta_ref, x1_ref, gfin_ref, out_ref,
                  oc_ref, sem):
    i = pl.program_id(0)
    tm = x1_ref.shape[0]
    lmax = oc_ref.shape[0]

    @pl.when(i == 0)
    def _():
        oc_ref[...] = jnp.zeros_like(oc_ref)

    def seg(e):
        j = i * N_EXPERTS + e
        return _segment_copy(os_ref, oc_ref, sem, goff_ref[j], voff_ref[j], pc_ref[j])

    for e in range(N_EXPERTS):
        @pl.when(pc_ref[i * N_EXPERTS + e] > 0)
        def _():
            seg(e).start()

    meta = meta_ref[...]
    col = lax.broadcasted_iota(jnp.int32, (tm, lmax), 1)
    p = jnp.zeros((tm, lmax), f32)
    for kk in range(TOP_K):
        slot = meta[:, kk:kk + 1].astype(jnp.int32)
        p = jnp.where(col == slot, meta[:, TOP_K + kk:TOP_K + kk + 1], p)
    pb = p.astype(bf16)

    for e in range(N_EXPERTS):
        @pl.when(pc_ref[i * N_EXPERTS + e] > 0)
        def _():
            seg(e).wait()

    xo = x1_ref[...] + _dot(pb, oc_ref[...])
    out_ref[...] = _rms(xo, gfin_ref[...]) if final_norm else xo


def _combine(goff, pc, voff, os, meta, x1, gfin, final_norm):
    tokens = x1.shape[0]
    tm = min(MOE_TM, tokens)
    nt = tokens // tm
    return pl.pallas_call(
        functools.partial(_combine_body, final_norm),
        grid_spec=pltpu.PrefetchScalarGridSpec(
            num_scalar_prefetch=3,
            grid=(nt,),
            in_specs=[pl.BlockSpec(memory_space=pl.ANY),
                      pl.BlockSpec((tm, LANES), lambda i, *_: (i, 0)),
                      pl.BlockSpec((tm, D_MODEL), lambda i, *_: (i, 0)),
                      pl.BlockSpec((1, D_MODEL), lambda i, *_: (0, 0))],
            out_specs=pl.BlockSpec((tm, D_MODEL), lambda i, *_: (i, 0)),
            scratch_shapes=[pltpu.VMEM((_lmax(tm), D_MODEL), bf16), pltpu.SemaphoreType.DMA],
        ),
        out_shape=jax.ShapeDtypeStruct((tokens, D_MODEL), f32),
        compiler_params=pltpu.CompilerParams(
            dimension_semantics=("arbitrary",), vmem_limit_bytes=VMEM_LIMIT),
        name="combine",
    )(goff, pc, voff, os, meta, x1, gfin)


def _plan(counts, nt):
    c = counts.reshape(nt, 8, LANES)[:, 0, :N_EXPERTS]
    pc = (c + (BF16_ROWS - 1)) // BF16_ROWS * BF16_ROWS
    voff = jnp.cumsum(pc, axis=1) - pc
    gsize = jnp.sum(pc, axis=0)
    gend = jnp.cumsum(gsize)
    gstart = gend - gsize
    goff = gstart[None, :] + jnp.cumsum(pc, axis=0) - pc
    i32 = lambda a: a.reshape(-1).astype(jnp.int32)
    return i32(pc), i32(voff), i32(goff), i32(gstart), i32(gsize), i32(gend[-1:])


def kernel(x, g_mix, w_in, w_alpha_up, b_alpha, g_gla, w_conv, w_branch_a, w_branch_b, w_out, g_ffn, w_router, b_router, w_gate_up, b_gate_up, w_down, b_down, g_final):
    batch, seq, _ = x.shape
    tokens = batch * seq
    depth = w_in.shape[0]
    x2 = x.reshape(tokens, D_MODEL)
    tm = min(MOE_TM, tokens)
    nt = tokens // tm
    rows_total = tokens * TOP_K + nt * N_EXPERTS * (BF16_ROWS - 1)
    rows_total = -(-rows_total // BF16_ROWS) * BF16_ROWS

    for l in range(depth):
        wi = w_in[l]
        a0 = 2 * QK_W + 2 * V_W
        wmain = jnp.concatenate([wi[:, :a0], wi[:, a0 + GLA_RANK:]], axis=1).astype(bf16)
        walr = jnp.pad(wi[:, a0:a0 + GLA_RANK], ((0, 0), (0, LANES - GLA_RANK))).astype(bf16)
        wup = jnp.pad(w_alpha_up[l], ((0, LANES - GLA_RANK), (0, 0))).astype(bf16)
        wr = jnp.pad(w_router[l], ((0, 0), (0, LANES - N_EXPERTS)))
        wrh = wr.astype(bf16)
        wrl = (wr - wrh.astype(f32)).astype(bf16)
        br = jnp.pad(b_router[l], (0, LANES - N_EXPERTS)).reshape(1, LANES)
        wconv = jnp.pad(w_conv[l], ((0, 8 - CONV_K), (0, 0)))

        x1, h2, logits = _mixer(
            x2, batch, seq, g_mix[l].reshape(1, D_MODEL), wmain, walr, wup,
            b_alpha[l].reshape(1, QK_W), g_gla[l].reshape(1, V_W), wconv,
            w_branch_a[l].astype(bf16), w_branch_b[l].astype(bf16), w_out[l].astype(bf16),
            g_ffn[l].reshape(1, D_MODEL), wrh, wrl, br)

        meta, metat, counts = _route(logits)
        pc, voff, goff, gstart, gsize, used = _plan(counts, nt)
        xs = _dispatch(goff, pc, voff, used, h2, metat, rows_total)
        os = _ffn(gstart, gsize, used, xs,
                  w_gate_up[l], b_gate_up[l].reshape(N_EXPERTS, 1, 2 * D_FF),
                  w_down[l], b_down[l].reshape(N_EXPERTS, 1, D_MODEL))
        x2 = _combine(goff, pc, voff, os, meta, x1, g_final.reshape(1, D_MODEL), l == depth - 1)
    return x2.reshape(batch, seq, D_MODEL)
```

```python
import functools

import jax
import jax.numpy as jnp
from jax import lax
from jax.experimental import pallas as pl
from jax.experimental.pallas import tpu as pltpu

D_MODEL = 1024
GLA_HEADS = 4
GLA_DK = 128
GLA_DV = 256
GLA_RANK = 16
GLA_TAU = 16.0
GLA_CHUNK = 64
_CHUNK_SHIFT = GLA_CHUNK.bit_length() - 1
CONV_K = 3
N_EXPERTS = 32
TOP_K = 4
D_FF = 1024
SWIGLU_LIMIT = 7.0
SWIGLU_ALPHA = 1.702
EPS = 1e-6

QK_W = GLA_HEADS * GLA_DK
V_W = GLA_HEADS * GLA_DV

LANES = 128
BF16_ROWS = 16
VMEM_LIMIT = 56 * 1024 * 1024

MIX_TS = 256
MOE_TM = 512
FFN_TM = 512
_TILE_DMA_PRIORITY = 1

_C_Q, _C_K, _C_V, _C_R = 0, QK_W, 2 * QK_W, 2 * QK_W + V_W
_C_CB = _C_R + V_W
_C_CC = _C_CB + D_MODEL
_C_CX = _C_CC + D_MODEL
_C_GA = _C_CX + D_MODEL
_C_GB = _C_GA + D_MODEL
_W_MAIN = _C_GB + D_MODEL

f32 = jnp.float32
bf16 = jnp.bfloat16


def _rms(x, g):
    return x * lax.rsqrt(jnp.mean(x * x, axis=-1, keepdims=True) + EPS) * g


def _dot(a, b):
    return jnp.dot(a, b, preferred_element_type=f32)


def _split_bf16(a):
    hi = a.astype(bf16)
    lo = (a - hi.astype(f32)).astype(bf16)
    return hi, lo


def _mixer_body(x_ref, gmix_ref, wmain_ref, walr_ref, wup_ref, balpha_ref, ggla_ref, wconv_ref,
                wa_ref, wb_ref, wo_ref, gffn_ref, wrh_ref, wrl_ref, br_ref,
                x1_ref, h2_ref, logit_ref,
                state_ref, ubuf_ref, obuf_ref):
    ts = x_ref.shape[0]
    nchunk = ts // GLA_CHUNK

    @pl.when(pl.program_id(1) == 0)
    def _():
        state_ref[...] = jnp.zeros_like(state_ref)
        ubuf_ref[0:8, :] = jnp.zeros((8, D_MODEL), f32)

    x = x_ref[...]
    hb = _rms(x, gmix_ref[...]).astype(bf16)

    def proj(c0, width):
        return _dot(hb, wmain_ref[:, c0:c0 + width])

    a_lr = _dot(hb, walr_ref[...])
    pre = _dot(a_lr.astype(bf16), wup_ref[...]) + balpha_ref[...]
    log_a = -(jnp.maximum(-pre, 0.0) + jnp.log1p(jnp.exp(-jnp.abs(pre)))) * (1.0 / GLA_TAU)
    ri = lax.broadcasted_iota(jnp.int32, (ts, ts), 0)
    ci = lax.broadcasted_iota(jnp.int32, (ts, ts), 1)
    same_chunk = (ri >> _CHUNK_SHIFT) == (ci >> _CHUNK_SHIFT)
    causal = jnp.logical_and(same_chunk, ci <= ri)
    tri = jnp.where(causal, 1.0, 0.0).astype(bf16)
    blk = jnp.where(same_chunk, 1.0, 0.0).astype(bf16)
    la_hi, la_lo = _split_bf16(log_a)
    b = _dot(tri, la_hi) + _dot(tri, la_lo)
    b_tot = _dot(blk, la_hi) + _dot(blk, la_lo)
    e_pos = jnp.exp(b)
    e_neg = jnp.exp(-b)
    e_tail = jnp.exp(b_tot - b)
    e_tot = jnp.exp(b_tot)

    q = proj(_C_Q, QK_W)
    k = proj(_C_K, QK_W)
    q_dec = (q * (GLA_DK ** -0.5) * e_pos).astype(bf16)
    k_inv = (k * e_neg).astype(bf16)
    k_tail = (k * e_tail).astype(bf16)
    vb = proj(_C_V, V_W).astype(bf16)

    for hh in range(GLA_HEADS):
        qs = slice(hh * GLA_DK, (hh + 1) * GLA_DK)
        vs = slice(hh * GLA_DV, (hh + 1) * GLA_DV)
        sc = lax.dot_general(q_dec[:, qs], k_inv[:, qs], (((1,), (1,)), ((), ())),
                             preferred_element_type=f32)
        sc = jnp.where(causal, sc, 0.0).astype(bf16)
        obuf_ref[:, vs] = _dot(sc, vb[:, vs])
        st = state_ref[hh]
        for c in range(nchunk):
            rs = slice(c * GLA_CHUNK, (c + 1) * GLA_CHUNK)
            o_inter = lax.dot_general(q_dec[rs, qs], st.astype(bf16), (((1,), (1,)), ((), ())),
                                      preferred_element_type=f32)
            obuf_ref[rs, vs] += o_inter
            upd = lax.dot_general(vb[rs, vs], k_tail[rs, qs], (((0,), (0,)), ((), ())),
                                  preferred_element_type=f32)
            st = st * e_tot[c * GLA_CHUNK:c * GLA_CHUNK + 1, qs] + upd
        state_ref[hh] = st

    r = proj(_C_R, V_W)
    ggla = ggla_ref[...]
    o_parts = []
    for hh in range(GLA_HEADS):
        vs = slice(hh * GLA_DV, (hh + 1) * GLA_DV)
        o_h = _rms(obuf_ref[:, vs], ggla[:, vs])
        r_h = r[:, vs]
        o_parts.append((o_h * (r_h * jax.nn.sigmoid(r_h))).astype(bf16))
    y_a = _dot(jnp.concatenate(o_parts, axis=1), wa_ref[...])

    u = proj(_C_CC, D_MODEL) * proj(_C_CX, D_MODEL)
    ubuf_ref[8:8 + ts, :] = u
    wc = wconv_ref[...]
    y_conv = (wc[0:1, :] * ubuf_ref[6:6 + ts, :] + wc[1:2, :] * ubuf_ref[7:7 + ts, :]
              + wc[2:3, :] * u)
    ubuf_ref[0:8, :] = ubuf_ref[ts:ts + 8, :]
    y_b = _dot((proj(_C_CB, D_MODEL) * y_conv).astype(bf16), wb_ref[...])

    mixed = (jax.nn.sigmoid(proj(_C_GA, D_MODEL)) * y_a
             + jax.nn.sigmoid(proj(_C_GB, D_MODEL)) * y_b)
    x1 = x + _dot(mixed.astype(bf16), wo_ref[...])
    x1_ref[...] = x1

    h2 = _rms(x1, gffn_ref[...])
    h2_hi, h2_lo = _split_bf16(h2)
    h2_ref[...] = h2_hi
    wrh = wrh_ref[...]
    logit_ref[...] = (_dot(h2_hi, wrh) + _dot(h2_lo, wrh) + _dot(h2_hi, wrl_ref[...])
                      + br_ref[...])


def _mixer(x2, batch, seq, gmix, wmain, walr, wup, balpha, ggla, wconv, wa, wb, wo, gffn, wrh, wrl, br):
    ts = min(MIX_TS, seq)
    ns = seq // ts
    tokens = batch * seq

    def const(shape):
        return pl.BlockSpec(shape, lambda b, s: (0,) * len(shape), pipeline_mode=pl.Buffered(1))

    def row(width, dtype):
        return pl.BlockSpec((ts, width), lambda b, s: (b * ns + s, 0))

    return pl.pallas_call(
        _mixer_body,
        grid=(batch, ns),
        in_specs=[row(D_MODEL, f32), const(gmix.shape), const(wmain.shape), const(walr.shape),
                  const(wup.shape), const(balpha.shape), const(ggla.shape), const(wconv.shape),
                  const(wa.shape), const(wb.shape), const(wo.shape), const(gffn.shape),
                  const(wrh.shape), const(wrl.shape), const(br.shape)],
        out_specs=[row(D_MODEL, f32), row(D_MODEL, bf16), row(LANES, f32)],
        out_shape=[jax.ShapeDtypeStruct((tokens, D_MODEL), f32),
                   jax.ShapeDtypeStruct((tokens, D_MODEL), bf16),
                   jax.ShapeDtypeStruct((tokens, LANES), f32)],
        scratch_shapes=[pltpu.VMEM((GLA_HEADS, GLA_DV, GLA_DK), f32),
                        pltpu.VMEM((ts + 8, D_MODEL), f32),
                        pltpu.VMEM((ts, V_W), f32)],
        compiler_params=pltpu.CompilerParams(
            dimension_semantics=("parallel", "arbitrary"), vmem_limit_bytes=VMEM_LIMIT),
        name="mixer",
    )(x2, gmix, wmain, walr, wup, balpha, ggla, wconv, wa, wb, wo, gffn, wrh, wrl, br)


def _route_body(logit_ref, meta_ref, metat_ref, cnt_ref):
    tm = logit_ref.shape[0]
    lane = lax.broadcasted_iota(jnp.int32, (tm, LANES), 1)
    lane_f = lane.astype(f32)
    lg = jnp.where(lane < N_EXPERTS, logit_ref[...], -jnp.inf)
    sels, tops = [], []
    for _ in range(TOP_K):
        m = jnp.max(lg, axis=-1, keepdims=True)
        first = jnp.min(jnp.where(lg == m, lane_f, float(LANES)), axis=-1, keepdims=True)
        sel = lane_f == first
        sels.append(sel)
        tops.append(m)
        lg = jnp.where(sel, -jnp.inf, lg)
    ps = [jnp.exp(t - tops[0]) for t in tops]
    denom = ps[0] + ps[1] + ps[2] + ps[3]
    gates = [p / denom for p in ps]

    onehot = jnp.zeros((tm, LANES), f32)
    for sel in sels:
        onehot = onehot + jnp.where(sel, 1.0, 0.0)
    onehot_b = onehot.astype(bf16)
    ri = lax.broadcasted_iota(jnp.int32, (tm, tm), 0)
    ci = lax.broadcasted_iota(jnp.int32, (tm, tm), 1)
    strict_lower = jnp.where(ci < ri, 1.0, 0.0).astype(bf16)
    rank = _dot(strict_lower, onehot_b)
    counts = _dot(jnp.ones((8, tm), bf16), onehot_b)
    blocks = jnp.floor((counts + (BF16_ROWS - 1)) * (1.0 / BF16_ROWS))
    ui = lax.broadcasted_iota(jnp.int32, (LANES, LANES), 0)
    uj = lax.broadcasted_iota(jnp.int32, (LANES, LANES), 1)
    strict_upper = jnp.where(ui < uj, 1.0, 0.0).astype(bf16)
    voff = _dot(blocks.astype(bf16), strict_upper) * float(BF16_ROWS)
    slot = rank + voff[0:1, :]

    meta = jnp.zeros((tm, LANES), f32)
    for kk in range(TOP_K):
        lpos = jnp.sum(jnp.where(sels[kk], slot, 0.0), axis=-1, keepdims=True)
        meta = jnp.where(lane == kk, lpos, meta)
        meta = jnp.where(lane == TOP_K + kk, gates[kk], meta)
    meta_ref[...] = meta
    metat_ref[...] = meta.T[0:8, :]
    cnt_ref[...] = counts.astype(jnp.int32)


def _route(logits):
    tokens = logits.shape[0]
    tm = min(MOE_TM, tokens)
    nt = tokens // tm
    return pl.pallas_call(
        _route_body,
        grid=(nt,),
        in_specs=[pl.BlockSpec((tm, LANES), lambda i: (i, 0))],
        out_specs=[pl.BlockSpec((tm, LANES), lambda i: (i, 0)),
                   pl.BlockSpec((8, tm), lambda i: (i, 0)),
                   pl.BlockSpec((8, LANES), lambda i: (i, 0))],
        out_shape=[jax.ShapeDtypeStruct((tokens, LANES), f32),
                   jax.ShapeDtypeStruct((nt * 8, tm), f32),
                   jax.ShapeDtypeStruct((nt * 8, LANES), jnp.int32)],
        compiler_params=pltpu.CompilerParams(
            dimension_semantics=("parallel",), vmem_limit_bytes=VMEM_LIMIT),
        name="route",
    )(logits)


def _lmax(tm):
    full = tm * TOP_K + N_EXPERTS * BF16_ROWS
    return -(-full // 256) * 256


def _segment_copy(src, dst, sem, src_off, dst_off, rows):
    src_off = pl.multiple_of(src_off, BF16_ROWS)
    dst_off = pl.multiple_of(dst_off, BF16_ROWS)
    rows = pl.multiple_of(rows, BF16_ROWS)
    return pltpu.make_async_copy(src.at[pl.ds(src_off, rows)], dst.at[pl.ds(dst_off, rows)], sem)


def _zero_rows(zero_ref, dst_ref, sem, first, start):
    tile = zero_ref.shape[0]
    n = dst_ref.shape[0] - first
    n_full = n // tile

    def full(j):
        return _segment_copy(zero_ref, dst_ref, sem, 0, first + j * tile, tile)

    def rest():
        return _segment_copy(zero_ref, dst_ref, sem, 0, first + n_full * tile, n - n_full * tile)

    def step(j, carry):
        full(j).start() if start else full(j).wait()
        return carry

    lax.fori_loop(0, n_full, step, 0)

    @pl.when(n - n_full * tile > 0)
    def _():
        rest().start() if start else rest().wait()


def _dispatch_body(goff_ref, pc_ref, voff_ref, used_ref,
                   h2_ref, metat_ref, xs_ref, xc_ref, zero_ref, sem, zsem):
    i = pl.program_id(0)
    nt = pl.num_programs(0)
    tm = h2_ref.shape[0]
    lmax = xc_ref.shape[0]

    slots = metat_ref[0:TOP_K, :].astype(jnp.int32)
    row = lax.broadcasted_iota(jnp.int32, (lmax, tm), 0)
    pt = jnp.zeros((lmax, tm), f32)
    for kk in range(TOP_K):
        pt = jnp.where(row == slots[kk:kk + 1, :], 1.0, pt)
    xc_ref[...] = _dot(pt.astype(bf16), h2_ref[...]).astype(bf16)

    def seg(e):
        j = i * N_EXPERTS + e
        return _segment_copy(xc_ref, xs_ref, sem, voff_ref[j], goff_ref[j], pc_ref[j])

    for e in range(N_EXPERTS):
        @pl.when(pc_ref[i * N_EXPERTS + e] > 0)
        def _():
            seg(e).start()

    @pl.when(i == 0)
    def _():
        zero_ref[...] = jnp.zeros_like(zero_ref)
        _zero_rows(zero_ref, xs_ref, zsem, used_ref[0], start=True)

    @pl.when(i == nt - 1)
    def _():
        _zero_rows(zero_ref, xs_ref, zsem, used_ref[0], start=False)

    for e in range(N_EXPERTS):
        @pl.when(pc_ref[i * N_EXPERTS + e] > 0)
        def _():
            seg(e).wait()


def _dispatch(goff, pc, voff, used, h2, metat, rows_total):
    tokens = h2.shape[0]
    tm = min(MOE_TM, tokens)
    nt = tokens // tm
    return pl.pallas_call(
        _dispatch_body,
        grid_spec=pltpu.PrefetchScalarGridSpec(
            num_scalar_prefetch=4,
            grid=(nt,),
            in_specs=[pl.BlockSpec((tm, D_MODEL), lambda i, *_: (i, 0)),
                      pl.BlockSpec((8, tm), lambda i, *_: (i, 0))],
            out_specs=pl.BlockSpec(memory_space=pl.ANY),
            scratch_shapes=[pltpu.VMEM((_lmax(tm), D_MODEL), bf16),
                            pltpu.VMEM((FFN_TM, D_MODEL), bf16),
                            pltpu.SemaphoreType.DMA, pltpu.SemaphoreType.DMA],
        ),
        out_shape=jax.ShapeDtypeStruct((rows_total, D_MODEL), bf16),
        compiler_params=pltpu.CompilerParams(
            dimension_semantics=("arbitrary",), vmem_limit_bytes=VMEM_LIMIT),
        name="dispatch",
    )(goff, pc, voff, used, h2, metat)


def _ffn_body(gstart_ref, gsize_ref, used_ref, xs_ref, wgu_ref, bgu_ref, wd_ref, bd_ref, os_ref,
              xbuf_ref, obuf_ref, zero_ref, wgub_ref, wdb_ref, state_ref, isem, osem, zsem):
    e = pl.program_id(0)
    ne = pl.num_programs(0)
    tile = xbuf_ref.shape[1]
    base = gstart_ref[e]
    n = gsize_ref[e]
    n_tiles = (n + tile - 1) // tile

    def rows_of(size, t):
        return jnp.minimum(tile, size - t * tile)

    def in_copy(row0, rows, slot):
        return pltpu.make_async_copy(xs_ref.at[pl.ds(row0, rows)],
                                     xbuf_ref.at[slot, pl.ds(0, rows)], isem.at[slot])

    def out_copy(row0, rows, slot):
        return pltpu.make_async_copy(obuf_ref.at[slot, pl.ds(0, rows)],
                                     os_ref.at[pl.ds(row0, rows)], osem.at[slot])

    def start_tile(copy, row0, rows, slot):
        row0 = pl.multiple_of(row0, BF16_ROWS)

        @pl.when(rows == tile)
        def _():
            copy(row0, tile, slot).start(priority=_TILE_DMA_PRIORITY)

        @pl.when(rows < tile)
        def _():
            copy(row0, pl.multiple_of(rows, BF16_ROWS), slot).start()

    def wait_tile(copy, rows, slot):
        copy(0, pl.multiple_of(rows, BF16_ROWS), slot).wait()

    @pl.when(e == 0)
    def _():
        xbuf_ref[...] = jnp.zeros_like(xbuf_ref)
        zero_ref[...] = jnp.zeros_like(zero_ref)
        _zero_rows(zero_ref, os_ref, zsem, used_ref[0], start=True)
        state_ref[0] = 0
        state_ref[1] = 0
        state_ref[2] = 0
        state_ref[3] = 0

    done = state_ref[0]

    @pl.when(jnp.logical_and(n_tiles > 0, state_ref[1] == 0))
    def _():
        start_tile(in_copy, base, rows_of(n, 0), done & 1)

    nxt = jnp.minimum(e + 1, ne - 1)
    has_next = jnp.logical_and(e + 1 < ne, gsize_ref[nxt] > 0)

    @pl.when(n_tiles > 0)
    def _():
        chunk = 128
        for r in range(0, D_MODEL, chunk):
            wgub_ref[r:r + chunk, :] = wgu_ref[r:r + chunk, :].astype(bf16)
        for r in range(0, D_FF, chunk):
            wdb_ref[r:r + chunk, :] = wd_ref[r:r + chunk, :].astype(bf16)

    def step(t, carry):
        slot = (done + t) & 1

        @pl.when(t + 1 < n_tiles)
        def _():
            start_tile(in_copy, base + (t + 1) * tile, rows_of(n, t + 1), 1 - slot)

        @pl.when(jnp.logical_and(t + 1 == n_tiles, has_next))
        def _():
            start_tile(in_copy, gstart_ref[nxt], rows_of(gsize_ref[nxt], 0), 1 - slot)

        rows = rows_of(n, t)
        wait_tile(in_copy, rows, slot)

        @pl.when(state_ref[2 + slot] > 0)
        def _():
            wait_tile(out_copy, state_ref[2 + slot], slot)

        gu = _dot(xbuf_ref[slot], wgub_ref[...]) + bgu_ref[...]
        gate = jnp.minimum(gu[:, :D_FF], SWIGLU_LIMIT)
        lin = jnp.clip(gu[:, D_FF:], -SWIGLU_LIMIT, SWIGLU_LIMIT)
        act = (lin + 1.0) * (gate * jax.nn.sigmoid(SWIGLU_ALPHA * gate))
        out = _dot(act.astype(bf16), wdb_ref[...]) + bd_ref[...]
        obuf_ref[slot] = out.astype(bf16)
        start_tile(out_copy, base + t * tile, rows, slot)
        state_ref[2 + slot] = rows
        return carry

    lax.fori_loop(0, n_tiles, step, 0)

    @pl.when(n_tiles > 0)
    def _():
        state_ref[0] = done + n_tiles
        state_ref[1] = has_next.astype(jnp.int32)

    @pl.when(e == ne - 1)
    def _():
        for slot in range(2):
            @pl.when(state_ref[2 + slot] > 0)
            def _():
                wait_tile(out_copy, state_ref[2 + slot], slot)
        _zero_rows(zero_ref, os_ref, zsem, used_ref[0], start=False)


def _ffn(gstart, gsize, used, xs, wgu, bgu, wd, bd):
    rows_total = xs.shape[0]
    return pl.pallas_call(
        _ffn_body,
        grid_spec=pltpu.PrefetchScalarGridSpec(
            num_scalar_prefetch=3,
            grid=(N_EXPERTS,),
            in_specs=[pl.BlockSpec(memory_space=pl.ANY),
                      pl.BlockSpec((None, D_MODEL, 2 * D_FF), lambda e, *_: (e, 0, 0)),
                      pl.BlockSpec((None, 1, 2 * D_FF), lambda e, *_: (e, 0, 0)),
                      pl.BlockSpec((None, D_FF, D_MODEL), lambda e, *_: (e, 0, 0)),
                      pl.BlockSpec((None, 1, D_MODEL), lambda e, *_: (e, 0, 0))],
            out_specs=pl.BlockSpec(memory_space=pl.ANY),
            scratch_shapes=[pltpu.VMEM((2, FFN_TM, D_MODEL), bf16),
                            pltpu.VMEM((2, FFN_TM, D_MODEL), bf16),
                            pltpu.VMEM((FFN_TM, D_MODEL), bf16),
                            pltpu.VMEM((D_MODEL, 2 * D_FF), bf16),
                            pltpu.VMEM((D_FF, D_MODEL), bf16),
                            pltpu.SMEM((4,), jnp.int32),
                            pltpu.SemaphoreType.DMA((2,)), pltpu.SemaphoreType.DMA((2,)),
                            pltpu.SemaphoreType.DMA],
        ),
        out_shape=jax.ShapeDtypeStruct((rows_total, D_MODEL), bf16),
        compiler_params=pltpu.CompilerParams(
            dimension_semantics=("arbitrary",), vmem_limit_bytes=VMEM_LIMIT),
        name="experts",
    )(gstart, gsize, used, xs, wgu, bgu, wd, bd)


def _combine_body(final_norm, goff_ref, pc_ref, voff_ref, os_ref, meta_ref, x1_ref, gfin_ref, out_ref,
                  oc_ref, sem):
    i = pl.program_id(0)
    tm = x1_ref.shape[0]
    lmax = oc_ref.shape[0]

    @pl.when(i == 0)
    def _():
        oc_ref[...] = jnp.zeros_like(oc_ref)

    def seg(e):
        j = i * N_EXPERTS + e
        return _segment_copy(os_ref, oc_ref, sem, goff_ref[j], voff_ref[j], pc_ref[j])

    for e in range(N_EXPERTS):
        @pl.when(pc_ref[i * N_EXPERTS + e] > 0)
        def _():
            seg(e).start()

    meta = meta_ref[...]
    col = lax.broadcasted_iota(jnp.int32, (tm, lmax), 1)
    p = jnp.zeros((tm, lmax), f32)
    for kk in range(TOP_K):
        slot = meta[:, kk:kk + 1].astype(jnp.int32)
        p = jnp.where(col == slot, meta[:, TOP_K + kk:TOP_K + kk + 1], p)
    pb = p.astype(bf16)

    for e in range(N_EXPERTS):
        @pl.when(pc_ref[i * N_EXPERTS + e] > 0)
        def _():
            seg(e).wait()

    xo = x1_ref[...] + _dot(pb, oc_ref[...])
    out_ref[...] = _rms(xo, gfin_ref[...]) if final_norm else xo


def _combine(goff, pc, voff, os, meta, x1, gfin, final_norm):
    tokens = x1.shape[0]
    tm = min(MOE_TM, tokens)
    nt = tokens // tm
    return pl.pallas_call(
        functools.partial(_combine_body, final_norm),
        grid_spec=pltpu.PrefetchScalarGridSpec(
            num_scalar_prefetch=3,
            grid=(nt,),
            in_specs=[pl.BlockSpec(memory_space=pl.ANY),
                      pl.BlockSpec((tm, LANES), lambda i, *_: (i, 0)),
                      pl.BlockSpec((tm, D_MODEL), lambda i, *_: (i, 0)),
                      pl.BlockSpec((1, D_MODEL), lambda i, *_: (0, 0))],
            out_specs=pl.BlockSpec((tm, D_MODEL), lambda i, *_: (i, 0)),
            scratch_shapes=[pltpu.VMEM((_lmax(tm), D_MODEL), bf16), pltpu.SemaphoreType.DMA],
        ),
        out_shape=jax.ShapeDtypeStruct((tokens, D_MODEL), f32),
        compiler_params=pltpu.CompilerParams(
            dimension_semantics=("arbitrary",), vmem_limit_bytes=VMEM_LIMIT),
        name="combine",
    )(goff, pc, voff, os, meta, x1, gfin)


def _plan(counts, nt):
    c = counts.reshape(nt, 8, LANES)[:, 0, :N_EXPERTS]
    pc = (c + (BF16_ROWS - 1)) // BF16_ROWS * BF16_ROWS
    voff = jnp.cumsum(pc, axis=1) - pc
    gsize = jnp.sum(pc, axis=0)
    gend = jnp.cumsum(gsize)
    gstart = gend - gsize
    goff = gstart[None, :] + jnp.cumsum(pc, axis=0) - pc
    i32 = lambda a: a.reshape(-1).astype(jnp.int32)
    return i32(pc), i32(voff), i32(goff), i32(gstart), i32(gsize), i32(gend[-1:])


def kernel(x, g_mix, w_in, w_alpha_up, b_alpha, g_gla, w_conv, w_branch_a, w_branch_b, w_out, g_ffn, w_router, b_router, w_gate_up, b_gate_up, w_down, b_down, g_final):
    batch, seq, _ = x.shape
    tokens = batch * seq
    depth = w_in.shape[0]
    x2 = x.reshape(tokens, D_MODEL)
    tm = min(MOE_TM, tokens)
    nt = tokens // tm
    rows_total = tokens * TOP_K + nt * N_EXPERTS * (BF16_ROWS - 1)
    rows_total = -(-rows_total // BF16_ROWS) * BF16_ROWS

    for l in range(depth):
        wi = w_in[l]
        a0 = 2 * QK_W + 2 * V_W
        wmain = jnp.concatenate([wi[:, :a0], wi[:, a0 + GLA_RANK:]], axis=1).astype(bf16)
        walr = jnp.pad(wi[:, a0:a0 + GLA_RANK], ((0, 0), (0, LANES - GLA_RANK))).astype(bf16)
        wup = jnp.pad(w_alpha_up[l], ((0, LANES - GLA_RANK), (0, 0))).astype(bf16)
        wr = jnp.pad(w_router[l], ((0, 0), (0, LANES - N_EXPERTS)))
        wrh = wr.astype(bf16)
        wrl = (wr - wrh.astype(f32)).astype(bf16)
        br = jnp.pad(b_router[l], (0, LANES - N_EXPERTS)).reshape(1, LANES)
        wconv = jnp.pad(w_conv[l], ((0, 8 - CONV_K), (0, 0)))

        x1, h2, logits = _mixer(
            x2, batch, seq, g_mix[l].reshape(1, D_MODEL), wmain, walr, wup,
            b_alpha[l].reshape(1, QK_W), g_gla[l].reshape(1, V_W), wconv,
            w_branch_a[l].astype(bf16), w_branch_b[l].astype(bf16), w_out[l].astype(bf16),
            g_ffn[l].reshape(1, D_MODEL), wrh, wrl, br)

        meta, metat, counts = _route(logits)
        pc, voff, goff, gstart, gsize, used = _plan(counts, nt)
        xs = _dispatch(goff, pc, voff, used, h2, metat, rows_total)
        os = _ffn(gstart, gsize, used, xs,
                  w_gate_up[l], b_gate_up[l].reshape(N_EXPERTS, 1, 2 * D_FF),
                  w_down[l], b_down[l].reshape(N_EXPERTS, 1, D_MODEL))
        x2 = _combine(goff, pc, voff, os, meta, x1, g_final.reshape(1, D_MODEL), l == depth - 1)
    return x2.reshape(batch, seq, D_MODEL)
```

```python
import functools

import jax
import jax.numpy as jnp
from jax import lax
from jax.experimental import pallas as pl
from jax.experimental.pallas import tpu as pltpu

D_MODEL = 1024
GLA_HEADS = 4
GLA_DK = 128
GLA_DV = 256
GLA_RANK = 16
GLA_TAU = 16.0
GLA_CHUNK = 64
_CHUNK_SHIFT = GLA_CHUNK.bit_length() - 1
CONV_K = 3
N_EXPERTS = 32
TOP_K = 4
D_FF = 1024
SWIGLU_LIMIT = 7.0
SWIGLU_ALPHA = 1.702
EPS = 1e-6

QK_W = GLA_HEADS * GLA_DK
V_W = GLA_HEADS * GLA_DV

LANES = 128
BF16_ROWS = 16
VMEM_LIMIT = 56 * 1024 * 1024

MIX_TS = 256
MOE_TM = 512
FFN_TM = 512
_TILE_DMA_PRIORITY = 1

_C_Q, _C_K, _C_V, _C_R = 0, QK_W, 2 * QK_W, 2 * QK_W + V_W
_C_CB = _C_R + V_W
_C_CC = _C_CB + D_MODEL
_C_CX = _C_CC + D_MODEL
_C_GA = _C_CX + D_MODEL
_C_GB = _C_GA + D_MODEL
_W_MAIN = _C_GB + D_MODEL

f32 = jnp.float32
bf16 = jnp.bfloat16


def _rms(x, g):
    return x * lax.rsqrt(jnp.mean(x * x, axis=-1, keepdims=True) + EPS) * g


def _dot(a, b):
    return jnp.dot(a, b, preferred_element_type=f32)


def _split_bf16(a):
    hi = a.astype(bf16)
    lo = (a - hi.astype(f32)).astype(bf16)
    return hi, lo


def _mixer_body(x_ref, gmix_ref, wmain_ref, walr_ref, wup_ref, balpha_ref, ggla_ref, wconv_ref,
                wa_ref, wb_ref, wo_ref, gffn_ref, wrh_ref, wrl_ref, br_ref,
                x1_ref, h2_ref, logit_ref,
                state_ref, ubuf_ref, obuf_ref):
    ts = x_ref.shape[0]
    nchunk = ts // GLA_CHUNK

    @pl.when(pl.program_id(1) == 0)
    def _():
        state_ref[...] = jnp.zeros_like(state_ref)
        ubuf_ref[0:8, :] = jnp.zeros((8, D_MODEL), f32)

    x = x_ref[...]
    hb = _rms(x, gmix_ref[...]).astype(bf16)

    def proj(c0, width):
        return _dot(hb, wmain_ref[:, c0:c0 + width])

    a_lr = _dot(hb, walr_ref[...])
    pre = _dot(a_lr.astype(bf16), wup_ref[...]) + balpha_ref[...]
    log_a = -(jnp.maximum(-pre, 0.0) + jnp.log1p(jnp.exp(-jnp.abs(pre)))) * (1.0 / GLA_TAU)
    ri = lax.broadcasted_iota(jnp.int32, (ts, ts), 0)
    ci = lax.broadcasted_iota(jnp.int32, (ts, ts), 1)
    same_chunk = (ri >> _CHUNK_SHIFT) == (ci >> _CHUNK_SHIFT)
    causal = jnp.logical_and(same_chunk, ci <= ri)
    tri = jnp.where(causal, 1.0, 0.0).astype(bf16)
    blk = jnp.where(same_chunk, 1.0, 0.0).astype(bf16)
    la_hi, la_lo = _split_bf16(log_a)
    b = _dot(tri, la_hi) + _dot(tri, la_lo)
    b_tot = _dot(blk, la_hi) + _dot(blk, la_lo)
    e_pos = jnp.exp(b)
    e_neg = jnp.exp(-b)
    e_tail = jnp.exp(b_tot - b)
    e_tot = jnp.exp(b_tot)

    q = proj(_C_Q, QK_W)
    k = proj(_C_K, QK_W)
    q_dec = (q * (GLA_DK ** -0.5) * e_pos).astype(bf16)
    k_inv = (k * e_neg).astype(bf16)
    k_tail = (k * e_tail).astype(bf16)
    vb = proj(_C_V, V_W).astype(bf16)

    for hh in range(GLA_HEADS):
        qs = slice(hh * GLA_DK, (hh + 1) * GLA_DK)
        vs = slice(hh * GLA_DV, (hh + 1) * GLA_DV)
        sc = lax.dot_general(q_dec[:, qs], k_inv[:, qs], (((1,), (1,)), ((), ())),
                             preferred_element_type=f32)
        sc = jnp.where(causal, sc, 0.0).astype(bf16)
        obuf_ref[:, vs] = _dot(sc, vb[:, vs])
        st = state_ref[hh]
        for c in range(nchunk):
            rs = slice(c * GLA_CHUNK, (c + 1) * GLA_CHUNK)
            o_inter = lax.dot_general(q_dec[rs, qs], st.astype(bf16), (((1,), (1,)), ((), ())),
                                      preferred_element_type=f32)
            obuf_ref[rs, vs] += o_inter
            upd = lax.dot_general(vb[rs, vs], k_tail[rs, qs], (((0,), (0,)), ((), ())),
                                  preferred_element_type=f32)
            st = st * e_tot[c * GLA_CHUNK:c * GLA_CHUNK + 1, qs] + upd
        state_ref[hh] = st

    r = proj(_C_R, V_W)
    ggla = ggla_ref[...]
    o_parts = []
    for hh in range(GLA_HEADS):
        vs = slice(hh * GLA_DV, (hh + 1) * GLA_DV)
        o_h = _rms(obuf_ref[:, vs], ggla[:, vs])
        r_h = r[:, vs]
        o_parts.append((o_h * (r_h * jax.nn.sigmoid(r_h))).astype(bf16))
    y_a = _dot(jnp.concatenate(o_parts, axis=1), wa_ref[...])

    u = proj(_C_CC, D_MODEL) * proj(_C_CX, D_MODEL)
    ubuf_ref[8:8 + ts, :] = u
    wc = wconv_ref[...]
    y_conv = (wc[0:1, :] * ubuf_ref[6:6 + ts, :] + wc[1:2, :] * ubuf_ref[7:7 + ts, :]
              + wc[2:3, :] * u)
    ubuf_ref[0:8, :] = ubuf_ref[ts:ts + 8, :]
    y_b = _dot((proj(_C_CB, D_MODEL) * y_conv).astype(bf16), wb_ref[...])

    mixed = (jax.nn.sigmoid(proj(_C_GA, D_MODEL)) * y_a
             + jax.nn.sigmoid(proj(_C_GB, D_MODEL)) * y_b)
    x1 = x + _dot(mixed.astype(bf16), wo_ref[...])
    x1_ref[...] = x1

    h2 = _rms(x1, gffn_ref[...])
    h2_hi, h2_lo = _split_bf16(h2)
    h2_ref[...] = h2_hi
    wrh = wrh_ref[...]
    logit_ref[...] = (_dot(h2_hi, wrh) + _dot(h2_lo, wrh) + _dot(h2_hi, wrl_ref[...])
                      + br_ref[...])


def _mixer(x2, batch, seq, gmix, wmain, walr, wup, balpha, ggla, wconv, wa, wb, wo, gffn, wrh, wrl, br):
    ts = min(MIX_TS, seq)
    ns = seq // ts
    tokens = batch * seq

    def const(shape):
        return pl.BlockSpec(shape, lambda b, s: (0,) * len(shape), pipeline_mode=pl.Buffered(1))

    def row(width, dtype):
        return pl.BlockSpec((ts, width), lambda b, s: (b * ns + s, 0))

    return pl.pallas_call(
        _mixer_body,
        grid=(batch, ns),
        in_specs=[row(D_MODEL, f32), const(gmix.shape), const(wmain.shape), const(walr.shape),
                  const(wup.shape), const(balpha.shape), const(ggla.shape), const(wconv.shape),
                  const(wa.shape), const(wb.shape), const(wo.shape), const(gffn.shape),
                  const(wrh.shape), const(wrl.shape), const(br.shape)],
        out_specs=[row(D_MODEL, f32), row(D_MODEL, bf16), row(LANES, f32)],
        out_shape=[jax.ShapeDtypeStruct((tokens, D_MODEL), f32),
                   jax.ShapeDtypeStruct((tokens, D_MODEL), bf16),
                   jax.ShapeDtypeStruct((tokens, LANES), f32)],
        scratch_shapes=[pltpu.VMEM((GLA_HEADS, GLA_DV, GLA_DK), f32),
                        pltpu.VMEM((ts + 8, D_MODEL), f32),
                        pltpu.VMEM((ts, V_W), f32)],
        compiler_params=pltpu.CompilerParams(
            dimension_semantics=("parallel", "arbitrary"), vmem_limit_bytes=VMEM_LIMIT),
        name="mixer",
    )(x2, gmix, wmain, walr, wup, balpha, ggla, wconv, wa, wb, wo, gffn, wrh, wrl, br)


def _route_body(logit_ref, meta_ref, metat_ref, cnt_ref):
    tm = logit_ref.shape[0]
    lane = lax.broadcasted_iota(jnp.int32, (tm, LANES), 1)
    lane_f = lane.astype(f32)
    lg = jnp.where(lane < N_EXPERTS, logit_ref[...], -jnp.inf)
    sels, tops = [], []
    for _ in range(TOP_K):
        m = jnp.max(lg, axis=-1, keepdims=True)
        first = jnp.min(jnp.where(lg == m, lane_f, float(LANES)), axis=-1, keepdims=True)
        sel = lane_f == first
        sels.append(sel)
        tops.append(m)
        lg = jnp.where(sel, -jnp.inf, lg)
    ps = [jnp.exp(t - tops[0]) for t in tops]
    denom = ps[0] + ps[1] + ps[2] + ps[3]
    gates = [p / denom for p in ps]

    onehot = jnp.zeros((tm, LANES), f32)
    for sel in sels:
        onehot = onehot + jnp.where(sel, 1.0, 0.0)
    onehot_b = onehot.astype(bf16)
    ri = lax.broadcasted_iota(jnp.int32, (tm, tm), 0)
    ci = lax.broadcasted_iota(jnp.int32, (tm, tm), 1)
    strict_lower = jnp.where(ci < ri, 1.0, 0.0).astype(bf16)
    rank = _dot(strict_lower, onehot_b)
    counts = _dot(jnp.ones((8, tm), bf16), onehot_b)
    blocks = jnp.floor((counts + (BF16_ROWS - 1)) * (1.0 / BF16_ROWS))
    ui = lax.broadcasted_iota(jnp.int32, (LANES, LANES), 0)
    uj = lax.broadcasted_iota(jnp.int32, (LANES, LANES), 1)
    strict_upper = jnp.where(ui < uj, 1.0, 0.0).astype(bf16)
    voff = _dot(blocks.astype(bf16), strict_upper) * float(BF16_ROWS)
    slot = rank + voff[0:1, :]

    meta = jnp.zeros((tm, LANES), f32)
    for kk in range(TOP_K):
        lpos = jnp.sum(jnp.where(sels[kk], slot, 0.0), axis=-1, keepdims=True)
        meta = jnp.where(lane == kk, lpos, meta)
        meta = jnp.where(lane == TOP_K + kk, gates[kk], meta)
    meta_ref[...] = meta
    metat_ref[...] = meta.T[0:8, :]
    cnt_ref[...] = counts.astype(jnp.int32)


def _route(logits):
    tokens = logits.shape[0]
    tm = min(MOE_TM, tokens)
    nt = tokens // tm
    return pl.pallas_call(
        _route_body,
        grid=(nt,),
        in_specs=[pl.BlockSpec((tm, LANES), lambda i: (i, 0))],
        out_specs=[pl.BlockSpec((tm, LANES), lambda i: (i, 0)),
                   pl.BlockSpec((8, tm), lambda i: (i, 0)),
                   pl.BlockSpec((8, LANES), lambda i: (i, 0))],
        out_shape=[jax.ShapeDtypeStruct((tokens, LANES), f32),
                   jax.ShapeDtypeStruct((nt * 8, tm), f32),
                   jax.ShapeDtypeStruct((nt * 8, LANES), jnp.int32)],
        compiler_params=pltpu.CompilerParams(
            dimension_semantics=("parallel",), vmem_limit_bytes=VMEM_LIMIT),
        name="route",
    )(logits)


def _lmax(tm):
    full = tm * TOP_K + N_EXPERTS * BF16_ROWS
    return -(-full // 256) * 256


def _segment_copy(src, dst, sem, src_off, dst_off, rows):
    src_off = pl.multiple_of(src_off, BF16_ROWS)
    dst_off = pl.multiple_of(dst_off, BF16_ROWS)
    rows = pl.multiple_of(rows, BF16_ROWS)
    return pltpu.make_async_copy(src.at[pl.ds(src_off, rows)], dst.at[pl.ds(dst_off, rows)], sem)


def _wait_segments(src, dst, sem, pc_ref, voff_ref, i):
    last = i * N_EXPERTS + N_EXPERTS - 1
    total = voff_ref[last] + pc_ref[last]

    @pl.when(total > 0)
    def _():
        _segment_copy(src, dst, sem, 0, 0, total).wait()


def _zero_rows(zero_ref, dst_ref, sem, first, start):
    tile = zero_ref.shape[0]
    n = dst_ref.shape[0] - first
    n_full = n // tile

    def full(j):
        return _segment_copy(zero_ref, dst_ref, sem, 0, first + j * tile, tile)

    def rest():
        return _segment_copy(zero_ref, dst_ref, sem, 0, first + n_full * tile, n - n_full * tile)

    def step(j, carry):
        full(j).start() if start else full(j).wait()
        return carry

    lax.fori_loop(0, n_full, step, 0)

    @pl.when(n - n_full * tile > 0)
    def _():
        rest().start() if start else rest().wait()


def _dispatch_body(goff_ref, pc_ref, voff_ref, used_ref,
                   h2_ref, metat_ref, xs_ref, xc_ref, zero_ref, sem, zsem):
    i = pl.program_id(0)
    nt = pl.num_programs(0)
    tm = h2_ref.shape[0]
    lmax = xc_ref.shape[0]

    slots = metat_ref[0:TOP_K, :].astype(jnp.int32)
    row = lax.broadcasted_iota(jnp.int32, (lmax, tm), 0)
    pt = jnp.zeros((lmax, tm), f32)
    for kk in range(TOP_K):
        pt = jnp.where(row == slots[kk:kk + 1, :], 1.0, pt)
    xc_ref[...] = _dot(pt.astype(bf16), h2_ref[...]).astype(bf16)

    def seg(e):
        j = i * N_EXPERTS + e
        return _segment_copy(xc_ref, xs_ref, sem, voff_ref[j], goff_ref[j], pc_ref[j])

    for e in range(N_EXPERTS):
        @pl.when(pc_ref[i * N_EXPERTS + e] > 0)
        def _():
            seg(e).start()

    @pl.when(i == 0)
    def _():
        zero_ref[...] = jnp.zeros_like(zero_ref)
        _zero_rows(zero_ref, xs_ref, zsem, used_ref[0], start=True)

    @pl.when(i == nt - 1)
    def _():
        _zero_rows(zero_ref, xs_ref, zsem, used_ref[0], start=False)

    _wait_segments(xc_ref, xs_ref, sem, pc_ref, voff_ref, i)


def _dispatch(goff, pc, voff, used, h2, metat, rows_total):
    tokens = h2.shape[0]
    tm = min(MOE_TM, tokens)
    nt = tokens // tm
    return pl.pallas_call(
        _dispatch_body,
        grid_spec=pltpu.PrefetchScalarGridSpec(
            num_scalar_prefetch=4,
            grid=(nt,),
            in_specs=[pl.BlockSpec((tm, D_MODEL), lambda i, *_: (i, 0)),
                      pl.BlockSpec((8, tm), lambda i, *_: (i, 0))],
            out_specs=pl.BlockSpec(memory_space=pl.ANY),
            scratch_shapes=[pltpu.VMEM((_lmax(tm), D_MODEL), bf16),
                            pltpu.VMEM((FFN_TM, D_MODEL), bf16),
                            pltpu.SemaphoreType.DMA, pltpu.SemaphoreType.DMA],
        ),
        out_shape=jax.ShapeDtypeStruct((rows_total, D_MODEL), bf16),
        compiler_params=pltpu.CompilerParams(
            dimension_semantics=("arbitrary",), vmem_limit_bytes=VMEM_LIMIT),
        name="dispatch",
    )(goff, pc, voff, used, h2, metat)


def _ffn_body(gstart_ref, gsize_ref, used_ref, xs_ref, wgu_ref, bgu_ref, wd_ref, bd_ref, os_ref,
              xbuf_ref, obuf_ref, zero_ref, wgub_ref, wdb_ref, state_ref, isem, osem, zsem):
    e = pl.program_id(0)
    ne = pl.num_programs(0)
    tile = xbuf_ref.shape[1]
    base = gstart_ref[e]
    n = gsize_ref[e]
    n_tiles = (n + tile - 1) // tile

    def rows_of(size, t):
        return jnp.minimum(tile, size - t * tile)

    def in_copy(row0, rows, slot):
        return pltpu.make_async_copy(xs_ref.at[pl.ds(row0, rows)],
                                     xbuf_ref.at[slot, pl.ds(0, rows)], isem.at[slot])

    def out_copy(row0, rows, slot):
        return pltpu.make_async_copy(obuf_ref.at[slot, pl.ds(0, rows)],
                                     os_ref.at[pl.ds(row0, rows)], osem.at[slot])

    def start_tile(copy, row0, rows, slot):
        row0 = pl.multiple_of(row0, BF16_ROWS)

        @pl.when(rows == tile)
        def _():
            copy(row0, tile, slot).start(priority=_TILE_DMA_PRIORITY)

        @pl.when(rows < tile)
        def _():
            copy(row0, pl.multiple_of(rows, BF16_ROWS), slot).start()

    def wait_tile(copy, rows, slot):
        copy(0, pl.multiple_of(rows, BF16_ROWS), slot).wait()

    @pl.when(e == 0)
    def _():
        xbuf_ref[...] = jnp.zeros_like(xbuf_ref)
        zero_ref[...] = jnp.zeros_like(zero_ref)
        _zero_rows(zero_ref, os_ref, zsem, used_ref[0], start=True)
        state_ref[0] = 0
        state_ref[1] = 0
        state_ref[2] = 0
        state_ref[3] = 0

    done = state_ref[0]

    @pl.when(jnp.logical_and(n_tiles > 0, state_ref[1] == 0))
    def _():
        start_tile(in_copy, base, rows_of(n, 0), done & 1)

    nxt = jnp.minimum(e + 1, ne - 1)
    has_next = jnp.logical_and(e + 1 < ne, gsize_ref[nxt] > 0)

    @pl.when(n_tiles > 0)
    def _():
        chunk = 128
        for r in range(0, D_MODEL, chunk):
            wgub_ref[r:r + chunk, :] = wgu_ref[r:r + chunk, :].astype(bf16)
        for r in range(0, D_FF, chunk):
            wdb_ref[r:r + chunk, :] = wd_ref[r:r + chunk, :].astype(bf16)

    def step(t, carry):
        slot = (done + t) & 1

        @pl.when(t + 1 < n_tiles)
        def _():
            start_tile(in_copy, base + (t + 1) * tile, rows_of(n, t + 1), 1 - slot)

        @pl.when(jnp.logical_and(t + 1 == n_tiles, has_next))
        def _():
            start_tile(in_copy, gstart_ref[nxt], rows_of(gsize_ref[nxt], 0), 1 - slot)

        rows = rows_of(n, t)
        wait_tile(in_copy, rows, slot)

        @pl.when(state_ref[2 + slot] > 0)
        def _():
            wait_tile(out_copy, state_ref[2 + slot], slot)

        def ffn(m):
            gu = _dot(xbuf_ref[slot, 0:m, :], wgub_ref[...]) + bgu_ref[...]
            gate = jnp.minimum(gu[:, :D_FF], SWIGLU_LIMIT)
            lin = jnp.clip(gu[:, D_FF:], -SWIGLU_LIMIT, SWIGLU_LIMIT)
            act = (lin + 1.0) * (gate * jax.nn.sigmoid(SWIGLU_ALPHA * gate))
            out = _dot(act.astype(bf16), wdb_ref[...]) + bd_ref[...]
            obuf_ref[slot, 0:m, :] = out.astype(bf16)

        @pl.when(rows > tile // 2)
        def _():
            ffn(tile)

        @pl.when(rows <= tile // 2)
        def _():
            ffn(tile // 2)

        start_tile(out_copy, base + t * tile, rows, slot)
        state_ref[2 + slot] = rows
        return carry

    lax.fori_loop(0, n_tiles, step, 0)

    @pl.when(n_tiles > 0)
    def _():
        state_ref[0] = done + n_tiles
        state_ref[1] = has_next.astype(jnp.int32)

    @pl.when(e == ne - 1)
    def _():
        for slot in range(2):
            @pl.when(state_ref[2 + slot] > 0)
            def _():
                wait_tile(out_copy, state_ref[2 + slot], slot)
        _zero_rows(zero_ref, os_ref, zsem, used_ref[0], start=False)


def _ffn(gstart, gsize, used, xs, wgu, bgu, wd, bd):
    rows_total = xs.shape[0]
    return pl.pallas_call(
        _ffn_body,
        grid_spec=pltpu.PrefetchScalarGridSpec(
            num_scalar_prefetch=3,
            grid=(N_EXPERTS,),
            in_specs=[pl.BlockSpec(memory_space=pl.ANY),
                      pl.BlockSpec((None, D_MODEL, 2 * D_FF), lambda e, *_: (e, 0, 0)),
                      pl.BlockSpec((None, 1, 2 * D_FF), lambda e, *_: (e, 0, 0)),
                      pl.BlockSpec((None, D_FF, D_MODEL), lambda e, *_: (e, 0, 0)),
                      pl.BlockSpec((None, 1, D_MODEL), lambda e, *_: (e, 0, 0))],
            out_specs=pl.BlockSpec(memory_space=pl.ANY),
            scratch_shapes=[pltpu.VMEM((2, FFN_TM, D_MODEL), bf16),
                            pltpu.VMEM((2, FFN_TM, D_MODEL), bf16),
                            pltpu.VMEM((FFN_TM, D_MODEL), bf16),
                            pltpu.VMEM((D_MODEL, 2 * D_FF), bf16),
                            pltpu.VMEM((D_FF, D_MODEL), bf16),
                            pltpu.SMEM((4,), jnp.int32),
                            pltpu.SemaphoreType.DMA((2,)), pltpu.SemaphoreType.DMA((2,)),
                            pltpu.SemaphoreType.DMA],
        ),
        out_shape=jax.ShapeDtypeStruct((rows_total, D_MODEL), bf16),
        compiler_params=pltpu.CompilerParams(
            dimension_semantics=("arbitrary",), vmem_limit_bytes=VMEM_LIMIT),
        name="experts",
    )(gstart, gsize, used, xs, wgu, bgu, wd, bd)


def _combine_body(final_norm, goff_ref, pc_ref, voff_ref, os_ref, meta_ref, x1_ref, gfin_ref, out_ref,
                  oc_ref, sem):
    i = pl.program_id(0)
    tm = x1_ref.shape[0]
    lmax = oc_ref.shape[0]

    @pl.when(i == 0)
    def _():
        oc_ref[...] = jnp.zeros_like(oc_ref)

    def seg(e):
        j = i * N_EXPERTS + e
        return _segment_copy(os_ref, oc_ref, sem, goff_ref[j], voff_ref[j], pc_ref[j])

    for e in range(N_EXPERTS):
        @pl.when(pc_ref[i * N_EXPERTS + e] > 0)
        def _():
            seg(e).start()

    meta = meta_ref[...]
    col = lax.broadcasted_iota(jnp.int32, (tm, lmax), 1)
    p = jnp.zeros((tm, lmax), f32)
    for kk in range(TOP_K):
        slot = meta[:, kk:kk + 1].astype(jnp.int32)
        p = jnp.where(col == slot, meta[:, TOP_K + kk:TOP_K + kk + 1], p)
    pb = p.astype(bf16)

    _wait_segments(os_ref, oc_ref, sem, pc_ref, voff_ref, i)

    xo = x1_ref[...] + _dot(pb, oc_ref[...])
    out_ref[...] = _rms(xo, gfin_ref[...]) if final_norm else xo


def _combine(goff, pc, voff, os, meta, x1, gfin, final_norm):
    tokens = x1.shape[0]
    tm = min(MOE_TM, tokens)
    nt = tokens // tm
    return pl.pallas_call(
        functools.partial(_combine_body, final_norm),
        grid_spec=pltpu.PrefetchScalarGridSpec(
            num_scalar_prefetch=3,
            grid=(nt,),
            in_specs=[pl.BlockSpec(memory_space=pl.ANY),
                      pl.BlockSpec((tm, LANES), lambda i, *_: (i, 0)),
                      pl.BlockSpec((tm, D_MODEL), lambda i, *_: (i, 0)),
                      pl.BlockSpec((1, D_MODEL), lambda i, *_: (0, 0))],
            out_specs=pl.BlockSpec((tm, D_MODEL), lambda i, *_: (i, 0)),
            scratch_shapes=[pltpu.VMEM((_lmax(tm), D_MODEL), bf16), pltpu.SemaphoreType.DMA],
        ),
        out_shape=jax.ShapeDtypeStruct((tokens, D_MODEL), f32),
        compiler_params=pltpu.CompilerParams(
            dimension_semantics=("arbitrary",), vmem_limit_bytes=VMEM_LIMIT),
        name="combine",
    )(goff, pc, voff, os, meta, x1, gfin)


def _plan(counts, nt):
    c = counts.reshape(nt, 8, LANES)[:, 0, :N_EXPERTS]
    pc = (c + (BF16_ROWS - 1)) // BF16_ROWS * BF16_ROWS
    voff = jnp.cumsum(pc, axis=1) - pc
    gsize = jnp.sum(pc, axis=0)
    gend = jnp.cumsum(gsize)
    gstart = gend - gsize
    goff = gstart[None, :] + jnp.cumsum(pc, axis=0) - pc
    i32 = lambda a: a.reshape(-1).astype(jnp.int32)
    return i32(pc), i32(voff), i32(goff), i32(gstart), i32(gsize), i32(gend[-1:])


def kernel(x, g_mix, w_in, w_alpha_up, b_alpha, g_gla, w_conv, w_branch_a, w_branch_b, w_out, g_ffn, w_router, b_router, w_gate_up, b_gate_up, w_down, b_down, g_final):
    batch, seq, _ = x.shape
    tokens = batch * seq
    depth = w_in.shape[0]
    x2 = x.reshape(tokens, D_MODEL)
    tm = min(MOE_TM, tokens)
    nt = tokens // tm
    rows_total = tokens * TOP_K + nt * N_EXPERTS * (BF16_ROWS - 1)
    rows_total = -(-rows_total // BF16_ROWS) * BF16_ROWS

    for l in range(depth):
        wi = w_in[l]
        a0 = 2 * QK_W + 2 * V_W
        wmain = jnp.concatenate([wi[:, :a0], wi[:, a0 + GLA_RANK:]], axis=1).astype(bf16)
        walr = jnp.pad(wi[:, a0:a0 + GLA_RANK], ((0, 0), (0, LANES - GLA_RANK))).astype(bf16)
        wup = jnp.pad(w_alpha_up[l], ((0, LANES - GLA_RANK), (0, 0))).astype(bf16)
        wr = jnp.pad(w_router[l], ((0, 0), (0, LANES - N_EXPERTS)))
        wrh = wr.astype(bf16)
        wrl = (wr - wrh.astype(f32)).astype(bf16)
        br = jnp.pad(b_router[l], (0, LANES - N_EXPERTS)).reshape(1, LANES)
        wconv = jnp.pad(w_conv[l], ((0, 8 - CONV_K), (0, 0)))

        x1, h2, logits = _mixer(
            x2, batch, seq, g_mix[l].reshape(1, D_MODEL), wmain, walr, wup,
            b_alpha[l].reshape(1, QK_W), g_gla[l].reshape(1, V_W), wconv,
            w_branch_a[l].astype(bf16), w_branch_b[l].astype(bf16), w_out[l].astype(bf16),
            g_ffn[l].reshape(1, D_MODEL), wrh, wrl, br)

        meta, metat, counts = _route(logits)
        pc, voff, goff, gstart, gsize, used = _plan(counts, nt)
        xs = _dispatch(goff, pc, voff, used, h2, metat, rows_total)
        os = _ffn(gstart, gsize, used, xs,
                  w_gate_up[l], b_gate_up[l].reshape(N_EXPERTS, 1, 2 * D_FF),
                  w_down[l], b_down[l].reshape(N_EXPERTS, 1, D_MODEL))
        x2 = _combine(goff, pc, voff, os, meta, x1, g_final.reshape(1, D_MODEL), l == depth - 1)
    return x2.reshape(batch, seq, D_MODEL)
```

```python
import functools

import jax
import jax.numpy as jnp
from jax import lax
from jax.experimental import pallas as pl
from jax.experimental.pallas import tpu as pltpu

D_MODEL = 1024
GLA_HEADS = 4
GLA_DK = 128
GLA_DV = 256
GLA_RANK = 16
GLA_TAU = 16.0
GLA_CHUNK = 64
_CHUNK_SHIFT = GLA_CHUNK.bit_length() - 1
CONV_K = 3
N_EXPERTS = 32
TOP_K = 4
D_FF = 1024
SWIGLU_LIMIT = 7.0
SWIGLU_ALPHA = 1.702
EPS = 1e-6

QK_W = GLA_HEADS * GLA_DK
V_W = GLA_HEADS * GLA_DV

LANES = 128
BF16_ROWS = 16
VMEM_LIMIT = 56 * 1024 * 1024

MIX_TS = 256
FFN_TM = 512
_META_EXPERT, _META_GATE, _META_RANK = 0, TOP_K, 2 * TOP_K
_TILE_DMA_PRIORITY = 1

_C_Q, _C_K, _C_V, _C_R = 0, QK_W, 2 * QK_W, 2 * QK_W + V_W
_C_CB = _C_R + V_W
_C_CC = _C_CB + D_MODEL
_C_CX = _C_CC + D_MODEL
_C_GA = _C_CX + D_MODEL
_C_GB = _C_GA + D_MODEL
_W_MAIN = _C_GB + D_MODEL

f32 = jnp.float32
bf16 = jnp.bfloat16


def _rms(x, g):
    return x * lax.rsqrt(jnp.mean(x * x, axis=-1, keepdims=True) + EPS) * g


def _dot(a, b):
    return jnp.dot(a, b, preferred_element_type=f32)


def _split_bf16(a):
    hi = a.astype(bf16)
    lo = (a - hi.astype(f32)).astype(bf16)
    return hi, lo


def _mixer_body(x_ref, gmix_ref, wmain_ref, walr_ref, wup_ref, balpha_ref, ggla_ref, wconv_ref,
                wa_ref, wb_ref, wo_ref, gffn_ref, wrh_ref, wrl_ref, br_ref,
                x1_ref, h2_ref, meta_ref, cnt_ref,
                state_ref, ubuf_ref, obuf_ref):
    ts = x_ref.shape[0]
    nchunk = ts // GLA_CHUNK

    @pl.when(pl.program_id(1) == 0)
    def _():
        state_ref[...] = jnp.zeros_like(state_ref)
        ubuf_ref[0:8, :] = jnp.zeros((8, D_MODEL), f32)

    x = x_ref[...]
    hb = _rms(x, gmix_ref[...]).astype(bf16)

    def proj(c0, width):
        return _dot(hb, wmain_ref[:, c0:c0 + width])

    a_lr = _dot(hb, walr_ref[...])
    pre = _dot(a_lr.astype(bf16), wup_ref[...]) + balpha_ref[...]
    log_a = -(jnp.maximum(-pre, 0.0) + jnp.log1p(jnp.exp(-jnp.abs(pre)))) * (1.0 / GLA_TAU)
    ri = lax.broadcasted_iota(jnp.int32, (ts, ts), 0)
    ci = lax.broadcasted_iota(jnp.int32, (ts, ts), 1)
    same_chunk = (ri >> _CHUNK_SHIFT) == (ci >> _CHUNK_SHIFT)
    causal = jnp.logical_and(same_chunk, ci <= ri)
    tri = jnp.where(causal, 1.0, 0.0).astype(bf16)
    blk = jnp.where(same_chunk, 1.0, 0.0).astype(bf16)
    la_hi, la_lo = _split_bf16(log_a)
    b = _dot(tri, la_hi) + _dot(tri, la_lo)
    b_tot = _dot(blk, la_hi) + _dot(blk, la_lo)
    e_pos = jnp.exp(b)
    e_neg = jnp.exp(-b)
    e_tail = jnp.exp(b_tot - b)
    e_tot = jnp.exp(b_tot)

    q = proj(_C_Q, QK_W)
    k = proj(_C_K, QK_W)
    q_dec = (q * (GLA_DK ** -0.5) * e_pos).astype(bf16)
    k_inv = (k * e_neg).astype(bf16)
    k_tail = (k * e_tail).astype(bf16)
    vb = proj(_C_V, V_W).astype(bf16)

    for hh in range(GLA_HEADS):
        qs = slice(hh * GLA_DK, (hh + 1) * GLA_DK)
        vs = slice(hh * GLA_DV, (hh + 1) * GLA_DV)
        sc = lax.dot_general(q_dec[:, qs], k_inv[:, qs], (((1,), (1,)), ((), ())),
                             preferred_element_type=f32)
        sc = jnp.where(causal, sc, 0.0).astype(bf16)
        obuf_ref[:, vs] = _dot(sc, vb[:, vs])
        st = state_ref[hh]
        for c in range(nchunk):
            rs = slice(c * GLA_CHUNK, (c + 1) * GLA_CHUNK)
            o_inter = lax.dot_general(q_dec[rs, qs], st.astype(bf16), (((1,), (1,)), ((), ())),
                                      preferred_element_type=f32)
            obuf_ref[rs, vs] += o_inter
            upd = lax.dot_general(vb[rs, vs], k_tail[rs, qs], (((0,), (0,)), ((), ())),
                                  preferred_element_type=f32)
            st = st * e_tot[c * GLA_CHUNK:c * GLA_CHUNK + 1, qs] + upd
        state_ref[hh] = st

    r = proj(_C_R, V_W)
    ggla = ggla_ref[...]
    o_parts = []
    for hh in range(GLA_HEADS):
        vs = slice(hh * GLA_DV, (hh + 1) * GLA_DV)
        o_h = _rms(obuf_ref[:, vs], ggla[:, vs])
        r_h = r[:, vs]
        o_parts.append((o_h * (r_h * jax.nn.sigmoid(r_h))).astype(bf16))
    y_a = _dot(jnp.concatenate(o_parts, axis=1), wa_ref[...])

    u = proj(_C_CC, D_MODEL) * proj(_C_CX, D_MODEL)
    ubuf_ref[8:8 + ts, :] = u
    wc = wconv_ref[...]
    y_conv = (wc[0:1, :] * ubuf_ref[6:6 + ts, :] + wc[1:2, :] * ubuf_ref[7:7 + ts, :]
              + wc[2:3, :] * u)
    ubuf_ref[0:8, :] = ubuf_ref[ts:ts + 8, :]
    y_b = _dot((proj(_C_CB, D_MODEL) * y_conv).astype(bf16), wb_ref[...])

    mixed = (jax.nn.sigmoid(proj(_C_GA, D_MODEL)) * y_a
             + jax.nn.sigmoid(proj(_C_GB, D_MODEL)) * y_b)
    x1 = x + _dot(mixed.astype(bf16), wo_ref[...])
    x1_ref[...] = x1

    h2 = _rms(x1, gffn_ref[...])
    h2_hi, h2_lo = _split_bf16(h2)
    h2_ref[...] = h2_hi
    wrh = wrh_ref[...]
    logits = _dot(h2_hi, wrh) + _dot(h2_lo, wrh) + _dot(h2_hi, wrl_ref[...]) + br_ref[...]

    lane = lax.broadcasted_iota(jnp.int32, (ts, LANES), 1)
    lane_f = lane.astype(f32)
    lg = jnp.where(lane < N_EXPERTS, logits, -jnp.inf)
    sels, tops, firsts = [], [], []
    for _ in range(TOP_K):
        m = jnp.max(lg, axis=-1, keepdims=True)
        first = jnp.min(jnp.where(lg == m, lane_f, float(LANES)), axis=-1, keepdims=True)
        sel = lane_f == first
        sels.append(sel)
        tops.append(m)
        firsts.append(first)
        lg = jnp.where(sel, -jnp.inf, lg)
    ps = [jnp.exp(t - tops[0]) for t in tops]
    denom = ps[0] + ps[1] + ps[2] + ps[3]
    onehot = jnp.zeros((ts, LANES), f32)
    for sel in sels:
        onehot = onehot + jnp.where(sel, 1.0, 0.0)
    onehot_b = onehot.astype(bf16)
    rank = _dot(jnp.where(ci < ri, 1.0, 0.0).astype(bf16), onehot_b)
    cnt_ref[...] = _dot(jnp.ones((8, ts), bf16), onehot_b).astype(jnp.int32)
    meta = jnp.zeros((ts, LANES), f32)
    for kk in range(TOP_K):
        meta = jnp.where(lane == _META_EXPERT + kk, firsts[kk], meta)
        meta = jnp.where(lane == _META_GATE + kk, ps[kk] / denom, meta)
        rank_k = jnp.sum(jnp.where(sels[kk], rank, 0.0), axis=-1, keepdims=True)
        meta = jnp.where(lane == _META_RANK + kk, rank_k, meta)
    meta_ref[...] = meta


def _mixer(x2, batch, seq, gmix, wmain, walr, wup, balpha, ggla, wconv, wa, wb, wo, gffn, wrh, wrl, br):
    ts = min(MIX_TS, seq)
    ns = seq // ts
    tokens = batch * seq

    def const(shape):
        return pl.BlockSpec(shape, lambda b, s: (0,) * len(shape), pipeline_mode=pl.Buffered(1))

    def row(width, dtype):
        return pl.BlockSpec((ts, width), lambda b, s: (b * ns + s, 0))

    return pl.pallas_call(
        _mixer_body,
        grid=(batch, ns),
        in_specs=[row(D_MODEL, f32), const(gmix.shape), const(wmain.shape), const(walr.shape),
                  const(wup.shape), const(balpha.shape), const(ggla.shape), const(wconv.shape),
                  const(wa.shape), const(wb.shape), const(wo.shape), const(gffn.shape),
                  const(wrh.shape), const(wrl.shape), const(br.shape)],
        out_specs=[row(D_MODEL, f32), row(D_MODEL, bf16), row(LANES, f32),
                   pl.BlockSpec((8, LANES), lambda b, s: (b * ns + s, 0))],
        out_shape=[jax.ShapeDtypeStruct((tokens, D_MODEL), f32),
                   jax.ShapeDtypeStruct((tokens, D_MODEL), bf16),
                   jax.ShapeDtypeStruct((tokens, LANES), f32),
                   jax.ShapeDtypeStruct((batch * ns * 8, LANES), jnp.int32)],
        scratch_shapes=[pltpu.VMEM((GLA_HEADS, GLA_DV, GLA_DK), f32),
                        pltpu.VMEM((ts + 8, D_MODEL), f32),
                        pltpu.VMEM((ts, V_W), f32)],
        compiler_params=pltpu.CompilerParams(
            dimension_semantics=("parallel", "arbitrary"), vmem_limit_bytes=VMEM_LIMIT),
        name="mixer",
    )(x2, gmix, wmain, walr, wup, balpha, ggla, wconv, wa, wb, wo, gffn, wrh, wrl, br)


def _lmax(tm):
    full = tm * TOP_K + N_EXPERTS * 2 * (BF16_ROWS - 1)
    return -(-full // 256) * 256


def _token_slots(meta, base_row):
    lane_f = lax.broadcasted_iota(jnp.int32, meta.shape, 1).astype(f32)
    slots = []
    for kk in range(TOP_K):
        sel = lane_f == meta[:, _META_EXPERT + kk:_META_EXPERT + kk + 1]
        first = jnp.sum(jnp.where(sel, base_row, 0.0), axis=-1, keepdims=True)
        slots.append(first + meta[:, _META_RANK + kk:_META_RANK + kk + 1])
    return slots


def _segment_copy(src, dst, sem, src_off, dst_off, rows):
    src_off = pl.multiple_of(src_off, BF16_ROWS)
    dst_off = pl.multiple_of(dst_off, BF16_ROWS)
    rows = pl.multiple_of(rows, BF16_ROWS)
    return pltpu.make_async_copy(src.at[pl.ds(src_off, rows)], dst.at[pl.ds(dst_off, rows)], sem)


def _wait_rows(src, dst, sem, total):
    @pl.when(total > 0)
    def _():
        _segment_copy(src, dst, sem, 0, 0, total).wait()


def _zero_rows(zero_ref, dst_ref, sem, first, start):
    tile = zero_ref.shape[0]
    n = dst_ref.shape[0] - first
    n_full = n // tile

    def full(j):
        return _segment_copy(zero_ref, dst_ref, sem, 0, first + j * tile, tile)

    def rest():
        return _segment_copy(zero_ref, dst_ref, sem, 0, first + n_full * tile, n - n_full * tile)

    def step(j, carry):
        full(j).start() if start else full(j).wait()
        return carry

    lax.fori_loop(0, n_full, step, 0)

    @pl.when(n - n_full * tile > 0)
    def _():
        rest().start() if start else rest().wait()


def _dispatch_body(voff_ref, coff_ref, keep_ref, row0_ref, wlen_ref, wtot_ref, used_ref,
                   h2_ref, meta_ref, base_ref, xs_ref, xc_ref, carry_ref, zero_ref, sem, zsem):
    i = pl.program_id(0)
    nt = pl.num_programs(0)
    tm = h2_ref.shape[0]
    lmax = xc_ref.shape[1]
    slot = i & 1
    xc = xc_ref.at[slot]

    @pl.when(i == 0)
    def _():
        carry_ref[...] = jnp.zeros_like(carry_ref)
        zero_ref[...] = jnp.zeros_like(zero_ref)
        _zero_rows(zero_ref, xs_ref, zsem, used_ref[0], start=True)

    @pl.when(i >= 2)
    def _():
        _wait_rows(xc, xs_ref, sem.at[slot], wtot_ref[jnp.maximum(i - 2, 0)])

    meta = meta_ref[...]
    lane = lax.broadcasted_iota(jnp.int32, meta.shape, 1)
    slots = jnp.zeros(meta.shape, f32)
    for kk, s in enumerate(_token_slots(meta, base_ref[0:1, :])):
        slots = jnp.where(lane == kk, s, slots)
    slots_t = slots.T[0:8, :].astype(jnp.int32)
    row = lax.broadcasted_iota(jnp.int32, (lmax, tm), 0)
    pt = jnp.zeros((lmax, tm), f32)
    for kk in range(TOP_K):
        pt = jnp.where(row == slots_t[kk:kk + 1, :], 1.0, pt)
    xc[...] = _dot(pt.astype(bf16), h2_ref[...]).astype(bf16)

    def block(off):
        return xc.at[pl.ds(pl.multiple_of(off, BF16_ROWS), BF16_ROWS)]

    for e in range(N_EXPERTS):
        j = i * N_EXPERTS + e
        blk = block(voff_ref[j])
        blk[...] = blk[...] + carry_ref[e]
    for e in range(N_EXPERTS):
        j = i * N_EXPERTS + e
        tail = block(coff_ref[j])[...]
        carry_ref[e] = jnp.where(keep_ref[j] > 0, tail, jnp.zeros_like(tail))

    for e in range(N_EXPERTS):
        j = i * N_EXPERTS + e

        @pl.when(wlen_ref[j] > 0)
        def _():
            _segment_copy(xc, xs_ref, sem.at[slot], voff_ref[j], row0_ref[j], wlen_ref[j]).start()

    @pl.when(i == nt - 1)
    def _():
        @pl.when(i >= 1)
        def _():
            _wait_rows(xc, xs_ref, sem.at[1 - slot], wtot_ref[jnp.maximum(i - 1, 0)])
        _wait_rows(xc, xs_ref, sem.at[slot], wtot_ref[i])
        _zero_rows(zero_ref, xs_ref, zsem, used_ref[0], start=False)


def _dispatch(plan, h2, meta, rows_total):
    tokens = h2.shape[0]
    tm = min(MIX_TS, tokens)
    nt = tokens // tm
    scalars = (plan["voff"], plan["coff"], plan["keep"], plan["row0"], plan["wlen"], plan["wtot"],
               plan["used"])
    return pl.pallas_call(
        _dispatch_body,
        grid_spec=pltpu.PrefetchScalarGridSpec(
            num_scalar_prefetch=len(scalars),
            grid=(nt,),
            in_specs=[pl.BlockSpec((tm, D_MODEL), lambda i, *_: (i, 0)),
                      pl.BlockSpec((tm, LANES), lambda i, *_: (i, 0)),
                      pl.BlockSpec((8, LANES), lambda i, *_: (i, 0))],
            out_specs=pl.BlockSpec(memory_space=pl.ANY),
            scratch_shapes=[pltpu.VMEM((2, _lmax(tm), D_MODEL), bf16),
                            pltpu.VMEM((N_EXPERTS, BF16_ROWS, D_MODEL), bf16),
                            pltpu.VMEM((FFN_TM, D_MODEL), bf16),
                            pltpu.SemaphoreType.DMA((2,)), pltpu.SemaphoreType.DMA],
        ),
        out_shape=jax.ShapeDtypeStruct((rows_total, D_MODEL), bf16),
        compiler_params=pltpu.CompilerParams(
            dimension_semantics=("arbitrary",), vmem_limit_bytes=VMEM_LIMIT),
        name="dispatch",
    )(*scalars, h2, meta, plan["base"])


def _ffn_body(gstart_ref, gsize_ref, used_ref, xs_ref, wgu_ref, bgu_ref, wd_ref, bd_ref, os_ref,
              xbuf_ref, obuf_ref, zero_ref, wgub_ref, wdb_ref, state_ref, isem, osem, zsem):
    e = pl.program_id(0)
    ne = pl.num_programs(0)
    tile = xbuf_ref.shape[1]
    base = gstart_ref[e]
    n = gsize_ref[e]
    n_tiles = (n + tile - 1) // tile

    def rows_of(size, t):
        return jnp.minimum(tile, size - t * tile)

    def in_copy(row0, rows, slot):
        return pltpu.make_async_copy(xs_ref.at[pl.ds(row0, rows)],
                                     xbuf_ref.at[slot, pl.ds(0, rows)], isem.at[slot])

    def out_copy(row0, rows, slot):
        return pltpu.make_async_copy(obuf_ref.at[slot, pl.ds(0, rows)],
                                     os_ref.at[pl.ds(row0, rows)], osem.at[slot])

    def start_tile(copy, row0, rows, slot):
        row0 = pl.multiple_of(row0, BF16_ROWS)

        @pl.when(rows == tile)
        def _():
            copy(row0, tile, slot).start(priority=_TILE_DMA_PRIORITY)

        @pl.when(rows < tile)
        def _():
            copy(row0, pl.multiple_of(rows, BF16_ROWS), slot).start()

    def wait_tile(copy, rows, slot):
        copy(0, pl.multiple_of(rows, BF16_ROWS), slot).wait()

    @pl.when(e == 0)
    def _():
        xbuf_ref[...] = jnp.zeros_like(xbuf_ref)
        zero_ref[...] = jnp.zeros_like(zero_ref)
        _zero_rows(zero_ref, os_ref, zsem, used_ref[0], start=True)
        state_ref[0] = 0
        state_ref[1] = 0
        state_ref[2] = 0
        state_ref[3] = 0

    done = state_ref[0]

    @pl.when(jnp.logical_and(n_tiles > 0, state_ref[1] == 0))
    def _():
        start_tile(in_copy, base, rows_of(n, 0), done & 1)

    nxt = jnp.minimum(e + 1, ne - 1)
    has_next = jnp.logical_and(e + 1 < ne, gsize_ref[nxt] > 0)

    @pl.when(n_tiles > 0)
    def _():
        chunk = 128
        for r in range(0, D_MODEL, chunk):
            wgub_ref[r:r + chunk, :] = wgu_ref[r:r + chunk, :].astype(bf16)
        for r in range(0, D_FF, chunk):
            wdb_ref[r:r + chunk, :] = wd_ref[r:r + chunk, :].astype(bf16)

    def step(t, carry):
        slot = (done + t) & 1

        @pl.when(t + 1 < n_tiles)
        def _():
            start_tile(in_copy, base + (t + 1) * tile, rows_of(n, t + 1), 1 - slot)

        @pl.when(jnp.logical_and(t + 1 == n_tiles, has_next))
        def _():
            start_tile(in_copy, gstart_ref[nxt], rows_of(gsize_ref[nxt], 0), 1 - slot)

        rows = rows_of(n, t)
        wait_tile(in_copy, rows, slot)

        @pl.when(state_ref[2 + slot] > 0)
        def _():
            wait_tile(out_copy, state_ref[2 + slot], slot)

        def ffn(m):
            gu = _dot(xbuf_ref[slot, 0:m, :], wgub_ref[...]) + bgu_ref[...]
            gate = jnp.minimum(gu[:, :D_FF], SWIGLU_LIMIT)
            lin = jnp.clip(gu[:, D_FF:], -SWIGLU_LIMIT, SWIGLU_LIMIT)
            act = (lin + 1.0) * (gate * jax.nn.sigmoid(SWIGLU_ALPHA * gate))
            out = _dot(act.astype(bf16), wdb_ref[...]) + bd_ref[...]
            obuf_ref[slot, 0:m, :] = out.astype(bf16)

        @pl.when(rows > tile // 2)
        def _():
            ffn(tile)

        @pl.when(rows <= tile // 2)
        def _():
            ffn(tile // 2)

        start_tile(out_copy, base + t * tile, rows, slot)
        state_ref[2 + slot] = rows
        return carry

    lax.fori_loop(0, n_tiles, step, 0)

    @pl.when(n_tiles > 0)
    def _():
        state_ref[0] = done + n_tiles
        state_ref[1] = has_next.astype(jnp.int32)

    @pl.when(e == ne - 1)
    def _():
        for slot in range(2):
            @pl.when(state_ref[2 + slot] > 0)
            def _():
                wait_tile(out_copy, state_ref[2 + slot], slot)
        _zero_rows(zero_ref, os_ref, zsem, used_ref[0], start=False)


def _ffn(gstart, gsize, used, xs, wgu, bgu, wd, bd):
    rows_total = xs.shape[0]
    return pl.pallas_call(
        _ffn_body,
        grid_spec=pltpu.PrefetchScalarGridSpec(
            num_scalar_prefetch=3,
            grid=(N_EXPERTS,),
            in_specs=[pl.BlockSpec(memory_space=pl.ANY),
                      pl.BlockSpec((None, D_MODEL, 2 * D_FF), lambda e, *_: (e, 0, 0)),
                      pl.BlockSpec((None, 1, 2 * D_FF), lambda e, *_: (e, 0, 0)),
                      pl.BlockSpec((None, D_FF, D_MODEL), lambda e, *_: (e, 0, 0)),
                      pl.BlockSpec((None, 1, D_MODEL), lambda e, *_: (e, 0, 0))],
            out_specs=pl.BlockSpec(memory_space=pl.ANY),
            scratch_shapes=[pltpu.VMEM((2, FFN_TM, D_MODEL), bf16),
                            pltpu.VMEM((2, FFN_TM, D_MODEL), bf16),
                            pltpu.VMEM((FFN_TM, D_MODEL), bf16),
                            pltpu.VMEM((D_MODEL, 2 * D_FF), bf16),
                            pltpu.VMEM((D_FF, D_MODEL), bf16),
                            pltpu.SMEM((4,), jnp.int32),
                            pltpu.SemaphoreType.DMA((2,)), pltpu.SemaphoreType.DMA((2,)),
                            pltpu.SemaphoreType.DMA],
        ),
        out_shape=jax.ShapeDtypeStruct((rows_total, D_MODEL), bf16),
        compiler_params=pltpu.CompilerParams(
            dimension_semantics=("arbitrary",), vmem_limit_bytes=VMEM_LIMIT),
        name="experts",
    )(gstart, gsize, used, xs, wgu, bgu, wd, bd)


def _combine_body(final_norm, voff_ref, row0_ref, rlen_ref, rtot_ref,
                  os_ref, meta_ref, base_ref, x1_ref, gfin_ref, out_ref, oc_ref, sem):
    i = pl.program_id(0)
    nt = pl.num_programs(0)
    tm = x1_ref.shape[0]
    lmax = oc_ref.shape[1]
    slot = i & 1

    def fetch(tile, into):
        for e in range(N_EXPERTS):
            j = tile * N_EXPERTS + e

            @pl.when(rlen_ref[j] > 0)
            def _():
                _segment_copy(os_ref, oc_ref.at[into], sem.at[into], row0_ref[j], voff_ref[j],
                              rlen_ref[j]).start()

    @pl.when(i == 0)
    def _():
        oc_ref[...] = jnp.zeros_like(oc_ref)
        fetch(0, 0)

    @pl.when(i + 1 < nt)
    def _():
        fetch(jnp.minimum(i + 1, nt - 1), 1 - slot)

    meta = meta_ref[...]
    col = lax.broadcasted_iota(jnp.int32, (tm, lmax), 1)
    p = jnp.zeros((tm, lmax), f32)
    for kk, s in enumerate(_token_slots(meta, base_ref[0:1, :])):
        p = jnp.where(col == s.astype(jnp.int32), meta[:, _META_GATE + kk:_META_GATE + kk + 1], p)
    pb = p.astype(bf16)

    _wait_rows(os_ref, oc_ref.at[slot], sem.at[slot], rtot_ref[i])

    xo = x1_ref[...] + _dot(pb, oc_ref[slot])
    out_ref[...] = _rms(xo, gfin_ref[...]) if final_norm else xo


def _combine(plan, os, meta, x1, gfin, final_norm):
    tokens = x1.shape[0]
    tm = min(MIX_TS, tokens)
    nt = tokens // tm
    scalars = (plan["voff"], plan["row0"], plan["rlen"], plan["rtot"])
    return pl.pallas_call(
        functools.partial(_combine_body, final_norm),
        grid_spec=pltpu.PrefetchScalarGridSpec(
            num_scalar_prefetch=len(scalars),
            grid=(nt,),
            in_specs=[pl.BlockSpec(memory_space=pl.ANY),
                      pl.BlockSpec((tm, LANES), lambda i, *_: (i, 0)),
                      pl.BlockSpec((8, LANES), lambda i, *_: (i, 0)),
                      pl.BlockSpec((tm, D_MODEL), lambda i, *_: (i, 0)),
                      pl.BlockSpec((1, D_MODEL), lambda i, *_: (0, 0))],
            out_specs=pl.BlockSpec((tm, D_MODEL), lambda i, *_: (i, 0)),
            scratch_shapes=[pltpu.VMEM((2, _lmax(tm), D_MODEL), bf16),
                            pltpu.SemaphoreType.DMA((2,))],
        ),
        out_shape=jax.ShapeDtypeStruct((tokens, D_MODEL), f32),
        compiler_params=pltpu.CompilerParams(
            dimension_semantics=("arbitrary",), vmem_limit_bytes=VMEM_LIMIT),
        name="combine",
    )(*scalars, os, meta, plan["base"], x1, gfin)


def _plan(counts, nt):
    up = lambda v: (v + (BF16_ROWS - 1)) // BF16_ROWS * BF16_ROWS
    down = lambda v: v // BF16_ROWS * BF16_ROWS
    c = counts.reshape(nt, 8, LANES)[:, 0, :N_EXPERTS]
    gsize = up(jnp.sum(c, axis=0))
    gstart = jnp.cumsum(gsize) - gsize
    before = jnp.cumsum(c, axis=0) - c
    a = before % BF16_ROWS
    length = a + c
    voff = jnp.cumsum(up(length), axis=1) - up(length)
    last = jnp.arange(nt)[:, None] == nt - 1
    wlen = jnp.where(last, up(length), down(length))
    rlen = jnp.where(c > 0, up(length), 0)
    base = jnp.zeros((nt, 8, LANES), jnp.float32)
    base = base.at[:, :, :N_EXPERTS].set((voff + a).astype(jnp.float32)[:, None, :])
    i32 = lambda v: v.reshape(-1).astype(jnp.int32)
    return dict(
        voff=i32(voff), coff=i32(voff + down(length)), keep=i32(length % BF16_ROWS),
        row0=i32(gstart[None, :] + before - a), wlen=i32(wlen), wtot=i32(jnp.sum(wlen, axis=1)),
        rlen=i32(rlen), rtot=i32(jnp.sum(rlen, axis=1)), base=base.reshape(nt * 8, LANES),
        gstart=i32(gstart), gsize=i32(gsize), used=i32(jnp.sum(gsize, keepdims=True)))


def kernel(x, g_mix, w_in, w_alpha_up, b_alpha, g_gla, w_conv, w_branch_a, w_branch_b, w_out, g_ffn, w_router, b_router, w_gate_up, b_gate_up, w_down, b_down, g_final):
    batch, seq, _ = x.shape
    tokens = batch * seq
    depth = w_in.shape[0]
    x2 = x.reshape(tokens, D_MODEL)
    nt = tokens // min(MIX_TS, seq)
    rows_total = tokens * TOP_K + N_EXPERTS * (BF16_ROWS - 1)
    rows_total = -(-rows_total // BF16_ROWS) * BF16_ROWS

    for l in range(depth):
        wi = w_in[l]
        a0 = 2 * QK_W + 2 * V_W
        wmain = jnp.concatenate([wi[:, :a0], wi[:, a0 + GLA_RANK:]], axis=1).astype(bf16)
        walr = jnp.pad(wi[:, a0:a0 + GLA_RANK], ((0, 0), (0, LANES - GLA_RANK))).astype(bf16)
        wup = jnp.pad(w_alpha_up[l], ((0, LANES - GLA_RANK), (0, 0))).astype(bf16)
        wr = jnp.pad(w_router[l], ((0, 0), (0, LANES - N_EXPERTS)))
        wrh = wr.astype(bf16)
        wrl = (wr - wrh.astype(f32)).astype(bf16)
        br = jnp.pad(b_router[l], (0, LANES - N_EXPERTS)).reshape(1, LANES)
        wconv = jnp.pad(w_conv[l], ((0, 8 - CONV_K), (0, 0)))

        x1, h2, meta, counts = _mixer(
            x2, batch, seq, g_mix[l].reshape(1, D_MODEL), wmain, walr, wup,
            b_alpha[l].reshape(1, QK_W), g_gla[l].reshape(1, V_W), wconv,
            w_branch_a[l].astype(bf16), w_branch_b[l].astype(bf16), w_out[l].astype(bf16),
            g_ffn[l].reshape(1, D_MODEL), wrh, wrl, br)

        plan = _plan(counts, nt)
        xs = _dispatch(plan, h2, meta, rows_total)
        os = _ffn(plan["gstart"], plan["gsize"], plan["used"], xs,
                  w_gate_up[l], b_gate_up[l].reshape(N_EXPERTS, 1, 2 * D_FF),
                  w_down[l], b_down[l].reshape(N_EXPERTS, 1, D_MODEL))
        x2 = _combine(plan, os, meta, x1, g_final.reshape(1, D_MODEL), l == depth - 1)
    return x2.reshape(batch, seq, D_MODEL)
```

```python
import functools

import jax
import jax.numpy as jnp
from jax import lax
from jax.experimental import pallas as pl
from jax.experimental.pallas import tpu as pltpu

D_MODEL = 1024
GLA_HEADS = 4
GLA_DK = 128
GLA_DV = 256
GLA_RANK = 16
GLA_TAU = 16.0
GLA_CHUNK = 64
_CHUNK_SHIFT = GLA_CHUNK.bit_length() - 1
CONV_K = 3
N_EXPERTS = 32
TOP_K = 4
D_FF = 1024
SWIGLU_LIMIT = 7.0
SWIGLU_ALPHA = 1.702
EPS = 1e-6

QK_W = GLA_HEADS * GLA_DK
V_W = GLA_HEADS * GLA_DV

LANES = 128
BF16_ROWS = 16
VMEM_LIMIT = 56 * 1024 * 1024

MIX_TS = 256
ROUTE_ROWS = 1024
_CHUNK = 256
FFN_TM = 512
_META_EXPERT, _META_GATE, _META_RANK = 0, TOP_K, 2 * TOP_K
_TILE_DMA_PRIORITY = 1

_C_Q, _C_K, _C_V, _C_R = 0, QK_W, 2 * QK_W, 2 * QK_W + V_W
_C_CB = _C_R + V_W
_C_CC = _C_CB + D_MODEL
_C_CX = _C_CC + D_MODEL
_C_GA = _C_CX + D_MODEL
_C_GB = _C_GA + D_MODEL
_W_MAIN = _C_GB + D_MODEL

f32 = jnp.float32
bf16 = jnp.bfloat16


def _rms(x, g):
    return x * lax.rsqrt(jnp.mean(x * x, axis=-1, keepdims=True) + EPS) * g


def _dot(a, b):
    return jnp.dot(a, b, preferred_element_type=f32)


def _split_bf16(a):
    hi = a.astype(bf16)
    lo = (a - hi.astype(f32)).astype(bf16)
    return hi, lo


def _mixer_body(x_ref, gmix_ref, wmain_ref, walr_ref, wup_ref, balpha_ref, ggla_ref, wconv_ref,
                wa_ref, wb_ref, wo_ref, gffn_ref, wrh_ref, wrl_ref, br_ref,
                x1_ref, h2_ref, logit_ref,
                state_ref, ubuf_ref, obuf_ref):
    ts = x_ref.shape[0]
    nchunk = ts // GLA_CHUNK

    @pl.when(pl.program_id(1) == 0)
    def _():
        state_ref[...] = jnp.zeros_like(state_ref)
        ubuf_ref[0:8, :] = jnp.zeros((8, D_MODEL), f32)

    x = x_ref[...]
    hb = _rms(x, gmix_ref[...]).astype(bf16)

    def proj(c0, width):
        return _dot(hb, wmain_ref[:, c0:c0 + width])

    a_lr = _dot(hb, walr_ref[...])
    pre = _dot(a_lr.astype(bf16), wup_ref[...]) + balpha_ref[...]
    log_a = -(jnp.maximum(-pre, 0.0) + jnp.log1p(jnp.exp(-jnp.abs(pre)))) * (1.0 / GLA_TAU)
    ri = lax.broadcasted_iota(jnp.int32, (ts, ts), 0)
    ci = lax.broadcasted_iota(jnp.int32, (ts, ts), 1)
    same_chunk = (ri >> _CHUNK_SHIFT) == (ci >> _CHUNK_SHIFT)
    causal = jnp.logical_and(same_chunk, ci <= ri)
    tri = jnp.where(causal, 1.0, 0.0).astype(bf16)
    blk = jnp.where(same_chunk, 1.0, 0.0).astype(bf16)
    la_hi, la_lo = _split_bf16(log_a)
    b = _dot(tri, la_hi) + _dot(tri, la_lo)
    b_tot = _dot(blk, la_hi) + _dot(blk, la_lo)
    e_pos = jnp.exp(b)
    e_neg = jnp.exp(-b)
    e_tail = jnp.exp(b_tot - b)
    e_tot = jnp.exp(b_tot)

    q = proj(_C_Q, QK_W)
    k = proj(_C_K, QK_W)
    q_dec = (q * (GLA_DK ** -0.5) * e_pos).astype(bf16)
    k_inv = (k * e_neg).astype(bf16)
    k_tail = (k * e_tail).astype(bf16)
    vb = proj(_C_V, V_W).astype(bf16)

    for hh in range(GLA_HEADS):
        qs = slice(hh * GLA_DK, (hh + 1) * GLA_DK)
        vs = slice(hh * GLA_DV, (hh + 1) * GLA_DV)
        sc = lax.dot_general(q_dec[:, qs], k_inv[:, qs], (((1,), (1,)), ((), ())),
                             preferred_element_type=f32)
        sc = jnp.where(causal, sc, 0.0).astype(bf16)
        obuf_ref[:, vs] = _dot(sc, vb[:, vs])
        st = state_ref[hh]
        for c in range(nchunk):
            rs = slice(c * GLA_CHUNK, (c + 1) * GLA_CHUNK)
            o_inter = lax.dot_general(q_dec[rs, qs], st.astype(bf16), (((1,), (1,)), ((), ())),
                                      preferred_element_type=f32)
            obuf_ref[rs, vs] += o_inter
            upd = lax.dot_general(vb[rs, vs], k_tail[rs, qs], (((0,), (0,)), ((), ())),
                                  preferred_element_type=f32)
            st = st * e_tot[c * GLA_CHUNK:c * GLA_CHUNK + 1, qs] + upd
        state_ref[hh] = st

    r = proj(_C_R, V_W)
    ggla = ggla_ref[...]
    o_parts = []
    for hh in range(GLA_HEADS):
        vs = slice(hh * GLA_DV, (hh + 1) * GLA_DV)
        o_h = _rms(obuf_ref[:, vs], ggla[:, vs])
        r_h = r[:, vs]
        o_parts.append((o_h * (r_h * jax.nn.sigmoid(r_h))).astype(bf16))
    y_a = _dot(jnp.concatenate(o_parts, axis=1), wa_ref[...])

    u = proj(_C_CC, D_MODEL) * proj(_C_CX, D_MODEL)
    ubuf_ref[8:8 + ts, :] = u
    wc = wconv_ref[...]
    y_conv = (wc[0:1, :] * ubuf_ref[6:6 + ts, :] + wc[1:2, :] * ubuf_ref[7:7 + ts, :]
              + wc[2:3, :] * u)
    ubuf_ref[0:8, :] = ubuf_ref[ts:ts + 8, :]
    y_b = _dot((proj(_C_CB, D_MODEL) * y_conv).astype(bf16), wb_ref[...])

    mixed = (jax.nn.sigmoid(proj(_C_GA, D_MODEL)) * y_a
             + jax.nn.sigmoid(proj(_C_GB, D_MODEL)) * y_b)
    x1 = x + _dot(mixed.astype(bf16), wo_ref[...])
    x1_ref[...] = x1

    h2 = _rms(x1, gffn_ref[...])
    h2_hi, h2_lo = _split_bf16(h2)
    h2_ref[...] = h2_hi
    wrh = wrh_ref[...]
    logit_ref[...] = (_dot(h2_hi, wrh) + _dot(h2_lo, wrh) + _dot(h2_hi, wrl_ref[...])
                      + br_ref[...])


def _mixer(x2, batch, seq, gmix, wmain, walr, wup, balpha, ggla, wconv, wa, wb, wo, gffn, wrh, wrl, br):
    ts = min(MIX_TS, seq)
    ns = seq // ts
    tokens = batch * seq

    def const(shape):
        return pl.BlockSpec(shape, lambda b, s: (0,) * len(shape), pipeline_mode=pl.Buffered(1))

    def row(width, dtype):
        return pl.BlockSpec((ts, width), lambda b, s: (b * ns + s, 0))

    return pl.pallas_call(
        _mixer_body,
        grid=(batch, ns),
        in_specs=[row(D_MODEL, f32), const(gmix.shape), const(wmain.shape), const(walr.shape),
                  const(wup.shape), const(balpha.shape), const(ggla.shape), const(wconv.shape),
                  const(wa.shape), const(wb.shape), const(wo.shape), const(gffn.shape),
                  const(wrh.shape), const(wrl.shape), const(br.shape)],
        out_specs=[row(D_MODEL, f32), row(D_MODEL, bf16), row(LANES, f32)],
        out_shape=[jax.ShapeDtypeStruct((tokens, D_MODEL), f32),
                   jax.ShapeDtypeStruct((tokens, D_MODEL), bf16),
                   jax.ShapeDtypeStruct((tokens, LANES), f32)],
        scratch_shapes=[pltpu.VMEM((GLA_HEADS, GLA_DV, GLA_DK), f32),
                        pltpu.VMEM((ts + 8, D_MODEL), f32),
                        pltpu.VMEM((ts, V_W), f32)],
        compiler_params=pltpu.CompilerParams(
            dimension_semantics=("parallel", "arbitrary"), vmem_limit_bytes=VMEM_LIMIT),
        name="mixer",
    )(x2, gmix, wmain, walr, wup, balpha, ggla, wconv, wa, wb, wo, gffn, wrh, wrl, br)


def _route_body(tile, logit_ref, meta_ref, cnt_ref):
    rows = logit_ref.shape[0]
    lane = lax.broadcasted_iota(jnp.int32, (rows, LANES), 1)
    lane_f = lane.astype(f32)
    lg = jnp.where(lane < N_EXPERTS, logit_ref[...], -jnp.inf)
    sels, tops, firsts = [], [], []
    for _ in range(TOP_K):
        m = jnp.max(lg, axis=-1, keepdims=True)
        first = jnp.min(jnp.where(lg == m, lane_f, float(LANES)), axis=-1, keepdims=True)
        sel = lane_f == first
        sels.append(sel)
        tops.append(m)
        firsts.append(first)
        lg = jnp.where(sel, -jnp.inf, lg)
    ps = [jnp.exp(t - tops[0]) for t in tops]
    denom = ps[0] + ps[1] + ps[2] + ps[3]
    onehot = jnp.zeros((rows, LANES), f32)
    for sel in sels:
        onehot = onehot + jnp.where(sel, 1.0, 0.0)
    onehot_b = onehot.astype(bf16)
    ri = lax.broadcasted_iota(jnp.int32, (tile, tile), 0)
    ci = lax.broadcasted_iota(jnp.int32, (tile, tile), 1)
    strict_lower = jnp.where(ci < ri, 1.0, 0.0).astype(bf16)
    ones = jnp.ones((8, tile), bf16)
    ranks = []
    for j in range(rows // tile):
        part = onehot_b[j * tile:(j + 1) * tile, :]
        ranks.append(_dot(strict_lower, part))
        cnt_ref[j * 8:(j + 1) * 8, :] = _dot(ones, part).astype(jnp.int32)
    rank = jnp.concatenate(ranks, axis=0)
    meta = jnp.zeros((rows, LANES), f32)
    for kk in range(TOP_K):
        meta = jnp.where(lane == _META_EXPERT + kk, firsts[kk], meta)
        meta = jnp.where(lane == _META_GATE + kk, ps[kk] / denom, meta)
        rank_k = jnp.sum(jnp.where(sels[kk], rank, 0.0), axis=-1, keepdims=True)
        meta = jnp.where(lane == _META_RANK + kk, rank_k, meta)
    meta_ref[...] = meta


def _route(logits, tile):
    tokens = logits.shape[0]
    rows = min(ROUTE_ROWS, tokens)
    steps = tokens // rows
    per_step = rows // tile
    return pl.pallas_call(
        functools.partial(_route_body, tile),
        grid=(steps,),
        in_specs=[pl.BlockSpec((rows, LANES), lambda i: (i, 0))],
        out_specs=[pl.BlockSpec((rows, LANES), lambda i: (i, 0)),
                   pl.BlockSpec((per_step * 8, LANES), lambda i: (i, 0))],
        out_shape=[jax.ShapeDtypeStruct((tokens, LANES), f32),
                   jax.ShapeDtypeStruct((tokens // tile * 8, LANES), jnp.int32)],
        compiler_params=pltpu.CompilerParams(
            dimension_semantics=("parallel",), vmem_limit_bytes=VMEM_LIMIT),
        name="route",
    )(logits)


def _lmax(tm):
    full = tm * TOP_K + N_EXPERTS * 2 * (BF16_ROWS - 1)
    return -(-full // 256) * 256


def _token_slots(meta, base_row):
    lane_f = lax.broadcasted_iota(jnp.int32, meta.shape, 1).astype(f32)
    slots = []
    for kk in range(TOP_K):
        sel = lane_f == meta[:, _META_EXPERT + kk:_META_EXPERT + kk + 1]
        first = jnp.sum(jnp.where(sel, base_row, 0.0), axis=-1, keepdims=True)
        slots.append(first + meta[:, _META_RANK + kk:_META_RANK + kk + 1])
    return slots


def _segment_copy(src, dst, sem, src_off, dst_off, rows):
    src_off = pl.multiple_of(src_off, BF16_ROWS)
    dst_off = pl.multiple_of(dst_off, BF16_ROWS)
    rows = pl.multiple_of(rows, BF16_ROWS)
    return pltpu.make_async_copy(src.at[pl.ds(src_off, rows)], dst.at[pl.ds(dst_off, rows)], sem)


def _wait_rows(src, dst, sem, total):
    @pl.when(total > 0)
    def _():
        _segment_copy(src, dst, sem, 0, 0, total).wait()


def _zero_rows(zero_ref, dst_ref, sem, first, start):
    tile = zero_ref.shape[0]
    n = dst_ref.shape[0] - first
    n_full = n // tile

    def full(j):
        return _segment_copy(zero_ref, dst_ref, sem, 0, first + j * tile, tile)

    def rest():
        return _segment_copy(zero_ref, dst_ref, sem, 0, first + n_full * tile, n - n_full * tile)

    def step(j, carry):
        full(j).start() if start else full(j).wait()
        return carry

    lax.fori_loop(0, n_full, step, 0)

    @pl.when(n - n_full * tile > 0)
    def _():
        rest().start() if start else rest().wait()


def _dispatch_body(voff_ref, coff_ref, keep_ref, row0_ref, wlen_ref, wtot_ref, used_ref,
                   h2_ref, meta_ref, base_ref, meta_next_ref, base_next_ref, xs_ref,
                   xc_ref, pt0_ref, pt1_ref, carry_ref, zero_ref, sem, zsem):
    i = pl.program_id(0)
    nt = pl.num_programs(0)
    tm = h2_ref.shape[0]
    lmax = xc_ref.shape[1]
    slot = i & 1
    xc = xc_ref.at[slot]

    def slot_rows(m_ref, b_ref):
        meta = m_ref[...]
        lane = lax.broadcasted_iota(jnp.int32, meta.shape, 1)
        slots = jnp.zeros(meta.shape, f32)
        for kk, s in enumerate(_token_slots(meta, b_ref[0:1, :])):
            slots = jnp.where(lane == kk, s, slots)
        return slots.T[0:8, :].astype(jnp.int32)

    def one_hot_rows(slots_t, c):
        row = lax.broadcasted_iota(jnp.int32, (_CHUNK, tm), 0) + c * _CHUNK
        pt = jnp.zeros((_CHUNK, tm), f32)
        for kk in range(TOP_K):
            pt = jnp.where(row == slots_t[kk:kk + 1, :], 1.0, pt)
        return pt.astype(bf16)

    @pl.when(i == 0)
    def _():
        carry_ref[...] = jnp.zeros_like(carry_ref)
        zero_ref[...] = jnp.zeros_like(zero_ref)
        _zero_rows(zero_ref, xs_ref, zsem, used_ref[0], start=True)
        first = slot_rows(meta_ref, base_ref)
        for c in range(lmax // _CHUNK):
            pt0_ref[c * _CHUNK:(c + 1) * _CHUNK, :] = one_hot_rows(first, c)

    @pl.when(i >= 2)
    def _():
        _wait_rows(xc, xs_ref, sem.at[slot], wtot_ref[jnp.maximum(i - 2, 0)])

    def compact(pt_now_ref, pt_next_ref):
        h2 = h2_ref[...]
        nxt = slot_rows(meta_next_ref, base_next_ref)
        for c in range(lmax // _CHUNK):
            rows = slice(c * _CHUNK, (c + 1) * _CHUNK)
            xc[rows, :] = _dot(pt_now_ref[rows, :], h2).astype(bf16)
            pt_next_ref[rows, :] = one_hot_rows(nxt, c)

    @pl.when(slot == 0)
    def _():
        compact(pt0_ref, pt1_ref)

    @pl.when(slot == 1)
    def _():
        compact(pt1_ref, pt0_ref)

    def block(off):
        return xc.at[pl.ds(pl.multiple_of(off, BF16_ROWS), BF16_ROWS)]

    for e in range(N_EXPERTS):
        j = i * N_EXPERTS + e
        blk = block(voff_ref[j])
        blk[...] = blk[...] + carry_ref[e]
    for e in range(N_EXPERTS):
        j = i * N_EXPERTS + e
        tail = block(coff_ref[j])[...]
        carry_ref[e] = jnp.where(keep_ref[j] > 0, tail, jnp.zeros_like(tail))

    for e in range(N_EXPERTS):
        j = i * N_EXPERTS + e

        @pl.when(wlen_ref[j] > 0)
        def _():
            _segment_copy(xc, xs_ref, sem.at[slot], voff_ref[j], row0_ref[j], wlen_ref[j]).start()

    @pl.when(i == nt - 1)
    def _():
        @pl.when(i >= 1)
        def _():
            _wait_rows(xc, xs_ref, sem.at[1 - slot], wtot_ref[jnp.maximum(i - 1, 0)])
        _wait_rows(xc, xs_ref, sem.at[slot], wtot_ref[i])
        _zero_rows(zero_ref, xs_ref, zsem, used_ref[0], start=False)


def _dispatch(plan, h2, meta, rows_total):
    tokens = h2.shape[0]
    tm = min(MIX_TS, tokens)
    nt = tokens // tm
    scalars = (plan["voff"], plan["coff"], plan["keep"], plan["row0"], plan["wlen"], plan["wtot"],
               plan["used"])
    return pl.pallas_call(
        _dispatch_body,
        grid_spec=pltpu.PrefetchScalarGridSpec(
            num_scalar_prefetch=len(scalars),
            grid=(nt,),
            in_specs=[pl.BlockSpec((tm, D_MODEL), lambda i, *_: (i, 0)),
                      pl.BlockSpec((tm, LANES), lambda i, *_: (i, 0)),
                      pl.BlockSpec((8, LANES), lambda i, *_: (i, 0)),
                      pl.BlockSpec((tm, LANES), lambda i, *_: (jnp.minimum(i + 1, nt - 1), 0)),
                      pl.BlockSpec((8, LANES), lambda i, *_: (jnp.minimum(i + 1, nt - 1), 0))],
            out_specs=pl.BlockSpec(memory_space=pl.ANY),
            scratch_shapes=[pltpu.VMEM((2, _lmax(tm), D_MODEL), bf16),
                            pltpu.VMEM((_lmax(tm), tm), bf16),
                            pltpu.VMEM((_lmax(tm), tm), bf16),
                            pltpu.VMEM((N_EXPERTS, BF16_ROWS, D_MODEL), bf16),
                            pltpu.VMEM((FFN_TM, D_MODEL), bf16),
                            pltpu.SemaphoreType.DMA((2,)), pltpu.SemaphoreType.DMA],
        ),
        out_shape=jax.ShapeDtypeStruct((rows_total, D_MODEL), bf16),
        compiler_params=pltpu.CompilerParams(
            dimension_semantics=("arbitrary",), vmem_limit_bytes=VMEM_LIMIT),
        name="dispatch",
    )(*scalars, h2, meta, plan["base"], meta, plan["base"])


def _ffn_body(gstart_ref, gsize_ref, used_ref, xs_ref, wgu_ref, bgu_ref, wd_ref, bd_ref, os_ref,
              xbuf_ref, obuf_ref, zero_ref, wgub_ref, wdb_ref, state_ref, isem, osem, zsem):
    e = pl.program_id(0)
    ne = pl.num_programs(0)
    tile = xbuf_ref.shape[1]
    base = gstart_ref[e]
    n = gsize_ref[e]
    n_tiles = (n + tile - 1) // tile

    def rows_of(size, t):
        return jnp.minimum(tile, size - t * tile)

    def in_copy(row0, rows, slot):
        return pltpu.make_async_copy(xs_ref.at[pl.ds(row0, rows)],
                                     xbuf_ref.at[slot, pl.ds(0, rows)], isem.at[slot])

    def out_copy(row0, rows, slot):
        return pltpu.make_async_copy(obuf_ref.at[slot, pl.ds(0, rows)],
                                     os_ref.at[pl.ds(row0, rows)], osem.at[slot])

    def start_tile(copy, row0, rows, slot):
        row0 = pl.multiple_of(row0, BF16_ROWS)

        @pl.when(rows == tile)
        def _():
            copy(row0, tile, slot).start(priority=_TILE_DMA_PRIORITY)

        @pl.when(rows < tile)
        def _():
            copy(row0, pl.multiple_of(rows, BF16_ROWS), slot).start()

    def wait_tile(copy, rows, slot):
        copy(0, pl.multiple_of(rows, BF16_ROWS), slot).wait()

    @pl.when(e == 0)
    def _():
        xbuf_ref[...] = jnp.zeros_like(xbuf_ref)
        zero_ref[...] = jnp.zeros_like(zero_ref)
        _zero_rows(zero_ref, os_ref, zsem, used_ref[0], start=True)
        state_ref[0] = 0
        state_ref[1] = 0
        state_ref[2] = 0
        state_ref[3] = 0

    done = state_ref[0]

    @pl.when(jnp.logical_and(n_tiles > 0, state_ref[1] == 0))
    def _():
        start_tile(in_copy, base, rows_of(n, 0), done & 1)

    nxt = jnp.minimum(e + 1, ne - 1)
    has_next = jnp.logical_and(e + 1 < ne, gsize_ref[nxt] > 0)

    @pl.when(n_tiles > 0)
    def _():
        chunk = 128
        for r in range(0, D_MODEL, chunk):
            wgub_ref[r:r + chunk, :] = wgu_ref[r:r + chunk, :].astype(bf16)
        for r in range(0, D_FF, chunk):
            wdb_ref[r:r + chunk, :] = wd_ref[r:r + chunk, :].astype(bf16)

    def step(t, carry):
        slot = (done + t) & 1

        @pl.when(t + 1 < n_tiles)
        def _():
            start_tile(in_copy, base + (t + 1) * tile, rows_of(n, t + 1), 1 - slot)

        @pl.when(jnp.logical_and(t + 1 == n_tiles, has_next))
        def _():
            start_tile(in_copy, gstart_ref[nxt], rows_of(gsize_ref[nxt], 0), 1 - slot)

        rows = rows_of(n, t)
        wait_tile(in_copy, rows, slot)

        @pl.when(state_ref[2 + slot] > 0)
        def _():
            wait_tile(out_copy, state_ref[2 + slot], slot)

        def ffn(m):
            gu = _dot(xbuf_ref[slot, 0:m, :], wgub_ref[...]) + bgu_ref[...]
            gate = jnp.minimum(gu[:, :D_FF], SWIGLU_LIMIT)
            lin = jnp.clip(gu[:, D_FF:], -SWIGLU_LIMIT, SWIGLU_LIMIT)
            act = (lin + 1.0) * (gate * jax.nn.sigmoid(SWIGLU_ALPHA * gate))
            out = _dot(act.astype(bf16), wdb_ref[...]) + bd_ref[...]
            obuf_ref[slot, 0:m, :] = out.astype(bf16)

        @pl.when(rows > tile // 2)
        def _():
            ffn(tile)

        @pl.when(rows <= tile // 2)
        def _():
            ffn(tile // 2)

        start_tile(out_copy, base + t * tile, rows, slot)
        state_ref[2 + slot] = rows
        return carry

    lax.fori_loop(0, n_tiles, step, 0)

    @pl.when(n_tiles > 0)
    def _():
        state_ref[0] = done + n_tiles
        state_ref[1] = has_next.astype(jnp.int32)

    @pl.when(e == ne - 1)
    def _():
        for slot in range(2):
            @pl.when(state_ref[2 + slot] > 0)
            def _():
                wait_tile(out_copy, state_ref[2 + slot], slot)
        _zero_rows(zero_ref, os_ref, zsem, used_ref[0], start=False)


def _ffn(gstart, gsize, used, xs, wgu, bgu, wd, bd):
    rows_total = xs.shape[0]
    return pl.pallas_call(
        _ffn_body,
        grid_spec=pltpu.PrefetchScalarGridSpec(
            num_scalar_prefetch=3,
            grid=(N_EXPERTS,),
            in_specs=[pl.BlockSpec(memory_space=pl.ANY),
                      pl.BlockSpec((None, D_MODEL, 2 * D_FF), lambda e, *_: (e, 0, 0)),
                      pl.BlockSpec((None, 1, 2 * D_FF), lambda e, *_: (e, 0, 0)),
                      pl.BlockSpec((None, D_FF, D_MODEL), lambda e, *_: (e, 0, 0)),
                      pl.BlockSpec((None, 1, D_MODEL), lambda e, *_: (e, 0, 0))],
            out_specs=pl.BlockSpec(memory_space=pl.ANY),
            scratch_shapes=[pltpu.VMEM((2, FFN_TM, D_MODEL), bf16),
                            pltpu.VMEM((2, FFN_TM, D_MODEL), bf16),
                            pltpu.VMEM((FFN_TM, D_MODEL), bf16),
                            pltpu.VMEM((D_MODEL, 2 * D_FF), bf16),
                            pltpu.VMEM((D_FF, D_MODEL), bf16),
                            pltpu.SMEM((4,), jnp.int32),
                            pltpu.SemaphoreType.DMA((2,)), pltpu.SemaphoreType.DMA((2,)),
                            pltpu.SemaphoreType.DMA],
        ),
        out_shape=jax.ShapeDtypeStruct((rows_total, D_MODEL), bf16),
        compiler_params=pltpu.CompilerParams(
            dimension_semantics=("arbitrary",), vmem_limit_bytes=VMEM_LIMIT),
        name="experts",
    )(gstart, gsize, used, xs, wgu, bgu, wd, bd)


def _combine_body(final_norm, voff_ref, row0_ref, rlen_ref, rtot_ref,
                  os_ref, meta_ref, base_ref, meta_next_ref, base_next_ref, x1_ref, gfin_ref,
                  out_ref, oc_ref, p0_ref, p1_ref, sem):
    i = pl.program_id(0)
    nt = pl.num_programs(0)
    tm = x1_ref.shape[0]
    lmax = oc_ref.shape[1]
    slot = i & 1

    def slots_and_gates(m_ref, b_ref):
        meta = m_ref[...]
        slots = [s.astype(jnp.int32) for s in _token_slots(meta, b_ref[0:1, :])]
        gates = [meta[:, _META_GATE + kk:_META_GATE + kk + 1] for kk in range(TOP_K)]
        return slots, gates

    def gate_cols(slots, gates, c):
        col = lax.broadcasted_iota(jnp.int32, (tm, _CHUNK), 1) + c * _CHUNK
        p = jnp.zeros((tm, _CHUNK), f32)
        for s, g in zip(slots, gates):
            p = jnp.where(col == s, g, p)
        return p.astype(bf16)

    def fetch(tile, into):
        for e in range(N_EXPERTS):
            j = tile * N_EXPERTS + e

            @pl.when(rlen_ref[j] > 0)
            def _():
                _segment_copy(os_ref, oc_ref.at[into], sem.at[into], row0_ref[j], voff_ref[j],
                              rlen_ref[j]).start()

    @pl.when(i == 0)
    def _():
        oc_ref[...] = jnp.zeros_like(oc_ref)
        fetch(0, 0)
        slots, gates = slots_and_gates(meta_ref, base_ref)
        for c in range(lmax // _CHUNK):
            p0_ref[:, c * _CHUNK:(c + 1) * _CHUNK] = gate_cols(slots, gates, c)

    _wait_rows(os_ref, oc_ref.at[slot], sem.at[slot], rtot_ref[i])

    @pl.when(i + 1 < nt)
    def _():
        fetch(jnp.minimum(i + 1, nt - 1), 1 - slot)

    def restore(p_now_ref, p_next_ref, now):
        slots, gates = slots_and_gates(meta_next_ref, base_next_ref)
        p_now = p_now_ref[...]
        n_build = lmax // _CHUNK
        n_out = D_MODEL // _CHUNK
        ssq = jnp.zeros((tm, 1), f32)
        for n in range(n_out):
            cols = slice(n * _CHUNK, (n + 1) * _CHUNK)
            xo = x1_ref[:, cols] + _dot(p_now, oc_ref[now, :, cols])
            out_ref[:, cols] = xo
            ssq = ssq + jnp.sum(xo * xo, axis=-1, keepdims=True)
            for c in range(n * n_build // n_out, (n + 1) * n_build // n_out):
                p_next_ref[:, c * _CHUNK:(c + 1) * _CHUNK] = gate_cols(slots, gates, c)
        if final_norm:
            out_ref[...] = out_ref[...] * lax.rsqrt(ssq * (1.0 / D_MODEL) + EPS) * gfin_ref[...]

    @pl.when(slot == 0)
    def _():
        restore(p0_ref, p1_ref, 0)

    @pl.when(slot == 1)
    def _():
        restore(p1_ref, p0_ref, 1)


def _combine(plan, os, meta, x1, gfin, final_norm):
    tokens = x1.shape[0]
    tm = min(MIX_TS, tokens)
    nt = tokens // tm
    scalars = (plan["voff"], plan["row0"], plan["rlen"], plan["rtot"])
    return pl.pallas_call(
        functools.partial(_combine_body, final_norm),
        grid_spec=pltpu.PrefetchScalarGridSpec(
            num_scalar_prefetch=len(scalars),
            grid=(nt,),
            in_specs=[pl.BlockSpec(memory_space=pl.ANY),
                      pl.BlockSpec((tm, LANES), lambda i, *_: (i, 0)),
                      pl.BlockSpec((8, LANES), lambda i, *_: (i, 0)),
                      pl.BlockSpec((tm, LANES), lambda i, *_: (jnp.minimum(i + 1, nt - 1), 0)),
                      pl.BlockSpec((8, LANES), lambda i, *_: (jnp.minimum(i + 1, nt - 1), 0)),
                      pl.BlockSpec((tm, D_MODEL), lambda i, *_: (i, 0)),
                      pl.BlockSpec((1, D_MODEL), lambda i, *_: (0, 0))],
            out_specs=pl.BlockSpec((tm, D_MODEL), lambda i, *_: (i, 0)),
            scratch_shapes=[pltpu.VMEM((2, _lmax(tm), D_MODEL), bf16),
                            pltpu.VMEM((tm, _lmax(tm)), bf16),
                            pltpu.VMEM((tm, _lmax(tm)), bf16),
                            pltpu.SemaphoreType.DMA((2,))],
        ),
        out_shape=jax.ShapeDtypeStruct((tokens, D_MODEL), f32),
        compiler_params=pltpu.CompilerParams(
            dimension_semantics=("arbitrary",), vmem_limit_bytes=VMEM_LIMIT),
        name="combine",
    )(*scalars, os, meta, plan["base"], meta, plan["base"], x1, gfin)


def _plan(counts, nt):
    up = lambda v: (v + (BF16_ROWS - 1)) // BF16_ROWS * BF16_ROWS
    down = lambda v: v // BF16_ROWS * BF16_ROWS
    c = counts.reshape(nt, 8, LANES)[:, 0, :N_EXPERTS]
    gsize = up(jnp.sum(c, axis=0))
    gstart = jnp.cumsum(gsize) - gsize
    before = jnp.cumsum(c, axis=0) - c
    a = before % BF16_ROWS
    length = a + c
    voff = jnp.cumsum(up(length), axis=1) - up(length)
    last = jnp.arange(nt)[:, None] == nt - 1
    wlen = jnp.where(last, up(length), down(length))
    rlen = jnp.where(c > 0, up(length), 0)
    base = jnp.zeros((nt, 8, LANES), jnp.float32)
    base = base.at[:, :, :N_EXPERTS].set((voff + a).astype(jnp.float32)[:, None, :])
    i32 = lambda v: v.reshape(-1).astype(jnp.int32)
    return dict(
        voff=i32(voff), coff=i32(voff + down(length)), keep=i32(length % BF16_ROWS),
        row0=i32(gstart[None, :] + before - a), wlen=i32(wlen), wtot=i32(jnp.sum(wlen, axis=1)),
        rlen=i32(rlen), rtot=i32(jnp.sum(rlen, axis=1)), base=base.reshape(nt * 8, LANES),
        gstart=i32(gstart), gsize=i32(gsize), used=i32(jnp.sum(gsize, keepdims=True)))


def kernel(x, g_mix, w_in, w_alpha_up, b_alpha, g_gla, w_conv, w_branch_a, w_branch_b, w_out, g_ffn, w_router, b_router, w_gate_up, b_gate_up, w_down, b_down, g_final):
    batch, seq, _ = x.shape
    tokens = batch * seq
    depth = w_in.shape[0]
    x2 = x.reshape(tokens, D_MODEL)
    nt = tokens // min(MIX_TS, seq)
    rows_total = tokens * TOP_K + N_EXPERTS * (BF16_ROWS - 1)
    rows_total = -(-rows_total // BF16_ROWS) * BF16_ROWS

    for l in range(depth):
        wi = w_in[l]
        a0 = 2 * QK_W + 2 * V_W
        wmain = jnp.concatenate([wi[:, :a0], wi[:, a0 + GLA_RANK:]], axis=1).astype(bf16)
        walr = jnp.pad(wi[:, a0:a0 + GLA_RANK], ((0, 0), (0, LANES - GLA_RANK))).astype(bf16)
        wup = jnp.pad(w_alpha_up[l], ((0, LANES - GLA_RANK), (0, 0))).astype(bf16)
        wr = jnp.pad(w_router[l], ((0, 0), (0, LANES - N_EXPERTS)))
        wrh = wr.astype(bf16)
        wrl = (wr - wrh.astype(f32)).astype(bf16)
        br = jnp.pad(b_router[l], (0, LANES - N_EXPERTS)).reshape(1, LANES)
        wconv = jnp.pad(w_conv[l], ((0, 8 - CONV_K), (0, 0)))

        x1, h2, logits = _mixer(
            x2, batch, seq, g_mix[l].reshape(1, D_MODEL), wmain, walr, wup,
            b_alpha[l].reshape(1, QK_W), g_gla[l].reshape(1, V_W), wconv,
            w_branch_a[l].astype(bf16), w_branch_b[l].astype(bf16), w_out[l].astype(bf16),
            g_ffn[l].reshape(1, D_MODEL), wrh, wrl, br)

        meta, counts = _route(logits, min(MIX_TS, seq))
        plan = _plan(counts, nt)
        xs = _dispatch(plan, h2, meta, rows_total)
        os = _ffn(plan["gstart"], plan["gsize"], plan["used"], xs,
                  w_gate_up[l], b_gate_up[l].reshape(N_EXPERTS, 1, 2 * D_FF),
                  w_down[l], b_down[l].reshape(N_EXPERTS, 1, D_MODEL))
        x2 = _combine(plan, os, meta, x1, g_final.reshape(1, D_MODEL), l == depth - 1)
    return x2.reshape(batch, seq, D_MODEL)
```

```python
import functools

import jax
import jax.numpy as jnp
from jax import lax
from jax.experimental import pallas as pl
from jax.experimental.pallas import tpu as pltpu

D_MODEL = 1024
GLA_HEADS = 4
GLA_DK = 128
GLA_DV = 256
GLA_RANK = 16
GLA_TAU = 16.0
GLA_CHUNK = 64
_CHUNK_SHIFT = GLA_CHUNK.bit_length() - 1
CONV_K = 3
N_EXPERTS = 32
TOP_K = 4
D_FF = 1024
SWIGLU_LIMIT = 7.0
SWIGLU_ALPHA = 1.702
EPS = 1e-6

QK_W = GLA_HEADS * GLA_DK
V_W = GLA_HEADS * GLA_DV

LANES = 128
BF16_ROWS = 16
VMEM_LIMIT = 56 * 1024 * 1024

MIX_TS = 256
ROUTE_ROWS = 1024
_CHUNK = 256
FFN_TM = 512
_META_EXPERT, _META_GATE, _META_RANK = 0, TOP_K, 2 * TOP_K
_TILE_DMA_PRIORITY = 1

_C_Q, _C_K, _C_V, _C_R = 0, QK_W, 2 * QK_W, 2 * QK_W + V_W
_C_CB = _C_R + V_W
_C_CC = _C_CB + D_MODEL
_C_CX = _C_CC + D_MODEL
_C_GA = _C_CX + D_MODEL
_C_GB = _C_GA + D_MODEL
_W_MAIN = _C_GB + D_MODEL

f32 = jnp.float32
bf16 = jnp.bfloat16


def _rms(x, g):
    return x * lax.rsqrt(jnp.mean(x * x, axis=-1, keepdims=True) + EPS) * g


def _dot(a, b):
    return jnp.dot(a, b, preferred_element_type=f32)


def _split_bf16(a):
    hi = a.astype(bf16)
    lo = (a - hi.astype(f32)).astype(bf16)
    return hi, lo


def _mixer_body(x_ref, gmix_ref, wmain_ref, walr_ref, wup_ref, balpha_ref, ggla_ref, wconv_ref,
                wa_ref, wb_ref, wo_ref, gffn_ref, wrh_ref, wrl_ref, br_ref,
                x1_ref, h2_ref, logit_ref,
                state_ref, ubuf_ref, obuf_ref):
    ts = x_ref.shape[0]
    nchunk = ts // GLA_CHUNK

    @pl.when(pl.program_id(1) == 0)
    def _():
        state_ref[...] = jnp.zeros_like(state_ref)
        ubuf_ref[0:8, :] = jnp.zeros((8, D_MODEL), f32)

    x = x_ref[...]
    hb = _rms(x, gmix_ref[...]).astype(bf16)

    def proj(c0, width):
        return _dot(hb, wmain_ref[:, c0:c0 + width])

    a_lr = _dot(hb, walr_ref[...])
    pre = _dot(a_lr.astype(bf16), wup_ref[...]) + balpha_ref[...]
    log_a = -(jnp.maximum(-pre, 0.0) + jnp.log1p(jnp.exp(-jnp.abs(pre)))) * (1.0 / GLA_TAU)
    ri = lax.broadcasted_iota(jnp.int32, (ts, ts), 0)
    ci = lax.broadcasted_iota(jnp.int32, (ts, ts), 1)
    same_chunk = (ri >> _CHUNK_SHIFT) == (ci >> _CHUNK_SHIFT)
    causal = jnp.logical_and(same_chunk, ci <= ri)
    tri = jnp.where(causal, 1.0, 0.0).astype(bf16)
    la_hi, la_lo = _split_bf16(log_a)
    b = _dot(tri, la_hi) + _dot(tri, la_lo)
    lasts = [b[(c + 1) * GLA_CHUNK - 1:(c + 1) * GLA_CHUNK, :] for c in range(nchunk)]
    b_tot = jnp.concatenate([jnp.broadcast_to(r, (GLA_CHUNK, QK_W)) for r in lasts], axis=0)
    e_pos = jnp.exp(b)
    e_neg = jnp.exp(-b)
    e_tail = jnp.exp(b_tot - b)
    e_tot = [jnp.exp(r) for r in lasts]

    q = proj(_C_Q, QK_W)
    k = proj(_C_K, QK_W)
    q_dec = (q * (GLA_DK ** -0.5) * e_pos).astype(bf16)
    k_inv = (k * e_neg).astype(bf16)
    k_tail = (k * e_tail).astype(bf16)
    vb = proj(_C_V, V_W).astype(bf16)

    for hh in range(GLA_HEADS):
        qs = slice(hh * GLA_DK, (hh + 1) * GLA_DK)
        vs = slice(hh * GLA_DV, (hh + 1) * GLA_DV)
        sc = lax.dot_general(q_dec[:, qs], k_inv[:, qs], (((1,), (1,)), ((), ())),
                             preferred_element_type=f32)
        sc = jnp.where(causal, sc, 0.0).astype(bf16)
        obuf_ref[:, vs] = _dot(sc, vb[:, vs])
        st = state_ref[hh]
        for c in range(nchunk):
            rs = slice(c * GLA_CHUNK, (c + 1) * GLA_CHUNK)
            o_inter = lax.dot_general(q_dec[rs, qs], st.astype(bf16), (((1,), (1,)), ((), ())),
                                      preferred_element_type=f32)
            obuf_ref[rs, vs] += o_inter
            upd = lax.dot_general(vb[rs, vs], k_tail[rs, qs], (((0,), (0,)), ((), ())),
                                  preferred_element_type=f32)
            st = st * e_tot[c][:, qs] + upd
        state_ref[hh] = st

    r = proj(_C_R, V_W)
    ggla = ggla_ref[...]
    o_parts = []
    for hh in range(GLA_HEADS):
        vs = slice(hh * GLA_DV, (hh + 1) * GLA_DV)
        o_h = _rms(obuf_ref[:, vs], ggla[:, vs])
        r_h = r[:, vs]
        o_parts.append((o_h * (r_h * jax.nn.sigmoid(r_h))).astype(bf16))
    y_a = _dot(jnp.concatenate(o_parts, axis=1), wa_ref[...])

    u = proj(_C_CC, D_MODEL) * proj(_C_CX, D_MODEL)
    ubuf_ref[8:8 + ts, :] = u
    wc = wconv_ref[...]
    y_conv = (wc[0:1, :] * ubuf_ref[6:6 + ts, :] + wc[1:2, :] * ubuf_ref[7:7 + ts, :]
              + wc[2:3, :] * u)
    ubuf_ref[0:8, :] = ubuf_ref[ts:ts + 8, :]
    y_b = _dot((proj(_C_CB, D_MODEL) * y_conv).astype(bf16), wb_ref[...])

    mixed = (jax.nn.sigmoid(proj(_C_GA, D_MODEL)) * y_a
             + jax.nn.sigmoid(proj(_C_GB, D_MODEL)) * y_b)
    x1 = x + _dot(mixed.astype(bf16), wo_ref[...])
    x1_ref[...] = x1

    h2 = _rms(x1, gffn_ref[...])
    h2_hi, h2_lo = _split_bf16(h2)
    h2_ref[...] = h2_hi
    wrh = wrh_ref[...]
    logit_ref[...] = (_dot(h2_hi, wrh) + _dot(h2_lo, wrh) + _dot(h2_hi, wrl_ref[...])
                      + br_ref[...])


def _mixer(x2, batch, seq, gmix, wmain, walr, wup, balpha, ggla, wconv, wa, wb, wo, gffn, wrh, wrl, br):
    ts = min(MIX_TS, seq)
    ns = seq // ts
    tokens = batch * seq

    def const(shape):
        return pl.BlockSpec(shape, lambda b, s: (0,) * len(shape), pipeline_mode=pl.Buffered(1))

    def row(width, dtype):
        return pl.BlockSpec((ts, width), lambda b, s: (b * ns + s, 0))

    return pl.pallas_call(
        _mixer_body,
        grid=(batch, ns),
        in_specs=[row(D_MODEL, f32), const(gmix.shape), const(wmain.shape), const(walr.shape),
                  const(wup.shape), const(balpha.shape), const(ggla.shape), const(wconv.shape),
                  const(wa.shape), const(wb.shape), const(wo.shape), const(gffn.shape),
                  const(wrh.shape), const(wrl.shape), const(br.shape)],
        out_specs=[row(D_MODEL, f32), row(D_MODEL, bf16), row(LANES, f32)],
        out_shape=[jax.ShapeDtypeStruct((tokens, D_MODEL), f32),
                   jax.ShapeDtypeStruct((tokens, D_MODEL), bf16),
                   jax.ShapeDtypeStruct((tokens, LANES), f32)],
        scratch_shapes=[pltpu.VMEM((GLA_HEADS, GLA_DV, GLA_DK), f32),
                        pltpu.VMEM((ts + 8, D_MODEL), f32),
                        pltpu.VMEM((ts, V_W), f32)],
        compiler_params=pltpu.CompilerParams(
            dimension_semantics=("parallel", "arbitrary"), vmem_limit_bytes=VMEM_LIMIT),
        name="mixer",
    )(x2, gmix, wmain, walr, wup, balpha, ggla, wconv, wa, wb, wo, gffn, wrh, wrl, br)


def _route_body(tile, logit_ref, meta_ref, cnt_ref):
    rows = logit_ref.shape[0]
    lane = lax.broadcasted_iota(jnp.int32, (rows, LANES), 1)
    lane_f = lane.astype(f32)
    lg = jnp.where(lane < N_EXPERTS, logit_ref[...], -jnp.inf)
    sels, tops, firsts = [], [], []
    for _ in range(TOP_K):
        m = jnp.max(lg, axis=-1, keepdims=True)
        first = jnp.min(jnp.where(lg == m, lane_f, float(LANES)), axis=-1, keepdims=True)
        sel = lane_f == first
        sels.append(sel)
        tops.append(m)
        firsts.append(first)
        lg = jnp.where(sel, -jnp.inf, lg)
    ps = [jnp.exp(t - tops[0]) for t in tops]
    denom = ps[0] + ps[1] + ps[2] + ps[3]
    onehot = jnp.zeros((rows, LANES), f32)
    for sel in sels:
        onehot = onehot + jnp.where(sel, 1.0, 0.0)
    onehot_b = onehot.astype(bf16)
    ri = lax.broadcasted_iota(jnp.int32, (tile, tile), 0)
    ci = lax.broadcasted_iota(jnp.int32, (tile, tile), 1)
    strict_lower = jnp.where(ci < ri, 1.0, 0.0).astype(bf16)
    ones = jnp.ones((8, tile), bf16)
    ranks = []
    for j in range(rows // tile):
        part = onehot_b[j * tile:(j + 1) * tile, :]
        ranks.append(_dot(strict_lower, part))
        cnt_ref[j * 8:(j + 1) * 8, :] = _dot(ones, part).astype(jnp.int32)
    rank = jnp.concatenate(ranks, axis=0)
    meta = jnp.zeros((rows, LANES), f32)
    for kk in range(TOP_K):
        meta = jnp.where(lane == _META_EXPERT + kk, firsts[kk], meta)
        meta = jnp.where(lane == _META_GATE + kk, ps[kk] / denom, meta)
        rank_k = jnp.sum(jnp.where(sels[kk], rank, 0.0), axis=-1, keepdims=True)
        meta = jnp.where(lane == _META_RANK + kk, rank_k, meta)
    meta_ref[...] = meta


def _route(logits, tile):
    tokens = logits.shape[0]
    rows = min(ROUTE_ROWS, tokens)
    steps = tokens // rows
    per_step = rows // tile
    return pl.pallas_call(
        functools.partial(_route_body, tile),
        grid=(steps,),
        in_specs=[pl.BlockSpec((rows, LANES), lambda i: (i, 0))],
        out_specs=[pl.BlockSpec((rows, LANES), lambda i: (i, 0)),
                   pl.BlockSpec((per_step * 8, LANES), lambda i: (i, 0))],
        out_shape=[jax.ShapeDtypeStruct((tokens, LANES), f32),
                   jax.ShapeDtypeStruct((tokens // tile * 8, LANES), jnp.int32)],
        compiler_params=pltpu.CompilerParams(
            dimension_semantics=("parallel",), vmem_limit_bytes=VMEM_LIMIT),
        name="route",
    )(logits)


def _lmax(tm):
    full = tm * TOP_K + N_EXPERTS * 2 * (BF16_ROWS - 1)
    return -(-full // 256) * 256


def _token_slots(meta, base_row):
    lane_f = lax.broadcasted_iota(jnp.int32, meta.shape, 1).astype(f32)
    slots = []
    for kk in range(TOP_K):
        sel = lane_f == meta[:, _META_EXPERT + kk:_META_EXPERT + kk + 1]
        first = jnp.sum(jnp.where(sel, base_row, 0.0), axis=-1, keepdims=True)
        slots.append(first + meta[:, _META_RANK + kk:_META_RANK + kk + 1])
    return slots


def _segment_copy(src, dst, sem, src_off, dst_off, rows):
    src_off = pl.multiple_of(src_off, BF16_ROWS)
    dst_off = pl.multiple_of(dst_off, BF16_ROWS)
    rows = pl.multiple_of(rows, BF16_ROWS)
    return pltpu.make_async_copy(src.at[pl.ds(src_off, rows)], dst.at[pl.ds(dst_off, rows)], sem)


def _wait_rows(src, dst, sem, total):
    @pl.when(total > 0)
    def _():
        _segment_copy(src, dst, sem, 0, 0, total).wait()


def _zero_rows(zero_ref, dst_ref, sem, first, start):
    tile = zero_ref.shape[0]
    n = dst_ref.shape[0] - first
    n_full = n // tile

    def full(j):
        return _segment_copy(zero_ref, dst_ref, sem, 0, first + j * tile, tile)

    def rest():
        return _segment_copy(zero_ref, dst_ref, sem, 0, first + n_full * tile, n - n_full * tile)

    def step(j, carry):
        full(j).start() if start else full(j).wait()
        return carry

    lax.fori_loop(0, n_full, step, 0)

    @pl.when(n - n_full * tile > 0)
    def _():
        rest().start() if start else rest().wait()


def _dispatch_body(voff_ref, coff_ref, keep_ref, row0_ref, wlen_ref, wtot_ref, used_ref,
                   h2_ref, meta_ref, base_ref, meta_next_ref, base_next_ref, xs_ref,
                   xc0_ref, xc1_ref, pt0_ref, pt1_ref, carry_ref, zero_ref, sem, zsem):
    i = pl.program_id(0)
    nt = pl.num_programs(0)
    tm = h2_ref.shape[0]
    lmax = xc0_ref.shape[0]
    slot = i & 1

    def slot_rows(m_ref, b_ref):
        meta = m_ref[...]
        lane = lax.broadcasted_iota(jnp.int32, meta.shape, 1)
        slots = jnp.zeros(meta.shape, f32)
        for kk, s in enumerate(_token_slots(meta, b_ref[0:1, :])):
            slots = jnp.where(lane == kk, s, slots)
        return slots.T[0:8, :].astype(jnp.int32)

    def one_hot_rows(slots_t, c):
        row = lax.broadcasted_iota(jnp.int32, (_CHUNK, tm), 0) + c * _CHUNK
        pt = jnp.zeros((_CHUNK, tm), f32)
        for kk in range(TOP_K):
            pt = jnp.where(row == slots_t[kk:kk + 1, :], 1.0, pt)
        return pt.astype(bf16)

    @pl.when(i == 0)
    def _():
        carry_ref[...] = jnp.zeros_like(carry_ref)
        zero_ref[...] = jnp.zeros_like(zero_ref)
        _zero_rows(zero_ref, xs_ref, zsem, used_ref[0], start=True)
        first = slot_rows(meta_ref, base_ref)
        for c in range(lmax // _CHUNK):
            pt0_ref[c * _CHUNK:(c + 1) * _CHUNK, :] = one_hot_rows(first, c)
        xc1_ref[...] = jnp.zeros_like(xc1_ref)

    @pl.when(i >= 2)
    def _():
        _wait_rows(xc0_ref, xs_ref, sem.at[slot], wtot_ref[jnp.maximum(i - 2, 0)])

    def merge_carry(t, buf):
        def block(off):
            return buf.at[pl.ds(pl.multiple_of(off, BF16_ROWS), BF16_ROWS)]

        for e in range(N_EXPERTS):
            blk = block(voff_ref[t * N_EXPERTS + e])
            blk[...] = blk[...] + carry_ref[e]
        for e in range(N_EXPERTS):
            j = t * N_EXPERTS + e
            tail = block(coff_ref[j])[...]
            carry_ref[e] = jnp.where(keep_ref[j] > 0, tail, jnp.zeros_like(tail))

    def copy_out(t, buf, experts, copy_rows):
        for e in experts:
            j = t * N_EXPERTS + e
            _segment_copy(buf, xs_ref, sem.at[t & 1], voff_ref[j], row0_ref[j],
                          copy_rows(wlen_ref[j])).start()

    def compact(pt_now_ref, pt_next_ref, xc_now_ref, xc_prev_ref):
        prev = jnp.maximum(i - 1, 0)
        merge_carry(prev, xc_prev_ref)
        h2 = h2_ref[...]
        nxt = slot_rows(meta_next_ref, base_next_ref)
        copy_out(prev, xc_prev_ref, range(N_EXPERTS), lambda rows: jnp.where(i >= 1, rows, 0))
        for c in range(lmax // _CHUNK):
            rows = slice(c * _CHUNK, (c + 1) * _CHUNK)
            xc_now_ref[rows, :] = _dot(pt_now_ref[rows, :], h2).astype(bf16)
            pt_next_ref[rows, :] = one_hot_rows(nxt, c)

        @pl.when(i == nt - 1)
        def _():
            merge_carry(i, xc_now_ref)
            copy_out(i, xc_now_ref, range(N_EXPERTS), lambda rows: rows)

    @pl.when(slot == 0)
    def _():
        compact(pt0_ref, pt1_ref, xc0_ref, xc1_ref)

    @pl.when(slot == 1)
    def _():
        compact(pt1_ref, pt0_ref, xc1_ref, xc0_ref)

    @pl.when(i == nt - 1)
    def _():
        @pl.when(i >= 1)
        def _():
            _wait_rows(xc0_ref, xs_ref, sem.at[1 - slot], wtot_ref[jnp.maximum(i - 1, 0)])
        _wait_rows(xc0_ref, xs_ref, sem.at[slot], wtot_ref[i])
        _zero_rows(zero_ref, xs_ref, zsem, used_ref[0], start=False)


def _dispatch(plan, h2, meta, rows_total):
    tokens = h2.shape[0]
    tm = min(MIX_TS, tokens)
    nt = tokens // tm
    scalars = (plan["voff"], plan["coff"], plan["keep"], plan["row0"], plan["wlen"], plan["wtot"],
               plan["used"])
    return pl.pallas_call(
        _dispatch_body,
        grid_spec=pltpu.PrefetchScalarGridSpec(
            num_scalar_prefetch=len(scalars),
            grid=(nt,),
            in_specs=[pl.BlockSpec((tm, D_MODEL), lambda i, *_: (i, 0)),
                      pl.BlockSpec((tm, LANES), lambda i, *_: (i, 0)),
                      pl.BlockSpec((8, LANES), lambda i, *_: (i, 0)),
                      pl.BlockSpec((tm, LANES), lambda i, *_: (jnp.minimum(i + 1, nt - 1), 0)),
                      pl.BlockSpec((8, LANES), lambda i, *_: (jnp.minimum(i + 1, nt - 1), 0))],
            out_specs=pl.BlockSpec(memory_space=pl.ANY),
            scratch_shapes=[pltpu.VMEM((_lmax(tm), D_MODEL), bf16),
                            pltpu.VMEM((_lmax(tm), D_MODEL), bf16),
                            pltpu.VMEM((_lmax(tm), tm), bf16),
                            pltpu.VMEM((_lmax(tm), tm), bf16),
                            pltpu.VMEM((N_EXPERTS, BF16_ROWS, D_MODEL), bf16),
                            pltpu.VMEM((FFN_TM, D_MODEL), bf16),
                            pltpu.SemaphoreType.DMA((2,)), pltpu.SemaphoreType.DMA],
        ),
        out_shape=jax.ShapeDtypeStruct((rows_total, D_MODEL), bf16),
        compiler_params=pltpu.CompilerParams(
            dimension_semantics=("arbitrary",), vmem_limit_bytes=VMEM_LIMIT),
        name="dispatch",
    )(*scalars, h2, meta, plan["base"], meta, plan["base"])


def _ffn_body(gstart_ref, gsize_ref, used_ref, xs_ref, wgu_ref, bgu_ref, wd_ref, bd_ref, os_ref,
              xbuf_ref, obuf_ref, zero_ref, state_ref, isem, osem, zsem):
    e = pl.program_id(0)
    ne = pl.num_programs(0)
    tile = xbuf_ref.shape[1]
    base = gstart_ref[e]
    n = gsize_ref[e]
    n_tiles = (n + tile - 1) // tile

    def rows_of(size, t):
        return jnp.minimum(tile, size - t * tile)

    def in_copy(row0, rows, slot):
        return pltpu.make_async_copy(xs_ref.at[pl.ds(row0, rows)],
                                     xbuf_ref.at[slot, pl.ds(0, rows)], isem.at[slot])

    def out_copy(row0, rows, slot):
        return pltpu.make_async_copy(obuf_ref.at[slot, pl.ds(0, rows)],
                                     os_ref.at[pl.ds(row0, rows)], osem.at[slot])

    def start_tile(copy, row0, rows, slot):
        row0 = pl.multiple_of(row0, BF16_ROWS)

        @pl.when(rows == tile)
        def _():
            copy(row0, tile, slot).start(priority=_TILE_DMA_PRIORITY)

        @pl.when(rows < tile)
        def _():
            copy(row0, pl.multiple_of(rows, BF16_ROWS), slot).start()

    def wait_tile(copy, rows, slot):
        copy(0, pl.multiple_of(rows, BF16_ROWS), slot).wait()

    @pl.when(e == 0)
    def _():
        xbuf_ref[...] = jnp.zeros_like(xbuf_ref)
        zero_ref[...] = jnp.zeros_like(zero_ref)
        _zero_rows(zero_ref, os_ref, zsem, used_ref[0], start=True)
        state_ref[0] = 0
        state_ref[1] = 0
        state_ref[2] = 0
        state_ref[3] = 0

    done = state_ref[0]

    @pl.when(jnp.logical_and(n_tiles > 0, state_ref[1] == 0))
    def _():
        start_tile(in_copy, base, rows_of(n, 0), done & 1)

    nxt = jnp.minimum(e + 1, ne - 1)
    has_next = jnp.logical_and(e + 1 < ne, gsize_ref[nxt] > 0)

    def step(t, carry):
        slot = (done + t) & 1

        @pl.when(t + 1 < n_tiles)
        def _():
            start_tile(in_copy, base + (t + 1) * tile, rows_of(n, t + 1), 1 - slot)

        @pl.when(jnp.logical_and(t + 1 == n_tiles, has_next))
        def _():
            start_tile(in_copy, gstart_ref[nxt], rows_of(gsize_ref[nxt], 0), 1 - slot)

        rows = rows_of(n, t)
        wait_tile(in_copy, rows, slot)

        @pl.when(state_ref[2 + slot] > 0)
        def _():
            wait_tile(out_copy, state_ref[2 + slot], slot)

        def ffn(m):
            gu = _dot(xbuf_ref[slot, 0:m, :], wgu_ref[...].astype(bf16)) + bgu_ref[...]
            gate = jnp.minimum(gu[:, :D_FF], SWIGLU_LIMIT)
            lin = jnp.clip(gu[:, D_FF:], -SWIGLU_LIMIT, SWIGLU_LIMIT)
            act = (lin + 1.0) * (gate * jax.nn.sigmoid(SWIGLU_ALPHA * gate))
            out = _dot(act.astype(bf16), wd_ref[...].astype(bf16)) + bd_ref[...]
            obuf_ref[slot, 0:m, :] = out.astype(bf16)

        @pl.when(rows > tile // 2)
        def _():
            ffn(tile)

        @pl.when(rows <= tile // 2)
        def _():
            ffn(tile // 2)

        start_tile(out_copy, base + t * tile, rows, slot)
        state_ref[2 + slot] = rows
        return carry

    lax.fori_loop(0, n_tiles, step, 0)

    @pl.when(n_tiles > 0)
    def _():
        state_ref[0] = done + n_tiles
        state_ref[1] = has_next.astype(jnp.int32)

    @pl.when(e == ne - 1)
    def _():
        for slot in range(2):
            @pl.when(state_ref[2 + slot] > 0)
            def _():
                wait_tile(out_copy, state_ref[2 + slot], slot)
        _zero_rows(zero_ref, os_ref, zsem, used_ref[0], start=False)


def _ffn(gstart, gsize, used, xs, wgu, bgu, wd, bd):
    rows_total = xs.shape[0]
    return pl.pallas_call(
        _ffn_body,
        grid_spec=pltpu.PrefetchScalarGridSpec(
            num_scalar_prefetch=3,
            grid=(N_EXPERTS,),
            in_specs=[pl.BlockSpec(memory_space=pl.ANY),
                      pl.BlockSpec((None, D_MODEL, 2 * D_FF), lambda e, *_: (e, 0, 0)),
                      pl.BlockSpec((None, 1, 2 * D_FF), lambda e, *_: (e, 0, 0)),
                      pl.BlockSpec((None, D_FF, D_MODEL), lambda e, *_: (e, 0, 0)),
                      pl.BlockSpec((None, 1, D_MODEL), lambda e, *_: (e, 0, 0))],
            out_specs=pl.BlockSpec(memory_space=pl.ANY),
            scratch_shapes=[pltpu.VMEM((2, FFN_TM, D_MODEL), bf16),
                            pltpu.VMEM((2, FFN_TM, D_MODEL), bf16),
                            pltpu.VMEM((FFN_TM, D_MODEL), bf16),
                            pltpu.SMEM((4,), jnp.int32),
                            pltpu.SemaphoreType.DMA((2,)), pltpu.SemaphoreType.DMA((2,)),
                            pltpu.SemaphoreType.DMA],
        ),
        out_shape=jax.ShapeDtypeStruct((rows_total, D_MODEL), bf16),
        compiler_params=pltpu.CompilerParams(
            dimension_semantics=("arbitrary",), vmem_limit_bytes=VMEM_LIMIT),
        name="experts",
    )(gstart, gsize, used, xs, wgu, bgu, wd, bd)


def _combine_body(final_norm, voff_ref, row0_ref, rlen_ref, rtot_ref,
                  os_ref, meta_ref, base_ref, meta_next_ref, base_next_ref, x1_ref, gfin_ref,
                  out_ref, oc0_ref, oc1_ref, p0_ref, p1_ref, sem):
    i = pl.program_id(0)
    nt = pl.num_programs(0)
    tm = x1_ref.shape[0]
    lmax = oc0_ref.shape[0]
    slot = i & 1

    def slots_and_gates(m_ref, b_ref):
        meta = m_ref[...]
        slots = [s.astype(jnp.int32) for s in _token_slots(meta, b_ref[0:1, :])]
        gates = [meta[:, _META_GATE + kk:_META_GATE + kk + 1] for kk in range(TOP_K)]
        return slots, gates

    def gate_cols(slots, gates, c):
        col = lax.broadcasted_iota(jnp.int32, (tm, _CHUNK), 1) + c * _CHUNK
        p = jnp.zeros((tm, _CHUNK), f32)
        for s, g in zip(slots, gates):
            p = jnp.where(col == s, g, p)
        return p.astype(bf16)

    def fetch(tile, into_ref, sem_slot, rows_of):
        for e in range(N_EXPERTS):
            j = tile * N_EXPERTS + e
            _segment_copy(os_ref, into_ref, sem.at[sem_slot], row0_ref[j], voff_ref[j],
                          rows_of(rlen_ref[j])).start()

    @pl.when(i == 0)
    def _():
        oc0_ref[...] = jnp.zeros_like(oc0_ref)
        oc1_ref[...] = jnp.zeros_like(oc1_ref)
        fetch(0, oc0_ref, 0, lambda rows: rows)
        slots, gates = slots_and_gates(meta_ref, base_ref)
        for c in range(lmax // _CHUNK):
            p0_ref[:, c * _CHUNK:(c + 1) * _CHUNK] = gate_cols(slots, gates, c)

    _wait_rows(os_ref, oc0_ref, sem.at[slot], rtot_ref[i])

    def restore(p_now_ref, p_next_ref, oc_now_ref, oc_next_ref):
        fetch(jnp.minimum(i + 1, nt - 1), oc_next_ref, 1 - slot,
              lambda rows: jnp.where(i + 1 < nt, rows, 0))
        slots, gates = slots_and_gates(meta_next_ref, base_next_ref)
        p_now = p_now_ref[...]
        n_build = lmax // _CHUNK
        n_out = D_MODEL // _CHUNK
        ssq = jnp.zeros((tm, 1), f32)
        for n in range(n_out):
            cols = slice(n * _CHUNK, (n + 1) * _CHUNK)
            xo = x1_ref[:, cols] + _dot(p_now, oc_now_ref[:, cols])
            out_ref[:, cols] = xo
            ssq = ssq + jnp.sum(xo * xo, axis=-1, keepdims=True)
            for c in range(n * n_build // n_out, (n + 1) * n_build // n_out):
                p_next_ref[:, c * _CHUNK:(c + 1) * _CHUNK] = gate_cols(slots, gates, c)
        if final_norm:
            out_ref[...] = out_ref[...] * lax.rsqrt(ssq * (1.0 / D_MODEL) + EPS) * gfin_ref[...]

    @pl.when(slot == 0)
    def _():
        restore(p0_ref, p1_ref, oc0_ref, oc1_ref)

    @pl.when(slot == 1)
    def _():
        restore(p1_ref, p0_ref, oc1_ref, oc0_ref)


def _combine(plan, os, meta, x1, gfin, final_norm):
    tokens = x1.shape[0]
    tm = min(MIX_TS, tokens)
    nt = tokens // tm
    scalars = (plan["voff"], plan["row0"], plan["rlen"], plan["rtot"])
    return pl.pallas_call(
        functools.partial(_combine_body, final_norm),
        grid_spec=pltpu.PrefetchScalarGridSpec(
            num_scalar_prefetch=len(scalars),
            grid=(nt,),
            in_specs=[pl.BlockSpec(memory_space=pl.ANY),
                      pl.BlockSpec((tm, LANES), lambda i, *_: (i, 0)),
                      pl.BlockSpec((8, LANES), lambda i, *_: (i, 0)),
                      pl.BlockSpec((tm, LANES), lambda i, *_: (jnp.minimum(i + 1, nt - 1), 0)),
                      pl.BlockSpec((8, LANES), lambda i, *_: (jnp.minimum(i + 1, nt - 1), 0)),
                      pl.BlockSpec((tm, D_MODEL), lambda i, *_: (i, 0)),
                      pl.BlockSpec((1, D_MODEL), lambda i, *_: (0, 0))],
            out_specs=pl.BlockSpec((tm, D_MODEL), lambda i, *_: (i, 0)),
            scratch_shapes=[pltpu.VMEM((_lmax(tm), D_MODEL), bf16),
                            pltpu.VMEM((_lmax(tm), D_MODEL), bf16),
                            pltpu.VMEM((tm, _lmax(tm)), bf16),
                            pltpu.VMEM((tm, _lmax(tm)), bf16),
                            pltpu.SemaphoreType.DMA((2,))],
        ),
        out_shape=jax.ShapeDtypeStruct((tokens, D_MODEL), f32),
        compiler_params=pltpu.CompilerParams(
            dimension_semantics=("arbitrary",), vmem_limit_bytes=VMEM_LIMIT),
        name="combine",
    )(*scalars, os, meta, plan["base"], meta, plan["base"], x1, gfin)


def _plan(counts, nt):
    up = lambda v: (v + (BF16_ROWS - 1)) // BF16_ROWS * BF16_ROWS
    down = lambda v: v // BF16_ROWS * BF16_ROWS
    c = counts.reshape(nt, 8, LANES)[:, 0, :N_EXPERTS]
    gsize = up(jnp.sum(c, axis=0))
    gstart = jnp.cumsum(gsize) - gsize
    before = jnp.cumsum(c, axis=0) - c
    a = before % BF16_ROWS
    length = a + c
    voff = jnp.cumsum(up(length), axis=1) - up(length)
    last = jnp.arange(nt)[:, None] == nt - 1
    wlen = jnp.where(last, up(length), down(length))
    rlen = jnp.where(c > 0, up(length), 0)
    base = jnp.zeros((nt, 8, LANES), jnp.float32)
    base = base.at[:, :, :N_EXPERTS].set((voff + a).astype(jnp.float32)[:, None, :])
    i32 = lambda v: v.reshape(-1).astype(jnp.int32)
    return dict(
        voff=i32(voff), coff=i32(voff + down(length)), keep=i32(length % BF16_ROWS),
        row0=i32(gstart[None, :] + before - a), wlen=i32(wlen), wtot=i32(jnp.sum(wlen, axis=1)),
        rlen=i32(rlen), rtot=i32(jnp.sum(rlen, axis=1)), base=base.reshape(nt * 8, LANES),
        gstart=i32(gstart), gsize=i32(gsize), used=i32(jnp.sum(gsize, keepdims=True)))


def kernel(x, g_mix, w_in, w_alpha_up, b_alpha, g_gla, w_conv, w_branch_a, w_branch_b, w_out, g_ffn, w_router, b_router, w_gate_up, b_gate_up, w_down, b_down, g_final):
    batch, seq, _ = x.shape
    tokens = batch * seq
    depth = w_in.shape[0]
    x2 = x.reshape(tokens, D_MODEL)
    nt = tokens // min(MIX_TS, seq)
    rows_total = tokens * TOP_K + N_EXPERTS * (BF16_ROWS - 1)
    rows_total = -(-rows_total // BF16_ROWS) * BF16_ROWS

    for l in range(depth):
        wi = w_in[l]
        a0 = 2 * QK_W + 2 * V_W
        wmain = jnp.concatenate([wi[:, :a0], wi[:, a0 + GLA_RANK:]], axis=1).astype(bf16)
        walr = jnp.pad(wi[:, a0:a0 + GLA_RANK], ((0, 0), (0, LANES - GLA_RANK))).astype(bf16)
        wup = jnp.pad(w_alpha_up[l], ((0, LANES - GLA_RANK), (0, 0))).astype(bf16)
        wr = jnp.pad(w_router[l], ((0, 0), (0, LANES - N_EXPERTS)))
        wrh = wr.astype(bf16)
        wrl = (wr - wrh.astype(f32)).astype(bf16)
        br = jnp.pad(b_router[l], (0, LANES - N_EXPERTS)).reshape(1, LANES)
        wconv = jnp.pad(w_conv[l], ((0, 8 - CONV_K), (0, 0)))

        x1, h2, logits = _mixer(
            x2, batch, seq, g_mix[l].reshape(1, D_MODEL), wmain, walr, wup,
            b_alpha[l].reshape(1, QK_W), g_gla[l].reshape(1, V_W), wconv,
            w_branch_a[l].astype(bf16), w_branch_b[l].astype(bf16), w_out[l].astype(bf16),
            g_ffn[l].reshape(1, D_MODEL), wrh, wrl, br)

        meta, counts = _route(logits, min(MIX_TS, seq))
        plan = _plan(counts, nt)
        xs = _dispatch(plan, h2, meta, rows_total)
        os = _ffn(plan["gstart"], plan["gsize"], plan["used"], xs,
                  w_gate_up[l], b_gate_up[l].reshape(N_EXPERTS, 1, 2 * D_FF),
                  w_down[l], b_down[l].reshape(N_EXPERTS, 1, D_MODEL))
        x2 = _combine(plan, os, meta, x1, g_final.reshape(1, D_MODEL), l == depth - 1)
    return x2.reshape(batch, seq, D_MODEL)
```

```python
import functools

import jax
import jax.numpy as jnp
from jax import lax
from jax.experimental import pallas as pl
from jax.experimental.pallas import tpu as pltpu

D_MODEL = 1024
GLA_HEADS = 4
GLA_DK = 128
GLA_DV = 256
GLA_RANK = 16
GLA_TAU = 16.0
GLA_CHUNK = 64
_CHUNK_SHIFT = GLA_CHUNK.bit_length() - 1
CONV_K = 3
N_EXPERTS = 32
TOP_K = 4
D_FF = 1024
SWIGLU_LIMIT = 7.0
SWIGLU_ALPHA = 1.702
EPS = 1e-6

QK_W = GLA_HEADS * GLA_DK
V_W = GLA_HEADS * GLA_DV

LANES = 128
BF16_ROWS = 16
VMEM_LIMIT = 56 * 1024 * 1024

MIX_TS = 256
ROUTE_ROWS = 1024
_CHUNK = 256
FFN_TM = 1024
_META_EXPERT, _META_GATE, _META_RANK = 0, TOP_K, 2 * TOP_K
_TILE_DMA_PRIORITY = 1

_C_Q, _C_K, _C_V, _C_R = 0, QK_W, 2 * QK_W, 2 * QK_W + V_W
_C_CB = _C_R + V_W
_C_CC = _C_CB + D_MODEL
_C_CX = _C_CC + D_MODEL
_C_GA = _C_CX + D_MODEL
_C_GB = _C_GA + D_MODEL
_W_MAIN = _C_GB + D_MODEL

f32 = jnp.float32
bf16 = jnp.bfloat16


def _rms(x, g):
    return x * lax.rsqrt(jnp.mean(x * x, axis=-1, keepdims=True) + EPS) * g


def _dot(a, b):
    return jnp.dot(a, b, preferred_element_type=f32)


def _split_bf16(a):
    hi = a.astype(bf16)
    lo = (a - hi.astype(f32)).astype(bf16)
    return hi, lo


def _mixer_body(x_ref, gmix_ref, wmain_ref, walr_ref, wup_ref, balpha_ref, ggla_ref, wconv_ref,
                wa_ref, wb_ref, wo_ref, gffn_ref, wrh_ref, wrl_ref, br_ref,
                x1_ref, h2_ref, logit_ref,
                state_ref, ubuf_ref, obuf_ref):
    ts = x_ref.shape[0]
    nchunk = ts // GLA_CHUNK

    @pl.when(pl.program_id(1) == 0)
    def _():
        state_ref[...] = jnp.zeros_like(state_ref)
        ubuf_ref[0:8, :] = jnp.zeros((8, D_MODEL), f32)

    x = x_ref[...]
    hb = _rms(x, gmix_ref[...]).astype(bf16)

    def proj(c0, width):
        return _dot(hb, wmain_ref[:, c0:c0 + width])

    a_lr = _dot(hb, walr_ref[...])
    pre = _dot(a_lr.astype(bf16), wup_ref[...]) + balpha_ref[...]
    log_a = -(jnp.maximum(-pre, 0.0) + jnp.log1p(jnp.exp(-jnp.abs(pre)))) * (1.0 / GLA_TAU)
    ri = lax.broadcasted_iota(jnp.int32, (ts, ts), 0)
    ci = lax.broadcasted_iota(jnp.int32, (ts, ts), 1)
    same_chunk = (ri >> _CHUNK_SHIFT) == (ci >> _CHUNK_SHIFT)
    causal = jnp.logical_and(same_chunk, ci <= ri)
    tri = jnp.where(causal, 1.0, 0.0).astype(bf16)
    la_hi, la_lo = _split_bf16(log_a)
    b = _dot(tri, la_hi) + _dot(tri, la_lo)
    lasts = [b[(c + 1) * GLA_CHUNK - 1:(c + 1) * GLA_CHUNK, :] for c in range(nchunk)]
    b_tot = jnp.concatenate([jnp.broadcast_to(r, (GLA_CHUNK, QK_W)) for r in lasts], axis=0)
    e_pos = jnp.exp(b)
    e_neg = jnp.exp(-b)
    e_tail = jnp.exp(b_tot - b)
    e_tot = [jnp.exp(r) for r in lasts]

    q = proj(_C_Q, QK_W)
    k = proj(_C_K, QK_W)
    q_dec = (q * (GLA_DK ** -0.5) * e_pos).astype(bf16)
    k_inv = (k * e_neg).astype(bf16)
    k_tail = (k * e_tail).astype(bf16)
    vb = proj(_C_V, V_W).astype(bf16)

    for hh in range(GLA_HEADS):
        qs = slice(hh * GLA_DK, (hh + 1) * GLA_DK)
        vs = slice(hh * GLA_DV, (hh + 1) * GLA_DV)
        sc = lax.dot_general(q_dec[:, qs], k_inv[:, qs], (((1,), (1,)), ((), ())),
                             preferred_element_type=f32)
        sc = jnp.where(causal, sc, 0.0).astype(bf16)
        obuf_ref[:, vs] = _dot(sc, vb[:, vs])
        st = state_ref[hh]
        for c in range(nchunk):
            rs = slice(c * GLA_CHUNK, (c + 1) * GLA_CHUNK)
            o_inter = lax.dot_general(q_dec[rs, qs], st.astype(bf16), (((1,), (1,)), ((), ())),
                                      preferred_element_type=f32)
            obuf_ref[rs, vs] += o_inter
            upd = lax.dot_general(vb[rs, vs], k_tail[rs, qs], (((0,), (0,)), ((), ())),
                                  preferred_element_type=f32)
            st = st * e_tot[c][:, qs] + upd
        state_ref[hh] = st

    r = proj(_C_R, V_W)
    ggla = ggla_ref[...]
    o_parts = []
    for hh in range(GLA_HEADS):
        vs = slice(hh * GLA_DV, (hh + 1) * GLA_DV)
        o_h = _rms(obuf_ref[:, vs], ggla[:, vs])
        r_h = r[:, vs]
        o_parts.append((o_h * (r_h * jax.nn.sigmoid(r_h))).astype(bf16))
    y_a = _dot(jnp.concatenate(o_parts, axis=1), wa_ref[...])

    u = proj(_C_CC, D_MODEL) * proj(_C_CX, D_MODEL)
    ubuf_ref[8:8 + ts, :] = u
    wc = wconv_ref[...]
    y_conv = (wc[0:1, :] * ubuf_ref[6:6 + ts, :] + wc[1:2, :] * ubuf_ref[7:7 + ts, :]
              + wc[2:3, :] * u)
    ubuf_ref[0:8, :] = ubuf_ref[ts:ts + 8, :]
    y_b = _dot((proj(_C_CB, D_MODEL) * y_conv).astype(bf16), wb_ref[...])

    mixed = (jax.nn.sigmoid(proj(_C_GA, D_MODEL)) * y_a
             + jax.nn.sigmoid(proj(_C_GB, D_MODEL)) * y_b)
    x1 = x + _dot(mixed.astype(bf16), wo_ref[...])
    x1_ref[...] = x1

    h2 = _rms(x1, gffn_ref[...])
    h2_hi, h2_lo = _split_bf16(h2)
    h2_ref[...] = h2_hi
    wrh = wrh_ref[...]
    logit_ref[...] = (_dot(h2_hi, wrh) + _dot(h2_lo, wrh) + _dot(h2_hi, wrl_ref[...])
                      + br_ref[...])


def _mixer(x2, batch, seq, gmix, wmain, walr, wup, balpha, ggla, wconv, wa, wb, wo, gffn, wrh, wrl, br):
    ts = min(MIX_TS, seq)
    ns = seq // ts
    tokens = batch * seq

    def const(shape):
        return pl.BlockSpec(shape, lambda b, s: (0,) * len(shape), pipeline_mode=pl.Buffered(1))

    def row(width, dtype):
        return pl.BlockSpec((ts, width), lambda b, s: (b * ns + s, 0))

    return pl.pallas_call(
        _mixer_body,
        grid=(batch, ns),
        in_specs=[row(D_MODEL, f32), const(gmix.shape), const(wmain.shape), const(walr.shape),
                  const(wup.shape), const(balpha.shape), const(ggla.shape), const(wconv.shape),
                  const(wa.shape), const(wb.shape), const(wo.shape), const(gffn.shape),
                  const(wrh.shape), const(wrl.shape), const(br.shape)],
        out_specs=[row(D_MODEL, f32), row(D_MODEL, bf16), row(LANES, f32)],
        out_shape=[jax.ShapeDtypeStruct((tokens, D_MODEL), f32),
                   jax.ShapeDtypeStruct((tokens, D_MODEL), bf16),
                   jax.ShapeDtypeStruct((tokens, LANES), f32)],
        scratch_shapes=[pltpu.VMEM((GLA_HEADS, GLA_DV, GLA_DK), f32),
                        pltpu.VMEM((ts + 8, D_MODEL), f32),
                        pltpu.VMEM((ts, V_W), f32)],
        compiler_params=pltpu.CompilerParams(
            dimension_semantics=("parallel", "arbitrary"), vmem_limit_bytes=VMEM_LIMIT),
        name="mixer",
    )(x2, gmix, wmain, walr, wup, balpha, ggla, wconv, wa, wb, wo, gffn, wrh, wrl, br)


def _route_body(tile, logit_ref, meta_ref, cnt_ref):
    rows = logit_ref.shape[0]
    lane = lax.broadcasted_iota(jnp.int32, (rows, LANES), 1)
    lane_f = lane.astype(f32)
    lg = jnp.where(lane < N_EXPERTS, logit_ref[...], -jnp.inf)
    sels, tops, firsts = [], [], []
    for _ in range(TOP_K):
        m = jnp.max(lg, axis=-1, keepdims=True)
        first = jnp.min(jnp.where(lg == m, lane_f, float(LANES)), axis=-1, keepdims=True)
        sel = lane_f == first
        sels.append(sel)
        tops.append(m)
        firsts.append(first)
        lg = jnp.where(sel, -jnp.inf, lg)
    ps = [jnp.exp(t - tops[0]) for t in tops]
    denom = ps[0] + ps[1] + ps[2] + ps[3]
    onehot = jnp.zeros((rows, LANES), f32)
    for sel in sels:
        onehot = onehot + jnp.where(sel, 1.0, 0.0)
    onehot_b = onehot.astype(bf16)
    ri = lax.broadcasted_iota(jnp.int32, (tile, tile), 0)
    ci = lax.broadcasted_iota(jnp.int32, (tile, tile), 1)
    strict_lower = jnp.where(ci < ri, 1.0, 0.0).astype(bf16)
    ones = jnp.ones((8, tile), bf16)
    ranks = []
    for j in range(rows // tile):
        part = onehot_b[j * tile:(j + 1) * tile, :]
        ranks.append(_dot(strict_lower, part))
        cnt_ref[j * 8:(j + 1) * 8, :] = _dot(ones, part).astype(jnp.int32)
    rank = jnp.concatenate(ranks, axis=0)
    meta = jnp.zeros((rows, LANES), f32)
    for kk in range(TOP_K):
        meta = jnp.where(lane == _META_EXPERT + kk, firsts[kk], meta)
        meta = jnp.where(lane == _META_GATE + kk, ps[kk] / denom, meta)
        rank_k = jnp.sum(jnp.where(sels[kk], rank, 0.0), axis=-1, keepdims=True)
        meta = jnp.where(lane == _META_RANK + kk, rank_k, meta)
    meta_ref[...] = meta


def _route(logits, tile):
    tokens = logits.shape[0]
    rows = min(ROUTE_ROWS, tokens)
    steps = tokens // rows
    per_step = rows // tile
    return pl.pallas_call(
        functools.partial(_route_body, tile),
        grid=(steps,),
        in_specs=[pl.BlockSpec((rows, LANES), lambda i: (i, 0))],
        out_specs=[pl.BlockSpec((rows, LANES), lambda i: (i, 0)),
                   pl.BlockSpec((per_step * 8, LANES), lambda i: (i, 0))],
        out_shape=[jax.ShapeDtypeStruct((tokens, LANES), f32),
                   jax.ShapeDtypeStruct((tokens // tile * 8, LANES), jnp.int32)],
        compiler_params=pltpu.CompilerParams(
            dimension_semantics=("parallel",), vmem_limit_bytes=VMEM_LIMIT),
        name="route",
    )(logits)


def _lmax(tm):
    full = tm * TOP_K + N_EXPERTS * 2 * (BF16_ROWS - 1)
    return -(-full // 256) * 256


def _token_slots(meta, base_row):
    lane_f = lax.broadcasted_iota(jnp.int32, meta.shape, 1).astype(f32)
    slots = []
    for kk in range(TOP_K):
        sel = lane_f == meta[:, _META_EXPERT + kk:_META_EXPERT + kk + 1]
        first = jnp.sum(jnp.where(sel, base_row, 0.0), axis=-1, keepdims=True)
        slots.append(first + meta[:, _META_RANK + kk:_META_RANK + kk + 1])
    return slots


def _segment_copy(src, dst, sem, src_off, dst_off, rows):
    src_off = pl.multiple_of(src_off, BF16_ROWS)
    dst_off = pl.multiple_of(dst_off, BF16_ROWS)
    rows = pl.multiple_of(rows, BF16_ROWS)
    return pltpu.make_async_copy(src.at[pl.ds(src_off, rows)], dst.at[pl.ds(dst_off, rows)], sem)


def _wait_rows(src, dst, sem, total):
    @pl.when(total > 0)
    def _():
        _segment_copy(src, dst, sem, 0, 0, total).wait()


def _zero_rows(zero_ref, dst_ref, sem, first, start):
    tile = zero_ref.shape[0]
    n = dst_ref.shape[0] - first
    n_full = n // tile

    def full(j):
        return _segment_copy(zero_ref, dst_ref, sem, 0, first + j * tile, tile)

    def rest():
        return _segment_copy(zero_ref, dst_ref, sem, 0, first + n_full * tile, n - n_full * tile)

    def step(j, carry):
        full(j).start() if start else full(j).wait()
        return carry

    lax.fori_loop(0, n_full, step, 0)

    @pl.when(n - n_full * tile > 0)
    def _():
        rest().start() if start else rest().wait()


def _dispatch_body(voff_ref, coff_ref, keep_ref, row0_ref, wlen_ref, wtot_ref, used_ref,
                   h2_ref, meta_ref, base_ref, meta_next_ref, base_next_ref, xs_ref,
                   xc0_ref, xc1_ref, pt0_ref, pt1_ref, carry_ref, zero_ref, sem, zsem):
    i = pl.program_id(0)
    nt = pl.num_programs(0)
    tm = h2_ref.shape[0]
    lmax = xc0_ref.shape[0]
    slot = i & 1

    def slot_rows(m_ref, b_ref):
        meta = m_ref[...]
        lane = lax.broadcasted_iota(jnp.int32, meta.shape, 1)
        slots = jnp.zeros(meta.shape, f32)
        for kk, s in enumerate(_token_slots(meta, b_ref[0:1, :])):
            slots = jnp.where(lane == kk, s, slots)
        return slots.T[0:8, :].astype(jnp.int32)

    def one_hot_rows(slots_t, c):
        row = lax.broadcasted_iota(jnp.int32, (_CHUNK, tm), 0) + c * _CHUNK
        pt = jnp.zeros((_CHUNK, tm), f32)
        for kk in range(TOP_K):
            pt = jnp.where(row == slots_t[kk:kk + 1, :], 1.0, pt)
        return pt.astype(bf16)

    @pl.when(i == 0)
    def _():
        carry_ref[...] = jnp.zeros_like(carry_ref)
        zero_ref[...] = jnp.zeros_like(zero_ref)
        _zero_rows(zero_ref, xs_ref, zsem, used_ref[0], start=True)
        first = slot_rows(meta_ref, base_ref)
        for c in range(lmax // _CHUNK):
            pt0_ref[c * _CHUNK:(c + 1) * _CHUNK, :] = one_hot_rows(first, c)

    @pl.when(i >= 2)
    def _():
        _wait_rows(xc0_ref, xs_ref, sem.at[slot], wtot_ref[jnp.maximum(i - 2, 0)])

    def merge_carry(t, buf):
        def block(off):
            return buf.at[pl.ds(pl.multiple_of(off, BF16_ROWS), BF16_ROWS)]

        for e in range(N_EXPERTS):
            blk = block(voff_ref[t * N_EXPERTS + e])
            blk[...] = blk[...] + carry_ref[e]
        for e in range(N_EXPERTS):
            j = t * N_EXPERTS + e
            tail = block(coff_ref[j])[...]
            carry_ref[e] = jnp.where(keep_ref[j] > 0, tail, jnp.zeros_like(tail))

    def copy_out(t, buf):
        for e in range(N_EXPERTS):
            j = t * N_EXPERTS + e
            _segment_copy(buf, xs_ref, sem.at[slot], voff_ref[j], row0_ref[j], wlen_ref[j]).start()

    def compact(pt_now_ref, pt_next_ref, xc_now_ref):
        h2 = h2_ref[...]
        nxt = slot_rows(meta_next_ref, base_next_ref)
        for c in range(lmax // _CHUNK):
            rows = slice(c * _CHUNK, (c + 1) * _CHUNK)
            xc_now_ref[rows, :] = _dot(pt_now_ref[rows, :], h2).astype(bf16)
            pt_next_ref[rows, :] = one_hot_rows(nxt, c)
        merge_carry(i, xc_now_ref)
        copy_out(i, xc_now_ref)

    @pl.when(slot == 0)
    def _():
        compact(pt0_ref, pt1_ref, xc0_ref)

    @pl.when(slot == 1)
    def _():
        compact(pt1_ref, pt0_ref, xc1_ref)

    @pl.when(i == nt - 1)
    def _():
        @pl.when(i >= 1)
        def _():
            _wait_rows(xc0_ref, xs_ref, sem.at[1 - slot], wtot_ref[jnp.maximum(i - 1, 0)])
        _wait_rows(xc0_ref, xs_ref, sem.at[slot], wtot_ref[i])
        _zero_rows(zero_ref, xs_ref, zsem, used_ref[0], start=False)


def _dispatch(plan, h2, meta, rows_total):
    tokens = h2.shape[0]
    tm = min(MIX_TS, tokens)
    nt = tokens // tm
    scalars = (plan["voff"], plan["coff"], plan["keep"], plan["row0"], plan["wlen"], plan["wtot"],
               plan["used"])
    return pl.pallas_call(
        _dispatch_body,
        grid_spec=pltpu.PrefetchScalarGridSpec(
            num_scalar_prefetch=len(scalars),
            grid=(nt,),
            in_specs=[pl.BlockSpec((tm, D_MODEL), lambda i, *_: (i, 0)),
                      pl.BlockSpec((tm, LANES), lambda i, *_: (i, 0)),
                      pl.BlockSpec((8, LANES), lambda i, *_: (i, 0)),
                      pl.BlockSpec((tm, LANES), lambda i, *_: (jnp.minimum(i + 1, nt - 1), 0)),
                      pl.BlockSpec((8, LANES), lambda i, *_: (jnp.minimum(i + 1, nt - 1), 0))],
            out_specs=pl.BlockSpec(memory_space=pl.ANY),
            scratch_shapes=[pltpu.VMEM((_lmax(tm), D_MODEL), bf16),
                            pltpu.VMEM((_lmax(tm), D_MODEL), bf16),
                            pltpu.VMEM((_lmax(tm), tm), bf16),
                            pltpu.VMEM((_lmax(tm), tm), bf16),
                            pltpu.VMEM((N_EXPERTS, BF16_ROWS, D_MODEL), bf16),
                            pltpu.VMEM((FFN_TM, D_MODEL), bf16),
                            pltpu.SemaphoreType.DMA((2,)), pltpu.SemaphoreType.DMA],
        ),
        out_shape=jax.ShapeDtypeStruct((rows_total, D_MODEL), bf16),
        compiler_params=pltpu.CompilerParams(
            dimension_semantics=("arbitrary",), vmem_limit_bytes=VMEM_LIMIT),
        name="dispatch",
    )(*scalars, h2, meta, plan["base"], meta, plan["base"])


def _ffn_body(gstart_ref, gsize_ref, used_ref, xs_ref, wgu_ref, bgu_ref, wd_ref, bd_ref, os_ref,
              xbuf_ref, obuf_ref, zero_ref, state_ref, isem, osem, zsem):
    e = pl.program_id(0)
    ne = pl.num_programs(0)
    tile = xbuf_ref.shape[1]
    base = gstart_ref[e]
    n = gsize_ref[e]
    n_tiles = (n + tile - 1) // tile

    def rows_of(size, t):
        return jnp.minimum(tile, size - t * tile)

    def in_copy(row0, rows, slot):
        return pltpu.make_async_copy(xs_ref.at[pl.ds(row0, rows)],
                                     xbuf_ref.at[slot, pl.ds(0, rows)], isem.at[slot])

    def out_copy(row0, rows, slot):
        return pltpu.make_async_copy(obuf_ref.at[slot, pl.ds(0, rows)],
                                     os_ref.at[pl.ds(row0, rows)], osem.at[slot])

    def start_tile(copy, row0, rows, slot):
        row0 = pl.multiple_of(row0, BF16_ROWS)

        @pl.when(rows == tile)
        def _():
            copy(row0, tile, slot).start(priority=_TILE_DMA_PRIORITY)

        @pl.when(rows < tile)
        def _():
            copy(row0, pl.multiple_of(rows, BF16_ROWS), slot).start()

    def wait_tile(copy, rows, slot):
        copy(0, pl.multiple_of(rows, BF16_ROWS), slot).wait()

    @pl.when(e == 0)
    def _():
        xbuf_ref[...] = jnp.zeros_like(xbuf_ref)
        zero_ref[...] = jnp.zeros_like(zero_ref)
        _zero_rows(zero_ref, os_ref, zsem, used_ref[0], start=True)
        state_ref[0] = 0
        state_ref[1] = 0
        state_ref[2] = 0
        state_ref[3] = 0

    done = state_ref[0]

    @pl.when(jnp.logical_and(n_tiles > 0, state_ref[1] == 0))
    def _():
        start_tile(in_copy, base, rows_of(n, 0), done & 1)

    nxt = jnp.minimum(e + 1, ne - 1)
    has_next = jnp.logical_and(e + 1 < ne, gsize_ref[nxt] > 0)

    def step(t, carry):
        slot = (done + t) & 1

        @pl.when(t + 1 < n_tiles)
        def _():
            start_tile(in_copy, base + (t + 1) * tile, rows_of(n, t + 1), 1 - slot)

        @pl.when(jnp.logical_and(t + 1 == n_tiles, has_next))
        def _():
            start_tile(in_copy, gstart_ref[nxt], rows_of(gsize_ref[nxt], 0), 1 - slot)

        rows = rows_of(n, t)
        wait_tile(in_copy, rows, slot)

        @pl.when(state_ref[2 + slot] > 0)
        def _():
            wait_tile(out_copy, state_ref[2 + slot], slot)

        def ffn(r0, m):
            gu = _dot(xbuf_ref[slot, r0:r0 + m, :], wgu_ref[...].astype(bf16)) + bgu_ref[...]
            gate = jnp.minimum(gu[:, :D_FF], SWIGLU_LIMIT)
            lin = jnp.clip(gu[:, D_FF:], -SWIGLU_LIMIT, SWIGLU_LIMIT)
            act = (lin + 1.0) * (gate * jax.nn.sigmoid(SWIGLU_ALPHA * gate))
            out = _dot(act.astype(bf16), wd_ref[...].astype(bf16)) + bd_ref[...]
            obuf_ref[slot, r0:r0 + m, :] = out.astype(bf16)

        half, quarter = tile // 2, tile // 4
        for lo, hi, chains in ((half + quarter, tile, ((0, half), (half, half))),
                               (half, half + quarter, ((0, half), (half, quarter))),
                               (quarter, half, ((0, half),)),
                               (0, quarter, ((0, quarter),))):
            @pl.when(jnp.logical_and(rows > lo, rows <= hi))
            def _():
                for r0, m in chains:
                    ffn(r0, m)

        start_tile(out_copy, base + t * tile, rows, slot)
        state_ref[2 + slot] = rows
        return carry

    lax.fori_loop(0, n_tiles, step, 0)

    @pl.when(n_tiles > 0)
    def _():
        state_ref[0] = done + n_tiles
        state_ref[1] = has_next.astype(jnp.int32)

    @pl.when(e == ne - 1)
    def _():
        for slot in range(2):
            @pl.when(state_ref[2 + slot] > 0)
            def _():
                wait_tile(out_copy, state_ref[2 + slot], slot)
        _zero_rows(zero_ref, os_ref, zsem, used_ref[0], start=False)


def _ffn(gstart, gsize, used, xs, wgu, bgu, wd, bd):
    rows_total = xs.shape[0]
    return pl.pallas_call(
        _ffn_body,
        grid_spec=pltpu.PrefetchScalarGridSpec(
            num_scalar_prefetch=3,
            grid=(N_EXPERTS,),
            in_specs=[pl.BlockSpec(memory_space=pl.ANY),
                      pl.BlockSpec((None, D_MODEL, 2 * D_FF), lambda e, *_: (e, 0, 0)),
                      pl.BlockSpec((None, 1, 2 * D_FF), lambda e, *_: (e, 0, 0)),
                      pl.BlockSpec((None, D_FF, D_MODEL), lambda e, *_: (e, 0, 0)),
                      pl.BlockSpec((None, 1, D_MODEL), lambda e, *_: (e, 0, 0))],
            out_specs=pl.BlockSpec(memory_space=pl.ANY),
            scratch_shapes=[pltpu.VMEM((2, FFN_TM, D_MODEL), bf16),
                            pltpu.VMEM((2, FFN_TM, D_MODEL), bf16),
                            pltpu.VMEM((FFN_TM, D_MODEL), bf16),
                            pltpu.SMEM((4,), jnp.int32),
                            pltpu.SemaphoreType.DMA((2,)), pltpu.SemaphoreType.DMA((2,)),
                            pltpu.SemaphoreType.DMA],
        ),
        out_shape=jax.ShapeDtypeStruct((rows_total, D_MODEL), bf16),
        compiler_params=pltpu.CompilerParams(
            dimension_semantics=("arbitrary",), vmem_limit_bytes=VMEM_LIMIT),
        name="experts",
    )(gstart, gsize, used, xs, wgu, bgu, wd, bd)


def _combine_body(final_norm, voff_ref, row0_ref, rlen_ref, rtot_ref,
                  os_ref, meta_ref, base_ref, meta_next_ref, base_next_ref, x1_ref, gfin_ref,
                  out_ref, oc0_ref, oc1_ref, p0_ref, p1_ref, sem):
    i = pl.program_id(0)
    nt = pl.num_programs(0)
    tm = x1_ref.shape[0]
    lmax = oc0_ref.shape[0]
    slot = i & 1

    def slots_and_gates(m_ref, b_ref):
        meta = m_ref[...]
        slots = [s.astype(jnp.int32) for s in _token_slots(meta, b_ref[0:1, :])]
        gates = [meta[:, _META_GATE + kk:_META_GATE + kk + 1] for kk in range(TOP_K)]
        return slots, gates

    def gate_cols(slots, gates, c):
        col = lax.broadcasted_iota(jnp.int32, (tm, _CHUNK), 1) + c * _CHUNK
        p = jnp.zeros((tm, _CHUNK), f32)
        for s, g in zip(slots, gates):
            p = jnp.where(col == s, g, p)
        return p.astype(bf16)

    def fetch(tile, into_ref, sem_slot, rows_of):
        for e in range(N_EXPERTS):
            j = tile * N_EXPERTS + e
            _segment_copy(os_ref, into_ref, sem.at[sem_slot], row0_ref[j], voff_ref[j],
                          rows_of(rlen_ref[j])).start()

    @pl.when(i == 0)
    def _():
        oc0_ref[...] = jnp.zeros_like(oc0_ref)
        oc1_ref[...] = jnp.zeros_like(oc1_ref)
        fetch(0, oc0_ref, 0, lambda rows: rows)
        slots, gates = slots_and_gates(meta_ref, base_ref)
        for c in range(lmax // _CHUNK):
            p0_ref[:, c * _CHUNK:(c + 1) * _CHUNK] = gate_cols(slots, gates, c)

    _wait_rows(os_ref, oc0_ref, sem.at[slot], rtot_ref[i])

    def request_next(oc_next_ref):
        fetch(jnp.minimum(i + 1, nt - 1), oc_next_ref, 1 - slot,
              lambda rows: jnp.where(i + 1 < nt, rows, 0))

    @pl.when(slot == 0)
    def _():
        request_next(oc1_ref)

    @pl.when(slot == 1)
    def _():
        request_next(oc0_ref)

    def restore(p_now_ref, p_next_ref, oc_now_ref):
        slots, gates = slots_and_gates(meta_next_ref, base_next_ref)
        p_now = p_now_ref[...]
        n_build = lmax // _CHUNK
        n_out = D_MODEL // _CHUNK
        ssq = jnp.zeros((tm, 1), f32)
        for n in range(n_out):
            cols = slice(n * _CHUNK, (n + 1) * _CHUNK)
            xo = x1_ref[:, cols] + _dot(p_now, oc_now_ref[:, cols])
            out_ref[:, cols] = xo
            ssq = ssq + jnp.sum(xo * xo, axis=-1, keepdims=True)
            for c in range(n * n_build // n_out, (n + 1) * n_build // n_out):
                p_next_ref[:, c * _CHUNK:(c + 1) * _CHUNK] = gate_cols(slots, gates, c)
        if final_norm:
            out_ref[...] = out_ref[...] * lax.rsqrt(ssq * (1.0 / D_MODEL) + EPS) * gfin_ref[...]

    @pl.when(slot == 0)
    def _():
        restore(p0_ref, p1_ref, oc0_ref)

    @pl.when(slot == 1)
    def _():
        restore(p1_ref, p0_ref, oc1_ref)


def _combine(plan, os, meta, x1, gfin, final_norm):
    tokens = x1.shape[0]
    tm = min(MIX_TS, tokens)
    nt = tokens // tm
    scalars = (plan["voff"], plan["row0"], plan["rlen"], plan["rtot"])
    return pl.pallas_call(
        functools.partial(_combine_body, final_norm),
        grid_spec=pltpu.PrefetchScalarGridSpec(
            num_scalar_prefetch=len(scalars),
            grid=(nt,),
            in_specs=[pl.BlockSpec(memory_space=pl.ANY),
                      pl.BlockSpec((tm, LANES), lambda i, *_: (i, 0)),
                      pl.BlockSpec((8, LANES), lambda i, *_: (i, 0)),
                      pl.BlockSpec((tm, LANES), lambda i, *_: (jnp.minimum(i + 1, nt - 1), 0)),
                      pl.BlockSpec((8, LANES), lambda i, *_: (jnp.minimum(i + 1, nt - 1), 0)),
                      pl.BlockSpec((tm, D_MODEL), lambda i, *_: (i, 0)),
                      pl.BlockSpec((1, D_MODEL), lambda i, *_: (0, 0))],
            out_specs=pl.BlockSpec((tm, D_MODEL), lambda i, *_: (i, 0)),
            scratch_shapes=[pltpu.VMEM((_lmax(tm), D_MODEL), bf16),
                            pltpu.VMEM((_lmax(tm), D_MODEL), bf16),
                            pltpu.VMEM((tm, _lmax(tm)), bf16),
                            pltpu.VMEM((tm, _lmax(tm)), bf16),
                            pltpu.SemaphoreType.DMA((2,))],
        ),
        out_shape=jax.ShapeDtypeStruct((tokens, D_MODEL), f32),
        compiler_params=pltpu.CompilerParams(
            dimension_semantics=("arbitrary",), vmem_limit_bytes=VMEM_LIMIT),
        name="combine",
    )(*scalars, os, meta, plan["base"], meta, plan["base"], x1, gfin)


def _plan(counts, nt):
    up = lambda v: (v + (BF16_ROWS - 1)) // BF16_ROWS * BF16_ROWS
    down = lambda v: v // BF16_ROWS * BF16_ROWS
    c = counts.reshape(nt, 8, LANES)[:, 0, :N_EXPERTS]
    gsize = up(jnp.sum(c, axis=0))
    gstart = jnp.cumsum(gsize) - gsize
    before = jnp.cumsum(c, axis=0) - c
    a = before % BF16_ROWS
    length = a + c
    voff = jnp.cumsum(up(length), axis=1) - up(length)
    last = jnp.arange(nt)[:, None] == nt - 1
    wlen = jnp.where(last, up(length), down(length))
    rlen = jnp.where(c > 0, up(length), 0)
    base = jnp.zeros((nt, 8, LANES), jnp.float32)
    base = base.at[:, :, :N_EXPERTS].set((voff + a).astype(jnp.float32)[:, None, :])
    i32 = lambda v: v.reshape(-1).astype(jnp.int32)
    return dict(
        voff=i32(voff), coff=i32(voff + down(length)), keep=i32(length % BF16_ROWS),
        row0=i32(gstart[None, :] + before - a), wlen=i32(wlen), wtot=i32(jnp.sum(wlen, axis=1)),
        rlen=i32(rlen), rtot=i32(jnp.sum(rlen, axis=1)), base=base.reshape(nt * 8, LANES),
        gstart=i32(gstart), gsize=i32(gsize), used=i32(jnp.sum(gsize, keepdims=True)))


def kernel(x, g_mix, w_in, w_alpha_up, b_alpha, g_gla, w_conv, w_branch_a, w_branch_b, w_out, g_ffn, w_router, b_router, w_gate_up, b_gate_up, w_down, b_down, g_final):
    batch, seq, _ = x.shape
    tokens = batch * seq
    depth = w_in.shape[0]
    x2 = x.reshape(tokens, D_MODEL)
    nt = tokens // min(MIX_TS, seq)
    rows_total = tokens * TOP_K + N_EXPERTS * (BF16_ROWS - 1)
    rows_total = -(-rows_total // BF16_ROWS) * BF16_ROWS

    for l in range(depth):
        wi = w_in[l]
        a0 = 2 * QK_W + 2 * V_W
        wmain = jnp.concatenate([wi[:, :a0], wi[:, a0 + GLA_RANK:]], axis=1).astype(bf16)
        walr = jnp.pad(wi[:, a0:a0 + GLA_RANK], ((0, 0), (0, LANES - GLA_RANK))).astype(bf16)
        wup = jnp.pad(w_alpha_up[l], ((0, LANES - GLA_RANK), (0, 0))).astype(bf16)
        wr = jnp.pad(w_router[l], ((0, 0), (0, LANES - N_EXPERTS)))
        wrh = wr.astype(bf16)
        wrl = (wr - wrh.astype(f32)).astype(bf16)
        br = jnp.pad(b_router[l], (0, LANES - N_EXPERTS)).reshape(1, LANES)
        wconv = jnp.pad(w_conv[l], ((0, 8 - CONV_K), (0, 0)))

        x1, h2, logits = _mixer(
            x2, batch, seq, g_mix[l].reshape(1, D_MODEL), wmain, walr, wup,
            b_alpha[l].reshape(1, QK_W), g_gla[l].reshape(1, V_W), wconv,
            w_branch_a[l].astype(bf16), w_branch_b[l].astype(bf16), w_out[l].astype(bf16),
            g_ffn[l].reshape(1, D_MODEL), wrh, wrl, br)

        meta, counts = _route(logits, min(MIX_TS, seq))
        plan = _plan(counts, nt)
        xs = _dispatch(plan, h2, meta, rows_total)
        os = _ffn(plan["gstart"], plan["gsize"], plan["used"], xs,
                  w_gate_up[l], b_gate_up[l].reshape(N_EXPERTS, 1, 2 * D_FF),
                  w_down[l], b_down[l].reshape(N_EXPERTS, 1, D_MODEL))
        x2 = _combine(plan, os, meta, x1, g_final.reshape(1, D_MODEL), l == depth - 1)
    return x2.reshape(batch, seq, D_MODEL)
```

```python
import functools

import jax
import jax.numpy as jnp
from jax import lax
from jax.experimental import pallas as pl
from jax.experimental.pallas import tpu as pltpu

D_MODEL = 1024
GLA_HEADS = 4
GLA_DK = 128
GLA_DV = 256
GLA_RANK = 16
GLA_TAU = 16.0
GLA_CHUNK = 64
_CHUNK_SHIFT = GLA_CHUNK.bit_length() - 1
CONV_K = 3
N_EXPERTS = 32
TOP_K = 4
D_FF = 1024
SWIGLU_LIMIT = 7.0
SWIGLU_ALPHA = 1.702
EPS = 1e-6

QK_W = GLA_HEADS * GLA_DK
V_W = GLA_HEADS * GLA_DV

LANES = 128
BF16_ROWS = 16
VMEM_LIMIT = 56 * 1024 * 1024

MIX_TS = 256
MIX_ROWS = 2
ROUTE_ROWS = 1024
_CHUNK = 256
FFN_TM = 1024
_META_EXPERT, _META_GATE, _META_RANK = 0, TOP_K, 2 * TOP_K
_TILE_DMA_PRIORITY = 1

_C_Q, _C_K, _C_V, _C_R = 0, QK_W, 2 * QK_W, 2 * QK_W + V_W
_C_CB = _C_R + V_W
_C_CC = _C_CB + D_MODEL
_C_CX = _C_CC + D_MODEL
_C_GA = _C_CX + D_MODEL
_C_GB = _C_GA + D_MODEL
_W_MAIN = _C_GB + D_MODEL

f32 = jnp.float32
bf16 = jnp.bfloat16


def _rms(x, g):
    return x * lax.rsqrt(jnp.mean(x * x, axis=-1, keepdims=True) + EPS) * g


def _dot(a, b):
    return jnp.dot(a, b, preferred_element_type=f32)


def _split_bf16(a):
    hi = a.astype(bf16)
    lo = (a - hi.astype(f32)).astype(bf16)
    return hi, lo


def _mixer_body(x_ref, gmix_ref, wmain_ref, walr_ref, wup_ref, balpha_ref, ggla_ref, wconv_ref,
                wa_ref, wb_ref, wo_ref, gffn_ref, wrh_ref, wrl_ref, br_ref,
                x1_ref, h2_ref, logit_ref,
                state_ref, ubuf_ref, obuf_ref):
    nb, ts = x_ref.shape[0], x_ref.shape[1]
    m = nb * ts
    nchunk = ts // GLA_CHUNK
    seqs = [slice(ch * ts, (ch + 1) * ts) for ch in range(nb)]

    @pl.when(pl.program_id(1) == 0)
    def _():
        state_ref[...] = jnp.zeros_like(state_ref)
        ubuf_ref[:, 0:8, :] = jnp.zeros((nb, 8, D_MODEL), f32)

    x = x_ref[...].reshape(m, D_MODEL)
    hb = _rms(x, gmix_ref[...]).astype(bf16)

    def proj(c0, width):
        return _dot(hb, wmain_ref[:, c0:c0 + width])

    a_lr = _dot(hb, walr_ref[...])
    pre = _dot(a_lr.astype(bf16), wup_ref[...]) + balpha_ref[...]
    log_a = -(jnp.maximum(-pre, 0.0) + jnp.log1p(jnp.exp(-jnp.abs(pre)))) * (1.0 / GLA_TAU)
    ri = lax.broadcasted_iota(jnp.int32, (ts, ts), 0)
    ci = lax.broadcasted_iota(jnp.int32, (ts, ts), 1)
    same_chunk = (ri >> _CHUNK_SHIFT) == (ci >> _CHUNK_SHIFT)
    causal = jnp.logical_and(same_chunk, ci <= ri)
    tri = jnp.where(causal, 1.0, 0.0).astype(bf16)
    la_hi, la_lo = _split_bf16(log_a)
    b = jnp.concatenate([_dot(tri, la_hi[sq]) + _dot(tri, la_lo[sq]) for sq in seqs], axis=0)
    lasts = [b[(c + 1) * GLA_CHUNK - 1:(c + 1) * GLA_CHUNK, :] for c in range(nb * nchunk)]
    b_tot = jnp.concatenate([jnp.broadcast_to(r, (GLA_CHUNK, QK_W)) for r in lasts], axis=0)
    e_pos = jnp.exp(b)
    e_neg = jnp.exp(-b)
    e_tail = jnp.exp(b_tot - b)
    e_tot = [jnp.exp(r) for r in lasts]

    q = proj(_C_Q, QK_W)
    k = proj(_C_K, QK_W)
    q_dec = (q * (GLA_DK ** -0.5) * e_pos).astype(bf16)
    k_inv = (k * e_neg).astype(bf16)
    k_tail = (k * e_tail).astype(bf16)
    vb = proj(_C_V, V_W).astype(bf16)

    for ch, sq in enumerate(seqs):
        for hh in range(GLA_HEADS):
            qs = slice(hh * GLA_DK, (hh + 1) * GLA_DK)
            vs = slice(hh * GLA_DV, (hh + 1) * GLA_DV)
            sc = lax.dot_general(q_dec[sq, qs], k_inv[sq, qs], (((1,), (1,)), ((), ())),
                                 preferred_element_type=f32)
            sc = jnp.where(causal, sc, 0.0).astype(bf16)
            obuf_ref[sq, vs] = _dot(sc, vb[sq, vs])
            st = state_ref[ch, hh]
            for c in range(nchunk):
                rs = slice(ch * ts + c * GLA_CHUNK, ch * ts + (c + 1) * GLA_CHUNK)
                o_inter = lax.dot_general(q_dec[rs, qs], st.astype(bf16), (((1,), (1,)), ((), ())),
                                          preferred_element_type=f32)
                obuf_ref[rs, vs] += o_inter
                upd = lax.dot_general(vb[rs, vs], k_tail[rs, qs], (((0,), (0,)), ((), ())),
                                      preferred_element_type=f32)
                st = st * e_tot[ch * nchunk + c][:, qs] + upd
            state_ref[ch, hh] = st

    r = proj(_C_R, V_W)
    ggla = ggla_ref[...]
    o_parts = []
    for hh in range(GLA_HEADS):
        vs = slice(hh * GLA_DV, (hh + 1) * GLA_DV)
        o_h = _rms(obuf_ref[:, vs], ggla[:, vs])
        r_h = r[:, vs]
        o_parts.append((o_h * (r_h * jax.nn.sigmoid(r_h))).astype(bf16))
    y_a = _dot(jnp.concatenate(o_parts, axis=1), wa_ref[...])

    u = proj(_C_CC, D_MODEL) * proj(_C_CX, D_MODEL)
    wc = wconv_ref[...]
    y_convs = []
    for ch, sq in enumerate(seqs):
        ubuf_ref[ch, 8:8 + ts, :] = u[sq]
        y_convs.append(wc[0:1, :] * ubuf_ref[ch, 6:6 + ts, :] + wc[1:2, :] * ubuf_ref[ch, 7:7 + ts, :]
                       + wc[2:3, :] * u[sq])
        ubuf_ref[ch, 0:8, :] = ubuf_ref[ch, ts:ts + 8, :]
    y_conv = jnp.concatenate(y_convs, axis=0)
    y_b = _dot((proj(_C_CB, D_MODEL) * y_conv).astype(bf16), wb_ref[...])

    mixed = (jax.nn.sigmoid(proj(_C_GA, D_MODEL)) * y_a
             + jax.nn.sigmoid(proj(_C_GB, D_MODEL)) * y_b)
    x1 = x + _dot(mixed.astype(bf16), wo_ref[...])
    x1_ref[...] = x1.reshape(nb, ts, D_MODEL)

    h2 = _rms(x1, gffn_ref[...])
    h2_hi, h2_lo = _split_bf16(h2)
    h2_ref[...] = h2_hi.reshape(nb, ts, D_MODEL)
    wrh = wrh_ref[...]
    logits = _dot(h2_hi, wrh) + _dot(h2_lo, wrh) + _dot(h2_hi, wrl_ref[...]) + br_ref[...]
    logit_ref[...] = logits.reshape(nb, ts, LANES)


def _mixer(x3, gmix, wmain, walr, wup, balpha, ggla, wconv, wa, wb, wo, gffn, wrh, wrl, br):
    batch, seq, _ = x3.shape
    ts = min(MIX_TS, seq)
    ns = seq // ts
    tokens = batch * seq
    nb = MIX_ROWS if batch % MIX_ROWS == 0 else 1

    def const(shape):
        return pl.BlockSpec(shape, lambda b, s: (0,) * len(shape), pipeline_mode=pl.Buffered(1))

    def row(width):
        return pl.BlockSpec((nb, ts, width), lambda b, s: (b, s, 0))

    x1, h2, logits = pl.pallas_call(
        _mixer_body,
        grid=(batch // nb, ns),
        in_specs=[row(D_MODEL), const(gmix.shape), const(wmain.shape), const(walr.shape),
                  const(wup.shape), const(balpha.shape), const(ggla.shape), const(wconv.shape),
                  const(wa.shape), const(wb.shape), const(wo.shape), const(gffn.shape),
                  const(wrh.shape), const(wrl.shape), const(br.shape)],
        out_specs=[row(D_MODEL), row(D_MODEL), row(LANES)],
        out_shape=[jax.ShapeDtypeStruct((batch, seq, D_MODEL), f32),
                   jax.ShapeDtypeStruct((batch, seq, D_MODEL), bf16),
                   jax.ShapeDtypeStruct((batch, seq, LANES), f32)],
        scratch_shapes=[pltpu.VMEM((nb, GLA_HEADS, GLA_DV, GLA_DK), f32),
                        pltpu.VMEM((nb, ts + 8, D_MODEL), f32),
                        pltpu.VMEM((nb * ts, V_W), f32)],
        compiler_params=pltpu.CompilerParams(
            dimension_semantics=("parallel", "arbitrary"), vmem_limit_bytes=VMEM_LIMIT),
        name="mixer",
    )(x3, gmix, wmain, walr, wup, balpha, ggla, wconv, wa, wb, wo, gffn, wrh, wrl, br)
    return (x1.reshape(tokens, D_MODEL), h2.reshape(tokens, D_MODEL),
            logits.reshape(tokens, LANES))


def _route_body(tile, logit_ref, meta_ref, cnt_ref):
    rows = logit_ref.shape[0]
    lane = lax.broadcasted_iota(jnp.int32, (rows, LANES), 1)
    lane_f = lane.astype(f32)
    lg = jnp.where(lane < N_EXPERTS, logit_ref[...], -jnp.inf)
    sels, tops, firsts = [], [], []
    for _ in range(TOP_K):
        m = jnp.max(lg, axis=-1, keepdims=True)
        first = jnp.min(jnp.where(lg == m, lane_f, float(LANES)), axis=-1, keepdims=True)
        sel = lane_f == first
        sels.append(sel)
        tops.append(m)
        firsts.append(first)
        lg = jnp.where(sel, -jnp.inf, lg)
    ps = [jnp.exp(t - tops[0]) for t in tops]
    denom = ps[0] + ps[1] + ps[2] + ps[3]
    onehot = jnp.zeros((rows, LANES), f32)
    for sel in sels:
        onehot = onehot + jnp.where(sel, 1.0, 0.0)
    onehot_b = onehot.astype(bf16)
    ri = lax.broadcasted_iota(jnp.int32, (tile, tile), 0)
    ci = lax.broadcasted_iota(jnp.int32, (tile, tile), 1)
    strict_lower = jnp.where(ci < ri, 1.0, 0.0).astype(bf16)
    ones = jnp.ones((8, tile), bf16)
    ranks = []
    for j in range(rows // tile):
        part = onehot_b[j * tile:(j + 1) * tile, :]
        ranks.append(_dot(strict_lower, part))
        cnt_ref[j * 8:(j + 1) * 8, :] = _dot(ones, part).astype(jnp.int32)
    rank = jnp.concatenate(ranks, axis=0)
    meta = jnp.zeros((rows, LANES), f32)
    for kk in range(TOP_K):
        meta = jnp.where(lane == _META_EXPERT + kk, firsts[kk], meta)
        meta = jnp.where(lane == _META_GATE + kk, ps[kk] / denom, meta)
        rank_k = jnp.sum(jnp.where(sels[kk], rank, 0.0), axis=-1, keepdims=True)
        meta = jnp.where(lane == _META_RANK + kk, rank_k, meta)
    meta_ref[...] = meta


def _route(logits, tile):
    tokens = logits.shape[0]
    rows = min(ROUTE_ROWS, tokens)
    steps = tokens // rows
    per_step = rows // tile
    return pl.pallas_call(
        functools.partial(_route_body, tile),
        grid=(steps,),
        in_specs=[pl.BlockSpec((rows, LANES), lambda i: (i, 0))],
        out_specs=[pl.BlockSpec((rows, LANES), lambda i: (i, 0)),
                   pl.BlockSpec((per_step * 8, LANES), lambda i: (i, 0))],
        out_shape=[jax.ShapeDtypeStruct((tokens, LANES), f32),
                   jax.ShapeDtypeStruct((tokens // tile * 8, LANES), jnp.int32)],
        compiler_params=pltpu.CompilerParams(
            dimension_semantics=("parallel",), vmem_limit_bytes=VMEM_LIMIT),
        name="route",
    )(logits)


def _lmax(tm):
    full = tm * TOP_K + N_EXPERTS * 2 * (BF16_ROWS - 1)
    return -(-full // 256) * 256


def _token_slots(meta, base_row):
    lane_f = lax.broadcasted_iota(jnp.int32, meta.shape, 1).astype(f32)
    slots = []
    for kk in range(TOP_K):
        sel = lane_f == meta[:, _META_EXPERT + kk:_META_EXPERT + kk + 1]
        first = jnp.sum(jnp.where(sel, base_row, 0.0), axis=-1, keepdims=True)
        slots.append(first + meta[:, _META_RANK + kk:_META_RANK + kk + 1])
    return slots


def _segment_copy(src, dst, sem, src_off, dst_off, rows):
    src_off = pl.multiple_of(src_off, BF16_ROWS)
    dst_off = pl.multiple_of(dst_off, BF16_ROWS)
    rows = pl.multiple_of(rows, BF16_ROWS)
    return pltpu.make_async_copy(src.at[pl.ds(src_off, rows)], dst.at[pl.ds(dst_off, rows)], sem)


def _wait_rows(src, dst, sem, total):
    @pl.when(total > 0)
    def _():
        _segment_copy(src, dst, sem, 0, 0, total).wait()


def _zero_rows(zero_ref, dst_ref, sem, first, start):
    tile = zero_ref.shape[0]
    n = dst_ref.shape[0] - first
    n_full = n // tile

    def full(j):
        return _segment_copy(zero_ref, dst_ref, sem, 0, first + j * tile, tile)

    def rest():
        return _segment_copy(zero_ref, dst_ref, sem, 0, first + n_full * tile, n - n_full * tile)

    def step(j, carry):
        full(j).start() if start else full(j).wait()
        return carry

    lax.fori_loop(0, n_full, step, 0)

    @pl.when(n - n_full * tile > 0)
    def _():
        rest().start() if start else rest().wait()


def _dispatch_body(voff_ref, coff_ref, keep_ref, row0_ref, wlen_ref, wtot_ref, used_ref,
                   h2_ref, meta_ref, base_ref, meta_next_ref, base_next_ref, xs_ref,
                   xc0_ref, xc1_ref, pt0_ref, pt1_ref, carry_ref, zero_ref, sem, zsem):
    i = pl.program_id(0)
    nt = pl.num_programs(0)
    tm = h2_ref.shape[0]
    lmax = xc0_ref.shape[0]
    slot = i & 1

    def slot_rows(m_ref, b_ref):
        meta = m_ref[...]
        lane = lax.broadcasted_iota(jnp.int32, meta.shape, 1)
        slots = jnp.zeros(meta.shape, f32)
        for kk, s in enumerate(_token_slots(meta, b_ref[0:1, :])):
            slots = jnp.where(lane == kk, s, slots)
        return slots.T[0:8, :].astype(jnp.int32)

    def one_hot_rows(slots_t, c):
        row = lax.broadcasted_iota(jnp.int32, (_CHUNK, tm), 0) + c * _CHUNK
        pt = jnp.zeros((_CHUNK, tm), f32)
        for kk in range(TOP_K):
            pt = jnp.where(row == slots_t[kk:kk + 1, :], 1.0, pt)
        return pt.astype(bf16)

    @pl.when(i == 0)
    def _():
        carry_ref[...] = jnp.zeros_like(carry_ref)
        zero_ref[...] = jnp.zeros_like(zero_ref)
        _zero_rows(zero_ref, xs_ref, zsem, used_ref[0], start=True)
        first = slot_rows(meta_ref, base_ref)
        for c in range(lmax // _CHUNK):
            pt0_ref[c * _CHUNK:(c + 1) * _CHUNK, :] = one_hot_rows(first, c)

    @pl.when(i >= 2)
    def _():
        _wait_rows(xc0_ref, xs_ref, sem.at[slot], wtot_ref[jnp.maximum(i - 2, 0)])

    def merge_carry(t, buf):
        def block(off):
            return buf.at[pl.ds(pl.multiple_of(off, BF16_ROWS), BF16_ROWS)]

        for e in range(N_EXPERTS):
            blk = block(voff_ref[t * N_EXPERTS + e])
            blk[...] = blk[...] + carry_ref[e]
        for e in range(N_EXPERTS):
            j = t * N_EXPERTS + e
            tail = block(coff_ref[j])[...]
            carry_ref[e] = jnp.where(keep_ref[j] > 0, tail, jnp.zeros_like(tail))

    def copy_out(t, buf):
        for e in range(N_EXPERTS):
            j = t * N_EXPERTS + e
            _segment_copy(buf, xs_ref, sem.at[slot], voff_ref[j], row0_ref[j], wlen_ref[j]).start()

    def compact(pt_now_ref, pt_next_ref, xc_now_ref):
        h2 = h2_ref[...]
        nxt = slot_rows(meta_next_ref, base_next_ref)
        for c in range(lmax // _CHUNK):
            rows = slice(c * _CHUNK, (c + 1) * _CHUNK)
            xc_now_ref[rows, :] = _dot(pt_now_ref[rows, :], h2).astype(bf16)
            pt_next_ref[rows, :] = one_hot_rows(nxt, c)
        merge_carry(i, xc_now_ref)
        copy_out(i, xc_now_ref)

    @pl.when(slot == 0)
    def _():
        compact(pt0_ref, pt1_ref, xc0_ref)

    @pl.when(slot == 1)
    def _():
        compact(pt1_ref, pt0_ref, xc1_ref)

    @pl.when(i == nt - 1)
    def _():
        @pl.when(i >= 1)
        def _():
            _wait_rows(xc0_ref, xs_ref, sem.at[1 - slot], wtot_ref[jnp.maximum(i - 1, 0)])
        _wait_rows(xc0_ref, xs_ref, sem.at[slot], wtot_ref[i])
        _zero_rows(zero_ref, xs_ref, zsem, used_ref[0], start=False)


def _dispatch(plan, h2, meta, rows_total):
    tokens = h2.shape[0]
    tm = min(MIX_TS, tokens)
    nt = tokens // tm
    scalars = (plan["voff"], plan["coff"], plan["keep"], plan["row0"], plan["wlen"], plan["wtot"],
               plan["used"])
    return pl.pallas_call(
        _dispatch_body,
        grid_spec=pltpu.PrefetchScalarGridSpec(
            num_scalar_prefetch=len(scalars),
            grid=(nt,),
            in_specs=[pl.BlockSpec((tm, D_MODEL), lambda i, *_: (i, 0)),
                      pl.BlockSpec((tm, LANES), lambda i, *_: (i, 0)),
                      pl.BlockSpec((8, LANES), lambda i, *_: (i, 0)),
                      pl.BlockSpec((tm, LANES), lambda i, *_: (jnp.minimum(i + 1, nt - 1), 0)),
                      pl.BlockSpec((8, LANES), lambda i, *_: (jnp.minimum(i + 1, nt - 1), 0))],
            out_specs=pl.BlockSpec(memory_space=pl.ANY),
            scratch_shapes=[pltpu.VMEM((_lmax(tm), D_MODEL), bf16),
                            pltpu.VMEM((_lmax(tm), D_MODEL), bf16),
                            pltpu.VMEM((_lmax(tm), tm), bf16),
                            pltpu.VMEM((_lmax(tm), tm), bf16),
                            pltpu.VMEM((N_EXPERTS, BF16_ROWS, D_MODEL), bf16),
                            pltpu.VMEM((FFN_TM, D_MODEL), bf16),
                            pltpu.SemaphoreType.DMA((2,)), pltpu.SemaphoreType.DMA],
        ),
        out_shape=jax.ShapeDtypeStruct((rows_total, D_MODEL), bf16),
        compiler_params=pltpu.CompilerParams(
            dimension_semantics=("arbitrary",), vmem_limit_bytes=VMEM_LIMIT),
        name="dispatch",
    )(*scalars, h2, meta, plan["base"], meta, plan["base"])


def _ffn_body(gstart_ref, gsize_ref, used_ref, xs_ref, wgu_ref, bgu_ref, wd_ref, bd_ref, os_ref,
              xbuf_ref, obuf_ref, zero_ref, state_ref, isem, osem, zsem):
    e = pl.program_id(0)
    ne = pl.num_programs(0)
    tile = xbuf_ref.shape[1]
    base = gstart_ref[e]
    n = gsize_ref[e]
    n_tiles = (n + tile - 1) // tile

    def rows_of(size, t):
        return jnp.minimum(tile, size - t * tile)

    def in_copy(row0, rows, slot):
        return pltpu.make_async_copy(xs_ref.at[pl.ds(row0, rows)],
                                     xbuf_ref.at[slot, pl.ds(0, rows)], isem.at[slot])

    def out_copy(row0, rows, slot):
        return pltpu.make_async_copy(obuf_ref.at[slot, pl.ds(0, rows)],
                                     os_ref.at[pl.ds(row0, rows)], osem.at[slot])

    def start_tile(copy, row0, rows, slot):
        row0 = pl.multiple_of(row0, BF16_ROWS)

        @pl.when(rows == tile)
        def _():
            copy(row0, tile, slot).start(priority=_TILE_DMA_PRIORITY)

        @pl.when(rows < tile)
        def _():
            copy(row0, pl.multiple_of(rows, BF16_ROWS), slot).start()

    def wait_tile(copy, rows, slot):
        copy(0, pl.multiple_of(rows, BF16_ROWS), slot).wait()

    @pl.when(e == 0)
    def _():
        xbuf_ref[...] = jnp.zeros_like(xbuf_ref)
        zero_ref[...] = jnp.zeros_like(zero_ref)
        _zero_rows(zero_ref, os_ref, zsem, used_ref[0], start=True)
        state_ref[0] = 0
        state_ref[1] = 0
        state_ref[2] = 0
        state_ref[3] = 0

    done = state_ref[0]

    @pl.when(jnp.logical_and(n_tiles > 0, state_ref[1] == 0))
    def _():
        start_tile(in_copy, base, rows_of(n, 0), done & 1)

    nxt = jnp.minimum(e + 1, ne - 1)
    has_next = jnp.logical_and(e + 1 < ne, gsize_ref[nxt] > 0)

    def step(t, carry):
        slot = (done + t) & 1

        @pl.when(t + 1 < n_tiles)
        def _():
            start_tile(in_copy, base + (t + 1) * tile, rows_of(n, t + 1), 1 - slot)

        @pl.when(jnp.logical_and(t + 1 == n_tiles, has_next))
        def _():
            start_tile(in_copy, gstart_ref[nxt], rows_of(gsize_ref[nxt], 0), 1 - slot)

        rows = rows_of(n, t)
        wait_tile(in_copy, rows, slot)

        @pl.when(state_ref[2 + slot] > 0)
        def _():
            wait_tile(out_copy, state_ref[2 + slot], slot)

        def ffn(r0, m):
            gu = _dot(xbuf_ref[slot, r0:r0 + m, :], wgu_ref[...].astype(bf16)) + bgu_ref[...]
            gate = jnp.minimum(gu[:, :D_FF], SWIGLU_LIMIT)
            lin = jnp.clip(gu[:, D_FF:], -SWIGLU_LIMIT, SWIGLU_LIMIT)
            act = (lin + 1.0) * (gate * jax.nn.sigmoid(SWIGLU_ALPHA * gate))
            out = _dot(act.astype(bf16), wd_ref[...].astype(bf16)) + bd_ref[...]
            obuf_ref[slot, r0:r0 + m, :] = out.astype(bf16)

        half, quarter = tile // 2, tile // 4
        for lo, hi, chains in ((half + quarter, tile, ((0, half), (half, half))),
                               (half, half + quarter, ((0, half), (half, quarter))),
                               (quarter, half, ((0, half),)),
                               (0, quarter, ((0, quarter),))):
            @pl.when(jnp.logical_and(rows > lo, rows <= hi))
            def _():
                for r0, m in chains:
                    ffn(r0, m)

        start_tile(out_copy, base + t * tile, rows, slot)
        state_ref[2 + slot] = rows
        return carry

    lax.fori_loop(0, n_tiles, step, 0)

    @pl.when(n_tiles > 0)
    def _():
        state_ref[0] = done + n_tiles
        state_ref[1] = has_next.astype(jnp.int32)

    @pl.when(e == ne - 1)
    def _():
        for slot in range(2):
            @pl.when(state_ref[2 + slot] > 0)
            def _():
                wait_tile(out_copy, state_ref[2 + slot], slot)
        _zero_rows(zero_ref, os_ref, zsem, used_ref[0], start=False)


def _ffn(gstart, gsize, used, xs, wgu, bgu, wd, bd):
    rows_total = xs.shape[0]
    return pl.pallas_call(
        _ffn_body,
        grid_spec=pltpu.PrefetchScalarGridSpec(
            num_scalar_prefetch=3,
            grid=(N_EXPERTS,),
            in_specs=[pl.BlockSpec(memory_space=pl.ANY),
                      pl.BlockSpec((None, D_MODEL, 2 * D_FF), lambda e, *_: (e, 0, 0)),
                      pl.BlockSpec((None, 1, 2 * D_FF), lambda e, *_: (e, 0, 0)),
                      pl.BlockSpec((None, D_FF, D_MODEL), lambda e, *_: (e, 0, 0)),
                      pl.BlockSpec((None, 1, D_MODEL), lambda e, *_: (e, 0, 0))],
            out_specs=pl.BlockSpec(memory_space=pl.ANY),
            scratch_shapes=[pltpu.VMEM((2, FFN_TM, D_MODEL), bf16),
                            pltpu.VMEM((2, FFN_TM, D_MODEL), bf16),
                            pltpu.VMEM((FFN_TM, D_MODEL), bf16),
                            pltpu.SMEM((4,), jnp.int32),
                            pltpu.SemaphoreType.DMA((2,)), pltpu.SemaphoreType.DMA((2,)),
                            pltpu.SemaphoreType.DMA],
        ),
        out_shape=jax.ShapeDtypeStruct((rows_total, D_MODEL), bf16),
        compiler_params=pltpu.CompilerParams(
            dimension_semantics=("arbitrary",), vmem_limit_bytes=VMEM_LIMIT),
        name="experts",
    )(gstart, gsize, used, xs, wgu, bgu, wd, bd)


def _combine_body(final_norm, voff_ref, row0_ref, rlen_ref, rtot_ref,
                  os_ref, meta_ref, base_ref, meta_next_ref, base_next_ref, x1_ref, gfin_ref,
                  out_ref, oc0_ref, oc1_ref, p0_ref, p1_ref, sem):
    i = pl.program_id(0)
    nt = pl.num_programs(0)
    tm = x1_ref.shape[0]
    lmax = oc0_ref.shape[0]
    slot = i & 1

    def slots_and_gates(m_ref, b_ref):
        meta = m_ref[...]
        slots = [s.astype(jnp.int32) for s in _token_slots(meta, b_ref[0:1, :])]
        gates = [meta[:, _META_GATE + kk:_META_GATE + kk + 1] for kk in range(TOP_K)]
        return slots, gates

    def gate_cols(slots, gates, c):
        col = lax.broadcasted_iota(jnp.int32, (tm, _CHUNK), 1) + c * _CHUNK
        p = jnp.zeros((tm, _CHUNK), f32)
        for s, g in zip(slots, gates):
            p = jnp.where(col == s, g, p)
        return p.astype(bf16)

    def fetch(tile, into_ref, sem_slot, rows_of):
        for e in range(N_EXPERTS):
            j = tile * N_EXPERTS + e
            _segment_copy(os_ref, into_ref, sem.at[sem_slot], row0_ref[j], voff_ref[j],
                          rows_of(rlen_ref[j])).start()

    @pl.when(i == 0)
    def _():
        oc0_ref[...] = jnp.zeros_like(oc0_ref)
        oc1_ref[...] = jnp.zeros_like(oc1_ref)
        fetch(0, oc0_ref, 0, lambda rows: rows)
        slots, gates = slots_and_gates(meta_ref, base_ref)
        for c in range(lmax // _CHUNK):
            p0_ref[:, c * _CHUNK:(c + 1) * _CHUNK] = gate_cols(slots, gates, c)

    _wait_rows(os_ref, oc0_ref, sem.at[slot], rtot_ref[i])

    def request_next(oc_next_ref):
        fetch(jnp.minimum(i + 1, nt - 1), oc_next_ref, 1 - slot,
              lambda rows: jnp.where(i + 1 < nt, rows, 0))

    @pl.when(slot == 0)
    def _():
        request_next(oc1_ref)

    @pl.when(slot == 1)
    def _():
        request_next(oc0_ref)

    def restore(p_now_ref, p_next_ref, oc_now_ref):
        slots, gates = slots_and_gates(meta_next_ref, base_next_ref)
        p_now = p_now_ref[...]
        n_build = lmax // _CHUNK
        n_out = D_MODEL // _CHUNK
        ssq = jnp.zeros((tm, 1), f32)
        for n in range(n_out):
            cols = slice(n * _CHUNK, (n + 1) * _CHUNK)
            xo = x1_ref[:, cols] + _dot(p_now, oc_now_ref[:, cols])
            out_ref[:, cols] = xo
            ssq = ssq + jnp.sum(xo * xo, axis=-1, keepdims=True)
            for c in range(n * n_build // n_out, (n + 1) * n_build // n_out):
                p_next_ref[:, c * _CHUNK:(c + 1) * _CHUNK] = gate_cols(slots, gates, c)
        if final_norm:
            out_ref[...] = out_ref[...] * lax.rsqrt(ssq * (1.0 / D_MODEL) + EPS) * gfin_ref[...]

    @pl.when(slot == 0)
    def _():
        restore(p0_ref, p1_ref, oc0_ref)

    @pl.when(slot == 1)
    def _():
        restore(p1_ref, p0_ref, oc1_ref)


def _combine(plan, os, meta, x1, gfin, final_norm):
    tokens = x1.shape[0]
    tm = min(MIX_TS, tokens)
    nt = tokens // tm
    scalars = (plan["voff"], plan["row0"], plan["rlen"], plan["rtot"])
    return pl.pallas_call(
        functools.partial(_combine_body, final_norm),
        grid_spec=pltpu.PrefetchScalarGridSpec(
            num_scalar_prefetch=len(scalars),
            grid=(nt,),
            in_specs=[pl.BlockSpec(memory_space=pl.ANY),
                      pl.BlockSpec((tm, LANES), lambda i, *_: (i, 0)),
                      pl.BlockSpec((8, LANES), lambda i, *_: (i, 0)),
                      pl.BlockSpec((tm, LANES), lambda i, *_: (jnp.minimum(i + 1, nt - 1), 0)),
                      pl.BlockSpec((8, LANES), lambda i, *_: (jnp.minimum(i + 1, nt - 1), 0)),
                      pl.BlockSpec((tm, D_MODEL), lambda i, *_: (i, 0)),
                      pl.BlockSpec((1, D_MODEL), lambda i, *_: (0, 0))],
            out_specs=pl.BlockSpec((tm, D_MODEL), lambda i, *_: (i, 0)),
            scratch_shapes=[pltpu.VMEM((_lmax(tm), D_MODEL), bf16),
                            pltpu.VMEM((_lmax(tm), D_MODEL), bf16),
                            pltpu.VMEM((tm, _lmax(tm)), bf16),
                            pltpu.VMEM((tm, _lmax(tm)), bf16),
                            pltpu.SemaphoreType.DMA((2,))],
        ),
        out_shape=jax.ShapeDtypeStruct((tokens, D_MODEL), f32),
        compiler_params=pltpu.CompilerParams(
            dimension_semantics=("arbitrary",), vmem_limit_bytes=VMEM_LIMIT),
        name="combine",
    )(*scalars, os, meta, plan["base"], meta, plan["base"], x1, gfin)


def _plan(counts, nt):
    up = lambda v: (v + (BF16_ROWS - 1)) // BF16_ROWS * BF16_ROWS
    down = lambda v: v // BF16_ROWS * BF16_ROWS
    c = counts.reshape(nt, 8, LANES)[:, 0, :N_EXPERTS]
    gsize = up(jnp.sum(c, axis=0))
    gstart = jnp.cumsum(gsize) - gsize
    before = jnp.cumsum(c, axis=0) - c
    a = before % BF16_ROWS
    length = a + c
    voff = jnp.cumsum(up(length), axis=1) - up(length)
    last = jnp.arange(nt)[:, None] == nt - 1
    wlen = jnp.where(last, up(length), down(length))
    rlen = jnp.where(c > 0, up(length), 0)
    base = jnp.zeros((nt, 8, LANES), jnp.float32)
    base = base.at[:, :, :N_EXPERTS].set((voff + a).astype(jnp.float32)[:, None, :])
    i32 = lambda v: v.reshape(-1).astype(jnp.int32)
    return dict(
        voff=i32(voff), coff=i32(voff + down(length)), keep=i32(length % BF16_ROWS),
        row0=i32(gstart[None, :] + before - a), wlen=i32(wlen), wtot=i32(jnp.sum(wlen, axis=1)),
        rlen=i32(rlen), rtot=i32(jnp.sum(rlen, axis=1)), base=base.reshape(nt * 8, LANES),
        gstart=i32(gstart), gsize=i32(gsize), used=i32(jnp.sum(gsize, keepdims=True)))


def kernel(x, g_mix, w_in, w_alpha_up, b_alpha, g_gla, w_conv, w_branch_a, w_branch_b, w_out, g_ffn, w_router, b_router, w_gate_up, b_gate_up, w_down, b_down, g_final):
    batch, seq, _ = x.shape
    tokens = batch * seq
    depth = w_in.shape[0]
    x2 = x.reshape(tokens, D_MODEL)
    nt = tokens // min(MIX_TS, seq)
    rows_total = tokens * TOP_K + N_EXPERTS * (BF16_ROWS - 1)
    rows_total = -(-rows_total // BF16_ROWS) * BF16_ROWS

    for l in range(depth):
        wi = w_in[l]
        a0 = 2 * QK_W + 2 * V_W
        wmain = jnp.concatenate([wi[:, :a0], wi[:, a0 + GLA_RANK:]], axis=1).astype(bf16)
        walr = jnp.pad(wi[:, a0:a0 + GLA_RANK], ((0, 0), (0, LANES - GLA_RANK))).astype(bf16)
        wup = jnp.pad(w_alpha_up[l], ((0, LANES - GLA_RANK), (0, 0))).astype(bf16)
        wr = jnp.pad(w_router[l], ((0, 0), (0, LANES - N_EXPERTS)))
        wrh = wr.astype(bf16)
        wrl = (wr - wrh.astype(f32)).astype(bf16)
        br = jnp.pad(b_router[l], (0, LANES - N_EXPERTS)).reshape(1, LANES)
        wconv = jnp.pad(w_conv[l], ((0, 8 - CONV_K), (0, 0)))

        x1, h2, logits = _mixer(
            x2.reshape(batch, seq, D_MODEL), g_mix[l].reshape(1, D_MODEL), wmain, walr, wup,
            b_alpha[l].reshape(1, QK_W), g_gla[l].reshape(1, V_W), wconv,
            w_branch_a[l].astype(bf16), w_branch_b[l].astype(bf16), w_out[l].astype(bf16),
            g_ffn[l].reshape(1, D_MODEL), wrh, wrl, br)

        meta, counts = _route(logits, min(MIX_TS, seq))
        plan = _plan(counts, nt)
        xs = _dispatch(plan, h2, meta, rows_total)
        os = _ffn(plan["gstart"], plan["gsize"], plan["used"], xs,
                  w_gate_up[l], b_gate_up[l].reshape(N_EXPERTS, 1, 2 * D_FF),
                  w_down[l], b_down[l].reshape(N_EXPERTS, 1, D_MODEL))
        x2 = _combine(plan, os, meta, x1, g_final.reshape(1, D_MODEL), l == depth - 1)
    return x2.reshape(batch, seq, D_MODEL)
```

```python
import functools

import jax
import jax.numpy as jnp
from jax import lax
from jax.experimental import pallas as pl
from jax.experimental.pallas import tpu as pltpu

D_MODEL = 1024
GLA_HEADS = 4
GLA_DK = 128
GLA_DV = 256
GLA_RANK = 16
GLA_TAU = 16.0
GLA_CHUNK = 64
_CHUNK_SHIFT = GLA_CHUNK.bit_length() - 1
CONV_K = 3
N_EXPERTS = 32
TOP_K = 4
D_FF = 1024
SWIGLU_LIMIT = 7.0
SWIGLU_ALPHA = 1.702
EPS = 1e-6

QK_W = GLA_HEADS * GLA_DK
V_W = GLA_HEADS * GLA_DV

LANES = 128
BF16_ROWS = 16
VMEM_LIMIT = 56 * 1024 * 1024

MIX_TS = 256
MIX_ROWS = 2
ROUTE_ROWS = 1024
_CHUNK = 256
FFN_TM = 1024
_META_EXPERT, _META_GATE, _META_RANK = 0, TOP_K, 2 * TOP_K
_TILE_DMA_PRIORITY = 1

_C_Q, _C_K, _C_V, _C_R = 0, QK_W, 2 * QK_W, 2 * QK_W + V_W
_C_CB = _C_R + V_W
_C_CC = _C_CB + D_MODEL
_C_CX = _C_CC + D_MODEL
_C_GA = _C_CX + D_MODEL
_C_GB = _C_GA + D_MODEL
_W_MAIN = _C_GB + D_MODEL

f32 = jnp.float32
bf16 = jnp.bfloat16


def _rms(x, g):
    return x * lax.rsqrt(jnp.mean(x * x, axis=-1, keepdims=True) + EPS) * g


def _dot(a, b):
    return jnp.dot(a, b, preferred_element_type=f32)


def _split_bf16(a):
    hi = a.astype(bf16)
    lo = (a - hi.astype(f32)).astype(bf16)
    return hi, lo


def _mixer_body(x_ref, gmix_ref, wmain_ref, walr_ref, wup_ref, balpha_ref, ggla_ref, wconv_ref,
                wa_ref, wb_ref, wo_ref, gffn_ref, wrh_ref, wrl_ref, br_ref,
                x1_ref, h2_ref, logit_ref,
                state_ref, ubuf_ref, obuf_ref):
    nb, ts = x_ref.shape[0], x_ref.shape[1]
    m = nb * ts
    nchunk = ts // GLA_CHUNK
    seqs = [slice(ch * ts, (ch + 1) * ts) for ch in range(nb)]

    @pl.when(pl.program_id(1) == 0)
    def _():
        state_ref[...] = jnp.zeros_like(state_ref)
        ubuf_ref[:, 0:8, :] = jnp.zeros((nb, 8, D_MODEL), f32)

    x = x_ref[...].reshape(m, D_MODEL)
    hb = _rms(x, gmix_ref[...]).astype(bf16)

    def proj(c0, width):
        return _dot(hb, wmain_ref[:, c0:c0 + width])

    a_lr = _dot(hb, walr_ref[...])
    pre = _dot(a_lr.astype(bf16), wup_ref[...]) + balpha_ref[...]
    log_a = -(jnp.maximum(-pre, 0.0) + jnp.log1p(jnp.exp(-jnp.abs(pre)))) * (1.0 / GLA_TAU)
    ri = lax.broadcasted_iota(jnp.int32, (ts, ts), 0)
    ci = lax.broadcasted_iota(jnp.int32, (ts, ts), 1)
    same_chunk = (ri >> _CHUNK_SHIFT) == (ci >> _CHUNK_SHIFT)
    causal = jnp.logical_and(same_chunk, ci <= ri)
    tri = jnp.where(causal, 1.0, 0.0).astype(bf16)
    la_hi, la_lo = _split_bf16(log_a)
    b = jnp.concatenate([_dot(tri, la_hi[sq]) + _dot(tri, la_lo[sq]) for sq in seqs], axis=0)
    lasts = [b[(c + 1) * GLA_CHUNK - 1:(c + 1) * GLA_CHUNK, :] for c in range(nb * nchunk)]
    b_tot = jnp.concatenate([jnp.broadcast_to(r, (GLA_CHUNK, QK_W)) for r in lasts], axis=0)
    e_pos = jnp.exp(b)
    e_neg = jnp.exp(-b)
    e_tail = jnp.exp(b_tot - b)
    e_tot = [jnp.exp(r) for r in lasts]

    q = proj(_C_Q, QK_W)
    k = proj(_C_K, QK_W)
    q_dec = (q * (GLA_DK ** -0.5) * e_pos).astype(bf16)
    k_inv = (k * e_neg).astype(bf16)
    k_tail = (k * e_tail).astype(bf16)
    vb = proj(_C_V, V_W).astype(bf16)

    for ch, sq in enumerate(seqs):
        for hh in range(GLA_HEADS):
            qs = slice(hh * GLA_DK, (hh + 1) * GLA_DK)
            vs = slice(hh * GLA_DV, (hh + 1) * GLA_DV)
            sc = lax.dot_general(q_dec[sq, qs], k_inv[sq, qs], (((1,), (1,)), ((), ())),
                                 preferred_element_type=f32)
            sc = jnp.where(causal, sc, 0.0).astype(bf16)
            obuf_ref[sq, vs] = _dot(sc, vb[sq, vs])
            st = state_ref[ch, hh]
            for c in range(nchunk):
                rs = slice(ch * ts + c * GLA_CHUNK, ch * ts + (c + 1) * GLA_CHUNK)
                o_inter = lax.dot_general(q_dec[rs, qs], st.astype(bf16), (((1,), (1,)), ((), ())),
                                          preferred_element_type=f32)
                obuf_ref[rs, vs] += o_inter
                upd = lax.dot_general(vb[rs, vs], k_tail[rs, qs], (((0,), (0,)), ((), ())),
                                      preferred_element_type=f32)
                st = st * e_tot[ch * nchunk + c][:, qs] + upd
            state_ref[ch, hh] = st

    r = proj(_C_R, V_W)
    ggla = ggla_ref[...]
    o_parts = []
    for hh in range(GLA_HEADS):
        vs = slice(hh * GLA_DV, (hh + 1) * GLA_DV)
        o_h = _rms(obuf_ref[:, vs], ggla[:, vs])
        r_h = r[:, vs]
        o_parts.append((o_h * (r_h * jax.nn.sigmoid(r_h))).astype(bf16))
    y_a = _dot(jnp.concatenate(o_parts, axis=1), wa_ref[...])

    u = proj(_C_CC, D_MODEL) * proj(_C_CX, D_MODEL)
    wc = wconv_ref[...]
    y_convs = []
    for ch, sq in enumerate(seqs):
        ubuf_ref[ch, 8:8 + ts, :] = u[sq]
        y_convs.append(wc[0:1, :] * ubuf_ref[ch, 6:6 + ts, :] + wc[1:2, :] * ubuf_ref[ch, 7:7 + ts, :]
                       + wc[2:3, :] * u[sq])
        ubuf_ref[ch, 0:8, :] = ubuf_ref[ch, ts:ts + 8, :]
    y_conv = jnp.concatenate(y_convs, axis=0)
    y_b = _dot((proj(_C_CB, D_MODEL) * y_conv).astype(bf16), wb_ref[...])

    mixed = (jax.nn.sigmoid(proj(_C_GA, D_MODEL)) * y_a
             + jax.nn.sigmoid(proj(_C_GB, D_MODEL)) * y_b)
    x1 = x + _dot(mixed.astype(bf16), wo_ref[...])
    x1_ref[...] = x1.reshape(nb, ts, D_MODEL)

    h2 = _rms(x1, gffn_ref[...])
    h2_hi, h2_lo = _split_bf16(h2)
    h2_ref[...] = h2_hi.reshape(nb, ts, D_MODEL)
    wrh = wrh_ref[...]
    logits = _dot(h2_hi, wrh) + _dot(h2_lo, wrh) + _dot(h2_hi, wrl_ref[...]) + br_ref[...]
    logit_ref[...] = logits.reshape(nb, ts, LANES)


def _mixer(x3, gmix, wmain, walr, wup, balpha, ggla, wconv, wa, wb, wo, gffn, wrh, wrl, br):
    batch, seq, _ = x3.shape
    ts = min(MIX_TS, seq)
    ns = seq // ts
    tokens = batch * seq
    nb = MIX_ROWS if batch % MIX_ROWS == 0 else 1

    def const(shape):
        return pl.BlockSpec(shape, lambda b, s: (0,) * len(shape), pipeline_mode=pl.Buffered(1))

    def row(width):
        return pl.BlockSpec((nb, ts, width), lambda b, s: (b, s, 0))

    x1, h2, logits = pl.pallas_call(
        _mixer_body,
        grid=(batch // nb, ns),
        in_specs=[row(D_MODEL), const(gmix.shape), const(wmain.shape), const(walr.shape),
                  const(wup.shape), const(balpha.shape), const(ggla.shape), const(wconv.shape),
                  const(wa.shape), const(wb.shape), const(wo.shape), const(gffn.shape),
                  const(wrh.shape), const(wrl.shape), const(br.shape)],
        out_specs=[row(D_MODEL), row(D_MODEL), row(LANES)],
        out_shape=[jax.ShapeDtypeStruct((batch, seq, D_MODEL), f32),
                   jax.ShapeDtypeStruct((batch, seq, D_MODEL), bf16),
                   jax.ShapeDtypeStruct((batch, seq, LANES), f32)],
        scratch_shapes=[pltpu.VMEM((nb, GLA_HEADS, GLA_DV, GLA_DK), f32),
                        pltpu.VMEM((nb, ts + 8, D_MODEL), f32),
                        pltpu.VMEM((nb * ts, V_W), f32)],
        compiler_params=pltpu.CompilerParams(
            dimension_semantics=("parallel", "arbitrary"), vmem_limit_bytes=VMEM_LIMIT),
        name="mixer",
    )(x3, gmix, wmain, walr, wup, balpha, ggla, wconv, wa, wb, wo, gffn, wrh, wrl, br)
    return (x1.reshape(tokens, D_MODEL), h2.reshape(tokens, D_MODEL),
            logits.reshape(tokens, LANES))


def _route_body(tile, logit_ref, meta_ref, cnt_ref):
    rows = logit_ref.shape[0]
    lane = lax.broadcasted_iota(jnp.int32, (rows, LANES), 1)
    lane_f = lane.astype(f32)
    lg = jnp.where(lane < N_EXPERTS, logit_ref[...], -jnp.inf)
    sels, tops, firsts = [], [], []
    for _ in range(TOP_K):
        m = jnp.max(lg, axis=-1, keepdims=True)
        first = jnp.min(jnp.where(lg == m, lane_f, float(LANES)), axis=-1, keepdims=True)
        sel = lane_f == first
        sels.append(sel)
        tops.append(m)
        firsts.append(first)
        lg = jnp.where(sel, -jnp.inf, lg)
    ps = [jnp.exp(t - tops[0]) for t in tops]
    denom = ps[0] + ps[1] + ps[2] + ps[3]
    onehot = jnp.zeros((rows, LANES), f32)
    for sel in sels:
        onehot = onehot + jnp.where(sel, 1.0, 0.0)
    onehot_b = onehot.astype(bf16)
    ri = lax.broadcasted_iota(jnp.int32, (tile, tile), 0)
    ci = lax.broadcasted_iota(jnp.int32, (tile, tile), 1)
    strict_lower = jnp.where(ci < ri, 1.0, 0.0).astype(bf16)
    ones = jnp.ones((8, tile), bf16)
    ranks = []
    for j in range(rows // tile):
        part = onehot_b[j * tile:(j + 1) * tile, :]
        ranks.append(_dot(strict_lower, part))
        cnt_ref[j * 8:(j + 1) * 8, :] = _dot(ones, part).astype(jnp.int32)
    rank = jnp.concatenate(ranks, axis=0)
    meta = jnp.zeros((rows, LANES), f32)
    for kk in range(TOP_K):
        meta = jnp.where(lane == _META_EXPERT + kk, firsts[kk], meta)
        meta = jnp.where(lane == _META_GATE + kk, ps[kk] / denom, meta)
        rank_k = jnp.sum(jnp.where(sels[kk], rank, 0.0), axis=-1, keepdims=True)
        meta = jnp.where(lane == _META_RANK + kk, rank_k, meta)
    meta_ref[...] = meta


def _route(logits, tile):
    tokens = logits.shape[0]
    rows = min(ROUTE_ROWS, tokens)
    steps = tokens // rows
    per_step = rows // tile
    return pl.pallas_call(
        functools.partial(_route_body, tile),
        grid=(steps,),
        in_specs=[pl.BlockSpec((rows, LANES), lambda i: (i, 0))],
        out_specs=[pl.BlockSpec((rows, LANES), lambda i: (i, 0)),
                   pl.BlockSpec((per_step * 8, LANES), lambda i: (i, 0))],
        out_shape=[jax.ShapeDtypeStruct((tokens, LANES), f32),
                   jax.ShapeDtypeStruct((tokens // tile * 8, LANES), jnp.int32)],
        compiler_params=pltpu.CompilerParams(
            dimension_semantics=("parallel",), vmem_limit_bytes=VMEM_LIMIT),
        name="route",
    )(logits)


def _lmax(tm):
    full = tm * TOP_K + N_EXPERTS * 2 * (BF16_ROWS - 1)
    return -(-full // 256) * 256


def _token_slots(meta, base_row):
    lane_f = lax.broadcasted_iota(jnp.int32, meta.shape, 1).astype(f32)
    slots = []
    for kk in range(TOP_K):
        sel = lane_f == meta[:, _META_EXPERT + kk:_META_EXPERT + kk + 1]
        first = jnp.sum(jnp.where(sel, base_row, 0.0), axis=-1, keepdims=True)
        slots.append(first + meta[:, _META_RANK + kk:_META_RANK + kk + 1])
    return slots


def _segment_copy(src, dst, sem, src_off, dst_off, rows):
    src_off = pl.multiple_of(src_off, BF16_ROWS)
    dst_off = pl.multiple_of(dst_off, BF16_ROWS)
    rows = pl.multiple_of(rows, BF16_ROWS)
    return pltpu.make_async_copy(src.at[pl.ds(src_off, rows)], dst.at[pl.ds(dst_off, rows)], sem)


def _wait_rows(src, dst, sem, total):
    @pl.when(total > 0)
    def _():
        _segment_copy(src, dst, sem, 0, 0, total).wait()


def _zero_rows(zero_ref, dst_ref, sem, first, start):
    tile = zero_ref.shape[0]
    n = dst_ref.shape[0] - first
    n_full = n // tile

    def full(j):
        return _segment_copy(zero_ref, dst_ref, sem, 0, first + j * tile, tile)

    def rest():
        return _segment_copy(zero_ref, dst_ref, sem, 0, first + n_full * tile, n - n_full * tile)

    def step(j, carry):
        full(j).start() if start else full(j).wait()
        return carry

    lax.fori_loop(0, n_full, step, 0)

    @pl.when(n - n_full * tile > 0)
    def _():
        rest().start() if start else rest().wait()


def _dispatch_body(voff_ref, coff_ref, keep_ref, row0_ref, wlen_ref, wtot_ref, ltot_ref, used_ref,
                   h2_ref, meta_ref, base_ref, meta_next_ref, base_next_ref, xs_ref,
                   xc0_ref, xc1_ref, pt0_ref, pt1_ref, carry_ref, zero_ref, sem, zsem):
    i = pl.program_id(0)
    nt = pl.num_programs(0)
    tm = h2_ref.shape[0]
    lmax = xc0_ref.shape[0]
    slot = i & 1

    def slot_rows(m_ref, b_ref):
        meta = m_ref[...]
        lane = lax.broadcasted_iota(jnp.int32, meta.shape, 1)
        slots = jnp.zeros(meta.shape, f32)
        for kk, s in enumerate(_token_slots(meta, b_ref[0:1, :])):
            slots = jnp.where(lane == kk, s, slots)
        return slots.T[0:8, :].astype(jnp.int32)

    def one_hot_rows(slots_t, c):
        row = lax.broadcasted_iota(jnp.int32, (_CHUNK, tm), 0) + c * _CHUNK
        pt = jnp.zeros((_CHUNK, tm), f32)
        for kk in range(TOP_K):
            pt = jnp.where(row == slots_t[kk:kk + 1, :], 1.0, pt)
        return pt.astype(bf16)

    @pl.when(i == 0)
    def _():
        carry_ref[...] = jnp.zeros_like(carry_ref)
        zero_ref[...] = jnp.zeros_like(zero_ref)
        _zero_rows(zero_ref, xs_ref, zsem, used_ref[0], start=True)
        first = slot_rows(meta_ref, base_ref)
        for c in range(lmax // _CHUNK):
            pt0_ref[c * _CHUNK:(c + 1) * _CHUNK, :] = one_hot_rows(first, c)
        pt1_ref[...] = jnp.zeros_like(pt1_ref)
        xc0_ref[...] = jnp.zeros_like(xc0_ref)
        xc1_ref[...] = jnp.zeros_like(xc1_ref)

    @pl.when(i >= 2)
    def _():
        _wait_rows(xc0_ref, xs_ref, sem.at[slot], wtot_ref[jnp.maximum(i - 2, 0)])

    def merge_carry(t, buf):
        def block(off):
            return buf.at[pl.ds(pl.multiple_of(off, BF16_ROWS), BF16_ROWS)]

        for e in range(N_EXPERTS):
            blk = block(voff_ref[t * N_EXPERTS + e])
            blk[...] = blk[...] + carry_ref[e]
        for e in range(N_EXPERTS):
            j = t * N_EXPERTS + e
            tail = block(coff_ref[j])[...]
            carry_ref[e] = jnp.where(keep_ref[j] > 0, tail, jnp.zeros_like(tail))

    def copy_out(t, buf):
        for e in range(N_EXPERTS):
            j = t * N_EXPERTS + e
            _segment_copy(buf, xs_ref, sem.at[slot], voff_ref[j], row0_ref[j], wlen_ref[j]).start()

    def compact(pt_now_ref, pt_next_ref, xc_now_ref):
        h2 = h2_ref[...]
        nxt = slot_rows(meta_next_ref, base_next_ref)
        live = jnp.maximum(ltot_ref[i], ltot_ref[jnp.minimum(i + 1, nt - 1)])
        for c in range(lmax // _CHUNK):
            @pl.when(c * _CHUNK < live)
            def _():
                rows = slice(c * _CHUNK, (c + 1) * _CHUNK)
                xc_now_ref[rows, :] = _dot(pt_now_ref[rows, :], h2).astype(bf16)
                pt_next_ref[rows, :] = one_hot_rows(nxt, c)
        merge_carry(i, xc_now_ref)
        copy_out(i, xc_now_ref)

    @pl.when(slot == 0)
    def _():
        compact(pt0_ref, pt1_ref, xc0_ref)

    @pl.when(slot == 1)
    def _():
        compact(pt1_ref, pt0_ref, xc1_ref)

    @pl.when(i == nt - 1)
    def _():
        @pl.when(i >= 1)
        def _():
            _wait_rows(xc0_ref, xs_ref, sem.at[1 - slot], wtot_ref[jnp.maximum(i - 1, 0)])
        _wait_rows(xc0_ref, xs_ref, sem.at[slot], wtot_ref[i])
        _zero_rows(zero_ref, xs_ref, zsem, used_ref[0], start=False)


def _dispatch(plan, h2, meta, rows_total):
    tokens = h2.shape[0]
    tm = min(MIX_TS, tokens)
    nt = tokens // tm
    scalars = (plan["voff"], plan["coff"], plan["keep"], plan["row0"], plan["wlen"], plan["wtot"],
               plan["ltot"], plan["used"])
    return pl.pallas_call(
        _dispatch_body,
        grid_spec=pltpu.PrefetchScalarGridSpec(
            num_scalar_prefetch=len(scalars),
            grid=(nt,),
            in_specs=[pl.BlockSpec((tm, D_MODEL), lambda i, *_: (i, 0)),
                      pl.BlockSpec((tm, LANES), lambda i, *_: (i, 0)),
                      pl.BlockSpec((8, LANES), lambda i, *_: (i, 0)),
                      pl.BlockSpec((tm, LANES), lambda i, *_: (jnp.minimum(i + 1, nt - 1), 0)),
                      pl.BlockSpec((8, LANES), lambda i, *_: (jnp.minimum(i + 1, nt - 1), 0))],
            out_specs=pl.BlockSpec(memory_space=pl.ANY),
            scratch_shapes=[pltpu.VMEM((_lmax(tm), D_MODEL), bf16),
                            pltpu.VMEM((_lmax(tm), D_MODEL), bf16),
                            pltpu.VMEM((_lmax(tm), tm), bf16),
                            pltpu.VMEM((_lmax(tm), tm), bf16),
                            pltpu.VMEM((N_EXPERTS, BF16_ROWS, D_MODEL), bf16),
                            pltpu.VMEM((FFN_TM, D_MODEL), bf16),
                            pltpu.SemaphoreType.DMA((2,)), pltpu.SemaphoreType.DMA],
        ),
        out_shape=jax.ShapeDtypeStruct((rows_total, D_MODEL), bf16),
        compiler_params=pltpu.CompilerParams(
            dimension_semantics=("arbitrary",), vmem_limit_bytes=VMEM_LIMIT),
        name="dispatch",
    )(*scalars, h2, meta, plan["base"], meta, plan["base"])


def _ffn_body(gstart_ref, gsize_ref, used_ref, xs_ref, wgu_ref, bgu_ref, wd_ref, bd_ref, os_ref,
              xbuf_ref, obuf_ref, zero_ref, state_ref, isem, osem, zsem):
    e = pl.program_id(0)
    ne = pl.num_programs(0)
    tile = xbuf_ref.shape[1]
    base = gstart_ref[e]
    n = gsize_ref[e]
    n_tiles = (n + tile - 1) // tile

    def rows_of(size, t):
        return jnp.minimum(tile, size - t * tile)

    def in_copy(row0, rows, slot):
        return pltpu.make_async_copy(xs_ref.at[pl.ds(row0, rows)],
                                     xbuf_ref.at[slot, pl.ds(0, rows)], isem.at[slot])

    def out_copy(row0, rows, slot):
        return pltpu.make_async_copy(obuf_ref.at[slot, pl.ds(0, rows)],
                                     os_ref.at[pl.ds(row0, rows)], osem.at[slot])

    def start_tile(copy, row0, rows, slot):
        row0 = pl.multiple_of(row0, BF16_ROWS)

        @pl.when(rows == tile)
        def _():
            copy(row0, tile, slot).start(priority=_TILE_DMA_PRIORITY)

        @pl.when(rows < tile)
        def _():
            copy(row0, pl.multiple_of(rows, BF16_ROWS), slot).start()

    def wait_tile(copy, rows, slot):
        copy(0, pl.multiple_of(rows, BF16_ROWS), slot).wait()

    @pl.when(e == 0)
    def _():
        xbuf_ref[...] = jnp.zeros_like(xbuf_ref)
        zero_ref[...] = jnp.zeros_like(zero_ref)
        _zero_rows(zero_ref, os_ref, zsem, used_ref[0], start=True)
        state_ref[0] = 0
        state_ref[1] = 0
        state_ref[2] = 0
        state_ref[3] = 0

    done = state_ref[0]

    @pl.when(jnp.logical_and(n_tiles > 0, state_ref[1] == 0))
    def _():
        start_tile(in_copy, base, rows_of(n, 0), done & 1)

    nxt = jnp.minimum(e + 1, ne - 1)
    has_next = jnp.logical_and(e + 1 < ne, gsize_ref[nxt] > 0)

    def step(t, carry):
        slot = (done + t) & 1

        @pl.when(t + 1 < n_tiles)
        def _():
            start_tile(in_copy, base + (t + 1) * tile, rows_of(n, t + 1), 1 - slot)

        @pl.when(jnp.logical_and(t + 1 == n_tiles, has_next))
        def _():
            start_tile(in_copy, gstart_ref[nxt], rows_of(gsize_ref[nxt], 0), 1 - slot)

        rows = rows_of(n, t)
        wait_tile(in_copy, rows, slot)

        @pl.when(state_ref[2 + slot] > 0)
        def _():
            wait_tile(out_copy, state_ref[2 + slot], slot)

        def ffn(r0, m):
            gu = _dot(xbuf_ref[slot, r0:r0 + m, :], wgu_ref[...].astype(bf16)) + bgu_ref[...]
            gate = jnp.minimum(gu[:, :D_FF], SWIGLU_LIMIT)
            lin = jnp.clip(gu[:, D_FF:], -SWIGLU_LIMIT, SWIGLU_LIMIT)
            act = (lin + 1.0) * (gate * jax.nn.sigmoid(SWIGLU_ALPHA * gate))
            out = _dot(act.astype(bf16), wd_ref[...].astype(bf16)) + bd_ref[...]
            obuf_ref[slot, r0:r0 + m, :] = out.astype(bf16)

        half, quarter = tile // 2, tile // 4
        for lo, hi, chains in ((half + quarter, tile, ((0, half), (half, half))),
                               (half, half + quarter, ((0, half), (half, quarter))),
                               (quarter, half, ((0, half),)),
                               (0, quarter, ((0, quarter),))):
            @pl.when(jnp.logical_and(rows > lo, rows <= hi))
            def _():
                for r0, m in chains:
                    ffn(r0, m)

        start_tile(out_copy, base + t * tile, rows, slot)
        state_ref[2 + slot] = rows
        return carry

    lax.fori_loop(0, n_tiles, step, 0)

    @pl.when(n_tiles > 0)
    def _():
        state_ref[0] = done + n_tiles
        state_ref[1] = has_next.astype(jnp.int32)

    @pl.when(e == ne - 1)
    def _():
        for slot in range(2):
            @pl.when(state_ref[2 + slot] > 0)
            def _():
                wait_tile(out_copy, state_ref[2 + slot], slot)
        _zero_rows(zero_ref, os_ref, zsem, used_ref[0], start=False)


def _ffn(gstart, gsize, used, xs, wgu, bgu, wd, bd):
    rows_total = xs.shape[0]
    return pl.pallas_call(
        _ffn_body,
        grid_spec=pltpu.PrefetchScalarGridSpec(
            num_scalar_prefetch=3,
            grid=(N_EXPERTS,),
            in_specs=[pl.BlockSpec(memory_space=pl.ANY),
                      pl.BlockSpec((None, D_MODEL, 2 * D_FF), lambda e, *_: (e, 0, 0)),
                      pl.BlockSpec((None, 1, 2 * D_FF), lambda e, *_: (e, 0, 0)),
                      pl.BlockSpec((None, D_FF, D_MODEL), lambda e, *_: (e, 0, 0)),
                      pl.BlockSpec((None, 1, D_MODEL), lambda e, *_: (e, 0, 0))],
            out_specs=pl.BlockSpec(memory_space=pl.ANY),
            scratch_shapes=[pltpu.VMEM((2, FFN_TM, D_MODEL), bf16),
                            pltpu.VMEM((2, FFN_TM, D_MODEL), bf16),
                            pltpu.VMEM((FFN_TM, D_MODEL), bf16),
                            pltpu.SMEM((4,), jnp.int32),
                            pltpu.SemaphoreType.DMA((2,)), pltpu.SemaphoreType.DMA((2,)),
                            pltpu.SemaphoreType.DMA],
        ),
        out_shape=jax.ShapeDtypeStruct((rows_total, D_MODEL), bf16),
        compiler_params=pltpu.CompilerParams(
            dimension_semantics=("arbitrary",), vmem_limit_bytes=VMEM_LIMIT),
        name="experts",
    )(gstart, gsize, used, xs, wgu, bgu, wd, bd)


def _combine_body(final_norm, voff_ref, row0_ref, rlen_ref, rtot_ref, ltot_ref,
                  os_ref, meta_ref, base_ref, meta_next_ref, base_next_ref, x1_ref, gfin_ref,
                  out_ref, oc0_ref, oc1_ref, p0_ref, p1_ref, sem):
    i = pl.program_id(0)
    nt = pl.num_programs(0)
    tm = x1_ref.shape[0]
    lmax = oc0_ref.shape[0]
    slot = i & 1

    def slots_and_gates(m_ref, b_ref):
        meta = m_ref[...]
        slots = [s.astype(jnp.int32) for s in _token_slots(meta, b_ref[0:1, :])]
        gates = [meta[:, _META_GATE + kk:_META_GATE + kk + 1] for kk in range(TOP_K)]
        return slots, gates

    def gate_cols(slots, gates, c):
        col = lax.broadcasted_iota(jnp.int32, (tm, _CHUNK), 1) + c * _CHUNK
        p = jnp.zeros((tm, _CHUNK), f32)
        for s, g in zip(slots, gates):
            p = jnp.where(col == s, g, p)
        return p.astype(bf16)

    def fetch(tile, into_ref, sem_slot, rows_of):
        for e in range(N_EXPERTS):
            j = tile * N_EXPERTS + e
            _segment_copy(os_ref, into_ref, sem.at[sem_slot], row0_ref[j], voff_ref[j],
                          rows_of(rlen_ref[j])).start()

    @pl.when(i == 0)
    def _():
        oc0_ref[...] = jnp.zeros_like(oc0_ref)
        oc1_ref[...] = jnp.zeros_like(oc1_ref)
        fetch(0, oc0_ref, 0, lambda rows: rows)
        slots, gates = slots_and_gates(meta_ref, base_ref)
        for c in range(lmax // _CHUNK):
            p0_ref[:, c * _CHUNK:(c + 1) * _CHUNK] = gate_cols(slots, gates, c)
        p1_ref[...] = jnp.zeros_like(p1_ref)

    _wait_rows(os_ref, oc0_ref, sem.at[slot], rtot_ref[i])

    def request_next(oc_next_ref):
        fetch(jnp.minimum(i + 1, nt - 1), oc_next_ref, 1 - slot,
              lambda rows: jnp.where(i + 1 < nt, rows, 0))

    @pl.when(slot == 0)
    def _():
        request_next(oc1_ref)

    @pl.when(slot == 1)
    def _():
        request_next(oc0_ref)

    def restore(p_now_ref, p_next_ref, oc_now_ref):
        slots, gates = slots_and_gates(meta_next_ref, base_next_ref)
        lt_now = ltot_ref[i]
        live = jnp.maximum(lt_now, ltot_ref[jnp.minimum(i + 1, nt - 1)])
        out_ref[...] = x1_ref[...]
        for c in range(lmax // _CHUNK):
            @pl.when(c * _CHUNK < live)
            def _():
                ks = slice(c * _CHUNK, (c + 1) * _CHUNK)
                part = _dot(p_now_ref[:, ks], oc_now_ref[ks, :])
                out_ref[...] += jnp.where(c * _CHUNK < lt_now, part, 0.0)
                p_next_ref[:, ks] = gate_cols(slots, gates, c)
        if final_norm:
            out_ref[...] = _rms(out_ref[...], gfin_ref[...])

    @pl.when(slot == 0)
    def _():
        restore(p0_ref, p1_ref, oc0_ref)

    @pl.when(slot == 1)
    def _():
        restore(p1_ref, p0_ref, oc1_ref)


def _combine(plan, os, meta, x1, gfin, final_norm):
    tokens = x1.shape[0]
    tm = min(MIX_TS, tokens)
    nt = tokens // tm
    scalars = (plan["voff"], plan["row0"], plan["rlen"], plan["rtot"], plan["ltot"])
    return pl.pallas_call(
        functools.partial(_combine_body, final_norm),
        grid_spec=pltpu.PrefetchScalarGridSpec(
            num_scalar_prefetch=len(scalars),
            grid=(nt,),
            in_specs=[pl.BlockSpec(memory_space=pl.ANY),
                      pl.BlockSpec((tm, LANES), lambda i, *_: (i, 0)),
                      pl.BlockSpec((8, LANES), lambda i, *_: (i, 0)),
                      pl.BlockSpec((tm, LANES), lambda i, *_: (jnp.minimum(i + 1, nt - 1), 0)),
                      pl.BlockSpec((8, LANES), lambda i, *_: (jnp.minimum(i + 1, nt - 1), 0)),
                      pl.BlockSpec((tm, D_MODEL), lambda i, *_: (i, 0)),
                      pl.BlockSpec((1, D_MODEL), lambda i, *_: (0, 0))],
            out_specs=pl.BlockSpec((tm, D_MODEL), lambda i, *_: (i, 0)),
            scratch_shapes=[pltpu.VMEM((_lmax(tm), D_MODEL), bf16),
                            pltpu.VMEM((_lmax(tm), D_MODEL), bf16),
                            pltpu.VMEM((tm, _lmax(tm)), bf16),
                            pltpu.VMEM((tm, _lmax(tm)), bf16),
                            pltpu.SemaphoreType.DMA((2,))],
        ),
        out_shape=jax.ShapeDtypeStruct((tokens, D_MODEL), f32),
        compiler_params=pltpu.CompilerParams(
            dimension_semantics=("arbitrary",), vmem_limit_bytes=VMEM_LIMIT),
        name="combine",
    )(*scalars, os, meta, plan["base"], meta, plan["base"], x1, gfin)


def _plan(counts, nt):
    up = lambda v: (v + (BF16_ROWS - 1)) // BF16_ROWS * BF16_ROWS
    down = lambda v: v // BF16_ROWS * BF16_ROWS
    c = counts.reshape(nt, 8, LANES)[:, 0, :N_EXPERTS]
    gsize = up(jnp.sum(c, axis=0))
    gstart = jnp.cumsum(gsize) - gsize
    before = jnp.cumsum(c, axis=0) - c
    a = before % BF16_ROWS
    length = a + c
    voff = jnp.cumsum(up(length), axis=1) - up(length)
    last = jnp.arange(nt)[:, None] == nt - 1
    wlen = jnp.where(last, up(length), down(length))
    rlen = jnp.where(c > 0, up(length), 0)
    base = jnp.zeros((nt, 8, LANES), jnp.float32)
    base = base.at[:, :, :N_EXPERTS].set((voff + a).astype(jnp.float32)[:, None, :])
    i32 = lambda v: v.reshape(-1).astype(jnp.int32)
    return dict(
        voff=i32(voff), coff=i32(voff + down(length)), keep=i32(length % BF16_ROWS),
        row0=i32(gstart[None, :] + before - a), wlen=i32(wlen), wtot=i32(jnp.sum(wlen, axis=1)),
        rlen=i32(rlen), rtot=i32(jnp.sum(rlen, axis=1)), ltot=i32(jnp.sum(up(length), axis=1)),
        base=base.reshape(nt * 8, LANES),
        gstart=i32(gstart), gsize=i32(gsize), used=i32(jnp.sum(gsize, keepdims=True)))


def kernel(x, g_mix, w_in, w_alpha_up, b_alpha, g_gla, w_conv, w_branch_a, w_branch_b, w_out, g_ffn, w_router, b_router, w_gate_up, b_gate_up, w_down, b_down, g_final):
    batch, seq, _ = x.shape
    tokens = batch * seq
    depth = w_in.shape[0]
    x2 = x.reshape(tokens, D_MODEL)
    nt = tokens // min(MIX_TS, seq)
    rows_total = tokens * TOP_K + N_EXPERTS * (BF16_ROWS - 1)
    rows_total = -(-rows_total // BF16_ROWS) * BF16_ROWS

    for l in range(depth):
        wi = w_in[l]
        a0 = 2 * QK_W + 2 * V_W
        wmain = jnp.concatenate([wi[:, :a0].astype(bf16), wi[:, a0 + GLA_RANK:].astype(bf16)], axis=1)
        walr = jnp.pad(wi[:, a0:a0 + GLA_RANK], ((0, 0), (0, LANES - GLA_RANK))).astype(bf16)
        wup = jnp.pad(w_alpha_up[l], ((0, LANES - GLA_RANK), (0, 0))).astype(bf16)
        wr = jnp.pad(w_router[l], ((0, 0), (0, LANES - N_EXPERTS)))
        wrh = wr.astype(bf16)
        wrl = (wr - wrh.astype(f32)).astype(bf16)
        br = jnp.pad(b_router[l], (0, LANES - N_EXPERTS)).reshape(1, LANES)
        wconv = jnp.pad(w_conv[l], ((0, 8 - CONV_K), (0, 0)))

        x1, h2, logits = _mixer(
            x2.reshape(batch, seq, D_MODEL), g_mix[l].reshape(1, D_MODEL), wmain, walr, wup,
            b_alpha[l].reshape(1, QK_W), g_gla[l].reshape(1, V_W), wconv,
            w_branch_a[l].astype(bf16), w_branch_b[l].astype(bf16), w_out[l].astype(bf16),
            g_ffn[l].reshape(1, D_MODEL), wrh, wrl, br)

        meta, counts = _route(logits, min(MIX_TS, seq))
        plan = _plan(counts, nt)
        xs = _dispatch(plan, h2, meta, rows_total)
        os = _ffn(plan["gstart"], plan["gsize"], plan["used"], xs,
                  w_gate_up[l], b_gate_up[l].reshape(N_EXPERTS, 1, 2 * D_FF),
                  w_down[l], b_down[l].reshape(N_EXPERTS, 1, D_MODEL))
        x2 = _combine(plan, os, meta, x1, g_final.reshape(1, D_MODEL), l == depth - 1)
    return x2.reshape(batch, seq, D_MODEL)
```

```python
import functools

import jax
import jax.numpy as jnp
from jax import lax
from jax.experimental import pallas as pl
from jax.experimental.pallas import tpu as pltpu

D_MODEL = 1024
GLA_HEADS = 4
GLA_DK = 128
GLA_DV = 256
GLA_RANK = 16
GLA_TAU = 16.0
GLA_CHUNK = 64
_CHUNK_SHIFT = GLA_CHUNK.bit_length() - 1
CONV_K = 3
N_EXPERTS = 32
TOP_K = 4
D_FF = 1024
SWIGLU_LIMIT = 7.0
SWIGLU_ALPHA = 1.702
EPS = 1e-6

QK_W = GLA_HEADS * GLA_DK
V_W = GLA_HEADS * GLA_DV

LANES = 128
BF16_ROWS = 16
VMEM_LIMIT = 56 * 1024 * 1024

MIX_TS = 256
MIX_ROWS = 2
ROUTE_ROWS = 1024
_CHUNK = 256
FFN_TM = 1024
_META_EXPERT, _META_GATE, _META_RANK = 0, TOP_K, 2 * TOP_K
_TILE_DMA_PRIORITY = 1

_C_Q, _C_K, _C_V, _C_R = 0, QK_W, 2 * QK_W, 2 * QK_W + V_W
_C_CB = _C_R + V_W
_C_CC = _C_CB + D_MODEL
_C_CX = _C_CC + D_MODEL
_C_GA = _C_CX + D_MODEL
_C_GB = _C_GA + D_MODEL
_W_MAIN = _C_GB + D_MODEL

f32 = jnp.float32
bf16 = jnp.bfloat16


def _rms(x, g):
    return x * lax.rsqrt(jnp.mean(x * x, axis=-1, keepdims=True) + EPS) * g


def _dot(a, b):
    return jnp.dot(a, b, preferred_element_type=f32)


def _split_bf16(a):
    hi = a.astype(bf16)
    lo = (a - hi.astype(f32)).astype(bf16)
    return hi, lo


def _mixer_body(x_ref, gmix_ref, whead_ref, wtail_ref, walr_ref, wup_ref, balpha_ref, ggla_ref, wconv_ref,
                wa_ref, wb_ref, wo_ref, gffn_ref, wrh_ref, wrl_ref, br_ref,
                x1_ref, h2_ref, logit_ref,
                state_ref, ubuf_ref, obuf_ref):
    nb, ts = x_ref.shape[0], x_ref.shape[1]
    m = nb * ts
    nchunk = ts // GLA_CHUNK
    seqs = [slice(ch * ts, (ch + 1) * ts) for ch in range(nb)]

    @pl.when(pl.program_id(1) == 0)
    def _():
        state_ref[...] = jnp.zeros_like(state_ref)
        ubuf_ref[:, 0:8, :] = jnp.zeros((nb, 8, D_MODEL), f32)

    x = x_ref[...].reshape(m, D_MODEL)
    hb = _rms(x, gmix_ref[...]).astype(bf16)

    def proj(c0, width):
        if c0 < _C_CB:
            return _dot(hb, whead_ref[:, c0:c0 + width])
        return _dot(hb, wtail_ref[:, c0 - _C_CB:c0 - _C_CB + width])

    a_lr = _dot(hb, walr_ref[...])
    pre = _dot(a_lr.astype(bf16), wup_ref[...]) + balpha_ref[...]
    log_a = -(jnp.maximum(-pre, 0.0) + jnp.log1p(jnp.exp(-jnp.abs(pre)))) * (1.0 / GLA_TAU)
    ri = lax.broadcasted_iota(jnp.int32, (ts, ts), 0)
    ci = lax.broadcasted_iota(jnp.int32, (ts, ts), 1)
    same_chunk = (ri >> _CHUNK_SHIFT) == (ci >> _CHUNK_SHIFT)
    causal = jnp.logical_and(same_chunk, ci <= ri)
    tri = jnp.where(causal, 1.0, 0.0).astype(bf16)
    la_hi, la_lo = _split_bf16(log_a)
    b = jnp.concatenate([_dot(tri, la_hi[sq]) + _dot(tri, la_lo[sq]) for sq in seqs], axis=0)
    lasts = [b[(c + 1) * GLA_CHUNK - 1:(c + 1) * GLA_CHUNK, :] for c in range(nb * nchunk)]
    b_tot = jnp.concatenate([jnp.broadcast_to(r, (GLA_CHUNK, QK_W)) for r in lasts], axis=0)
    e_pos = jnp.exp(b)
    e_neg = jnp.exp(-b)
    e_tail = jnp.exp(b_tot - b)
    e_tot = [jnp.exp(r) for r in lasts]

    q = proj(_C_Q, QK_W)
    k = proj(_C_K, QK_W)
    q_dec = (q * (GLA_DK ** -0.5) * e_pos).astype(bf16)
    k_inv = (k * e_neg).astype(bf16)
    k_tail = (k * e_tail).astype(bf16)
    vb = proj(_C_V, V_W).astype(bf16)

    for ch, sq in enumerate(seqs):
        for hh in range(GLA_HEADS):
            qs = slice(hh * GLA_DK, (hh + 1) * GLA_DK)
            vs = slice(hh * GLA_DV, (hh + 1) * GLA_DV)
            sc = lax.dot_general(q_dec[sq, qs], k_inv[sq, qs], (((1,), (1,)), ((), ())),
                                 preferred_element_type=f32)
            sc = jnp.where(causal, sc, 0.0).astype(bf16)
            obuf_ref[sq, vs] = _dot(sc, vb[sq, vs])
            st = state_ref[ch, hh]
            for c in range(nchunk):
                rs = slice(ch * ts + c * GLA_CHUNK, ch * ts + (c + 1) * GLA_CHUNK)
                o_inter = lax.dot_general(q_dec[rs, qs], st.astype(bf16), (((1,), (1,)), ((), ())),
                                          preferred_element_type=f32)
                obuf_ref[rs, vs] += o_inter
                upd = lax.dot_general(vb[rs, vs], k_tail[rs, qs], (((0,), (0,)), ((), ())),
                                      preferred_element_type=f32)
                st = st * e_tot[ch * nchunk + c][:, qs] + upd
            state_ref[ch, hh] = st

    r = proj(_C_R, V_W)
    ggla = ggla_ref[...]
    o_parts = []
    for hh in range(GLA_HEADS):
        vs = slice(hh * GLA_DV, (hh + 1) * GLA_DV)
        o_h = _rms(obuf_ref[:, vs], ggla[:, vs])
        r_h = r[:, vs]
        o_parts.append((o_h * (r_h * jax.nn.sigmoid(r_h))).astype(bf16))
    y_a = _dot(jnp.concatenate(o_parts, axis=1), wa_ref[...])

    u = proj(_C_CC, D_MODEL) * proj(_C_CX, D_MODEL)
    wc = wconv_ref[...]
    y_convs = []
    for ch, sq in enumerate(seqs):
        ubuf_ref[ch, 8:8 + ts, :] = u[sq]
        y_convs.append(wc[0:1, :] * ubuf_ref[ch, 6:6 + ts, :] + wc[1:2, :] * ubuf_ref[ch, 7:7 + ts, :]
                       + wc[2:3, :] * u[sq])
        ubuf_ref[ch, 0:8, :] = ubuf_ref[ch, ts:ts + 8, :]
    y_conv = jnp.concatenate(y_convs, axis=0)
    y_b = _dot((proj(_C_CB, D_MODEL) * y_conv).astype(bf16), wb_ref[...])

    mixed = (jax.nn.sigmoid(proj(_C_GA, D_MODEL)) * y_a
             + jax.nn.sigmoid(proj(_C_GB, D_MODEL)) * y_b)
    x1 = x + _dot(mixed.astype(bf16), wo_ref[...])
    x1_ref[...] = x1.reshape(nb, ts, D_MODEL)

    h2 = _rms(x1, gffn_ref[...])
    h2_hi, h2_lo = _split_bf16(h2)
    h2_ref[...] = h2_hi.reshape(nb, ts, D_MODEL)
    wrh = wrh_ref[...]
    logits = _dot(h2_hi, wrh) + _dot(h2_lo, wrh) + _dot(h2_hi, wrl_ref[...]) + br_ref[...]
    logit_ref[...] = logits.reshape(nb, ts, LANES)


def _mixer(x3, gmix, whead, wtail, walr, wup, balpha, ggla, wconv, wa, wb, wo, gffn, wrh, wrl, br):
    batch, seq, _ = x3.shape
    ts = min(MIX_TS, seq)
    ns = seq // ts
    tokens = batch * seq
    nb = MIX_ROWS if batch % MIX_ROWS == 0 else 1

    def const(shape):
        return pl.BlockSpec(shape, lambda b, s: (0,) * len(shape), pipeline_mode=pl.Buffered(1))

    def row(width):
        return pl.BlockSpec((nb, ts, width), lambda b, s: (b, s, 0))

    x1, h2, logits = pl.pallas_call(
        _mixer_body,
        grid=(batch // nb, ns),
        in_specs=[row(D_MODEL), const(gmix.shape), const(whead.shape), const(wtail.shape),
                  const(walr.shape),
                  const(wup.shape), const(balpha.shape), const(ggla.shape), const(wconv.shape),
                  const(wa.shape), const(wb.shape), const(wo.shape), const(gffn.shape),
                  const(wrh.shape), const(wrl.shape), const(br.shape)],
        out_specs=[row(D_MODEL), row(D_MODEL), row(LANES)],
        out_shape=[jax.ShapeDtypeStruct((batch, seq, D_MODEL), f32),
                   jax.ShapeDtypeStruct((batch, seq, D_MODEL), bf16),
                   jax.ShapeDtypeStruct((batch, seq, LANES), f32)],
        scratch_shapes=[pltpu.VMEM((nb, GLA_HEADS, GLA_DV, GLA_DK), f32),
                        pltpu.VMEM((nb, ts + 8, D_MODEL), f32),
                        pltpu.VMEM((nb * ts, V_W), f32)],
        compiler_params=pltpu.CompilerParams(
            dimension_semantics=("parallel", "arbitrary"), vmem_limit_bytes=VMEM_LIMIT),
        name="mixer",
    )(x3, gmix, whead, wtail, walr, wup, balpha, ggla, wconv, wa, wb, wo, gffn, wrh, wrl, br)
    return (x1.reshape(tokens, D_MODEL), h2.reshape(tokens, D_MODEL),
            logits.reshape(tokens, LANES))


def _route_body(tile, logit_ref, meta_ref, cnt_ref):
    rows = logit_ref.shape[0]
    lane = lax.broadcasted_iota(jnp.int32, (rows, LANES), 1)
    lane_f = lane.astype(f32)
    lg = jnp.where(lane < N_EXPERTS, logit_ref[...], -jnp.inf)
    sels, tops, firsts = [], [], []
    for _ in range(TOP_K):
        m = jnp.max(lg, axis=-1, keepdims=True)
        first = jnp.min(jnp.where(lg == m, lane_f, float(LANES)), axis=-1, keepdims=True)
        sel = lane_f == first
        sels.append(sel)
        tops.append(m)
        firsts.append(first)
        lg = jnp.where(sel, -jnp.inf, lg)
    ps = [jnp.exp(t - tops[0]) for t in tops]
    denom = ps[0] + ps[1] + ps[2] + ps[3]
    onehot = jnp.zeros((rows, LANES), f32)
    for sel in sels:
        onehot = onehot + jnp.where(sel, 1.0, 0.0)
    onehot_b = onehot.astype(bf16)
    ri = lax.broadcasted_iota(jnp.int32, (tile, tile), 0)
    ci = lax.broadcasted_iota(jnp.int32, (tile, tile), 1)
    strict_lower = jnp.where(ci < ri, 1.0, 0.0).astype(bf16)
    ones = jnp.ones((8, tile), bf16)
    ranks = []
    for j in range(rows // tile):
        part = onehot_b[j * tile:(j + 1) * tile, :]
        ranks.append(_dot(strict_lower, part))
        cnt_ref[j * 8:(j + 1) * 8, :] = _dot(ones, part).astype(jnp.int32)
    rank = jnp.concatenate(ranks, axis=0)
    meta = jnp.zeros((rows, LANES), f32)
    for kk in range(TOP_K):
        meta = jnp.where(lane == _META_EXPERT + kk, firsts[kk], meta)
        meta = jnp.where(lane == _META_GATE + kk, ps[kk] / denom, meta)
        rank_k = jnp.sum(jnp.where(sels[kk], rank, 0.0), axis=-1, keepdims=True)
        meta = jnp.where(lane == _META_RANK + kk, rank_k, meta)
    meta_ref[...] = meta


def _route(logits, tile):
    tokens = logits.shape[0]
    rows = min(ROUTE_ROWS, tokens)
    steps = tokens // rows
    per_step = rows // tile
    return pl.pallas_call(
        functools.partial(_route_body, tile),
        grid=(steps,),
        in_specs=[pl.BlockSpec((rows, LANES), lambda i: (i, 0))],
        out_specs=[pl.BlockSpec((rows, LANES), lambda i: (i, 0)),
                   pl.BlockSpec((per_step * 8, LANES), lambda i: (i, 0))],
        out_shape=[jax.ShapeDtypeStruct((tokens, LANES), f32),
                   jax.ShapeDtypeStruct((tokens // tile * 8, LANES), jnp.int32)],
        compiler_params=pltpu.CompilerParams(
            dimension_semantics=("parallel",), vmem_limit_bytes=VMEM_LIMIT),
        name="route",
    )(logits)


def _lmax(tm):
    full = tm * TOP_K + N_EXPERTS * 2 * (BF16_ROWS - 1)
    return -(-full // 256) * 256


def _token_slots(meta, base_row):
    lane_f = lax.broadcasted_iota(jnp.int32, meta.shape, 1).astype(f32)
    slots = []
    for kk in range(TOP_K):
        sel = lane_f == meta[:, _META_EXPERT + kk:_META_EXPERT + kk + 1]
        first = jnp.sum(jnp.where(sel, base_row, 0.0), axis=-1, keepdims=True)
        slots.append(first + meta[:, _META_RANK + kk:_META_RANK + kk + 1])
    return slots


def _segment_copy(src, dst, sem, src_off, dst_off, rows):
    src_off = pl.multiple_of(src_off, BF16_ROWS)
    dst_off = pl.multiple_of(dst_off, BF16_ROWS)
    rows = pl.multiple_of(rows, BF16_ROWS)
    return pltpu.make_async_copy(src.at[pl.ds(src_off, rows)], dst.at[pl.ds(dst_off, rows)], sem)


def _wait_rows(src, dst, sem, total):
    @pl.when(total > 0)
    def _():
        _segment_copy(src, dst, sem, 0, 0, total).wait()


def _zero_rows(zero_ref, dst_ref, sem, first, start):
    tile = zero_ref.shape[0]
    n = dst_ref.shape[0] - first
    n_full = n // tile

    def full(j):
        return _segment_copy(zero_ref, dst_ref, sem, 0, first + j * tile, tile)

    def rest():
        return _segment_copy(zero_ref, dst_ref, sem, 0, first + n_full * tile, n - n_full * tile)

    def step(j, carry):
        full(j).start() if start else full(j).wait()
        return carry

    lax.fori_loop(0, n_full, step, 0)

    @pl.when(n - n_full * tile > 0)
    def _():
        rest().start() if start else rest().wait()


def _dispatch_body(voff_ref, coff_ref, keep_ref, row0_ref, wlen_ref, wtot_ref, used_ref,
                   h2_ref, meta_ref, base_ref, meta_next_ref, base_next_ref, xs_ref,
                   xc0_ref, xc1_ref, pt0_ref, pt1_ref, carry_ref, zero_ref, sem, zsem):
    i = pl.program_id(0)
    nt = pl.num_programs(0)
    tm = h2_ref.shape[0]
    lmax = xc0_ref.shape[0]
    slot = i & 1

    def slot_rows(m_ref, b_ref):
        meta = m_ref[...]
        lane = lax.broadcasted_iota(jnp.int32, meta.shape, 1)
        slots = jnp.zeros(meta.shape, f32)
        for kk, s in enumerate(_token_slots(meta, b_ref[0:1, :])):
            slots = jnp.where(lane == kk, s, slots)
        return slots.T[0:8, :].astype(jnp.int32)

    def one_hot_rows(slots_t, c):
        row = lax.broadcasted_iota(jnp.int32, (_CHUNK, tm), 0) + c * _CHUNK
        pt = jnp.zeros((_CHUNK, tm), f32)
        for kk in range(TOP_K):
            pt = jnp.where(row == slots_t[kk:kk + 1, :], 1.0, pt)
        return pt.astype(bf16)

    @pl.when(i == 0)
    def _():
        carry_ref[...] = jnp.zeros_like(carry_ref)
        zero_ref[...] = jnp.zeros_like(zero_ref)
        _zero_rows(zero_ref, xs_ref, zsem, used_ref[0], start=True)
        first = slot_rows(meta_ref, base_ref)
        for c in range(lmax // _CHUNK):
            pt0_ref[c * _CHUNK:(c + 1) * _CHUNK, :] = one_hot_rows(first, c)

    @pl.when(i >= 2)
    def _():
        _wait_rows(xc0_ref, xs_ref, sem.at[slot], wtot_ref[jnp.maximum(i - 2, 0)])

    def merge_carry(t, buf):
        def block(off):
            return buf.at[pl.ds(pl.multiple_of(off, BF16_ROWS), BF16_ROWS)]

        for e in range(N_EXPERTS):
            blk = block(voff_ref[t * N_EXPERTS + e])
            blk[...] = blk[...] + carry_ref[e]
        for e in range(N_EXPERTS):
            j = t * N_EXPERTS + e
            tail = block(coff_ref[j])[...]
            carry_ref[e] = jnp.where(keep_ref[j] > 0, tail, jnp.zeros_like(tail))

    def copy_out(t, buf):
        for e in range(N_EXPERTS):
            j = t * N_EXPERTS + e
            _segment_copy(buf, xs_ref, sem.at[slot], voff_ref[j], row0_ref[j], wlen_ref[j]).start()

    def compact(pt_now_ref, pt_next_ref, xc_now_ref):
        h2 = h2_ref[...]
        nxt = slot_rows(meta_next_ref, base_next_ref)
        for c in range(lmax // _CHUNK):
            rows = slice(c * _CHUNK, (c + 1) * _CHUNK)
            xc_now_ref[rows, :] = _dot(pt_now_ref[rows, :], h2).astype(bf16)
            pt_next_ref[rows, :] = one_hot_rows(nxt, c)
        merge_carry(i, xc_now_ref)
        copy_out(i, xc_now_ref)

    @pl.when(slot == 0)
    def _():
        compact(pt0_ref, pt1_ref, xc0_ref)

    @pl.when(slot == 1)
    def _():
        compact(pt1_ref, pt0_ref, xc1_ref)

    @pl.when(i == nt - 1)
    def _():
        @pl.when(i >= 1)
        def _():
            _wait_rows(xc0_ref, xs_ref, sem.at[1 - slot], wtot_ref[jnp.maximum(i - 1, 0)])
        _wait_rows(xc0_ref, xs_ref, sem.at[slot], wtot_ref[i])
        _zero_rows(zero_ref, xs_ref, zsem, used_ref[0], start=False)


def _dispatch(plan, h2, meta, rows_total):
    tokens = h2.shape[0]
    tm = min(MIX_TS, tokens)
    nt = tokens // tm
    scalars = (plan["voff"], plan["coff"], plan["keep"], plan["row0"], plan["wlen"], plan["wtot"],
               plan["used"])
    return pl.pallas_call(
        _dispatch_body,
        grid_spec=pltpu.PrefetchScalarGridSpec(
            num_scalar_prefetch=len(scalars),
            grid=(nt,),
            in_specs=[pl.BlockSpec((tm, D_MODEL), lambda i, *_: (i, 0)),
                      pl.BlockSpec((tm, LANES), lambda i, *_: (i, 0)),
                      pl.BlockSpec((8, LANES), lambda i, *_: (i, 0)),
                      pl.BlockSpec((tm, LANES), lambda i, *_: (jnp.minimum(i + 1, nt - 1), 0)),
                      pl.BlockSpec((8, LANES), lambda i, *_: (jnp.minimum(i + 1, nt - 1), 0))],
            out_specs=pl.BlockSpec(memory_space=pl.ANY),
            scratch_shapes=[pltpu.VMEM((_lmax(tm), D_MODEL), bf16),
                            pltpu.VMEM((_lmax(tm), D_MODEL), bf16),
                            pltpu.VMEM((_lmax(tm), tm), bf16),
                            pltpu.VMEM((_lmax(tm), tm), bf16),
                            pltpu.VMEM((N_EXPERTS, BF16_ROWS, D_MODEL), bf16),
                            pltpu.VMEM((FFN_TM, D_MODEL), bf16),
                            pltpu.SemaphoreType.DMA((2,)), pltpu.SemaphoreType.DMA],
        ),
        out_shape=jax.ShapeDtypeStruct((rows_total, D_MODEL), bf16),
        compiler_params=pltpu.CompilerParams(
            dimension_semantics=("arbitrary",), vmem_limit_bytes=VMEM_LIMIT),
        name="dispatch",
    )(*scalars, h2, meta, plan["base"], meta, plan["base"])


def _ffn_body(gstart_ref, gsize_ref, used_ref, xs_ref, wgu_ref, bgu_ref, wd_ref, bd_ref, os_ref,
              xbuf_ref, obuf_ref, zero_ref, state_ref, isem, osem, zsem):
    e = pl.program_id(0)
    ne = pl.num_programs(0)
    tile = xbuf_ref.shape[1]
    base = gstart_ref[e]
    n = gsize_ref[e]
    n_tiles = (n + tile - 1) // tile

    def rows_of(size, t):
        return jnp.minimum(tile, size - t * tile)

    def in_copy(row0, rows, slot):
        return pltpu.make_async_copy(xs_ref.at[pl.ds(row0, rows)],
                                     xbuf_ref.at[slot, pl.ds(0, rows)], isem.at[slot])

    def out_copy(row0, rows, slot):
        return pltpu.make_async_copy(obuf_ref.at[slot, pl.ds(0, rows)],
                                     os_ref.at[pl.ds(row0, rows)], osem.at[slot])

    def start_tile(copy, row0, rows, slot):
        row0 = pl.multiple_of(row0, BF16_ROWS)

        @pl.when(rows == tile)
        def _():
            copy(row0, tile, slot).start(priority=_TILE_DMA_PRIORITY)

        @pl.when(rows < tile)
        def _():
            copy(row0, pl.multiple_of(rows, BF16_ROWS), slot).start()

    def wait_tile(copy, rows, slot):
        copy(0, pl.multiple_of(rows, BF16_ROWS), slot).wait()

    @pl.when(e == 0)
    def _():
        xbuf_ref[...] = jnp.zeros_like(xbuf_ref)
        zero_ref[...] = jnp.zeros_like(zero_ref)
        _zero_rows(zero_ref, os_ref, zsem, used_ref[0], start=True)
        state_ref[0] = 0
        state_ref[1] = 0
        state_ref[2] = 0
        state_ref[3] = 0

    done = state_ref[0]

    @pl.when(jnp.logical_and(n_tiles > 0, state_ref[1] == 0))
    def _():
        start_tile(in_copy, base, rows_of(n, 0), done & 1)

    nxt = jnp.minimum(e + 1, ne - 1)
    has_next = jnp.logical_and(e + 1 < ne, gsize_ref[nxt] > 0)

    def step(t, carry):
        slot = (done + t) & 1

        @pl.when(t + 1 < n_tiles)
        def _():
            start_tile(in_copy, base + (t + 1) * tile, rows_of(n, t + 1), 1 - slot)

        @pl.when(jnp.logical_and(t + 1 == n_tiles, has_next))
        def _():
            start_tile(in_copy, gstart_ref[nxt], rows_of(gsize_ref[nxt], 0), 1 - slot)

        rows = rows_of(n, t)
        wait_tile(in_copy, rows, slot)

        @pl.when(state_ref[2 + slot] > 0)
        def _():
            wait_tile(out_copy, state_ref[2 + slot], slot)

        def ffn(r0, m):
            gu = _dot(xbuf_ref[slot, r0:r0 + m, :], wgu_ref[...].astype(bf16)) + bgu_ref[...]
            gate = jnp.minimum(gu[:, :D_FF], SWIGLU_LIMIT)
            lin = jnp.clip(gu[:, D_FF:], -SWIGLU_LIMIT, SWIGLU_LIMIT)
            act = (lin + 1.0) * (gate * jax.nn.sigmoid(SWIGLU_ALPHA * gate))
            out = _dot(act.astype(bf16), wd_ref[...].astype(bf16)) + bd_ref[...]
            obuf_ref[slot, r0:r0 + m, :] = out.astype(bf16)

        half, quarter, eighth = tile // 2, tile // 4, tile // 8
        for lo, hi, chains in ((half + quarter, tile, ((0, half), (half, half))),
                               (half + eighth, half + quarter, ((0, half), (half, quarter))),
                               (half, half + eighth, ((0, half), (half, eighth))),
                               (quarter, half, ((0, half),)),
                               (eighth, quarter, ((0, quarter),)),
                               (0, eighth, ((0, eighth),))):
            @pl.when(jnp.logical_and(rows > lo, rows <= hi))
            def _():
                for r0, m in chains:
                    ffn(r0, m)

        start_tile(out_copy, base + t * tile, rows, slot)
        state_ref[2 + slot] = rows
        return carry

    lax.fori_loop(0, n_tiles, step, 0)

    @pl.when(n_tiles > 0)
    def _():
        state_ref[0] = done + n_tiles
        state_ref[1] = has_next.astype(jnp.int32)

    @pl.when(e == ne - 1)
    def _():
        for slot in range(2):
            @pl.when(state_ref[2 + slot] > 0)
            def _():
                wait_tile(out_copy, state_ref[2 + slot], slot)
        _zero_rows(zero_ref, os_ref, zsem, used_ref[0], start=False)


def _ffn(gstart, gsize, used, xs, wgu, bgu, wd, bd):
    rows_total = xs.shape[0]
    return pl.pallas_call(
        _ffn_body,
        grid_spec=pltpu.PrefetchScalarGridSpec(
            num_scalar_prefetch=3,
            grid=(N_EXPERTS,),
            in_specs=[pl.BlockSpec(memory_space=pl.ANY),
                      pl.BlockSpec((None, D_MODEL, 2 * D_FF), lambda e, *_: (e, 0, 0)),
                      pl.BlockSpec((None, 1, 2 * D_FF), lambda e, *_: (e, 0, 0)),
                      pl.BlockSpec((None, D_FF, D_MODEL), lambda e, *_: (e, 0, 0)),
                      pl.BlockSpec((None, 1, D_MODEL), lambda e, *_: (e, 0, 0))],
            out_specs=pl.BlockSpec(memory_space=pl.ANY),
            scratch_shapes=[pltpu.VMEM((2, FFN_TM, D_MODEL), bf16),
                            pltpu.VMEM((2, FFN_TM, D_MODEL), bf16),
                            pltpu.VMEM((FFN_TM, D_MODEL), bf16),
                            pltpu.SMEM((4,), jnp.int32),
                            pltpu.SemaphoreType.DMA((2,)), pltpu.SemaphoreType.DMA((2,)),
                            pltpu.SemaphoreType.DMA],
        ),
        out_shape=jax.ShapeDtypeStruct((rows_total, D_MODEL), bf16),
        compiler_params=pltpu.CompilerParams(
            dimension_semantics=("arbitrary",), vmem_limit_bytes=VMEM_LIMIT),
        name="experts",
    )(gstart, gsize, used, xs, wgu, bgu, wd, bd)


def _combine_body(final_norm, voff_ref, row0_ref, rlen_ref, rtot_ref,
                  os_ref, meta_ref, base_ref, meta_next_ref, base_next_ref, x1_ref, gfin_ref,
                  out_ref, oc0_ref, oc1_ref, p0_ref, p1_ref, sem):
    i = pl.program_id(0)
    nt = pl.num_programs(0)
    tm = x1_ref.shape[0]
    lmax = oc0_ref.shape[0]
    slot = i & 1

    def slots_and_gates(m_ref, b_ref):
        meta = m_ref[...]
        slots = [s.astype(jnp.int32) for s in _token_slots(meta, b_ref[0:1, :])]
        gates = [meta[:, _META_GATE + kk:_META_GATE + kk + 1] for kk in range(TOP_K)]
        return slots, gates

    def gate_cols(slots, gates, c):
        col = lax.broadcasted_iota(jnp.int32, (tm, _CHUNK), 1) + c * _CHUNK
        p = jnp.zeros((tm, _CHUNK), f32)
        for s, g in zip(slots, gates):
            p = jnp.where(col == s, g, p)
        return p.astype(bf16)

    def fetch(tile, into_ref, sem_slot, rows_of):
        for e in range(N_EXPERTS):
            j = tile * N_EXPERTS + e
            _segment_copy(os_ref, into_ref, sem.at[sem_slot], row0_ref[j], voff_ref[j],
                          rows_of(rlen_ref[j])).start()

    @pl.when(i == 0)
    def _():
        oc0_ref[...] = jnp.zeros_like(oc0_ref)
        oc1_ref[...] = jnp.zeros_like(oc1_ref)
        fetch(0, oc0_ref, 0, lambda rows: rows)
        slots, gates = slots_and_gates(meta_ref, base_ref)
        for c in range(lmax // _CHUNK):
            p0_ref[:, c * _CHUNK:(c + 1) * _CHUNK] = gate_cols(slots, gates, c)

    _wait_rows(os_ref, oc0_ref, sem.at[slot], rtot_ref[i])

    def request_next(oc_next_ref):
        fetch(jnp.minimum(i + 1, nt - 1), oc_next_ref, 1 - slot,
              lambda rows: jnp.where(i + 1 < nt, rows, 0))

    @pl.when(slot == 0)
    def _():
        request_next(oc1_ref)

    @pl.when(slot == 1)
    def _():
        request_next(oc0_ref)

    def restore(p_now_ref, p_next_ref, oc_now_ref):
        slots, gates = slots_and_gates(meta_next_ref, base_next_ref)
        p_now = p_now_ref[...]
        n_build = lmax // _CHUNK
        n_out = D_MODEL // _CHUNK
        ssq = jnp.zeros((tm, 1), f32)
        for n in range(n_out):
            cols = slice(n * _CHUNK, (n + 1) * _CHUNK)
            xo = x1_ref[:, cols] + _dot(p_now, oc_now_ref[:, cols])
            out_ref[:, cols] = xo
            ssq = ssq + jnp.sum(xo * xo, axis=-1, keepdims=True)
            for c in range(n * n_build // n_out, (n + 1) * n_build // n_out):
                p_next_ref[:, c * _CHUNK:(c + 1) * _CHUNK] = gate_cols(slots, gates, c)
        if final_norm:
            out_ref[...] = out_ref[...] * lax.rsqrt(ssq * (1.0 / D_MODEL) + EPS) * gfin_ref[...]

    @pl.when(slot == 0)
    def _():
        restore(p0_ref, p1_ref, oc0_ref)

    @pl.when(slot == 1)
    def _():
        restore(p1_ref, p0_ref, oc1_ref)


def _combine(plan, os, meta, x1, gfin, final_norm):
    tokens = x1.shape[0]
    tm = min(MIX_TS, tokens)
    nt = tokens // tm
    scalars = (plan["voff"], plan["row0"], plan["rlen"], plan["rtot"])
    return pl.pallas_call(
        functools.partial(_combine_body, final_norm),
        grid_spec=pltpu.PrefetchScalarGridSpec(
            num_scalar_prefetch=len(scalars),
            grid=(nt,),
            in_specs=[pl.BlockSpec(memory_space=pl.ANY),
                      pl.BlockSpec((tm, LANES), lambda i, *_: (i, 0)),
                      pl.BlockSpec((8, LANES), lambda i, *_: (i, 0)),
                      pl.BlockSpec((tm, LANES), lambda i, *_: (jnp.minimum(i + 1, nt - 1), 0)),
                      pl.BlockSpec((8, LANES), lambda i, *_: (jnp.minimum(i + 1, nt - 1), 0)),
                      pl.BlockSpec((tm, D_MODEL), lambda i, *_: (i, 0)),
                      pl.BlockSpec((1, D_MODEL), lambda i, *_: (0, 0))],
            out_specs=pl.BlockSpec((tm, D_MODEL), lambda i, *_: (i, 0)),
            scratch_shapes=[pltpu.VMEM((_lmax(tm), D_MODEL), bf16),
                            pltpu.VMEM((_lmax(tm), D_MODEL), bf16),
                            pltpu.VMEM((tm, _lmax(tm)), bf16),
                            pltpu.VMEM((tm, _lmax(tm)), bf16),
                            pltpu.SemaphoreType.DMA((2,))],
        ),
        out_shape=jax.ShapeDtypeStruct((tokens, D_MODEL), f32),
        compiler_params=pltpu.CompilerParams(
            dimension_semantics=("arbitrary",), vmem_limit_bytes=VMEM_LIMIT),
        name="combine",
    )(*scalars, os, meta, plan["base"], meta, plan["base"], x1, gfin)


def _plan(counts, nt):
    up = lambda v: (v + (BF16_ROWS - 1)) // BF16_ROWS * BF16_ROWS
    down = lambda v: v // BF16_ROWS * BF16_ROWS
    c = counts.reshape(nt, 8, LANES)[:, 0, :N_EXPERTS]
    gsize = up(jnp.sum(c, axis=0))
    gstart = jnp.cumsum(gsize) - gsize
    before = jnp.cumsum(c, axis=0) - c
    a = before % BF16_ROWS
    length = a + c
    voff = jnp.cumsum(up(length), axis=1) - up(length)
    last = jnp.arange(nt)[:, None] == nt - 1
    wlen = jnp.where(last, up(length), down(length))
    rlen = jnp.where(c > 0, up(length), 0)
    base = jnp.zeros((nt, 8, LANES), jnp.float32)
    base = base.at[:, :, :N_EXPERTS].set((voff + a).astype(jnp.float32)[:, None, :])
    i32 = lambda v: v.reshape(-1).astype(jnp.int32)
    return dict(
        voff=i32(voff), coff=i32(voff + down(length)), keep=i32(length % BF16_ROWS),
        row0=i32(gstart[None, :] + before - a), wlen=i32(wlen), wtot=i32(jnp.sum(wlen, axis=1)),
        rlen=i32(rlen), rtot=i32(jnp.sum(rlen, axis=1)), base=base.reshape(nt * 8, LANES),
        gstart=i32(gstart), gsize=i32(gsize), used=i32(jnp.sum(gsize, keepdims=True)))


def kernel(x, g_mix, w_in, w_alpha_up, b_alpha, g_gla, w_conv, w_branch_a, w_branch_b, w_out, g_ffn, w_router, b_router, w_gate_up, b_gate_up, w_down, b_down, g_final):
    batch, seq, _ = x.shape
    tokens = batch * seq
    depth = w_in.shape[0]
    x2 = x.reshape(tokens, D_MODEL)
    nt = tokens // min(MIX_TS, seq)
    rows_total = tokens * TOP_K + N_EXPERTS * (BF16_ROWS - 1)
    rows_total = -(-rows_total // BF16_ROWS) * BF16_ROWS

    for l in range(depth):
        wi = w_in[l]
        a0 = 2 * QK_W + 2 * V_W
        whead = wi[:, :a0].astype(bf16)
        wtail = wi[:, a0 + GLA_RANK:].astype(bf16)
        walr = jnp.pad(wi[:, a0:a0 + GLA_RANK], ((0, 0), (0, LANES - GLA_RANK))).astype(bf16)
        wup = jnp.pad(w_alpha_up[l], ((0, LANES - GLA_RANK), (0, 0))).astype(bf16)
        wr = jnp.pad(w_router[l], ((0, 0), (0, LANES - N_EXPERTS)))
        wrh = wr.astype(bf16)
        wrl = (wr - wrh.astype(f32)).astype(bf16)
        br = jnp.pad(b_router[l], (0, LANES - N_EXPERTS)).reshape(1, LANES)
        wconv = jnp.pad(w_conv[l], ((0, 8 - CONV_K), (0, 0)))

        x1, h2, logits = _mixer(
            x2.reshape(batch, seq, D_MODEL), g_mix[l].reshape(1, D_MODEL), whead, wtail, walr, wup,
            b_alpha[l].reshape(1, QK_W), g_gla[l].reshape(1, V_W), wconv,
            w_branch_a[l].astype(bf16), w_branch_b[l].astype(bf16), w_out[l].astype(bf16),
            g_ffn[l].reshape(1, D_MODEL), wrh, wrl, br)

        meta, counts = _route(logits, min(MIX_TS, seq))
        plan = _plan(counts, nt)
        xs = _dispatch(plan, h2, meta, rows_total)
        os = _ffn(plan["gstart"], plan["gsize"], plan["used"], xs,
                  w_gate_up[l], b_gate_up[l].reshape(N_EXPERTS, 1, 2 * D_FF),
                  w_down[l], b_down[l].reshape(N_EXPERTS, 1, D_MODEL))
        x2 = _combine(plan, os, meta, x1, g_final.reshape(1, D_MODEL), l == depth - 1)
    return x2.reshape(batch, seq, D_MODEL)
```

```python
import functools

import jax
import jax.numpy as jnp
from jax import lax
from jax.experimental import pallas as pl
from jax.experimental.pallas import tpu as pltpu

D_MODEL = 1024
GLA_HEADS = 4
GLA_DK = 128
GLA_DV = 256
GLA_RANK = 16
GLA_TAU = 16.0
GLA_CHUNK = 64
_CHUNK_SHIFT = GLA_CHUNK.bit_length() - 1
CONV_K = 3
N_EXPERTS = 32
TOP_K = 4
D_FF = 1024
SWIGLU_LIMIT = 7.0
SWIGLU_ALPHA = 1.702
EPS = 1e-6

QK_W = GLA_HEADS * GLA_DK
V_W = GLA_HEADS * GLA_DV

LANES = 128
BF16_ROWS = 16
VMEM_LIMIT = 56 * 1024 * 1024

MIX_TS = 256
MIX_ROWS = 2
ROUTE_ROWS = 1024
_CHUNK = 256
FFN_TM = 1024
_META_EXPERT, _META_GATE, _META_RANK = 0, TOP_K, 2 * TOP_K
_TILE_DMA_PRIORITY = 1

_C_Q, _C_K, _C_V, _C_R = 0, QK_W, 2 * QK_W, 2 * QK_W + V_W
_C_CB = _C_R + V_W
_C_CC = _C_CB + D_MODEL
_C_CX = _C_CC + D_MODEL
_C_GA = _C_CX + D_MODEL
_C_GB = _C_GA + D_MODEL
_W_MAIN = _C_GB + D_MODEL

f32 = jnp.float32
bf16 = jnp.bfloat16


def _rms(x, g):
    return x * lax.rsqrt(jnp.mean(x * x, axis=-1, keepdims=True) + EPS) * g


def _dot(a, b):
    return jnp.dot(a, b, preferred_element_type=f32)


def _split_bf16(a):
    hi = a.astype(bf16)
    lo = (a - hi.astype(f32)).astype(bf16)
    return hi, lo


def _mixer_body(x_ref, gmix_ref, wint_ref, wup_ref, balpha_ref, ggla_ref, wconv_ref,
                wa_ref, wb_ref, wo_ref, gffn_ref, wrh_ref, wrl_ref, br_ref,
                x1_ref, h2_ref, logit_ref,
                state_ref, ubuf_ref, obuf_ref):
    nb, ts = x_ref.shape[0], x_ref.shape[1]
    m = nb * ts
    nchunk = ts // GLA_CHUNK
    seqs = [slice(ch * ts, (ch + 1) * ts) for ch in range(nb)]

    @pl.when(pl.program_id(1) == 0)
    def _():
        state_ref[...] = jnp.zeros_like(state_ref)
        ubuf_ref[:, 0:8, :] = jnp.zeros((nb, 8, D_MODEL), f32)

    x = x_ref[...].reshape(m, D_MODEL)
    hb = _rms(x, gmix_ref[...]).astype(bf16)

    def dot_t(a, w_rows):
        return lax.dot_general(a, w_rows, (((1,), (1,)), ((), ())), preferred_element_type=f32)

    def proj(c0, width):
        r0 = c0 if c0 < _C_CB else c0 + GLA_RANK
        return dot_t(hb, wint_ref[r0:r0 + width, :])

    a_lr = dot_t(hb, wint_ref[_C_CB:_C_CB + LANES, :])
    pre = _dot(a_lr.astype(bf16), wup_ref[...]) + balpha_ref[...]
    log_a = -(jnp.maximum(-pre, 0.0) + jnp.log1p(jnp.exp(-jnp.abs(pre)))) * (1.0 / GLA_TAU)
    ri = lax.broadcasted_iota(jnp.int32, (ts, ts), 0)
    ci = lax.broadcasted_iota(jnp.int32, (ts, ts), 1)
    same_chunk = (ri >> _CHUNK_SHIFT) == (ci >> _CHUNK_SHIFT)
    causal = jnp.logical_and(same_chunk, ci <= ri)
    tri = jnp.where(causal, 1.0, 0.0).astype(bf16)
    la_hi, la_lo = _split_bf16(log_a)
    b = jnp.concatenate([_dot(tri, la_hi[sq]) + _dot(tri, la_lo[sq]) for sq in seqs], axis=0)
    lasts = [b[(c + 1) * GLA_CHUNK - 1:(c + 1) * GLA_CHUNK, :] for c in range(nb * nchunk)]
    b_tot = jnp.concatenate([jnp.broadcast_to(r, (GLA_CHUNK, QK_W)) for r in lasts], axis=0)
    e_pos = jnp.exp(b)
    e_neg = jnp.exp(-b)
    e_tail = jnp.exp(b_tot - b)
    e_tot = [jnp.exp(r) for r in lasts]

    q = proj(_C_Q, QK_W)
    k = proj(_C_K, QK_W)
    q_dec = (q * (GLA_DK ** -0.5) * e_pos).astype(bf16)
    k_inv = (k * e_neg).astype(bf16)
    k_tail = (k * e_tail).astype(bf16)
    vb = proj(_C_V, V_W).astype(bf16)

    for ch, sq in enumerate(seqs):
        for hh in range(GLA_HEADS):
            qs = slice(hh * GLA_DK, (hh + 1) * GLA_DK)
            vs = slice(hh * GLA_DV, (hh + 1) * GLA_DV)
            sc = lax.dot_general(q_dec[sq, qs], k_inv[sq, qs], (((1,), (1,)), ((), ())),
                                 preferred_element_type=f32)
            sc = jnp.where(causal, sc, 0.0).astype(bf16)
            obuf_ref[sq, vs] = _dot(sc, vb[sq, vs])
            st = state_ref[ch, hh]
            for c in range(nchunk):
                rs = slice(ch * ts + c * GLA_CHUNK, ch * ts + (c + 1) * GLA_CHUNK)
                o_inter = lax.dot_general(q_dec[rs, qs], st.astype(bf16), (((1,), (1,)), ((), ())),
                                          preferred_element_type=f32)
                obuf_ref[rs, vs] += o_inter
                upd = lax.dot_general(vb[rs, vs], k_tail[rs, qs], (((0,), (0,)), ((), ())),
                                      preferred_element_type=f32)
                st = st * e_tot[ch * nchunk + c][:, qs] + upd
            state_ref[ch, hh] = st

    r = proj(_C_R, V_W)
    ggla = ggla_ref[...]
    o_parts = []
    for hh in range(GLA_HEADS):
        vs = slice(hh * GLA_DV, (hh + 1) * GLA_DV)
        o_h = _rms(obuf_ref[:, vs], ggla[:, vs])
        r_h = r[:, vs]
        o_parts.append((o_h * (r_h * jax.nn.sigmoid(r_h))).astype(bf16))
    y_a = _dot(jnp.concatenate(o_parts, axis=1), wa_ref[...])

    u = proj(_C_CC, D_MODEL) * proj(_C_CX, D_MODEL)
    wc = wconv_ref[...]
    y_convs = []
    for ch, sq in enumerate(seqs):
        ubuf_ref[ch, 8:8 + ts, :] = u[sq]
        y_convs.append(wc[0:1, :] * ubuf_ref[ch, 6:6 + ts, :] + wc[1:2, :] * ubuf_ref[ch, 7:7 + ts, :]
                       + wc[2:3, :] * u[sq])
        ubuf_ref[ch, 0:8, :] = ubuf_ref[ch, ts:ts + 8, :]
    y_conv = jnp.concatenate(y_convs, axis=0)
    y_b = _dot((proj(_C_CB, D_MODEL) * y_conv).astype(bf16), wb_ref[...])

    mixed = (jax.nn.sigmoid(proj(_C_GA, D_MODEL)) * y_a
             + jax.nn.sigmoid(proj(_C_GB, D_MODEL)) * y_b)
    x1 = x + _dot(mixed.astype(bf16), wo_ref[...])
    x1_ref[...] = x1.reshape(nb, ts, D_MODEL)

    h2 = _rms(x1, gffn_ref[...])
    h2_hi, h2_lo = _split_bf16(h2)
    h2_ref[...] = h2_hi.reshape(nb, ts, D_MODEL)
    wrh = wrh_ref[...]
    logits = _dot(h2_hi, wrh) + _dot(h2_lo, wrh) + _dot(h2_hi, wrl_ref[...]) + br_ref[...]
    logit_ref[...] = logits.reshape(nb, ts, LANES)


def _mixer(x3, gmix, wint, wup, balpha, ggla, wconv, wa, wb, wo, gffn, wrh, wrl, br):
    batch, seq, _ = x3.shape
    ts = min(MIX_TS, seq)
    ns = seq // ts
    tokens = batch * seq
    nb = MIX_ROWS if batch % MIX_ROWS == 0 else 1

    def const(shape):
        return pl.BlockSpec(shape, lambda b, s: (0,) * len(shape), pipeline_mode=pl.Buffered(1))

    def row(width):
        return pl.BlockSpec((nb, ts, width), lambda b, s: (b, s, 0))

    x1, h2, logits = pl.pallas_call(
        _mixer_body,
        grid=(batch // nb, ns),
        in_specs=[row(D_MODEL), const(gmix.shape), const(wint.shape),
                  const(wup.shape), const(balpha.shape), const(ggla.shape), const(wconv.shape),
                  const(wa.shape), const(wb.shape), const(wo.shape), const(gffn.shape),
                  const(wrh.shape), const(wrl.shape), const(br.shape)],
        out_specs=[row(D_MODEL), row(D_MODEL), row(LANES)],
        out_shape=[jax.ShapeDtypeStruct((batch, seq, D_MODEL), f32),
                   jax.ShapeDtypeStruct((batch, seq, D_MODEL), bf16),
                   jax.ShapeDtypeStruct((batch, seq, LANES), f32)],
        scratch_shapes=[pltpu.VMEM((nb, GLA_HEADS, GLA_DV, GLA_DK), f32),
                        pltpu.VMEM((nb, ts + 8, D_MODEL), f32),
                        pltpu.VMEM((nb * ts, V_W), f32)],
        compiler_params=pltpu.CompilerParams(
            dimension_semantics=("parallel", "arbitrary"), vmem_limit_bytes=VMEM_LIMIT),
        name="mixer",
    )(x3, gmix, wint, wup, balpha, ggla, wconv, wa, wb, wo, gffn, wrh, wrl, br)
    return (x1.reshape(tokens, D_MODEL), h2.reshape(tokens, D_MODEL),
            logits.reshape(tokens, LANES))


def _route_body(tile, logit_ref, meta_ref, cnt_ref):
    rows = logit_ref.shape[0]
    lane = lax.broadcasted_iota(jnp.int32, (rows, LANES), 1)
    lane_f = lane.astype(f32)
    lg = jnp.where(lane < N_EXPERTS, logit_ref[...], -jnp.inf)
    sels, tops, firsts = [], [], []
    for _ in range(TOP_K):
        m = jnp.max(lg, axis=-1, keepdims=True)
        first = jnp.min(jnp.where(lg == m, lane_f, float(LANES)), axis=-1, keepdims=True)
        sel = lane_f == first
        sels.append(sel)
        tops.append(m)
        firsts.append(first)
        lg = jnp.where(sel, -jnp.inf, lg)
    ps = [jnp.exp(t - tops[0]) for t in tops]
    denom = ps[0] + ps[1] + ps[2] + ps[3]
    onehot = jnp.zeros((rows, LANES), f32)
    for sel in sels:
        onehot = onehot + jnp.where(sel, 1.0, 0.0)
    onehot_b = onehot.astype(bf16)
    ri = lax.broadcasted_iota(jnp.int32, (tile, tile), 0)
    ci = lax.broadcasted_iota(jnp.int32, (tile, tile), 1)
    strict_lower = jnp.where(ci < ri, 1.0, 0.0).astype(bf16)
    ones = jnp.ones((8, tile), bf16)
    ranks = []
    for j in range(rows // tile):
        part = onehot_b[j * tile:(j + 1) * tile, :]
        ranks.append(_dot(strict_lower, part))
        cnt_ref[j * 8:(j + 1) * 8, :] = _dot(ones, part).astype(jnp.int32)
    rank = jnp.concatenate(ranks, axis=0)
    meta = jnp.zeros((rows, LANES), f32)
    for kk in range(TOP_K):
        meta = jnp.where(lane == _META_EXPERT + kk, firsts[kk], meta)
        meta = jnp.where(lane == _META_GATE + kk, ps[kk] / denom, meta)
        rank_k = jnp.sum(jnp.where(sels[kk], rank, 0.0), axis=-1, keepdims=True)
        meta = jnp.where(lane == _META_RANK + kk, rank_k, meta)
    meta_ref[...] = meta


def _route(logits, tile):
    tokens = logits.shape[0]
    rows = min(ROUTE_ROWS, tokens)
    steps = tokens // rows
    per_step = rows // tile
    return pl.pallas_call(
        functools.partial(_route_body, tile),
        grid=(steps,),
        in_specs=[pl.BlockSpec((rows, LANES), lambda i: (i, 0))],
        out_specs=[pl.BlockSpec((rows, LANES), lambda i: (i, 0)),
                   pl.BlockSpec((per_step * 8, LANES), lambda i: (i, 0))],
        out_shape=[jax.ShapeDtypeStruct((tokens, LANES), f32),
                   jax.ShapeDtypeStruct((tokens // tile * 8, LANES), jnp.int32)],
        compiler_params=pltpu.CompilerParams(
            dimension_semantics=("parallel",), vmem_limit_bytes=VMEM_LIMIT),
        name="route",
    )(logits)


def _lmax(tm):
    full = tm * TOP_K + N_EXPERTS * 2 * (BF16_ROWS - 1)
    return -(-full // 256) * 256


def _token_slots(meta, base_row):
    lane_f = lax.broadcasted_iota(jnp.int32, meta.shape, 1).astype(f32)
    slots = []
    for kk in range(TOP_K):
        sel = lane_f == meta[:, _META_EXPERT + kk:_META_EXPERT + kk + 1]
        first = jnp.sum(jnp.where(sel, base_row, 0.0), axis=-1, keepdims=True)
        slots.append(first + meta[:, _META_RANK + kk:_META_RANK + kk + 1])
    return slots


def _segment_copy(src, dst, sem, src_off, dst_off, rows):
    src_off = pl.multiple_of(src_off, BF16_ROWS)
    dst_off = pl.multiple_of(dst_off, BF16_ROWS)
    rows = pl.multiple_of(rows, BF16_ROWS)
    return pltpu.make_async_copy(src.at[pl.ds(src_off, rows)], dst.at[pl.ds(dst_off, rows)], sem)


def _wait_rows(src, dst, sem, total):
    @pl.when(total > 0)
    def _():
        _segment_copy(src, dst, sem, 0, 0, total).wait()


def _zero_rows(zero_ref, dst_ref, sem, first, start):
    tile = zero_ref.shape[0]
    n = dst_ref.shape[0] - first
    n_full = n // tile

    def full(j):
        return _segment_copy(zero_ref, dst_ref, sem, 0, first + j * tile, tile)

    def rest():
        return _segment_copy(zero_ref, dst_ref, sem, 0, first + n_full * tile, n - n_full * tile)

    def step(j, carry):
        full(j).start() if start else full(j).wait()
        return carry

    lax.fori_loop(0, n_full, step, 0)

    @pl.when(n - n_full * tile > 0)
    def _():
        rest().start() if start else rest().wait()


def _dispatch_body(voff_ref, coff_ref, keep_ref, row0_ref, wlen_ref, wtot_ref, used_ref,
                   h2_ref, meta_ref, base_ref, meta_next_ref, base_next_ref, xs_ref,
                   xc0_ref, xc1_ref, pt0_ref, pt1_ref, carry_ref, zero_ref, sem, zsem):
    i = pl.program_id(0)
    nt = pl.num_programs(0)
    tm = h2_ref.shape[0]
    lmax = xc0_ref.shape[0]
    slot = i & 1

    def slot_rows(m_ref, b_ref):
        meta = m_ref[...]
        lane = lax.broadcasted_iota(jnp.int32, meta.shape, 1)
        slots = jnp.zeros(meta.shape, f32)
        for kk, s in enumerate(_token_slots(meta, b_ref[0:1, :])):
            slots = jnp.where(lane == kk, s, slots)
        return slots.T[0:8, :].astype(jnp.int32)

    def one_hot_rows(slots_t, c):
        row = lax.broadcasted_iota(jnp.int32, (_CHUNK, tm), 0) + c * _CHUNK
        pt = jnp.zeros((_CHUNK, tm), f32)
        for kk in range(TOP_K):
            pt = jnp.where(row == slots_t[kk:kk + 1, :], 1.0, pt)
        return pt.astype(bf16)

    @pl.when(i == 0)
    def _():
        carry_ref[...] = jnp.zeros_like(carry_ref)
        zero_ref[...] = jnp.zeros_like(zero_ref)
        _zero_rows(zero_ref, xs_ref, zsem, used_ref[0], start=True)
        first = slot_rows(meta_ref, base_ref)
        for c in range(lmax // _CHUNK):
            pt0_ref[c * _CHUNK:(c + 1) * _CHUNK, :] = one_hot_rows(first, c)

    @pl.when(i >= 2)
    def _():
        _wait_rows(xc0_ref, xs_ref, sem.at[slot], wtot_ref[jnp.maximum(i - 2, 0)])

    def merge_carry(t, buf):
        def block(off):
            return buf.at[pl.ds(pl.multiple_of(off, BF16_ROWS), BF16_ROWS)]

        for e in range(N_EXPERTS):
            blk = block(voff_ref[t * N_EXPERTS + e])
            blk[...] = blk[...] + carry_ref[e]
        for e in range(N_EXPERTS):
            j = t * N_EXPERTS + e
            tail = block(coff_ref[j])[...]
            carry_ref[e] = jnp.where(keep_ref[j] > 0, tail, jnp.zeros_like(tail))

    def copy_out(t, buf):
        for e in range(N_EXPERTS):
            j = t * N_EXPERTS + e
            _segment_copy(buf, xs_ref, sem.at[slot], voff_ref[j], row0_ref[j], wlen_ref[j]).start()

    def compact(pt_now_ref, pt_next_ref, xc_now_ref):
        h2 = h2_ref[...]
        nxt = slot_rows(meta_next_ref, base_next_ref)
        for c in range(lmax // _CHUNK):
            rows = slice(c * _CHUNK, (c + 1) * _CHUNK)
            xc_now_ref[rows, :] = _dot(pt_now_ref[rows, :], h2).astype(bf16)
            pt_next_ref[rows, :] = one_hot_rows(nxt, c)
        merge_carry(i, xc_now_ref)
        copy_out(i, xc_now_ref)

    @pl.when(slot == 0)
    def _():
        compact(pt0_ref, pt1_ref, xc0_ref)

    @pl.when(slot == 1)
    def _():
        compact(pt1_ref, pt0_ref, xc1_ref)

    @pl.when(i == nt - 1)
    def _():
        @pl.when(i >= 1)
        def _():
            _wait_rows(xc0_ref, xs_ref, sem.at[1 - slot], wtot_ref[jnp.maximum(i - 1, 0)])
        _wait_rows(xc0_ref, xs_ref, sem.at[slot], wtot_ref[i])
        _zero_rows(zero_ref, xs_ref, zsem, used_ref[0], start=False)


def _dispatch(plan, h2, meta, rows_total):
    tokens = h2.shape[0]
    tm = min(MIX_TS, tokens)
    nt = tokens // tm
    scalars = (plan["voff"], plan["coff"], plan["keep"], plan["row0"], plan["wlen"], plan["wtot"],
               plan["used"])
    return pl.pallas_call(
        _dispatch_body,
        grid_spec=pltpu.PrefetchScalarGridSpec(
            num_scalar_prefetch=len(scalars),
            grid=(nt,),
            in_specs=[pl.BlockSpec((tm, D_MODEL), lambda i, *_: (i, 0)),
                      pl.BlockSpec((tm, LANES), lambda i, *_: (i, 0)),
                      pl.BlockSpec((8, LANES), lambda i, *_: (i, 0)),
                      pl.BlockSpec((tm, LANES), lambda i, *_: (jnp.minimum(i + 1, nt - 1), 0)),
                      pl.BlockSpec((8, LANES), lambda i, *_: (jnp.minimum(i + 1, nt - 1), 0))],
            out_specs=pl.BlockSpec(memory_space=pl.ANY),
            scratch_shapes=[pltpu.VMEM((_lmax(tm), D_MODEL), bf16),
                            pltpu.VMEM((_lmax(tm), D_MODEL), bf16),
                            pltpu.VMEM((_lmax(tm), tm), bf16),
                            pltpu.VMEM((_lmax(tm), tm), bf16),
                            pltpu.VMEM((N_EXPERTS, BF16_ROWS, D_MODEL), bf16),
                            pltpu.VMEM((FFN_TM, D_MODEL), bf16),
                            pltpu.SemaphoreType.DMA((2,)), pltpu.SemaphoreType.DMA],
        ),
        out_shape=jax.ShapeDtypeStruct((rows_total, D_MODEL), bf16),
        compiler_params=pltpu.CompilerParams(
            dimension_semantics=("arbitrary",), vmem_limit_bytes=VMEM_LIMIT),
        name="dispatch",
    )(*scalars, h2, meta, plan["base"], meta, plan["base"])


def _ffn_body(gstart_ref, gsize_ref, used_ref, xs_ref, wgu_ref, bgu_ref, wd_ref, bd_ref, os_ref,
              xbuf_ref, obuf_ref, zero_ref, state_ref, isem, osem, zsem):
    e = pl.program_id(0)
    ne = pl.num_programs(0)
    tile = xbuf_ref.shape[1]
    base = gstart_ref[e]
    n = gsize_ref[e]
    n_tiles = (n + tile - 1) // tile

    def rows_of(size, t):
        return jnp.minimum(tile, size - t * tile)

    def in_copy(row0, rows, slot):
        return pltpu.make_async_copy(xs_ref.at[pl.ds(row0, rows)],
                                     xbuf_ref.at[slot, pl.ds(0, rows)], isem.at[slot])

    def out_copy(row0, rows, slot):
        return pltpu.make_async_copy(obuf_ref.at[slot, pl.ds(0, rows)],
                                     os_ref.at[pl.ds(row0, rows)], osem.at[slot])

    def start_tile(copy, row0, rows, slot):
        row0 = pl.multiple_of(row0, BF16_ROWS)

        @pl.when(rows == tile)
        def _():
            copy(row0, tile, slot).start(priority=_TILE_DMA_PRIORITY)

        @pl.when(rows < tile)
        def _():
            copy(row0, pl.multiple_of(rows, BF16_ROWS), slot).start()

    def wait_tile(copy, rows, slot):
        copy(0, pl.multiple_of(rows, BF16_ROWS), slot).wait()

    @pl.when(e == 0)
    def _():
        xbuf_ref[...] = jnp.zeros_like(xbuf_ref)
        zero_ref[...] = jnp.zeros_like(zero_ref)
        _zero_rows(zero_ref, os_ref, zsem, used_ref[0], start=True)
        state_ref[0] = 0
        state_ref[1] = 0
        state_ref[2] = 0
        state_ref[3] = 0

    done = state_ref[0]

    @pl.when(jnp.logical_and(n_tiles > 0, state_ref[1] == 0))
    def _():
        start_tile(in_copy, base, rows_of(n, 0), done & 1)

    nxt = jnp.minimum(e + 1, ne - 1)
    has_next = jnp.logical_and(e + 1 < ne, gsize_ref[nxt] > 0)

    def step(t, carry):
        slot = (done + t) & 1

        @pl.when(t + 1 < n_tiles)
        def _():
            start_tile(in_copy, base + (t + 1) * tile, rows_of(n, t + 1), 1 - slot)

        @pl.when(jnp.logical_and(t + 1 == n_tiles, has_next))
        def _():
            start_tile(in_copy, gstart_ref[nxt], rows_of(gsize_ref[nxt], 0), 1 - slot)

        rows = rows_of(n, t)
        wait_tile(in_copy, rows, slot)

        @pl.when(state_ref[2 + slot] > 0)
        def _():
            wait_tile(out_copy, state_ref[2 + slot], slot)

        def ffn(r0, m):
            gu = _dot(xbuf_ref[slot, r0:r0 + m, :], wgu_ref[...].astype(bf16)) + bgu_ref[...]
            gate = jnp.minimum(gu[:, :D_FF], SWIGLU_LIMIT)
            lin = jnp.clip(gu[:, D_FF:], -SWIGLU_LIMIT, SWIGLU_LIMIT)
            act = (lin + 1.0) * (gate * jax.nn.sigmoid(SWIGLU_ALPHA * gate))
            out = _dot(act.astype(bf16), wd_ref[...].astype(bf16)) + bd_ref[...]
            obuf_ref[slot, r0:r0 + m, :] = out.astype(bf16)

        half, quarter, eighth = tile // 2, tile // 4, tile // 8
        for lo, hi, chains in ((half + quarter, tile, ((0, half), (half, half))),
                               (half + eighth, half + quarter, ((0, half), (half, quarter))),
                               (half, half + eighth, ((0, half), (half, eighth))),
                               (quarter, half, ((0, half),)),
                               (eighth, quarter, ((0, quarter),)),
                               (0, eighth, ((0, eighth),))):
            @pl.when(jnp.logical_and(rows > lo, rows <= hi))
            def _():
                for r0, m in chains:
                    ffn(r0, m)

        start_tile(out_copy, base + t * tile, rows, slot)
        state_ref[2 + slot] = rows
        return carry

    lax.fori_loop(0, n_tiles, step, 0)

    @pl.when(n_tiles > 0)
    def _():
        state_ref[0] = done + n_tiles
        state_ref[1] = has_next.astype(jnp.int32)

    @pl.when(e == ne - 1)
    def _():
        for slot in range(2):
            @pl.when(state_ref[2 + slot] > 0)
            def _():
                wait_tile(out_copy, state_ref[2 + slot], slot)
        _zero_rows(zero_ref, os_ref, zsem, used_ref[0], start=False)


def _ffn(gstart, gsize, used, xs, wgu, bgu, wd, bd):
    rows_total = xs.shape[0]
    return pl.pallas_call(
        _ffn_body,
        grid_spec=pltpu.PrefetchScalarGridSpec(
            num_scalar_prefetch=3,
            grid=(N_EXPERTS,),
            in_specs=[pl.BlockSpec(memory_space=pl.ANY),
                      pl.BlockSpec((None, D_MODEL, 2 * D_FF), lambda e, *_: (e, 0, 0)),
                      pl.BlockSpec((None, 1, 2 * D_FF), lambda e, *_: (e, 0, 0)),
                      pl.BlockSpec((None, D_FF, D_MODEL), lambda e, *_: (e, 0, 0)),
                      pl.BlockSpec((None, 1, D_MODEL), lambda e, *_: (e, 0, 0))],
            out_specs=pl.BlockSpec(memory_space=pl.ANY),
            scratch_shapes=[pltpu.VMEM((2, FFN_TM, D_MODEL), bf16),
                            pltpu.VMEM((2, FFN_TM, D_MODEL), bf16),
                            pltpu.VMEM((FFN_TM, D_MODEL), bf16),
                            pltpu.SMEM((4,), jnp.int32),
                            pltpu.SemaphoreType.DMA((2,)), pltpu.SemaphoreType.DMA((2,)),
                            pltpu.SemaphoreType.DMA],
        ),
        out_shape=jax.ShapeDtypeStruct((rows_total, D_MODEL), bf16),
        compiler_params=pltpu.CompilerParams(
            dimension_semantics=("arbitrary",), vmem_limit_bytes=VMEM_LIMIT),
        name="experts",
    )(gstart, gsize, used, xs, wgu, bgu, wd, bd)


def _combine_body(final_norm, voff_ref, row0_ref, rlen_ref, rtot_ref,
                  os_ref, meta_ref, base_ref, meta_next_ref, base_next_ref, x1_ref, gfin_ref,
                  out_ref, oc0_ref, oc1_ref, p0_ref, p1_ref, sem):
    i = pl.program_id(0)
    nt = pl.num_programs(0)
    tm = x1_ref.shape[0]
    lmax = oc0_ref.shape[0]
    slot = i & 1

    def slots_and_gates(m_ref, b_ref):
        meta = m_ref[...]
        slots = [s.astype(jnp.int32) for s in _token_slots(meta, b_ref[0:1, :])]
        gates = [meta[:, _META_GATE + kk:_META_GATE + kk + 1] for kk in range(TOP_K)]
        return slots, gates

    def gate_cols(slots, gates, c):
        col = lax.broadcasted_iota(jnp.int32, (tm, _CHUNK), 1) + c * _CHUNK
        p = jnp.zeros((tm, _CHUNK), f32)
        for s, g in zip(slots, gates):
            p = jnp.where(col == s, g, p)
        return p.astype(bf16)

    def fetch(tile, into_ref, sem_slot, rows_of):
        for e in range(N_EXPERTS):
            j = tile * N_EXPERTS + e
            _segment_copy(os_ref, into_ref, sem.at[sem_slot], row0_ref[j], voff_ref[j],
                          rows_of(rlen_ref[j])).start()

    @pl.when(i == 0)
    def _():
        oc0_ref[...] = jnp.zeros_like(oc0_ref)
        oc1_ref[...] = jnp.zeros_like(oc1_ref)
        fetch(0, oc0_ref, 0, lambda rows: rows)
        slots, gates = slots_and_gates(meta_ref, base_ref)
        for c in range(lmax // _CHUNK):
            p0_ref[:, c * _CHUNK:(c + 1) * _CHUNK] = gate_cols(slots, gates, c)

    _wait_rows(os_ref, oc0_ref, sem.at[slot], rtot_ref[i])

    def request_next(oc_next_ref):
        fetch(jnp.minimum(i + 1, nt - 1), oc_next_ref, 1 - slot,
              lambda rows: jnp.where(i + 1 < nt, rows, 0))

    @pl.when(slot == 0)
    def _():
        request_next(oc1_ref)

    @pl.when(slot == 1)
    def _():
        request_next(oc0_ref)

    def restore(p_now_ref, p_next_ref, oc_now_ref):
        slots, gates = slots_and_gates(meta_next_ref, base_next_ref)
        p_now = p_now_ref[...]
        n_build = lmax // _CHUNK
        n_out = D_MODEL // _CHUNK
        ssq = jnp.zeros((tm, 1), f32)
        for n in range(n_out):
            cols = slice(n * _CHUNK, (n + 1) * _CHUNK)
            xo = x1_ref[:, cols] + _dot(p_now, oc_now_ref[:, cols])
            out_ref[:, cols] = xo
            ssq = ssq + jnp.sum(xo * xo, axis=-1, keepdims=True)
            for c in range(n * n_build // n_out, (n + 1) * n_build // n_out):
                p_next_ref[:, c * _CHUNK:(c + 1) * _CHUNK] = gate_cols(slots, gates, c)
        if final_norm:
            out_ref[...] = out_ref[...] * lax.rsqrt(ssq * (1.0 / D_MODEL) + EPS) * gfin_ref[...]

    @pl.when(slot == 0)
    def _():
        restore(p0_ref, p1_ref, oc0_ref)

    @pl.when(slot == 1)
    def _():
        restore(p1_ref, p0_ref, oc1_ref)


def _combine(plan, os, meta, x1, gfin, final_norm):
    tokens = x1.shape[0]
    tm = min(MIX_TS, tokens)
    nt = tokens // tm
    scalars = (plan["voff"], plan["row0"], plan["rlen"], plan["rtot"])
    return pl.pallas_call(
        functools.partial(_combine_body, final_norm),
        grid_spec=pltpu.PrefetchScalarGridSpec(
            num_scalar_prefetch=len(scalars),
            grid=(nt,),
            in_specs=[pl.BlockSpec(memory_space=pl.ANY),
                      pl.BlockSpec((tm, LANES), lambda i, *_: (i, 0)),
                      pl.BlockSpec((8, LANES), lambda i, *_: (i, 0)),
                      pl.BlockSpec((tm, LANES), lambda i, *_: (jnp.minimum(i + 1, nt - 1), 0)),
                      pl.BlockSpec((8, LANES), lambda i, *_: (jnp.minimum(i + 1, nt - 1), 0)),
                      pl.BlockSpec((tm, D_MODEL), lambda i, *_: (i, 0)),
                      pl.BlockSpec((1, D_MODEL), lambda i, *_: (0, 0))],
            out_specs=pl.BlockSpec((tm, D_MODEL), lambda i, *_: (i, 0)),
            scratch_shapes=[pltpu.VMEM((_lmax(tm), D_MODEL), bf16),
                            pltpu.VMEM((_lmax(tm), D_MODEL), bf16),
                            pltpu.VMEM((tm, _lmax(tm)), bf16),
                            pltpu.VMEM((tm, _lmax(tm)), bf16),
                            pltpu.SemaphoreType.DMA((2,))],
        ),
        out_shape=jax.ShapeDtypeStruct((tokens, D_MODEL), f32),
        compiler_params=pltpu.CompilerParams(
            dimension_semantics=("arbitrary",), vmem_limit_bytes=VMEM_LIMIT),
        name="combine",
    )(*scalars, os, meta, plan["base"], meta, plan["base"], x1, gfin)


def _plan(counts, nt):
    up = lambda v: (v + (BF16_ROWS - 1)) // BF16_ROWS * BF16_ROWS
    down = lambda v: v // BF16_ROWS * BF16_ROWS
    c = counts.reshape(nt, 8, LANES)[:, 0, :N_EXPERTS]
    gsize = up(jnp.sum(c, axis=0))
    gstart = jnp.cumsum(gsize) - gsize
    before = jnp.cumsum(c, axis=0) - c
    a = before % BF16_ROWS
    length = a + c
    voff = jnp.cumsum(up(length), axis=1) - up(length)
    last = jnp.arange(nt)[:, None] == nt - 1
    wlen = jnp.where(last, up(length), down(length))
    rlen = jnp.where(c > 0, up(length), 0)
    base = jnp.zeros((nt, 8, LANES), jnp.float32)
    base = base.at[:, :, :N_EXPERTS].set((voff + a).astype(jnp.float32)[:, None, :])
    i32 = lambda v: v.reshape(-1).astype(jnp.int32)
    return dict(
        voff=i32(voff), coff=i32(voff + down(length)), keep=i32(length % BF16_ROWS),
        row0=i32(gstart[None, :] + before - a), wlen=i32(wlen), wtot=i32(jnp.sum(wlen, axis=1)),
        rlen=i32(rlen), rtot=i32(jnp.sum(rlen, axis=1)), base=base.reshape(nt * 8, LANES),
        gstart=i32(gstart), gsize=i32(gsize), used=i32(jnp.sum(gsize, keepdims=True)))


def kernel(x, g_mix, w_in, w_alpha_up, b_alpha, g_gla, w_conv, w_branch_a, w_branch_b, w_out, g_ffn, w_router, b_router, w_gate_up, b_gate_up, w_down, b_down, g_final):
    batch, seq, _ = x.shape
    tokens = batch * seq
    depth = w_in.shape[0]
    x2 = x.reshape(tokens, D_MODEL)
    nt = tokens // min(MIX_TS, seq)
    rows_total = tokens * TOP_K + N_EXPERTS * (BF16_ROWS - 1)
    rows_total = -(-rows_total // BF16_ROWS) * BF16_ROWS

    for l in range(depth):
        wint = jnp.swapaxes(w_in[l], 0, 1).astype(bf16)
        wup = jnp.pad(w_alpha_up[l], ((0, LANES - GLA_RANK), (0, 0))).astype(bf16)
        wr = jnp.pad(w_router[l], ((0, 0), (0, LANES - N_EXPERTS)))
        wrh = wr.astype(bf16)
        wrl = (wr - wrh.astype(f32)).astype(bf16)
        br = jnp.pad(b_router[l], (0, LANES - N_EXPERTS)).reshape(1, LANES)
        wconv = jnp.pad(w_conv[l], ((0, 8 - CONV_K), (0, 0)))

        x1, h2, logits = _mixer(
            x2.reshape(batch, seq, D_MODEL), g_mix[l].reshape(1, D_MODEL), wint, wup,
            b_alpha[l].reshape(1, QK_W), g_gla[l].reshape(1, V_W), wconv,
            w_branch_a[l].astype(bf16), w_branch_b[l].astype(bf16), w_out[l].astype(bf16),
            g_ffn[l].reshape(1, D_MODEL), wrh, wrl, br)

        meta, counts = _route(logits, min(MIX_TS, seq))
        plan = _plan(counts, nt)
        xs = _dispatch(plan, h2, meta, rows_total)
        os = _ffn(plan["gstart"], plan["gsize"], plan["used"], xs,
                  w_gate_up[l], b_gate_up[l].reshape(N_EXPERTS, 1, 2 * D_FF),
                  w_down[l], b_down[l].reshape(N_EXPERTS, 1, D_MODEL))
        x2 = _combine(plan, os, meta, x1, g_final.reshape(1, D_MODEL), l == depth - 1)
    return x2.reshape(batch, seq, D_MODEL)
```

```python
import functools

import jax
import jax.numpy as jnp
from jax import lax
from jax.experimental import pallas as pl
from jax.experimental.pallas import tpu as pltpu

D_MODEL = 1024
GLA_HEADS = 4
GLA_DK = 128
GLA_DV = 256
GLA_RANK = 16
GLA_TAU = 16.0
GLA_CHUNK = 64
_CHUNK_SHIFT = GLA_CHUNK.bit_length() - 1
CONV_K = 3
N_EXPERTS = 32
TOP_K = 4
D_FF = 1024
SWIGLU_LIMIT = 7.0
SWIGLU_ALPHA = 1.702
EPS = 1e-6

QK_W = GLA_HEADS * GLA_DK
V_W = GLA_HEADS * GLA_DV

LANES = 128
BF16_ROWS = 16
VMEM_LIMIT = 56 * 1024 * 1024

MIX_TS = 256
MIX_ROWS = 2
ROUTE_ROWS = 1024
_CHUNK = 256
FFN_TM = 1024
_META_EXPERT, _META_GATE, _META_RANK = 0, TOP_K, 2 * TOP_K
_TILE_DMA_PRIORITY = 1

_C_Q, _C_K, _C_V, _C_R = 0, QK_W, 2 * QK_W, 2 * QK_W + V_W
_C_CB = _C_R + V_W
_C_CC = _C_CB + D_MODEL
_C_CX = _C_CC + D_MODEL
_C_GA = _C_CX + D_MODEL
_C_GB = _C_GA + D_MODEL
_W_MAIN = _C_GB + D_MODEL

f32 = jnp.float32
bf16 = jnp.bfloat16


def _rms(x, g):
    return x * lax.rsqrt(jnp.mean(x * x, axis=-1, keepdims=True) + EPS) * g


def _dot(a, b):
    return jnp.dot(a, b, preferred_element_type=f32)


def _split_bf16(a):
    hi = a.astype(bf16)
    lo = (a - hi.astype(f32)).astype(bf16)
    return hi, lo


def _mixer_body(x_ref, gmix_ref, wint_ref, wup_ref, balpha_ref, ggla_ref, wconv_ref,
                wa_ref, wb_ref, wo_ref, gffn_ref, wr_ref, br_ref,
                x1_ref, h2_ref, logit_ref,
                state_ref, ubuf_ref, obuf_ref):
    nb, ts = x_ref.shape[0], x_ref.shape[1]
    m = nb * ts
    nchunk = ts // GLA_CHUNK
    seqs = [slice(ch * ts, (ch + 1) * ts) for ch in range(nb)]

    @pl.when(pl.program_id(1) == 0)
    def _():
        state_ref[...] = jnp.zeros_like(state_ref)
        ubuf_ref[:, 0:8, :] = jnp.zeros((nb, 8, D_MODEL), f32)

    x = x_ref[...].reshape(m, D_MODEL)
    hb = _rms(x, gmix_ref[...]).astype(bf16)

    def dot_t(a, w_rows):
        return lax.dot_general(a, w_rows, (((1,), (1,)), ((), ())), preferred_element_type=f32)

    def proj(c0, width):
        r0 = c0 if c0 < _C_CB else c0 + GLA_RANK
        return dot_t(hb, wint_ref[r0:r0 + width, :])

    a_lr = dot_t(hb, wint_ref[_C_CB:_C_CB + LANES, :])
    pre = _dot(a_lr.astype(bf16), wup_ref[...]) + balpha_ref[...]
    log_a = -(jnp.maximum(-pre, 0.0) + jnp.log1p(jnp.exp(-jnp.abs(pre)))) * (1.0 / GLA_TAU)
    ri = lax.broadcasted_iota(jnp.int32, (ts, ts), 0)
    ci = lax.broadcasted_iota(jnp.int32, (ts, ts), 1)
    same_chunk = (ri >> _CHUNK_SHIFT) == (ci >> _CHUNK_SHIFT)
    causal = jnp.logical_and(same_chunk, ci <= ri)
    tri = jnp.where(causal, 1.0, 0.0).astype(bf16)
    la_hi, la_lo = _split_bf16(log_a)
    b = jnp.concatenate([_dot(tri, la_hi[sq]) + _dot(tri, la_lo[sq]) for sq in seqs], axis=0)
    lasts = [b[(c + 1) * GLA_CHUNK - 1:(c + 1) * GLA_CHUNK, :] for c in range(nb * nchunk)]
    b_tot = jnp.concatenate([jnp.broadcast_to(r, (GLA_CHUNK, QK_W)) for r in lasts], axis=0)
    e_pos = jnp.exp(b)
    e_neg = jnp.exp(-b)
    e_tail = jnp.exp(b_tot - b)
    e_tot = [jnp.exp(r) for r in lasts]

    q = proj(_C_Q, QK_W)
    k = proj(_C_K, QK_W)
    q_dec = (q * (GLA_DK ** -0.5) * e_pos).astype(bf16)
    k_inv = (k * e_neg).astype(bf16)
    k_tail = (k * e_tail).astype(bf16)
    vb = proj(_C_V, V_W).astype(bf16)

    for ch, sq in enumerate(seqs):
        for hh in range(GLA_HEADS):
            qs = slice(hh * GLA_DK, (hh + 1) * GLA_DK)
            vs = slice(hh * GLA_DV, (hh + 1) * GLA_DV)
            sc = lax.dot_general(q_dec[sq, qs], k_inv[sq, qs], (((1,), (1,)), ((), ())),
                                 preferred_element_type=f32)
            sc = jnp.where(causal, sc, 0.0).astype(bf16)
            obuf_ref[sq, vs] = _dot(sc, vb[sq, vs])
            st = state_ref[ch, hh]
            for c in range(nchunk):
                rs = slice(ch * ts + c * GLA_CHUNK, ch * ts + (c + 1) * GLA_CHUNK)
                o_inter = lax.dot_general(q_dec[rs, qs], st.astype(bf16), (((1,), (1,)), ((), ())),
                                          preferred_element_type=f32)
                obuf_ref[rs, vs] += o_inter
                upd = lax.dot_general(vb[rs, vs], k_tail[rs, qs], (((0,), (0,)), ((), ())),
                                      preferred_element_type=f32)
                st = st * e_tot[ch * nchunk + c][:, qs] + upd
            state_ref[ch, hh] = st

    r = proj(_C_R, V_W)
    ggla = ggla_ref[...]
    o_parts = []
    for hh in range(GLA_HEADS):
        vs = slice(hh * GLA_DV, (hh + 1) * GLA_DV)
        o_h = _rms(obuf_ref[:, vs], ggla[:, vs])
        r_h = r[:, vs]
        o_parts.append((o_h * (r_h * jax.nn.sigmoid(r_h))).astype(bf16))
    y_a = _dot(jnp.concatenate(o_parts, axis=1), wa_ref[...])

    u = proj(_C_CC, D_MODEL) * proj(_C_CX, D_MODEL)
    wc = wconv_ref[...]
    y_convs = []
    for ch, sq in enumerate(seqs):
        ubuf_ref[ch, 8:8 + ts, :] = u[sq]
        y_convs.append(wc[0:1, :] * ubuf_ref[ch, 6:6 + ts, :] + wc[1:2, :] * ubuf_ref[ch, 7:7 + ts, :]
                       + wc[2:3, :] * u[sq])
        ubuf_ref[ch, 0:8, :] = ubuf_ref[ch, ts:ts + 8, :]
    y_conv = jnp.concatenate(y_convs, axis=0)
    y_b = _dot((proj(_C_CB, D_MODEL) * y_conv).astype(bf16), wb_ref[...])

    mixed = (jax.nn.sigmoid(proj(_C_GA, D_MODEL)) * y_a
             + jax.nn.sigmoid(proj(_C_GB, D_MODEL)) * y_b)
    x1 = x + _dot(mixed.astype(bf16), wo_ref[...])
    x1_ref[...] = x1.reshape(nb, ts, D_MODEL)

    h2 = _rms(x1, gffn_ref[...]).astype(bf16)
    h2_ref[...] = h2.reshape(nb, ts, D_MODEL)
    logits = _dot(h2, wr_ref[...]) + br_ref[...]
    logit_ref[...] = logits.reshape(nb, ts, LANES)


def _mixer(x3, gmix, wint, wup, balpha, ggla, wconv, wa, wb, wo, gffn, wr, br):
    batch, seq, _ = x3.shape
    ts = min(MIX_TS, seq)
    ns = seq // ts
    tokens = batch * seq
    nb = MIX_ROWS if batch % MIX_ROWS == 0 else 1

    def const(shape):
        return pl.BlockSpec(shape, lambda b, s: (0,) * len(shape), pipeline_mode=pl.Buffered(1))

    def row(width):
        return pl.BlockSpec((nb, ts, width), lambda b, s: (b, s, 0))

    x1, h2, logits = pl.pallas_call(
        _mixer_body,
        grid=(batch // nb, ns),
        in_specs=[row(D_MODEL), const(gmix.shape), const(wint.shape),
                  const(wup.shape), const(balpha.shape), const(ggla.shape), const(wconv.shape),
                  const(wa.shape), const(wb.shape), const(wo.shape), const(gffn.shape),
                  const(wr.shape), const(br.shape)],
        out_specs=[row(D_MODEL), row(D_MODEL), row(LANES)],
        out_shape=[jax.ShapeDtypeStruct((batch, seq, D_MODEL), f32),
                   jax.ShapeDtypeStruct((batch, seq, D_MODEL), bf16),
                   jax.ShapeDtypeStruct((batch, seq, LANES), f32)],
        scratch_shapes=[pltpu.VMEM((nb, GLA_HEADS, GLA_DV, GLA_DK), f32),
                        pltpu.VMEM((nb, ts + 8, D_MODEL), f32),
                        pltpu.VMEM((nb * ts, V_W), f32)],
        compiler_params=pltpu.CompilerParams(
            dimension_semantics=("parallel", "arbitrary"), vmem_limit_bytes=VMEM_LIMIT),
        name="mixer",
    )(x3, gmix, wint, wup, balpha, ggla, wconv, wa, wb, wo, gffn, wr, br)
    return (x1.reshape(tokens, D_MODEL), h2.reshape(tokens, D_MODEL),
            logits.reshape(tokens, LANES))


def _route_body(tile, logit_ref, meta_ref, cnt_ref):
    rows = logit_ref.shape[0]
    lane = lax.broadcasted_iota(jnp.int32, (rows, LANES), 1)
    lane_f = lane.astype(f32)
    lg = jnp.where(lane < N_EXPERTS, logit_ref[...], -jnp.inf)
    sels, tops, firsts = [], [], []
    for _ in range(TOP_K):
        m = jnp.max(lg, axis=-1, keepdims=True)
        first = jnp.min(jnp.where(lg == m, lane_f, float(LANES)), axis=-1, keepdims=True)
        sel = lane_f == first
        sels.append(sel)
        tops.append(m)
        firsts.append(first)
        lg = jnp.where(sel, -jnp.inf, lg)
    ps = [jnp.exp(t - tops[0]) for t in tops]
    denom = ps[0] + ps[1] + ps[2] + ps[3]
    onehot = jnp.zeros((rows, LANES), f32)
    for sel in sels:
        onehot = onehot + jnp.where(sel, 1.0, 0.0)
    onehot_b = onehot.astype(bf16)
    ri = lax.broadcasted_iota(jnp.int32, (tile, tile), 0)
    ci = lax.broadcasted_iota(jnp.int32, (tile, tile), 1)
    strict_lower = jnp.where(ci < ri, 1.0, 0.0).astype(bf16)
    ones = jnp.ones((8, tile), bf16)
    ranks = []
    for j in range(rows // tile):
        part = onehot_b[j * tile:(j + 1) * tile, :]
        ranks.append(_dot(strict_lower, part))
        cnt_ref[j * 8:(j + 1) * 8, :] = _dot(ones, part).astype(jnp.int32)
    rank = jnp.concatenate(ranks, axis=0)
    meta = jnp.zeros((rows, LANES), f32)
    for kk in range(TOP_K):
        meta = jnp.where(lane == _META_EXPERT + kk, firsts[kk], meta)
        meta = jnp.where(lane == _META_GATE + kk, ps[kk] / denom, meta)
        rank_k = jnp.sum(jnp.where(sels[kk], rank, 0.0), axis=-1, keepdims=True)
        meta = jnp.where(lane == _META_RANK + kk, rank_k, meta)
    meta_ref[...] = meta


def _route(logits, tile):
    tokens = logits.shape[0]
    rows = min(ROUTE_ROWS, tokens)
    steps = tokens // rows
    per_step = rows // tile
    return pl.pallas_call(
        functools.partial(_route_body, tile),
        grid=(steps,),
        in_specs=[pl.BlockSpec((rows, LANES), lambda i: (i, 0))],
        out_specs=[pl.BlockSpec((rows, LANES), lambda i: (i, 0)),
                   pl.BlockSpec((per_step * 8, LANES), lambda i: (i, 0))],
        out_shape=[jax.ShapeDtypeStruct((tokens, LANES), f32),
                   jax.ShapeDtypeStruct((tokens // tile * 8, LANES), jnp.int32)],
        compiler_params=pltpu.CompilerParams(
            dimension_semantics=("parallel",), vmem_limit_bytes=VMEM_LIMIT),
        name="route",
    )(logits)


def _lmax(tm):
    full = tm * TOP_K + N_EXPERTS * 2 * (BF16_ROWS - 1)
    return -(-full // 256) * 256


def _token_slots(meta, base_row):
    lane_f = lax.broadcasted_iota(jnp.int32, meta.shape, 1).astype(f32)
    slots = []
    for kk in range(TOP_K):
        sel = lane_f == meta[:, _META_EXPERT + kk:_META_EXPERT + kk + 1]
        first = jnp.sum(jnp.where(sel, base_row, 0.0), axis=-1, keepdims=True)
        slots.append(first + meta[:, _META_RANK + kk:_META_RANK + kk + 1])
    return slots


def _segment_copy(src, dst, sem, src_off, dst_off, rows):
    src_off = pl.multiple_of(src_off, BF16_ROWS)
    dst_off = pl.multiple_of(dst_off, BF16_ROWS)
    rows = pl.multiple_of(rows, BF16_ROWS)
    return pltpu.make_async_copy(src.at[pl.ds(src_off, rows)], dst.at[pl.ds(dst_off, rows)], sem)


def _wait_rows(src, dst, sem, total):
    @pl.when(total > 0)
    def _():
        _segment_copy(src, dst, sem, 0, 0, total).wait()


def _zero_rows(zero_ref, dst_ref, sem, first, start):
    tile = zero_ref.shape[0]
    n = dst_ref.shape[0] - first
    n_full = n // tile

    def full(j):
        return _segment_copy(zero_ref, dst_ref, sem, 0, first + j * tile, tile)

    def rest():
        return _segment_copy(zero_ref, dst_ref, sem, 0, first + n_full * tile, n - n_full * tile)

    def step(j, carry):
        full(j).start() if start else full(j).wait()
        return carry

    lax.fori_loop(0, n_full, step, 0)

    @pl.when(n - n_full * tile > 0)
    def _():
        rest().start() if start else rest().wait()


def _dispatch_body(voff_ref, coff_ref, keep_ref, row0_ref, wlen_ref, wtot_ref, used_ref,
                   h2_ref, meta_ref, base_ref, meta_next_ref, base_next_ref, xs_ref,
                   xc0_ref, xc1_ref, pt0_ref, pt1_ref, carry_ref, zero_ref, sem, zsem):
    i = pl.program_id(0)
    nt = pl.num_programs(0)
    tm = h2_ref.shape[0]
    lmax = xc0_ref.shape[0]
    slot = i & 1

    def slot_rows(m_ref, b_ref):
        meta = m_ref[...]
        lane = lax.broadcasted_iota(jnp.int32, meta.shape, 1)
        slots = jnp.zeros(meta.shape, f32)
        for kk, s in enumerate(_token_slots(meta, b_ref[0:1, :])):
            slots = jnp.where(lane == kk, s, slots)
        return slots.T[0:8, :].astype(jnp.int32)

    def one_hot_rows(slots_t, c):
        row = lax.broadcasted_iota(jnp.int32, (_CHUNK, tm), 0) + c * _CHUNK
        pt = jnp.zeros((_CHUNK, tm), f32)
        for kk in range(TOP_K):
            pt = jnp.where(row == slots_t[kk:kk + 1, :], 1.0, pt)
        return pt.astype(bf16)

    @pl.when(i == 0)
    def _():
        carry_ref[...] = jnp.zeros_like(carry_ref)
        zero_ref[...] = jnp.zeros_like(zero_ref)
        _zero_rows(zero_ref, xs_ref, zsem, used_ref[0], start=True)
        first = slot_rows(meta_ref, base_ref)
        for c in range(lmax // _CHUNK):
            pt0_ref[c * _CHUNK:(c + 1) * _CHUNK, :] = one_hot_rows(first, c)

    @pl.when(i >= 2)
    def _():
        _wait_rows(xc0_ref, xs_ref, sem.at[slot], wtot_ref[jnp.maximum(i - 2, 0)])

    def merge_carry(t, buf):
        def block(off):
            return buf.at[pl.ds(pl.multiple_of(off, BF16_ROWS), BF16_ROWS)]

        for e in range(N_EXPERTS):
            blk = block(voff_ref[t * N_EXPERTS + e])
            blk[...] = blk[...] + carry_ref[e]
        for e in range(N_EXPERTS):
            j = t * N_EXPERTS + e
            tail = block(coff_ref[j])[...]
            carry_ref[e] = jnp.where(keep_ref[j] > 0, tail, jnp.zeros_like(tail))

    def copy_out(t, buf):
        for e in range(N_EXPERTS):
            j = t * N_EXPERTS + e
            _segment_copy(buf, xs_ref, sem.at[slot], voff_ref[j], row0_ref[j], wlen_ref[j]).start()

    def compact(pt_now_ref, pt_next_ref, xc_now_ref):
        h2 = h2_ref[...]
        nxt = slot_rows(meta_next_ref, base_next_ref)
        for c in range(lmax // _CHUNK):
            rows = slice(c * _CHUNK, (c + 1) * _CHUNK)
            xc_now_ref[rows, :] = _dot(pt_now_ref[rows, :], h2).astype(bf16)
            pt_next_ref[rows, :] = one_hot_rows(nxt, c)
        merge_carry(i, xc_now_ref)
        copy_out(i, xc_now_ref)

    @pl.when(slot == 0)
    def _():
        compact(pt0_ref, pt1_ref, xc0_ref)

    @pl.when(slot == 1)
    def _():
        compact(pt1_ref, pt0_ref, xc1_ref)

    @pl.when(i == nt - 1)
    def _():
        @pl.when(i >= 1)
        def _():
            _wait_rows(xc0_ref, xs_ref, sem.at[1 - slot], wtot_ref[jnp.maximum(i - 1, 0)])
        _wait_rows(xc0_ref, xs_ref, sem.at[slot], wtot_ref[i])
        _zero_rows(zero_ref, xs_ref, zsem, used_ref[0], start=False)


def _dispatch(plan, h2, meta, rows_total):
    tokens = h2.shape[0]
    tm = min(MIX_TS, tokens)
    nt = tokens // tm
    scalars = (plan["voff"], plan["coff"], plan["keep"], plan["row0"], plan["wlen"], plan["wtot"],
               plan["used"])
    return pl.pallas_call(
        _dispatch_body,
        grid_spec=pltpu.PrefetchScalarGridSpec(
            num_scalar_prefetch=len(scalars),
            grid=(nt,),
            in_specs=[pl.BlockSpec((tm, D_MODEL), lambda i, *_: (i, 0)),
                      pl.BlockSpec((tm, LANES), lambda i, *_: (i, 0)),
                      pl.BlockSpec((8, LANES), lambda i, *_: (i, 0)),
                      pl.BlockSpec((tm, LANES), lambda i, *_: (jnp.minimum(i + 1, nt - 1), 0)),
                      pl.BlockSpec((8, LANES), lambda i, *_: (jnp.minimum(i + 1, nt - 1), 0))],
            out_specs=pl.BlockSpec(memory_space=pl.ANY),
            scratch_shapes=[pltpu.VMEM((_lmax(tm), D_MODEL), bf16),
                            pltpu.VMEM((_lmax(tm), D_MODEL), bf16),
                            pltpu.VMEM((_lmax(tm), tm), bf16),
                            pltpu.VMEM((_lmax(tm), tm), bf16),
                            pltpu.VMEM((N_EXPERTS, BF16_ROWS, D_MODEL), bf16),
                            pltpu.VMEM((FFN_TM, D_MODEL), bf16),
                            pltpu.SemaphoreType.DMA((2,)), pltpu.SemaphoreType.DMA],
        ),
        out_shape=jax.ShapeDtypeStruct((rows_total, D_MODEL), bf16),
        compiler_params=pltpu.CompilerParams(
            dimension_semantics=("arbitrary",), vmem_limit_bytes=VMEM_LIMIT),
        name="dispatch",
    )(*scalars, h2, meta, plan["base"], meta, plan["base"])


def _ffn_body(gstart_ref, gsize_ref, used_ref, xs_ref, wgu_ref, bgu_ref, wd_ref, bd_ref, os_ref,
              xbuf_ref, obuf_ref, zero_ref, state_ref, isem, osem, zsem):
    e = pl.program_id(0)
    ne = pl.num_programs(0)
    tile = xbuf_ref.shape[1]
    base = gstart_ref[e]
    n = gsize_ref[e]
    n_tiles = (n + tile - 1) // tile

    def rows_of(size, t):
        return jnp.minimum(tile, size - t * tile)

    def in_copy(row0, rows, slot):
        return pltpu.make_async_copy(xs_ref.at[pl.ds(row0, rows)],
                                     xbuf_ref.at[slot, pl.ds(0, rows)], isem.at[slot])

    def out_copy(row0, rows, slot):
        return pltpu.make_async_copy(obuf_ref.at[slot, pl.ds(0, rows)],
                                     os_ref.at[pl.ds(row0, rows)], osem.at[slot])

    def start_tile(copy, row0, rows, slot):
        row0 = pl.multiple_of(row0, BF16_ROWS)

        @pl.when(rows == tile)
        def _():
            copy(row0, tile, slot).start(priority=_TILE_DMA_PRIORITY)

        @pl.when(rows < tile)
        def _():
            copy(row0, pl.multiple_of(rows, BF16_ROWS), slot).start()

    def wait_tile(copy, rows, slot):
        copy(0, pl.multiple_of(rows, BF16_ROWS), slot).wait()

    @pl.when(e == 0)
    def _():
        xbuf_ref[...] = jnp.zeros_like(xbuf_ref)
        zero_ref[...] = jnp.zeros_like(zero_ref)
        _zero_rows(zero_ref, os_ref, zsem, used_ref[0], start=True)
        state_ref[0] = 0
        state_ref[1] = 0
        state_ref[2] = 0
        state_ref[3] = 0

    done = state_ref[0]

    @pl.when(jnp.logical_and(n_tiles > 0, state_ref[1] == 0))
    def _():
        start_tile(in_copy, base, rows_of(n, 0), done & 1)

    nxt = jnp.minimum(e + 1, ne - 1)
    has_next = jnp.logical_and(e + 1 < ne, gsize_ref[nxt] > 0)

    def step(t, carry):
        slot = (done + t) & 1

        @pl.when(t + 1 < n_tiles)
        def _():
            start_tile(in_copy, base + (t + 1) * tile, rows_of(n, t + 1), 1 - slot)

        @pl.when(jnp.logical_and(t + 1 == n_tiles, has_next))
        def _():
            start_tile(in_copy, gstart_ref[nxt], rows_of(gsize_ref[nxt], 0), 1 - slot)

        rows = rows_of(n, t)
        wait_tile(in_copy, rows, slot)

        @pl.when(state_ref[2 + slot] > 0)
        def _():
            wait_tile(out_copy, state_ref[2 + slot], slot)

        def ffn(r0, m):
            gu = _dot(xbuf_ref[slot, r0:r0 + m, :], wgu_ref[...].astype(bf16)) + bgu_ref[...]
            gate = jnp.minimum(gu[:, :D_FF], SWIGLU_LIMIT)
            lin = jnp.clip(gu[:, D_FF:], -SWIGLU_LIMIT, SWIGLU_LIMIT)
            act = (lin + 1.0) * (gate * jax.nn.sigmoid(SWIGLU_ALPHA * gate))
            out = _dot(act.astype(bf16), wd_ref[...].astype(bf16)) + bd_ref[...]
            obuf_ref[slot, r0:r0 + m, :] = out.astype(bf16)

        half, quarter, eighth = tile // 2, tile // 4, tile // 8
        for lo, hi, chains in ((half + quarter, tile, ((0, half), (half, half))),
                               (half + eighth, half + quarter, ((0, half), (half, quarter))),
                               (half, half + eighth, ((0, half), (half, eighth))),
                               (quarter, half, ((0, half),)),
                               (eighth, quarter, ((0, quarter),)),
                               (0, eighth, ((0, eighth),))):
            @pl.when(jnp.logical_and(rows > lo, rows <= hi))
            def _():
                for r0, m in chains:
                    ffn(r0, m)

        start_tile(out_copy, base + t * tile, rows, slot)
        state_ref[2 + slot] = rows
        return carry

    lax.fori_loop(0, n_tiles, step, 0)

    @pl.when(n_tiles > 0)
    def _():
        state_ref[0] = done + n_tiles
        state_ref[1] = has_next.astype(jnp.int32)

    @pl.when(e == ne - 1)
    def _():
        for slot in range(2):
            @pl.when(state_ref[2 + slot] > 0)
            def _():
                wait_tile(out_copy, state_ref[2 + slot], slot)
        _zero_rows(zero_ref, os_ref, zsem, used_ref[0], start=False)


def _ffn(gstart, gsize, used, xs, wgu, bgu, wd, bd):
    rows_total = xs.shape[0]
    return pl.pallas_call(
        _ffn_body,
        grid_spec=pltpu.PrefetchScalarGridSpec(
            num_scalar_prefetch=3,
            grid=(N_EXPERTS,),
            in_specs=[pl.BlockSpec(memory_space=pl.ANY),
                      pl.BlockSpec((None, D_MODEL, 2 * D_FF), lambda e, *_: (e, 0, 0)),
                      pl.BlockSpec((None, 1, 2 * D_FF), lambda e, *_: (e, 0, 0)),
                      pl.BlockSpec((None, D_FF, D_MODEL), lambda e, *_: (e, 0, 0)),
                      pl.BlockSpec((None, 1, D_MODEL), lambda e, *_: (e, 0, 0))],
            out_specs=pl.BlockSpec(memory_space=pl.ANY),
            scratch_shapes=[pltpu.VMEM((2, FFN_TM, D_MODEL), bf16),
                            pltpu.VMEM((2, FFN_TM, D_MODEL), bf16),
                            pltpu.VMEM((FFN_TM, D_MODEL), bf16),
                            pltpu.SMEM((4,), jnp.int32),
                            pltpu.SemaphoreType.DMA((2,)), pltpu.SemaphoreType.DMA((2,)),
                            pltpu.SemaphoreType.DMA],
        ),
        out_shape=jax.ShapeDtypeStruct((rows_total, D_MODEL), bf16),
        compiler_params=pltpu.CompilerParams(
            dimension_semantics=("arbitrary",), vmem_limit_bytes=VMEM_LIMIT),
        name="experts",
    )(gstart, gsize, used, xs, wgu, bgu, wd, bd)


def _combine_body(final_norm, voff_ref, row0_ref, rlen_ref, rtot_ref,
                  os_ref, meta_ref, base_ref, meta_next_ref, base_next_ref, x1_ref, gfin_ref,
                  out_ref, oc0_ref, oc1_ref, p0_ref, p1_ref, sem):
    i = pl.program_id(0)
    nt = pl.num_programs(0)
    tm = x1_ref.shape[0]
    lmax = oc0_ref.shape[0]
    slot = i & 1

    def slots_and_gates(m_ref, b_ref):
        meta = m_ref[...]
        slots = [s.astype(jnp.int32) for s in _token_slots(meta, b_ref[0:1, :])]
        gates = [meta[:, _META_GATE + kk:_META_GATE + kk + 1] for kk in range(TOP_K)]
        return slots, gates

    def gate_cols(slots, gates, c):
        col = lax.broadcasted_iota(jnp.int32, (tm, _CHUNK), 1) + c * _CHUNK
        p = jnp.zeros((tm, _CHUNK), f32)
        for s, g in zip(slots, gates):
            p = jnp.where(col == s, g, p)
        return p.astype(bf16)

    def fetch(tile, into_ref, sem_slot, rows_of):
        for e in range(N_EXPERTS):
            j = tile * N_EXPERTS + e
            _segment_copy(os_ref, into_ref, sem.at[sem_slot], row0_ref[j], voff_ref[j],
                          rows_of(rlen_ref[j])).start()

    @pl.when(i == 0)
    def _():
        oc0_ref[...] = jnp.zeros_like(oc0_ref)
        oc1_ref[...] = jnp.zeros_like(oc1_ref)
        fetch(0, oc0_ref, 0, lambda rows: rows)
        slots, gates = slots_and_gates(meta_ref, base_ref)
        for c in range(lmax // _CHUNK):
            p0_ref[:, c * _CHUNK:(c + 1) * _CHUNK] = gate_cols(slots, gates, c)

    _wait_rows(os_ref, oc0_ref, sem.at[slot], rtot_ref[i])

    def request_next(oc_next_ref):
        fetch(jnp.minimum(i + 1, nt - 1), oc_next_ref, 1 - slot,
              lambda rows: jnp.where(i + 1 < nt, rows, 0))

    @pl.when(slot == 0)
    def _():
        request_next(oc1_ref)

    @pl.when(slot == 1)
    def _():
        request_next(oc0_ref)

    def restore(p_now_ref, p_next_ref, oc_now_ref):
        slots, gates = slots_and_gates(meta_next_ref, base_next_ref)
        p_now = p_now_ref[...]
        n_build = lmax // _CHUNK
        n_out = D_MODEL // _CHUNK
        ssq = jnp.zeros((tm, 1), f32)
        for n in range(n_out):
            cols = slice(n * _CHUNK, (n + 1) * _CHUNK)
            xo = x1_ref[:, cols] + _dot(p_now, oc_now_ref[:, cols])
            out_ref[:, cols] = xo
            ssq = ssq + jnp.sum(xo * xo, axis=-1, keepdims=True)
            for c in range(n * n_build // n_out, (n + 1) * n_build // n_out):
                p_next_ref[:, c * _CHUNK:(c + 1) * _CHUNK] = gate_cols(slots, gates, c)
        if final_norm:
            out_ref[...] = out_ref[...] * lax.rsqrt(ssq * (1.0 / D_MODEL) + EPS) * gfin_ref[...]

    @pl.when(slot == 0)
    def _():
        restore(p0_ref, p1_ref, oc0_ref)

    @pl.when(slot == 1)
    def _():
        restore(p1_ref, p0_ref, oc1_ref)


def _combine(plan, os, meta, x1, gfin, final_norm):
    tokens = x1.shape[0]
    tm = min(MIX_TS, tokens)
    nt = tokens // tm
    scalars = (plan["voff"], plan["row0"], plan["rlen"], plan["rtot"])
    return pl.pallas_call(
        functools.partial(_combine_body, final_norm),
        grid_spec=pltpu.PrefetchScalarGridSpec(
            num_scalar_prefetch=len(scalars),
            grid=(nt,),
            in_specs=[pl.BlockSpec(memory_space=pl.ANY),
                      pl.BlockSpec((tm, LANES), lambda i, *_: (i, 0)),
                      pl.BlockSpec((8, LANES), lambda i, *_: (i, 0)),
                      pl.BlockSpec((tm, LANES), lambda i, *_: (jnp.minimum(i + 1, nt - 1), 0)),
                      pl.BlockSpec((8, LANES), lambda i, *_: (jnp.minimum(i + 1, nt - 1), 0)),
                      pl.BlockSpec((tm, D_MODEL), lambda i, *_: (i, 0)),
                      pl.BlockSpec((1, D_MODEL), lambda i, *_: (0, 0))],
            out_specs=pl.BlockSpec((tm, D_MODEL), lambda i, *_: (i, 0)),
            scratch_shapes=[pltpu.VMEM((_lmax(tm), D_MODEL), bf16),
                            pltpu.VMEM((_lmax(tm), D_MODEL), bf16),
                            pltpu.VMEM((tm, _lmax(tm)), bf16),
                            pltpu.VMEM((tm, _lmax(tm)), bf16),
                            pltpu.SemaphoreType.DMA((2,))],
        ),
        out_shape=jax.ShapeDtypeStruct((tokens, D_MODEL), f32),
        compiler_params=pltpu.CompilerParams(
            dimension_semantics=("arbitrary",), vmem_limit_bytes=VMEM_LIMIT),
        name="combine",
    )(*scalars, os, meta, plan["base"], meta, plan["base"], x1, gfin)


def _plan(counts, nt):
    up = lambda v: (v + (BF16_ROWS - 1)) // BF16_ROWS * BF16_ROWS
    down = lambda v: v // BF16_ROWS * BF16_ROWS
    c = counts.reshape(nt, 8, LANES)[:, 0, :N_EXPERTS]
    gsize = up(jnp.sum(c, axis=0))
    gstart = jnp.cumsum(gsize) - gsize
    before = jnp.cumsum(c, axis=0) - c
    a = before % BF16_ROWS
    length = a + c
    voff = jnp.cumsum(up(length), axis=1) - up(length)
    last = jnp.arange(nt)[:, None] == nt - 1
    wlen = jnp.where(last, up(length), down(length))
    rlen = jnp.where(c > 0, up(length), 0)
    base = jnp.zeros((nt, 8, LANES), jnp.float32)
    base = base.at[:, :, :N_EXPERTS].set((voff + a).astype(jnp.float32)[:, None, :])
    i32 = lambda v: v.reshape(-1).astype(jnp.int32)
    return dict(
        voff=i32(voff), coff=i32(voff + down(length)), keep=i32(length % BF16_ROWS),
        row0=i32(gstart[None, :] + before - a), wlen=i32(wlen), wtot=i32(jnp.sum(wlen, axis=1)),
        rlen=i32(rlen), rtot=i32(jnp.sum(rlen, axis=1)), base=base.reshape(nt * 8, LANES),
        gstart=i32(gstart), gsize=i32(gsize), used=i32(jnp.sum(gsize, keepdims=True)))


def kernel(x, g_mix, w_in, w_alpha_up, b_alpha, g_gla, w_conv, w_branch_a, w_branch_b, w_out, g_ffn, w_router, b_router, w_gate_up, b_gate_up, w_down, b_down, g_final):
    batch, seq, _ = x.shape
    tokens = batch * seq
    depth = w_in.shape[0]
    x2 = x.reshape(tokens, D_MODEL)
    nt = tokens // min(MIX_TS, seq)
    rows_total = tokens * TOP_K + N_EXPERTS * (BF16_ROWS - 1)
    rows_total = -(-rows_total // BF16_ROWS) * BF16_ROWS

    for l in range(depth):
        wint = jnp.swapaxes(w_in[l], 0, 1).astype(bf16)
        wup = jnp.pad(w_alpha_up[l], ((0, LANES - GLA_RANK), (0, 0))).astype(bf16)
        wr = jnp.pad(w_router[l], ((0, 0), (0, LANES - N_EXPERTS)))
        wr = wr.astype(bf16)
        br = jnp.pad(b_router[l], (0, LANES - N_EXPERTS)).reshape(1, LANES)
        wconv = jnp.pad(w_conv[l], ((0, 8 - CONV_K), (0, 0)))

        x1, h2, logits = _mixer(
            x2.reshape(batch, seq, D_MODEL), g_mix[l].reshape(1, D_MODEL), wint, wup,
            b_alpha[l].reshape(1, QK_W), g_gla[l].reshape(1, V_W), wconv,
            w_branch_a[l].astype(bf16), w_branch_b[l].astype(bf16), w_out[l].astype(bf16),
            g_ffn[l].reshape(1, D_MODEL), wr, br)

        meta, counts = _route(logits, min(MIX_TS, seq))
        plan = _plan(counts, nt)
        xs = _dispatch(plan, h2, meta, rows_total)
        os = _ffn(plan["gstart"], plan["gsize"], plan["used"], xs,
                  w_gate_up[l], b_gate_up[l].reshape(N_EXPERTS, 1, 2 * D_FF),
                  w_down[l], b_down[l].reshape(N_EXPERTS, 1, D_MODEL))
        x2 = _combine(plan, os, meta, x1, g_final.reshape(1, D_MODEL), l == depth - 1)
    return x2.reshape(batch, seq, D_MODEL)
```

```python
import functools

import jax
import jax.numpy as jnp
from jax import lax
from jax.experimental import pallas as pl
from jax.experimental.pallas import tpu as pltpu

D_MODEL = 1024
GLA_HEADS = 4
GLA_DK = 128
GLA_DV = 256
GLA_RANK = 16
GLA_TAU = 16.0
GLA_CHUNK = 64
_CHUNK_SHIFT = GLA_CHUNK.bit_length() - 1
CONV_K = 3
N_EXPERTS = 32
TOP_K = 4
D_FF = 1024
SWIGLU_LIMIT = 7.0
SWIGLU_ALPHA = 1.702
EPS = 1e-6

QK_W = GLA_HEADS * GLA_DK
V_W = GLA_HEADS * GLA_DV

LANES = 128
BF16_ROWS = 16
VMEM_LIMIT = 56 * 1024 * 1024

MIX_TS = 256
MIX_ROWS = 2
ROUTE_ROWS = 1024
_CHUNK = 256
FFN_TM = 1024
_META_EXPERT, _META_GATE, _META_RANK = 0, TOP_K, 2 * TOP_K
_TILE_DMA_PRIORITY = 1

_C_Q, _C_K, _C_V, _C_R = 0, QK_W, 2 * QK_W, 2 * QK_W + V_W
_C_CB = _C_R + V_W
_C_CC = _C_CB + D_MODEL
_C_CX = _C_CC + D_MODEL
_C_GA = _C_CX + D_MODEL
_C_GB = _C_GA + D_MODEL
_W_MAIN = _C_GB + D_MODEL

f32 = jnp.float32
bf16 = jnp.bfloat16


def _rms(x, g):
    return x * lax.rsqrt(jnp.mean(x * x, axis=-1, keepdims=True) + EPS) * g


def _dot(a, b):
    return jnp.dot(a, b, preferred_element_type=f32)


def _split_bf16(a):
    hi = a.astype(bf16)
    lo = (a - hi.astype(f32)).astype(bf16)
    return hi, lo


def _mixer_body(x_ref, gmix_ref, wint_ref, wup_ref, balpha_ref, ggla_ref, wconv_ref,
                wa_ref, wb_ref, wo_ref, gffn_ref, wr_ref, br_ref,
                x1_ref, h2_ref, logit_ref,
                state_ref, ubuf_ref, obuf_ref):
    nb, ts = x_ref.shape[0], x_ref.shape[1]
    m = nb * ts
    nchunk = ts // GLA_CHUNK
    seqs = [slice(ch * ts, (ch + 1) * ts) for ch in range(nb)]

    @pl.when(pl.program_id(1) == 0)
    def _():
        state_ref[...] = jnp.zeros_like(state_ref)
        ubuf_ref[:, 0:8, :] = jnp.zeros((nb, 8, D_MODEL), f32)

    x = x_ref[...].reshape(m, D_MODEL)
    hb = _rms(x, gmix_ref[...]).astype(bf16)

    def dot_t(a, w_rows):
        return lax.dot_general(a, w_rows, (((1,), (1,)), ((), ())), preferred_element_type=f32)

    def proj(c0, width):
        r0 = c0 if c0 < _C_CB else c0 + GLA_RANK
        return dot_t(hb, wint_ref[r0:r0 + width, :])

    a_lr = dot_t(hb, wint_ref[_C_CB:_C_CB + LANES, :])
    pre = _dot(a_lr.astype(bf16), wup_ref[...]) + balpha_ref[...]
    log_a = -(jnp.maximum(-pre, 0.0) + jnp.log1p(jnp.exp(-jnp.abs(pre)))) * (1.0 / GLA_TAU)
    ri = lax.broadcasted_iota(jnp.int32, (ts, ts), 0)
    ci = lax.broadcasted_iota(jnp.int32, (ts, ts), 1)
    same_chunk = (ri >> _CHUNK_SHIFT) == (ci >> _CHUNK_SHIFT)
    causal = jnp.logical_and(same_chunk, ci <= ri)
    tri = jnp.where(causal, 1.0, 0.0).astype(bf16)
    la_hi, la_lo = _split_bf16(log_a)
    b = jnp.concatenate([_dot(tri, la_hi[sq]) + _dot(tri, la_lo[sq]) for sq in seqs], axis=0)
    lasts = [b[(c + 1) * GLA_CHUNK - 1:(c + 1) * GLA_CHUNK, :] for c in range(nb * nchunk)]
    b_tot = jnp.concatenate([jnp.broadcast_to(r, (GLA_CHUNK, QK_W)) for r in lasts], axis=0)
    e_pos = jnp.exp(b)
    e_neg = jnp.exp(-b)
    e_tail = jnp.exp(b_tot - b)
    e_tot = [jnp.exp(r) for r in lasts]

    q = proj(_C_Q, QK_W)
    k = proj(_C_K, QK_W)
    q_dec = (q * (GLA_DK ** -0.5) * e_pos).astype(bf16)
    k_inv = (k * e_neg).astype(bf16)
    k_tail = (k * e_tail).astype(bf16)
    vb = proj(_C_V, V_W).astype(bf16)

    for ch, sq in enumerate(seqs):
        for hh in range(GLA_HEADS):
            qs = slice(hh * GLA_DK, (hh + 1) * GLA_DK)
            vs = slice(hh * GLA_DV, (hh + 1) * GLA_DV)
            sc = lax.dot_general(q_dec[sq, qs], k_inv[sq, qs], (((1,), (1,)), ((), ())),
                                 preferred_element_type=f32)
            sc = jnp.where(causal, sc, 0.0).astype(bf16)
            obuf_ref[sq, vs] = _dot(sc, vb[sq, vs])
            st = state_ref[ch, hh]
            for c in range(nchunk):
                rs = slice(ch * ts + c * GLA_CHUNK, ch * ts + (c + 1) * GLA_CHUNK)
                o_inter = lax.dot_general(q_dec[rs, qs], st.astype(bf16), (((1,), (1,)), ((), ())),
                                          preferred_element_type=f32)
                obuf_ref[rs, vs] += o_inter
                upd = lax.dot_general(vb[rs, vs], k_tail[rs, qs], (((0,), (0,)), ((), ())),
                                      preferred_element_type=f32)
                st = st * e_tot[ch * nchunk + c][:, qs] + upd
            state_ref[ch, hh] = st

    r = proj(_C_R, V_W)
    ggla = ggla_ref[...]
    o_parts = []
    for hh in range(GLA_HEADS):
        vs = slice(hh * GLA_DV, (hh + 1) * GLA_DV)
        o_h = _rms(obuf_ref[:, vs], ggla[:, vs])
        r_h = r[:, vs]
        o_parts.append((o_h * (r_h * jax.nn.sigmoid(r_h))).astype(bf16))
    y_a = _dot(jnp.concatenate(o_parts, axis=1), wa_ref[...].astype(bf16))

    u = proj(_C_CC, D_MODEL) * proj(_C_CX, D_MODEL)
    wc = wconv_ref[...]
    y_convs = []
    for ch, sq in enumerate(seqs):
        ubuf_ref[ch, 8:8 + ts, :] = u[sq]
        y_convs.append(wc[0:1, :] * ubuf_ref[ch, 6:6 + ts, :] + wc[1:2, :] * ubuf_ref[ch, 7:7 + ts, :]
                       + wc[2:3, :] * u[sq])
        ubuf_ref[ch, 0:8, :] = ubuf_ref[ch, ts:ts + 8, :]
    y_conv = jnp.concatenate(y_convs, axis=0)
    y_b = _dot((proj(_C_CB, D_MODEL) * y_conv).astype(bf16), wb_ref[...].astype(bf16))

    mixed = (jax.nn.sigmoid(proj(_C_GA, D_MODEL)) * y_a
             + jax.nn.sigmoid(proj(_C_GB, D_MODEL)) * y_b)
    x1 = x + _dot(mixed.astype(bf16), wo_ref[...].astype(bf16))
    x1_ref[...] = x1.reshape(nb, ts, D_MODEL)

    h2 = _rms(x1, gffn_ref[...]).astype(bf16)
    h2_ref[...] = h2.reshape(nb, ts, D_MODEL)
    logits = _dot(h2, wr_ref[...]) + br_ref[...]
    logit_ref[...] = logits.reshape(nb, ts, LANES)


def _mixer(x3, gmix, wint, wup, balpha, ggla, wconv, wa, wb, wo, gffn, wr, br):
    batch, seq, _ = x3.shape
    ts = min(MIX_TS, seq)
    ns = seq // ts
    tokens = batch * seq
    nb = MIX_ROWS if batch % MIX_ROWS == 0 else 1

    def const(shape):
        return pl.BlockSpec(shape, lambda b, s: (0,) * len(shape), pipeline_mode=pl.Buffered(1))

    def row(width):
        return pl.BlockSpec((nb, ts, width), lambda b, s: (b, s, 0))

    x1, h2, logits = pl.pallas_call(
        _mixer_body,
        grid=(batch // nb, ns),
        in_specs=[row(D_MODEL), const(gmix.shape), const(wint.shape),
                  const(wup.shape), const(balpha.shape), const(ggla.shape), const(wconv.shape),
                  const(wa.shape), const(wb.shape), const(wo.shape), const(gffn.shape),
                  const(wr.shape), const(br.shape)],
        out_specs=[row(D_MODEL), row(D_MODEL), row(LANES)],
        out_shape=[jax.ShapeDtypeStruct((batch, seq, D_MODEL), f32),
                   jax.ShapeDtypeStruct((batch, seq, D_MODEL), bf16),
                   jax.ShapeDtypeStruct((batch, seq, LANES), f32)],
        scratch_shapes=[pltpu.VMEM((nb, GLA_HEADS, GLA_DV, GLA_DK), f32),
                        pltpu.VMEM((nb, ts + 8, D_MODEL), f32),
                        pltpu.VMEM((nb * ts, V_W), f32)],
        compiler_params=pltpu.CompilerParams(
            dimension_semantics=("parallel", "arbitrary"), vmem_limit_bytes=VMEM_LIMIT),
        name="mixer",
    )(x3, gmix, wint, wup, balpha, ggla, wconv, wa, wb, wo, gffn, wr, br)
    return (x1.reshape(tokens, D_MODEL), h2.reshape(tokens, D_MODEL),
            logits.reshape(tokens, LANES))


def _route_body(tile, logit_ref, meta_ref, cnt_ref):
    rows = logit_ref.shape[0]
    lane = lax.broadcasted_iota(jnp.int32, (rows, LANES), 1)
    lane_f = lane.astype(f32)
    lg = jnp.where(lane < N_EXPERTS, logit_ref[...], -jnp.inf)
    sels, tops, firsts = [], [], []
    for _ in range(TOP_K):
        m = jnp.max(lg, axis=-1, keepdims=True)
        first = jnp.min(jnp.where(lg == m, lane_f, float(LANES)), axis=-1, keepdims=True)
        sel = lane_f == first
        sels.append(sel)
        tops.append(m)
        firsts.append(first)
        lg = jnp.where(sel, -jnp.inf, lg)
    ps = [jnp.exp(t - tops[0]) for t in tops]
    denom = ps[0] + ps[1] + ps[2] + ps[3]
    onehot = jnp.zeros((rows, LANES), f32)
    for sel in sels:
        onehot = onehot + jnp.where(sel, 1.0, 0.0)
    onehot_b = onehot.astype(bf16)
    ri = lax.broadcasted_iota(jnp.int32, (tile, tile), 0)
    ci = lax.broadcasted_iota(jnp.int32, (tile, tile), 1)
    strict_lower = jnp.where(ci < ri, 1.0, 0.0).astype(bf16)
    ones = jnp.ones((8, tile), bf16)
    ranks = []
    for j in range(rows // tile):
        part = onehot_b[j * tile:(j + 1) * tile, :]
        ranks.append(_dot(strict_lower, part))
        cnt_ref[j * 8:(j + 1) * 8, :] = _dot(ones, part).astype(jnp.int32)
    rank = jnp.concatenate(ranks, axis=0)
    meta = jnp.zeros((rows, LANES), f32)
    all_ones = jnp.ones((LANES, LANES), bf16)
    for kk in range(TOP_K):
        meta = jnp.where(lane == _META_EXPERT + kk, firsts[kk], meta)
        meta = jnp.where(lane == _META_GATE + kk, ps[kk] / denom, meta)
        rank_k = _dot(jnp.where(sels[kk], rank, 0.0).astype(bf16), all_ones)
        meta = jnp.where(lane == _META_RANK + kk, rank_k, meta)
    meta_ref[...] = meta


def _route(logits, tile):
    tokens = logits.shape[0]
    rows = min(ROUTE_ROWS, tokens)
    steps = tokens // rows
    per_step = rows // tile
    return pl.pallas_call(
        functools.partial(_route_body, tile),
        grid=(steps,),
        in_specs=[pl.BlockSpec((rows, LANES), lambda i: (i, 0))],
        out_specs=[pl.BlockSpec((rows, LANES), lambda i: (i, 0)),
                   pl.BlockSpec((per_step * 8, LANES), lambda i: (i, 0))],
        out_shape=[jax.ShapeDtypeStruct((tokens, LANES), f32),
                   jax.ShapeDtypeStruct((tokens // tile * 8, LANES), jnp.int32)],
        compiler_params=pltpu.CompilerParams(
            dimension_semantics=("parallel",), vmem_limit_bytes=VMEM_LIMIT),
        name="route",
    )(logits)


def _lmax(tm):
    full = tm * TOP_K + N_EXPERTS * 2 * (BF16_ROWS - 1)
    return -(-full // 256) * 256


def _token_slots(meta, base_row):
    lane_f = lax.broadcasted_iota(jnp.int32, meta.shape, 1).astype(f32)
    slots = []
    for kk in range(TOP_K):
        sel = lane_f == meta[:, _META_EXPERT + kk:_META_EXPERT + kk + 1]
        first = jnp.sum(jnp.where(sel, base_row, 0.0), axis=-1, keepdims=True)
        slots.append(first + meta[:, _META_RANK + kk:_META_RANK + kk + 1])
    return slots


def _segment_copy(src, dst, sem, src_off, dst_off, rows):
    src_off = pl.multiple_of(src_off, BF16_ROWS)
    dst_off = pl.multiple_of(dst_off, BF16_ROWS)
    rows = pl.multiple_of(rows, BF16_ROWS)
    return pltpu.make_async_copy(src.at[pl.ds(src_off, rows)], dst.at[pl.ds(dst_off, rows)], sem)


def _wait_rows(src, dst, sem, total):
    @pl.when(total > 0)
    def _():
        _segment_copy(src, dst, sem, 0, 0, total).wait()


def _zero_rows(zero_ref, dst_ref, sem, first, start):
    tile = zero_ref.shape[0]
    n = dst_ref.shape[0] - first
    n_full = n // tile

    def full(j):
        return _segment_copy(zero_ref, dst_ref, sem, 0, first + j * tile, tile)

    def rest():
        return _segment_copy(zero_ref, dst_ref, sem, 0, first + n_full * tile, n - n_full * tile)

    def step(j, carry):
        full(j).start() if start else full(j).wait()
        return carry

    lax.fori_loop(0, n_full, step, 0)

    @pl.when(n - n_full * tile > 0)
    def _():
        rest().start() if start else rest().wait()


def _dispatch_body(voff_ref, coff_ref, keep_ref, row0_ref, wlen_ref, wtot_ref, used_ref,
                   h2_ref, meta_ref, base_ref, meta_next_ref, base_next_ref, xs_ref,
                   xc0_ref, xc1_ref, pt0_ref, pt1_ref, carry_ref, zero_ref, sem, zsem):
    i = pl.program_id(0)
    nt = pl.num_programs(0)
    tm = h2_ref.shape[0]
    lmax = xc0_ref.shape[0]
    slot = i & 1

    def slot_rows(m_ref, b_ref):
        meta = m_ref[...]
        lane = lax.broadcasted_iota(jnp.int32, meta.shape, 1)
        slots = jnp.zeros(meta.shape, f32)
        for kk, s in enumerate(_token_slots(meta, b_ref[0:1, :])):
            slots = jnp.where(lane == kk, s, slots)
        return slots.T[0:8, :].astype(jnp.int32)

    def one_hot_rows(slots_t, c):
        row = lax.broadcasted_iota(jnp.int32, (_CHUNK, tm), 0) + c * _CHUNK
        pt = jnp.zeros((_CHUNK, tm), f32)
        for kk in range(TOP_K):
            pt = jnp.where(row == slots_t[kk:kk + 1, :], 1.0, pt)
        return pt.astype(bf16)

    @pl.when(i == 0)
    def _():
        carry_ref[...] = jnp.zeros_like(carry_ref)
        zero_ref[...] = jnp.zeros_like(zero_ref)
        _zero_rows(zero_ref, xs_ref, zsem, used_ref[0], start=True)
        first = slot_rows(meta_ref, base_ref)
        for c in range(lmax // _CHUNK):
            pt0_ref[c * _CHUNK:(c + 1) * _CHUNK, :] = one_hot_rows(first, c)

    @pl.when(i >= 2)
    def _():
        _wait_rows(xc0_ref, xs_ref, sem.at[slot], wtot_ref[jnp.maximum(i - 2, 0)])

    def merge_carry(t, buf):
        def block(off):
            return buf.at[pl.ds(pl.multiple_of(off, BF16_ROWS), BF16_ROWS)]

        for e in range(N_EXPERTS):
            blk = block(voff_ref[t * N_EXPERTS + e])
            blk[...] = blk[...] + carry_ref[e]
        for e in range(N_EXPERTS):
            j = t * N_EXPERTS + e
            tail = block(coff_ref[j])[...]
            carry_ref[e] = jnp.where(keep_ref[j] > 0, tail, jnp.zeros_like(tail))

    def copy_out(t, buf):
        for e in range(N_EXPERTS):
            j = t * N_EXPERTS + e
            _segment_copy(buf, xs_ref, sem.at[slot], voff_ref[j], row0_ref[j], wlen_ref[j]).start()

    def compact(pt_now_ref, pt_next_ref, xc_now_ref):
        h2 = h2_ref[...]
        nxt = slot_rows(meta_next_ref, base_next_ref)
        for c in range(lmax // _CHUNK):
            rows = slice(c * _CHUNK, (c + 1) * _CHUNK)
            xc_now_ref[rows, :] = _dot(pt_now_ref[rows, :], h2).astype(bf16)
            pt_next_ref[rows, :] = one_hot_rows(nxt, c)
        merge_carry(i, xc_now_ref)
        copy_out(i, xc_now_ref)

    @pl.when(slot == 0)
    def _():
        compact(pt0_ref, pt1_ref, xc0_ref)

    @pl.when(slot == 1)
    def _():
        compact(pt1_ref, pt0_ref, xc1_ref)

    @pl.when(i == nt - 1)
    def _():
        @pl.when(i >= 1)
        def _():
            _wait_rows(xc0_ref, xs_ref, sem.at[1 - slot], wtot_ref[jnp.maximum(i - 1, 0)])
        _wait_rows(xc0_ref, xs_ref, sem.at[slot], wtot_ref[i])
        _zero_rows(zero_ref, xs_ref, zsem, used_ref[0], start=False)


def _dispatch(plan, h2, meta, rows_total):
    tokens = h2.shape[0]
    tm = min(MIX_TS, tokens)
    nt = tokens // tm
    scalars = (plan["voff"], plan["coff"], plan["keep"], plan["row0"], plan["wlen"], plan["wtot"],
               plan["used"])
    return pl.pallas_call(
        _dispatch_body,
        grid_spec=pltpu.PrefetchScalarGridSpec(
            num_scalar_prefetch=len(scalars),
            grid=(nt,),
            in_specs=[pl.BlockSpec((tm, D_MODEL), lambda i, *_: (i, 0)),
                      pl.BlockSpec((tm, LANES), lambda i, *_: (i, 0)),
                      pl.BlockSpec((8, LANES), lambda i, *_: (i, 0)),
                      pl.BlockSpec((tm, LANES), lambda i, *_: (jnp.minimum(i + 1, nt - 1), 0)),
                      pl.BlockSpec((8, LANES), lambda i, *_: (jnp.minimum(i + 1, nt - 1), 0))],
            out_specs=pl.BlockSpec(memory_space=pl.ANY),
            scratch_shapes=[pltpu.VMEM((_lmax(tm), D_MODEL), bf16),
                            pltpu.VMEM((_lmax(tm), D_MODEL), bf16),
                            pltpu.VMEM((_lmax(tm), tm), bf16),
                            pltpu.VMEM((_lmax(tm), tm), bf16),
                            pltpu.VMEM((N_EXPERTS, BF16_ROWS, D_MODEL), bf16),
                            pltpu.VMEM((FFN_TM, D_MODEL), bf16),
                            pltpu.SemaphoreType.DMA((2,)), pltpu.SemaphoreType.DMA],
        ),
        out_shape=jax.ShapeDtypeStruct((rows_total, D_MODEL), bf16),
        compiler_params=pltpu.CompilerParams(
            dimension_semantics=("arbitrary",), vmem_limit_bytes=VMEM_LIMIT),
        name="dispatch",
    )(*scalars, h2, meta, plan["base"], meta, plan["base"])


def _ffn_body(gstart_ref, gsize_ref, used_ref, xs_ref, wgu_ref, bgu_ref, wd_ref, bd_ref, os_ref,
              xbuf_ref, obuf_ref, zero_ref, state_ref, isem, osem, zsem):
    e = pl.program_id(0)
    ne = pl.num_programs(0)
    tile = xbuf_ref.shape[1]
    base = gstart_ref[e]
    n = gsize_ref[e]
    n_tiles = (n + tile - 1) // tile

    def rows_of(size, t):
        return jnp.minimum(tile, size - t * tile)

    def in_copy(row0, rows, slot):
        return pltpu.make_async_copy(xs_ref.at[pl.ds(row0, rows)],
                                     xbuf_ref.at[slot, pl.ds(0, rows)], isem.at[slot])

    def out_copy(row0, rows, slot):
        return pltpu.make_async_copy(obuf_ref.at[slot, pl.ds(0, rows)],
                                     os_ref.at[pl.ds(row0, rows)], osem.at[slot])

    def start_tile(copy, row0, rows, slot):
        row0 = pl.multiple_of(row0, BF16_ROWS)

        @pl.when(rows == tile)
        def _():
            copy(row0, tile, slot).start(priority=_TILE_DMA_PRIORITY)

        @pl.when(rows < tile)
        def _():
            copy(row0, pl.multiple_of(rows, BF16_ROWS), slot).start()

    def wait_tile(copy, rows, slot):
        copy(0, pl.multiple_of(rows, BF16_ROWS), slot).wait()

    @pl.when(e == 0)
    def _():
        xbuf_ref[...] = jnp.zeros_like(xbuf_ref)
        zero_ref[...] = jnp.zeros_like(zero_ref)
        _zero_rows(zero_ref, os_ref, zsem, used_ref[0], start=True)
        state_ref[0] = 0
        state_ref[1] = 0
        state_ref[2] = 0
        state_ref[3] = 0

    done = state_ref[0]

    @pl.when(jnp.logical_and(n_tiles > 0, state_ref[1] == 0))
    def _():
        start_tile(in_copy, base, rows_of(n, 0), done & 1)

    nxt = jnp.minimum(e + 1, ne - 1)
    has_next = jnp.logical_and(e + 1 < ne, gsize_ref[nxt] > 0)

    def step(t, carry):
        slot = (done + t) & 1

        @pl.when(t + 1 < n_tiles)
        def _():
            start_tile(in_copy, base + (t + 1) * tile, rows_of(n, t + 1), 1 - slot)

        @pl.when(jnp.logical_and(t + 1 == n_tiles, has_next))
        def _():
            start_tile(in_copy, gstart_ref[nxt], rows_of(gsize_ref[nxt], 0), 1 - slot)

        rows = rows_of(n, t)
        wait_tile(in_copy, rows, slot)

        @pl.when(state_ref[2 + slot] > 0)
        def _():
            wait_tile(out_copy, state_ref[2 + slot], slot)

        def ffn(r0, m):
            gu = _dot(xbuf_ref[slot, r0:r0 + m, :], wgu_ref[...].astype(bf16)) + bgu_ref[...]
            gate = jnp.minimum(gu[:, :D_FF], SWIGLU_LIMIT)
            lin = jnp.clip(gu[:, D_FF:], -SWIGLU_LIMIT, SWIGLU_LIMIT)
            act = (lin + 1.0) * (gate * jax.nn.sigmoid(SWIGLU_ALPHA * gate))
            out = _dot(act.astype(bf16), wd_ref[...].astype(bf16)) + bd_ref[...]
            obuf_ref[slot, r0:r0 + m, :] = out.astype(bf16)

        half, quarter, eighth = tile // 2, tile // 4, tile // 8
        for lo, hi, chains in ((half + quarter, tile, ((0, half), (half, half))),
                               (half + eighth, half + quarter, ((0, half), (half, quarter))),
                               (half, half + eighth, ((0, half), (half, eighth))),
                               (quarter, half, ((0, half),)),
                               (eighth, quarter, ((0, quarter),)),
                               (0, eighth, ((0, eighth),))):
            @pl.when(jnp.logical_and(rows > lo, rows <= hi))
            def _():
                for r0, m in chains:
                    ffn(r0, m)

        start_tile(out_copy, base + t * tile, rows, slot)
        state_ref[2 + slot] = rows
        return carry

    lax.fori_loop(0, n_tiles, step, 0)

    @pl.when(n_tiles > 0)
    def _():
        state_ref[0] = done + n_tiles
        state_ref[1] = has_next.astype(jnp.int32)

    @pl.when(e == ne - 1)
    def _():
        for slot in range(2):
            @pl.when(state_ref[2 + slot] > 0)
            def _():
                wait_tile(out_copy, state_ref[2 + slot], slot)
        _zero_rows(zero_ref, os_ref, zsem, used_ref[0], start=False)


def _ffn(gstart, gsize, used, xs, wgu, bgu, wd, bd):
    rows_total = xs.shape[0]
    return pl.pallas_call(
        _ffn_body,
        grid_spec=pltpu.PrefetchScalarGridSpec(
            num_scalar_prefetch=3,
            grid=(N_EXPERTS,),
            in_specs=[pl.BlockSpec(memory_space=pl.ANY),
                      pl.BlockSpec((None, D_MODEL, 2 * D_FF), lambda e, *_: (e, 0, 0)),
                      pl.BlockSpec((None, 1, 2 * D_FF), lambda e, *_: (e, 0, 0)),
                      pl.BlockSpec((None, D_FF, D_MODEL), lambda e, *_: (e, 0, 0)),
                      pl.BlockSpec((None, 1, D_MODEL), lambda e, *_: (e, 0, 0))],
            out_specs=pl.BlockSpec(memory_space=pl.ANY),
            scratch_shapes=[pltpu.VMEM((2, FFN_TM, D_MODEL), bf16),
                            pltpu.VMEM((2, FFN_TM, D_MODEL), bf16),
                            pltpu.VMEM((FFN_TM, D_MODEL), bf16),
                            pltpu.SMEM((4,), jnp.int32),
                            pltpu.SemaphoreType.DMA((2,)), pltpu.SemaphoreType.DMA((2,)),
                            pltpu.SemaphoreType.DMA],
        ),
        out_shape=jax.ShapeDtypeStruct((rows_total, D_MODEL), bf16),
        compiler_params=pltpu.CompilerParams(
            dimension_semantics=("arbitrary",), vmem_limit_bytes=VMEM_LIMIT),
        name="experts",
    )(gstart, gsize, used, xs, wgu, bgu, wd, bd)


def _combine_body(final_norm, voff_ref, row0_ref, rlen_ref, rtot_ref,
                  os_ref, meta_ref, base_ref, meta_next_ref, base_next_ref, x1_ref, gfin_ref,
                  out_ref, oc0_ref, oc1_ref, p0_ref, p1_ref, sem):
    i = pl.program_id(0)
    nt = pl.num_programs(0)
    tm = x1_ref.shape[0]
    lmax = oc0_ref.shape[0]
    slot = i & 1

    def slots_and_gates(m_ref, b_ref):
        meta = m_ref[...]
        slots = [s.astype(jnp.int32) for s in _token_slots(meta, b_ref[0:1, :])]
        gates = [meta[:, _META_GATE + kk:_META_GATE + kk + 1] for kk in range(TOP_K)]
        return slots, gates

    def gate_cols(slots, gates, c):
        col = lax.broadcasted_iota(jnp.int32, (tm, _CHUNK), 1) + c * _CHUNK
        p = jnp.zeros((tm, _CHUNK), f32)
        for s, g in zip(slots, gates):
            p = jnp.where(col == s, g, p)
        return p.astype(bf16)

    def fetch(tile, into_ref, sem_slot, rows_of):
        for e in range(N_EXPERTS):
            j = tile * N_EXPERTS + e
            _segment_copy(os_ref, into_ref, sem.at[sem_slot], row0_ref[j], voff_ref[j],
                          rows_of(rlen_ref[j])).start()

    @pl.when(i == 0)
    def _():
        oc0_ref[...] = jnp.zeros_like(oc0_ref)
        oc1_ref[...] = jnp.zeros_like(oc1_ref)
        fetch(0, oc0_ref, 0, lambda rows: rows)
        slots, gates = slots_and_gates(meta_ref, base_ref)
        for c in range(lmax // _CHUNK):
            p0_ref[:, c * _CHUNK:(c + 1) * _CHUNK] = gate_cols(slots, gates, c)

    _wait_rows(os_ref, oc0_ref, sem.at[slot], rtot_ref[i])

    def request_next(oc_next_ref):
        fetch(jnp.minimum(i + 1, nt - 1), oc_next_ref, 1 - slot,
              lambda rows: jnp.where(i + 1 < nt, rows, 0))

    @pl.when(slot == 0)
    def _():
        request_next(oc1_ref)

    @pl.when(slot == 1)
    def _():
        request_next(oc0_ref)

    def restore(p_now_ref, p_next_ref, oc_now_ref):
        slots, gates = slots_and_gates(meta_next_ref, base_next_ref)
        p_now = p_now_ref[...]
        n_build = lmax // _CHUNK
        n_out = D_MODEL // _CHUNK
        ssq = jnp.zeros((tm, 1), f32)
        for n in range(n_out):
            cols = slice(n * _CHUNK, (n + 1) * _CHUNK)
            xo = x1_ref[:, cols] + _dot(p_now, oc_now_ref[:, cols])
            out_ref[:, cols] = xo
            ssq = ssq + jnp.sum(xo * xo, axis=-1, keepdims=True)
            for c in range(n * n_build // n_out, (n + 1) * n_build // n_out):
                p_next_ref[:, c * _CHUNK:(c + 1) * _CHUNK] = gate_cols(slots, gates, c)
        if final_norm:
            out_ref[...] = out_ref[...] * lax.rsqrt(ssq * (1.0 / D_MODEL) + EPS) * gfin_ref[...]

    @pl.when(slot == 0)
    def _():
        restore(p0_ref, p1_ref, oc0_ref)

    @pl.when(slot == 1)
    def _():
        restore(p1_ref, p0_ref, oc1_ref)


def _combine(plan, os, meta, x1, gfin, final_norm):
    tokens = x1.shape[0]
    tm = min(MIX_TS, tokens)
    nt = tokens // tm
    scalars = (plan["voff"], plan["row0"], plan["rlen"], plan["rtot"])
    return pl.pallas_call(
        functools.partial(_combine_body, final_norm),
        grid_spec=pltpu.PrefetchScalarGridSpec(
            num_scalar_prefetch=len(scalars),
            grid=(nt,),
            in_specs=[pl.BlockSpec(memory_space=pl.ANY),
                      pl.BlockSpec((tm, LANES), lambda i, *_: (i, 0)),
                      pl.BlockSpec((8, LANES), lambda i, *_: (i, 0)),
                      pl.BlockSpec((tm, LANES), lambda i, *_: (jnp.minimum(i + 1, nt - 1), 0)),
                      pl.BlockSpec((8, LANES), lambda i, *_: (jnp.minimum(i + 1, nt - 1), 0)),
                      pl.BlockSpec((tm, D_MODEL), lambda i, *_: (i, 0)),
                      pl.BlockSpec((1, D_MODEL), lambda i, *_: (0, 0))],
            out_specs=pl.BlockSpec((tm, D_MODEL), lambda i, *_: (i, 0)),
            scratch_shapes=[pltpu.VMEM((_lmax(tm), D_MODEL), bf16),
                            pltpu.VMEM((_lmax(tm), D_MODEL), bf16),
                            pltpu.VMEM((tm, _lmax(tm)), bf16),
                            pltpu.VMEM((tm, _lmax(tm)), bf16),
                            pltpu.SemaphoreType.DMA((2,))],
        ),
        out_shape=jax.ShapeDtypeStruct((tokens, D_MODEL), f32),
        compiler_params=pltpu.CompilerParams(
            dimension_semantics=("arbitrary",), vmem_limit_bytes=VMEM_LIMIT),
        name="combine",
    )(*scalars, os, meta, plan["base"], meta, plan["base"], x1, gfin)


def _plan(counts, nt):
    up = lambda v: (v + (BF16_ROWS - 1)) // BF16_ROWS * BF16_ROWS
    down = lambda v: v // BF16_ROWS * BF16_ROWS
    c = counts.reshape(nt, 8, LANES)[:, 0, :N_EXPERTS]
    gsize = up(jnp.sum(c, axis=0))
    gstart = jnp.cumsum(gsize) - gsize
    before = jnp.cumsum(c, axis=0) - c
    a = before % BF16_ROWS
    length = a + c
    voff = jnp.cumsum(up(length), axis=1) - up(length)
    last = jnp.arange(nt)[:, None] == nt - 1
    wlen = jnp.where(last, up(length), down(length))
    rlen = jnp.where(c > 0, up(length), 0)
    base = jnp.zeros((nt, 8, LANES), jnp.float32)
    base = base.at[:, :, :N_EXPERTS].set((voff + a).astype(jnp.float32)[:, None, :])
    i32 = lambda v: v.reshape(-1).astype(jnp.int32)
    return dict(
        voff=i32(voff), coff=i32(voff + down(length)), keep=i32(length % BF16_ROWS),
        row0=i32(gstart[None, :] + before - a), wlen=i32(wlen), wtot=i32(jnp.sum(wlen, axis=1)),
        rlen=i32(rlen), rtot=i32(jnp.sum(rlen, axis=1)), base=base.reshape(nt * 8, LANES),
        gstart=i32(gstart), gsize=i32(gsize), used=i32(jnp.sum(gsize, keepdims=True)))


def kernel(x, g_mix, w_in, w_alpha_up, b_alpha, g_gla, w_conv, w_branch_a, w_branch_b, w_out, g_ffn, w_router, b_router, w_gate_up, b_gate_up, w_down, b_down, g_final):
    batch, seq, _ = x.shape
    tokens = batch * seq
    depth = w_in.shape[0]
    x2 = x.reshape(tokens, D_MODEL)
    nt = tokens // min(MIX_TS, seq)
    rows_total = tokens * TOP_K + N_EXPERTS * (BF16_ROWS - 1)
    rows_total = -(-rows_total // BF16_ROWS) * BF16_ROWS

    for l in range(depth):
        wint = jnp.swapaxes(w_in[l], 0, 1).astype(bf16)
        wup = jnp.pad(w_alpha_up[l], ((0, LANES - GLA_RANK), (0, 0))).astype(bf16)
        wr = jnp.pad(w_router[l], ((0, 0), (0, LANES - N_EXPERTS)))
        wr = wr.astype(bf16)
        br = jnp.pad(b_router[l], (0, LANES - N_EXPERTS)).reshape(1, LANES)
        wconv = jnp.pad(w_conv[l], ((0, 8 - CONV_K), (0, 0)))

        x1, h2, logits = _mixer(
            x2.reshape(batch, seq, D_MODEL), g_mix[l].reshape(1, D_MODEL), wint, wup,
            b_alpha[l].reshape(1, QK_W), g_gla[l].reshape(1, V_W), wconv,
            w_branch_a[l], w_branch_b[l], w_out[l],
            g_ffn[l].reshape(1, D_MODEL), wr, br)

        meta, counts = _route(logits, min(MIX_TS, seq))
        plan = _plan(counts, nt)
        xs = _dispatch(plan, h2, meta, rows_total)
        os = _ffn(plan["gstart"], plan["gsize"], plan["used"], xs,
                  w_gate_up[l], b_gate_up[l].reshape(N_EXPERTS, 1, 2 * D_FF),
                  w_down[l], b_down[l].reshape(N_EXPERTS, 1, D_MODEL))
        x2 = _combine(plan, os, meta, x1, g_final.reshape(1, D_MODEL), l == depth - 1)
    return x2.reshape(batch, seq, D_MODEL)
```

```python
import functools

import jax
import jax.numpy as jnp
from jax import lax
from jax.experimental import pallas as pl
from jax.experimental.pallas import tpu as pltpu

D_MODEL = 1024
GLA_HEADS = 4
GLA_DK = 128
GLA_DV = 256
GLA_RANK = 16
GLA_TAU = 16.0
GLA_CHUNK = 64
_CHUNK_SHIFT = GLA_CHUNK.bit_length() - 1
CONV_K = 3
N_EXPERTS = 32
TOP_K = 4
D_FF = 1024
SWIGLU_LIMIT = 7.0
SWIGLU_ALPHA = 1.702
EPS = 1e-6

QK_W = GLA_HEADS * GLA_DK
V_W = GLA_HEADS * GLA_DV

LANES = 128
BF16_ROWS = 16
VMEM_LIMIT = 56 * 1024 * 1024

MIX_TS = 256
MIX_ROWS = 2
ROUTE_ROWS = 1024
_CHUNK = 256
FFN_TM = 1024
_META_EXPERT, _META_GATE, _META_RANK = 0, TOP_K, 2 * TOP_K

_C_Q, _C_K, _C_V, _C_R = 0, QK_W, 2 * QK_W, 2 * QK_W + V_W
_C_CB = _C_R + V_W
_C_CC = _C_CB + D_MODEL
_C_CX = _C_CC + D_MODEL
_C_GA = _C_CX + D_MODEL
_C_GB = _C_GA + D_MODEL

f32 = jnp.float32
bf16 = jnp.bfloat16


def _rms(x, g):
    return x * lax.rsqrt(jnp.mean(x * x, axis=-1, keepdims=True) + EPS) * g


def _dot(a, b):
    return jnp.dot(a, b, preferred_element_type=f32)


def _split_bf16(a):
    hi = a.astype(bf16)
    lo = (a - hi.astype(f32)).astype(bf16)
    return hi, lo


def _mixer_body(x_ref, gmix_ref, wint_ref, wup_ref, balpha_ref, ggla_ref, wconv_ref,
                wa_ref, wb_ref, wo_ref, gffn_ref, wr_ref, br_ref,
                x1_ref, h2_ref, logit_ref,
                state_ref, ubuf_ref, obuf_ref):
    nb, ts = x_ref.shape[0], x_ref.shape[1]
    m = nb * ts
    nchunk = ts // GLA_CHUNK
    seqs = [slice(ch * ts, (ch + 1) * ts) for ch in range(nb)]

    @pl.when(pl.program_id(1) == 0)
    def _():
        state_ref[...] = jnp.zeros_like(state_ref)
        ubuf_ref[:, 0:8, :] = jnp.zeros((nb, 8, D_MODEL), f32)

    x = x_ref[...].reshape(m, D_MODEL)
    hb = _rms(x, gmix_ref[...]).astype(bf16)

    def dot_t(a, w_rows):
        return lax.dot_general(a, w_rows, (((1,), (1,)), ((), ())), preferred_element_type=f32)

    def proj(c0, width):
        r0 = c0 if c0 < _C_CB else c0 + GLA_RANK
        return dot_t(hb, wint_ref[r0:r0 + width, :])

    a_lr = dot_t(hb, wint_ref[_C_CB:_C_CB + LANES, :])
    pre = _dot(a_lr.astype(bf16), wup_ref[...]) + balpha_ref[...]
    log_a = -(jnp.maximum(-pre, 0.0) + jnp.log1p(jnp.exp(-jnp.abs(pre)))) * (1.0 / GLA_TAU)
    ri = lax.broadcasted_iota(jnp.int32, (ts, ts), 0)
    ci = lax.broadcasted_iota(jnp.int32, (ts, ts), 1)
    same_chunk = (ri >> _CHUNK_SHIFT) == (ci >> _CHUNK_SHIFT)
    causal = jnp.logical_and(same_chunk, ci <= ri)
    tri = jnp.where(causal, 1.0, 0.0).astype(bf16)
    la_hi, la_lo = _split_bf16(log_a)
    b = jnp.concatenate([_dot(tri, la_hi[sq]) + _dot(tri, la_lo[sq]) for sq in seqs], axis=0)
    lasts = [b[(c + 1) * GLA_CHUNK - 1:(c + 1) * GLA_CHUNK, :] for c in range(nb * nchunk)]
    b_tot = jnp.concatenate([jnp.broadcast_to(r, (GLA_CHUNK, QK_W)) for r in lasts], axis=0)
    e_pos = jnp.exp(b)
    e_neg = jnp.exp(-b)
    e_tail = jnp.exp(b_tot - b)
    e_tot = [jnp.exp(r) for r in lasts]

    q = proj(_C_Q, QK_W)
    k = proj(_C_K, QK_W)
    q_dec = (q * (GLA_DK ** -0.5) * e_pos).astype(bf16)
    k_inv = (k * e_neg).astype(bf16)
    k_tail = (k * e_tail).astype(bf16)
    vb = proj(_C_V, V_W).astype(bf16)

    for ch, sq in enumerate(seqs):
        for hh in range(GLA_HEADS):
            qs = slice(hh * GLA_DK, (hh + 1) * GLA_DK)
            vs = slice(hh * GLA_DV, (hh + 1) * GLA_DV)
            sc = lax.dot_general(q_dec[sq, qs], k_inv[sq, qs], (((1,), (1,)), ((), ())),
                                 preferred_element_type=f32)
            sc = jnp.where(causal, sc, 0.0).astype(bf16)
            obuf_ref[sq, vs] = _dot(sc, vb[sq, vs])
            st = state_ref[ch, hh]
            for c in range(nchunk):
                rs = slice(ch * ts + c * GLA_CHUNK, ch * ts + (c + 1) * GLA_CHUNK)
                o_inter = lax.dot_general(q_dec[rs, qs], st.astype(bf16), (((1,), (1,)), ((), ())),
                                          preferred_element_type=f32)
                obuf_ref[rs, vs] += o_inter
                upd = lax.dot_general(vb[rs, vs], k_tail[rs, qs], (((0,), (0,)), ((), ())),
                                      preferred_element_type=f32)
                st = st * e_tot[ch * nchunk + c][:, qs] + upd
            state_ref[ch, hh] = st

    r = proj(_C_R, V_W)
    ggla = ggla_ref[...]
    o_parts = []
    for hh in range(GLA_HEADS):
        vs = slice(hh * GLA_DV, (hh + 1) * GLA_DV)
        o_h = _rms(obuf_ref[:, vs], ggla[:, vs])
        r_h = r[:, vs]
        o_parts.append((o_h * (r_h * jax.nn.sigmoid(r_h))).astype(bf16))
    y_a = _dot(jnp.concatenate(o_parts, axis=1), wa_ref[...].astype(bf16))

    u = proj(_C_CC, D_MODEL) * proj(_C_CX, D_MODEL)
    wc = wconv_ref[...]
    y_convs = []
    for ch, sq in enumerate(seqs):
        ubuf_ref[ch, 8:8 + ts, :] = u[sq]
        y_convs.append(wc[0:1, :] * ubuf_ref[ch, 6:6 + ts, :] + wc[1:2, :] * ubuf_ref[ch, 7:7 + ts, :]
                       + wc[2:3, :] * u[sq])
        ubuf_ref[ch, 0:8, :] = ubuf_ref[ch, ts:ts + 8, :]
    y_conv = jnp.concatenate(y_convs, axis=0)
    y_b = _dot((proj(_C_CB, D_MODEL) * y_conv).astype(bf16), wb_ref[...].astype(bf16))

    mixed = (jax.nn.sigmoid(proj(_C_GA, D_MODEL)) * y_a
             + jax.nn.sigmoid(proj(_C_GB, D_MODEL)) * y_b)
    x1 = x + _dot(mixed.astype(bf16), wo_ref[...].astype(bf16))
    x1_ref[...] = x1.reshape(nb, ts, D_MODEL)

    h2 = _rms(x1, gffn_ref[...]).astype(bf16)
    h2_ref[...] = h2.reshape(nb, ts, D_MODEL)
    logits = _dot(h2, wr_ref[...]) + br_ref[...]
    logit_ref[...] = logits.reshape(nb, ts, LANES)


def _mixer(x3, gmix, wint, wup, balpha, ggla, wconv, wa, wb, wo, gffn, wr, br):
    batch, seq, _ = x3.shape
    ts = min(MIX_TS, seq)
    ns = seq // ts
    tokens = batch * seq
    nb = MIX_ROWS if batch % MIX_ROWS == 0 else 1

    def const(shape):
        return pl.BlockSpec(shape, lambda b, s: (0,) * len(shape), pipeline_mode=pl.Buffered(1))

    def row(width):
        return pl.BlockSpec((nb, ts, width), lambda b, s: (b, s, 0))

    x1, h2, logits = pl.pallas_call(
        _mixer_body,
        grid=(batch // nb, ns),
        in_specs=[row(D_MODEL), const(gmix.shape), const(wint.shape),
                  const(wup.shape), const(balpha.shape), const(ggla.shape), const(wconv.shape),
                  const(wa.shape), const(wb.shape), const(wo.shape), const(gffn.shape),
                  const(wr.shape), const(br.shape)],
        out_specs=[row(D_MODEL), row(D_MODEL), row(LANES)],
        out_shape=[jax.ShapeDtypeStruct((batch, seq, D_MODEL), f32),
                   jax.ShapeDtypeStruct((batch, seq, D_MODEL), bf16),
                   jax.ShapeDtypeStruct((batch, seq, LANES), f32)],
        scratch_shapes=[pltpu.VMEM((nb, GLA_HEADS, GLA_DV, GLA_DK), f32),
                        pltpu.VMEM((nb, ts + 8, D_MODEL), f32),
                        pltpu.VMEM((nb * ts, V_W), f32)],
        compiler_params=pltpu.CompilerParams(
            dimension_semantics=("parallel", "arbitrary"), vmem_limit_bytes=VMEM_LIMIT),
        name="mixer",
    )(x3, gmix, wint, wup, balpha, ggla, wconv, wa, wb, wo, gffn, wr, br)
    return (x1.reshape(tokens, D_MODEL), h2.reshape(tokens, D_MODEL),
            logits.reshape(tokens, LANES))


def _route_body(tile, logit_ref, meta_ref, cnt_ref):
    rows = logit_ref.shape[0]
    lane = lax.broadcasted_iota(jnp.int32, (rows, LANES), 1)
    lane_f = lane.astype(f32)
    lg = jnp.where(lane < N_EXPERTS, logit_ref[...], -jnp.inf)
    sels, tops, firsts = [], [], []
    for _ in range(TOP_K):
        m = jnp.max(lg, axis=-1, keepdims=True)
        first = jnp.min(jnp.where(lg == m, lane_f, float(LANES)), axis=-1, keepdims=True)
        sel = lane_f == first
        sels.append(sel)
        tops.append(m)
        firsts.append(first)
        lg = jnp.where(sel, -jnp.inf, lg)
    ps = [jnp.exp(t - tops[0]) for t in tops]
    denom = ps[0] + ps[1] + ps[2] + ps[3]
    onehot = jnp.zeros((rows, LANES), f32)
    for sel in sels:
        onehot = onehot + jnp.where(sel, 1.0, 0.0)
    onehot_b = onehot.astype(bf16)
    ri = lax.broadcasted_iota(jnp.int32, (tile, tile), 0)
    ci = lax.broadcasted_iota(jnp.int32, (tile, tile), 1)
    strict_lower = jnp.where(ci < ri, 1.0, 0.0).astype(bf16)
    ones = jnp.ones((8, tile), bf16)
    ranks = []
    for j in range(rows // tile):
        part = onehot_b[j * tile:(j + 1) * tile, :]
        ranks.append(_dot(strict_lower, part))
        cnt_ref[j * 8:(j + 1) * 8, :] = _dot(ones, part).astype(jnp.int32)
    rank = jnp.concatenate(ranks, axis=0)
    meta = jnp.zeros((rows, LANES), f32)
    all_ones = jnp.ones((LANES, LANES), bf16)
    for kk in range(TOP_K):
        meta = jnp.where(lane == _META_EXPERT + kk, firsts[kk], meta)
        meta = jnp.where(lane == _META_GATE + kk, ps[kk] / denom, meta)
        rank_k = _dot(jnp.where(sels[kk], rank, 0.0).astype(bf16), all_ones)
        meta = jnp.where(lane == _META_RANK + kk, rank_k, meta)
    meta_ref[...] = meta


def _route(logits, tile):
    tokens = logits.shape[0]
    rows = min(ROUTE_ROWS, tokens)
    steps = tokens // rows
    per_step = rows // tile
    return pl.pallas_call(
        functools.partial(_route_body, tile),
        grid=(steps,),
        in_specs=[pl.BlockSpec((rows, LANES), lambda i: (i, 0))],
        out_specs=[pl.BlockSpec((rows, LANES), lambda i: (i, 0)),
                   pl.BlockSpec((per_step * 8, LANES), lambda i: (i, 0))],
        out_shape=[jax.ShapeDtypeStruct((tokens, LANES), f32),
                   jax.ShapeDtypeStruct((tokens // tile * 8, LANES), jnp.int32)],
        compiler_params=pltpu.CompilerParams(
            dimension_semantics=("parallel",), vmem_limit_bytes=VMEM_LIMIT),
        name="route",
    )(logits)


def _lmax(tm):
    full = tm * TOP_K + N_EXPERTS * 2 * (BF16_ROWS - 1)
    return -(-full // 256) * 256


def _token_slots(meta, base_row):
    lane_f = lax.broadcasted_iota(jnp.int32, meta.shape, 1).astype(f32)
    slots = []
    for kk in range(TOP_K):
        sel = lane_f == meta[:, _META_EXPERT + kk:_META_EXPERT + kk + 1]
        first = jnp.sum(jnp.where(sel, base_row, 0.0), axis=-1, keepdims=True)
        slots.append(first + meta[:, _META_RANK + kk:_META_RANK + kk + 1])
    return slots


def _segment_copy(src, dst, sem, src_off, dst_off, rows):
    src_off = pl.multiple_of(src_off, BF16_ROWS)
    dst_off = pl.multiple_of(dst_off, BF16_ROWS)
    rows = pl.multiple_of(rows, BF16_ROWS)
    return pltpu.make_async_copy(src.at[pl.ds(src_off, rows)], dst.at[pl.ds(dst_off, rows)], sem)


def _wait_rows(src, dst, sem, total):
    @pl.when(total > 0)
    def _():
        _segment_copy(src, dst, sem, 0, 0, total).wait()


def _zero_rows(zero_ref, dst_ref, sem, first, start):
    tile = zero_ref.shape[0]
    n = dst_ref.shape[0] - first
    n_full = n // tile

    def full(j):
        return _segment_copy(zero_ref, dst_ref, sem, 0, first + j * tile, tile)

    def rest():
        return _segment_copy(zero_ref, dst_ref, sem, 0, first + n_full * tile, n - n_full * tile)

    def step(j, carry):
        full(j).start() if start else full(j).wait()
        return carry

    lax.fori_loop(0, n_full, step, 0)

    @pl.when(n - n_full * tile > 0)
    def _():
        rest().start() if start else rest().wait()


def _dispatch_body(voff_ref, coff_ref, keep_ref, row0_ref, wlen_ref, wtot_ref, used_ref,
                   h2_ref, meta_ref, base_ref, meta_next_ref, base_next_ref, xs_ref,
                   xc0_ref, xc1_ref, pt0_ref, pt1_ref, carry_ref, zero_ref, sem, zsem):
    i = pl.program_id(0)
    nt = pl.num_programs(0)
    tm = h2_ref.shape[0]
    lmax = xc0_ref.shape[0]
    slot = i & 1

    def slot_rows(m_ref, b_ref):
        meta = m_ref[...]
        lane = lax.broadcasted_iota(jnp.int32, meta.shape, 1)
        slots = jnp.zeros(meta.shape, f32)
        for kk, s in enumerate(_token_slots(meta, b_ref[0:1, :])):
            slots = jnp.where(lane == kk, s, slots)
        return slots.T[0:8, :].astype(jnp.int32)

    def one_hot_rows(slots_t, c):
        row = lax.broadcasted_iota(jnp.int32, (_CHUNK, tm), 0) + c * _CHUNK
        pt = jnp.zeros((_CHUNK, tm), f32)
        for kk in range(TOP_K):
            pt = jnp.where(row == slots_t[kk:kk + 1, :], 1.0, pt)
        return pt.astype(bf16)

    @pl.when(i == 0)
    def _():
        carry_ref[...] = jnp.zeros_like(carry_ref)
        zero_ref[...] = jnp.zeros_like(zero_ref)
        _zero_rows(zero_ref, xs_ref, zsem, used_ref[0], start=True)
        first = slot_rows(meta_ref, base_ref)
        for c in range(lmax // _CHUNK):
            pt0_ref[c * _CHUNK:(c + 1) * _CHUNK, :] = one_hot_rows(first, c)

    @pl.when(i >= 2)
    def _():
        _wait_rows(xc0_ref, xs_ref, sem.at[slot], wtot_ref[jnp.maximum(i - 2, 0)])

    def merge_carry(t, buf):
        def block(off):
            return buf.at[pl.ds(pl.multiple_of(off, BF16_ROWS), BF16_ROWS)]

        for e in range(N_EXPERTS):
            blk = block(voff_ref[t * N_EXPERTS + e])
            blk[...] = blk[...] + carry_ref[e]
        for e in range(N_EXPERTS):
            j = t * N_EXPERTS + e
            tail = block(coff_ref[j])[...]
            carry_ref[e] = jnp.where(keep_ref[j] > 0, tail, jnp.zeros_like(tail))

    def copy_out(t, buf):
        for e in range(N_EXPERTS):
            j = t * N_EXPERTS + e
            _segment_copy(buf, xs_ref, sem.at[slot], voff_ref[j], row0_ref[j], wlen_ref[j]).start()

    def compact(pt_now_ref, pt_next_ref, xc_now_ref):
        h2 = h2_ref[...]
        nxt = slot_rows(meta_next_ref, base_next_ref)
        for c in range(lmax // _CHUNK):
            rows = slice(c * _CHUNK, (c + 1) * _CHUNK)
            xc_now_ref[rows, :] = _dot(pt_now_ref[rows, :], h2).astype(bf16)
            pt_next_ref[rows, :] = one_hot_rows(nxt, c)
        merge_carry(i, xc_now_ref)
        copy_out(i, xc_now_ref)

    @pl.when(slot == 0)
    def _():
        compact(pt0_ref, pt1_ref, xc0_ref)

    @pl.when(slot == 1)
    def _():
        compact(pt1_ref, pt0_ref, xc1_ref)

    @pl.when(i == nt - 1)
    def _():
        @pl.when(i >= 1)
        def _():
            _wait_rows(xc0_ref, xs_ref, sem.at[1 - slot], wtot_ref[jnp.maximum(i - 1, 0)])
        _wait_rows(xc0_ref, xs_ref, sem.at[slot], wtot_ref[i])
        _zero_rows(zero_ref, xs_ref, zsem, used_ref[0], start=False)


def _dispatch(plan, h2, meta, rows_total):
    tokens = h2.shape[0]
    tm = min(MIX_TS, tokens)
    nt = tokens // tm
    scalars = (plan["voff"], plan["coff"], plan["keep"], plan["row0"], plan["wlen"], plan["wtot"],
               plan["used"])
    return pl.pallas_call(
        _dispatch_body,
        grid_spec=pltpu.PrefetchScalarGridSpec(
            num_scalar_prefetch=len(scalars),
            grid=(nt,),
            in_specs=[pl.BlockSpec((tm, D_MODEL), lambda i, *_: (i, 0)),
                      pl.BlockSpec((tm, LANES), lambda i, *_: (i, 0)),
                      pl.BlockSpec((8, LANES), lambda i, *_: (i, 0)),
                      pl.BlockSpec((tm, LANES), lambda i, *_: (jnp.minimum(i + 1, nt - 1), 0)),
                      pl.BlockSpec((8, LANES), lambda i, *_: (jnp.minimum(i + 1, nt - 1), 0))],
            out_specs=pl.BlockSpec(memory_space=pl.ANY),
            scratch_shapes=[pltpu.VMEM((_lmax(tm), D_MODEL), bf16),
                            pltpu.VMEM((_lmax(tm), D_MODEL), bf16),
                            pltpu.VMEM((_lmax(tm), tm), bf16),
                            pltpu.VMEM((_lmax(tm), tm), bf16),
                            pltpu.VMEM((N_EXPERTS, BF16_ROWS, D_MODEL), bf16),
                            pltpu.VMEM((FFN_TM, D_MODEL), bf16),
                            pltpu.SemaphoreType.DMA((2,)), pltpu.SemaphoreType.DMA],
        ),
        out_shape=jax.ShapeDtypeStruct((rows_total, D_MODEL), bf16),
        compiler_params=pltpu.CompilerParams(
            dimension_semantics=("arbitrary",), vmem_limit_bytes=VMEM_LIMIT),
        name="dispatch",
    )(*scalars, h2, meta, plan["base"], meta, plan["base"])


def _ffn_body(gstart_ref, gsize_ref, used_ref, xs_ref, wgu_ref, bgu_ref, wd_ref, bd_ref, os_ref,
              xbuf_ref, obuf_ref, zero_ref, state_ref, isem, osem, zsem):
    e = pl.program_id(0)
    ne = pl.num_programs(0)
    tile = xbuf_ref.shape[1]
    base = gstart_ref[e]
    n = gsize_ref[e]
    n_tiles = (n + tile - 1) // tile

    def rows_of(size, t):
        return jnp.minimum(tile, size - t * tile)

    def in_copy(row0, rows, slot):
        return pltpu.make_async_copy(xs_ref.at[pl.ds(row0, rows)],
                                     xbuf_ref.at[slot, pl.ds(0, rows)], isem.at[slot])

    def out_copy(row0, rows, slot):
        return pltpu.make_async_copy(obuf_ref.at[slot, pl.ds(0, rows)],
                                     os_ref.at[pl.ds(row0, rows)], osem.at[slot])

    def start_tile(copy, row0, rows, slot):
        copy(pl.multiple_of(row0, BF16_ROWS), pl.multiple_of(rows, BF16_ROWS), slot).start()

    def wait_tile(copy, rows, slot):
        copy(0, pl.multiple_of(rows, BF16_ROWS), slot).wait()

    @pl.when(e == 0)
    def _():
        xbuf_ref[...] = jnp.zeros_like(xbuf_ref)
        zero_ref[...] = jnp.zeros_like(zero_ref)
        _zero_rows(zero_ref, os_ref, zsem, used_ref[0], start=True)
        state_ref[0] = 0
        state_ref[1] = 0
        state_ref[2] = 0
        state_ref[3] = 0

    done = state_ref[0]

    @pl.when(jnp.logical_and(n_tiles > 0, state_ref[1] == 0))
    def _():
        start_tile(in_copy, base, rows_of(n, 0), done & 1)

    nxt = jnp.minimum(e + 1, ne - 1)
    has_next = jnp.logical_and(e + 1 < ne, gsize_ref[nxt] > 0)

    def step(t, carry):
        slot = (done + t) & 1

        @pl.when(t + 1 < n_tiles)
        def _():
            start_tile(in_copy, base + (t + 1) * tile, rows_of(n, t + 1), 1 - slot)

        @pl.when(jnp.logical_and(t + 1 == n_tiles, has_next))
        def _():
            start_tile(in_copy, gstart_ref[nxt], rows_of(gsize_ref[nxt], 0), 1 - slot)

        rows = rows_of(n, t)
        wait_tile(in_copy, rows, slot)

        @pl.when(state_ref[2 + slot] > 0)
        def _():
            wait_tile(out_copy, state_ref[2 + slot], slot)

        def ffn(r0, m):
            gu = _dot(xbuf_ref[slot, r0:r0 + m, :], wgu_ref[...].astype(bf16)) + bgu_ref[...]
            gate = jnp.minimum(gu[:, :D_FF], SWIGLU_LIMIT)
            lin = jnp.clip(gu[:, D_FF:], -SWIGLU_LIMIT, SWIGLU_LIMIT)
            act = (lin + 1.0) * (gate * jax.nn.sigmoid(SWIGLU_ALPHA * gate))
            out = _dot(act.astype(bf16), wd_ref[...].astype(bf16)) + bd_ref[...]
            obuf_ref[slot, r0:r0 + m, :] = out.astype(bf16)

        half, quarter, eighth = tile // 2, tile // 4, tile // 8
        for lo, hi, chains in ((half + quarter, tile, ((0, half), (half, half))),
                               (half + eighth, half + quarter, ((0, half), (half, quarter))),
                               (half, half + eighth, ((0, half), (half, eighth))),
                               (quarter, half, ((0, half),)),
                               (eighth, quarter, ((0, quarter),)),
                               (0, eighth, ((0, eighth),))):
            @pl.when(jnp.logical_and(rows > lo, rows <= hi))
            def _():
                for r0, m in chains:
                    ffn(r0, m)

        start_tile(out_copy, base + t * tile, rows, slot)
        state_ref[2 + slot] = rows
        return carry

    lax.fori_loop(0, n_tiles, step, 0)

    @pl.when(n_tiles > 0)
    def _():
        state_ref[0] = done + n_tiles
        state_ref[1] = has_next.astype(jnp.int32)

    @pl.when(e == ne - 1)
    def _():
        for slot in range(2):
            @pl.when(state_ref[2 + slot] > 0)
            def _():
                wait_tile(out_copy, state_ref[2 + slot], slot)
        _zero_rows(zero_ref, os_ref, zsem, used_ref[0], start=False)


def _ffn(gstart, gsize, used, xs, wgu, bgu, wd, bd):
    rows_total = xs.shape[0]
    return pl.pallas_call(
        _ffn_body,
        grid_spec=pltpu.PrefetchScalarGridSpec(
            num_scalar_prefetch=3,
            grid=(N_EXPERTS,),
            in_specs=[pl.BlockSpec(memory_space=pl.ANY),
                      pl.BlockSpec((None, D_MODEL, 2 * D_FF), lambda e, *_: (e, 0, 0)),
                      pl.BlockSpec((None, 1, 2 * D_FF), lambda e, *_: (e, 0, 0)),
                      pl.BlockSpec((None, D_FF, D_MODEL), lambda e, *_: (e, 0, 0)),
                      pl.BlockSpec((None, 1, D_MODEL), lambda e, *_: (e, 0, 0))],
            out_specs=pl.BlockSpec(memory_space=pl.ANY),
            scratch_shapes=[pltpu.VMEM((2, FFN_TM, D_MODEL), bf16),
                            pltpu.VMEM((2, FFN_TM, D_MODEL), bf16),
                            pltpu.VMEM((FFN_TM, D_MODEL), bf16),
                            pltpu.SMEM((4,), jnp.int32),
                            pltpu.SemaphoreType.DMA((2,)), pltpu.SemaphoreType.DMA((2,)),
                            pltpu.SemaphoreType.DMA],
        ),
        out_shape=jax.ShapeDtypeStruct((rows_total, D_MODEL), bf16),
        compiler_params=pltpu.CompilerParams(
            dimension_semantics=("arbitrary",), vmem_limit_bytes=VMEM_LIMIT),
        name="experts",
    )(gstart, gsize, used, xs, wgu, bgu, wd, bd)


def _combine_body(final_norm, voff_ref, row0_ref, rlen_ref, rtot_ref,
                  os_ref, meta_ref, base_ref, meta_next_ref, base_next_ref, x1_ref, gfin_ref,
                  out_ref, oc0_ref, oc1_ref, p0_ref, p1_ref, sem):
    i = pl.program_id(0)
    nt = pl.num_programs(0)
    tm = x1_ref.shape[0]
    lmax = oc0_ref.shape[0]
    slot = i & 1

    def slots_and_gates(m_ref, b_ref):
        meta = m_ref[...]
        slots = [s.astype(jnp.int32) for s in _token_slots(meta, b_ref[0:1, :])]
        gates = [meta[:, _META_GATE + kk:_META_GATE + kk + 1] for kk in range(TOP_K)]
        return slots, gates

    def gate_cols(slots, gates, c):
        col = lax.broadcasted_iota(jnp.int32, (tm, _CHUNK), 1) + c * _CHUNK
        p = jnp.zeros((tm, _CHUNK), f32)
        for s, g in zip(slots, gates):
            p = jnp.where(col == s, g, p)
        return p.astype(bf16)

    def fetch(tile, into_ref, sem_slot, rows_of):
        for e in range(N_EXPERTS):
            j = tile * N_EXPERTS + e
            _segment_copy(os_ref, into_ref, sem.at[sem_slot], row0_ref[j], voff_ref[j],
                          rows_of(rlen_ref[j])).start()

    @pl.when(i == 0)
    def _():
        oc0_ref[...] = jnp.zeros_like(oc0_ref)
        oc1_ref[...] = jnp.zeros_like(oc1_ref)
        fetch(0, oc0_ref, 0, lambda rows: rows)
        slots, gates = slots_and_gates(meta_ref, base_ref)
        for c in range(lmax // _CHUNK):
            p0_ref[:, c * _CHUNK:(c + 1) * _CHUNK] = gate_cols(slots, gates, c)

    _wait_rows(os_ref, oc0_ref, sem.at[slot], rtot_ref[i])

    def request_next(oc_next_ref):
        fetch(jnp.minimum(i + 1, nt - 1), oc_next_ref, 1 - slot,
              lambda rows: jnp.where(i + 1 < nt, rows, 0))

    @pl.when(slot == 0)
    def _():
        request_next(oc1_ref)

    @pl.when(slot == 1)
    def _():
        request_next(oc0_ref)

    def restore(p_now_ref, p_next_ref, oc_now_ref):
        slots, gates = slots_and_gates(meta_next_ref, base_next_ref)
        p_now = p_now_ref[...]
        n_build = lmax // _CHUNK
        n_out = D_MODEL // _CHUNK
        ssq = jnp.zeros((tm, 1), f32)
        for n in range(n_out):
            cols = slice(n * _CHUNK, (n + 1) * _CHUNK)
            xo = x1_ref[:, cols] + _dot(p_now, oc_now_ref[:, cols])
            out_ref[:, cols] = xo
            ssq = ssq + jnp.sum(xo * xo, axis=-1, keepdims=True)
            for c in range(n * n_build // n_out, (n + 1) * n_build // n_out):
                p_next_ref[:, c * _CHUNK:(c + 1) * _CHUNK] = gate_cols(slots, gates, c)
        if final_norm:
            out_ref[...] = out_ref[...] * lax.rsqrt(ssq * (1.0 / D_MODEL) + EPS) * gfin_ref[...]

    @pl.when(slot == 0)
    def _():
        restore(p0_ref, p1_ref, oc0_ref)

    @pl.when(slot == 1)
    def _():
        restore(p1_ref, p0_ref, oc1_ref)


def _combine(plan, os, meta, x1, gfin, final_norm):
    tokens = x1.shape[0]
    tm = min(MIX_TS, tokens)
    nt = tokens // tm
    scalars = (plan["voff"], plan["row0"], plan["rlen"], plan["rtot"])
    return pl.pallas_call(
        functools.partial(_combine_body, final_norm),
        grid_spec=pltpu.PrefetchScalarGridSpec(
            num_scalar_prefetch=len(scalars),
            grid=(nt,),
            in_specs=[pl.BlockSpec(memory_space=pl.ANY),
                      pl.BlockSpec((tm, LANES), lambda i, *_: (i, 0)),
                      pl.BlockSpec((8, LANES), lambda i, *_: (i, 0)),
                      pl.BlockSpec((tm, LANES), lambda i, *_: (jnp.minimum(i + 1, nt - 1), 0)),
                      pl.BlockSpec((8, LANES), lambda i, *_: (jnp.minimum(i + 1, nt - 1), 0)),
                      pl.BlockSpec((tm, D_MODEL), lambda i, *_: (i, 0)),
                      pl.BlockSpec((1, D_MODEL), lambda i, *_: (0, 0))],
            out_specs=pl.BlockSpec((tm, D_MODEL), lambda i, *_: (i, 0)),
            scratch_shapes=[pltpu.VMEM((_lmax(tm), D_MODEL), bf16),
                            pltpu.VMEM((_lmax(tm), D_MODEL), bf16),
                            pltpu.VMEM((tm, _lmax(tm)), bf16),
                            pltpu.VMEM((tm, _lmax(tm)), bf16),
                            pltpu.SemaphoreType.DMA((2,))],
        ),
        out_shape=jax.ShapeDtypeStruct((tokens, D_MODEL), f32),
        compiler_params=pltpu.CompilerParams(
            dimension_semantics=("arbitrary",), vmem_limit_bytes=VMEM_LIMIT),
        name="combine",
    )(*scalars, os, meta, plan["base"], meta, plan["base"], x1, gfin)


def _plan(counts, nt):
    up = lambda v: (v + (BF16_ROWS - 1)) // BF16_ROWS * BF16_ROWS
    down = lambda v: v // BF16_ROWS * BF16_ROWS
    c = counts.reshape(nt, 8, LANES)[:, 0, :N_EXPERTS]
    gsize = up(jnp.sum(c, axis=0))
    gstart = jnp.cumsum(gsize) - gsize
    before = jnp.cumsum(c, axis=0) - c
    a = before % BF16_ROWS
    length = a + c
    voff = jnp.cumsum(up(length), axis=1) - up(length)
    last = jnp.arange(nt)[:, None] == nt - 1
    wlen = jnp.where(last, up(length), down(length))
    rlen = jnp.where(c > 0, up(length), 0)
    base = jnp.zeros((nt, 8, LANES), jnp.float32)
    base = base.at[:, :, :N_EXPERTS].set((voff + a).astype(jnp.float32)[:, None, :])
    i32 = lambda v: v.reshape(-1).astype(jnp.int32)
    return dict(
        voff=i32(voff), coff=i32(voff + down(length)), keep=i32(length % BF16_ROWS),
        row0=i32(gstart[None, :] + before - a), wlen=i32(wlen), wtot=i32(jnp.sum(wlen, axis=1)),
        rlen=i32(rlen), rtot=i32(jnp.sum(rlen, axis=1)), base=base.reshape(nt * 8, LANES),
        gstart=i32(gstart), gsize=i32(gsize), used=i32(jnp.sum(gsize, keepdims=True)))


def kernel(x, g_mix, w_in, w_alpha_up, b_alpha, g_gla, w_conv, w_branch_a, w_branch_b, w_out, g_ffn, w_router, b_router, w_gate_up, b_gate_up, w_down, b_down, g_final):
    batch, seq, _ = x.shape
    tokens = batch * seq
    depth = w_in.shape[0]
    x2 = x.reshape(tokens, D_MODEL)
    nt = tokens // min(MIX_TS, seq)
    rows_total = tokens * TOP_K + N_EXPERTS * (BF16_ROWS - 1)
    rows_total = -(-rows_total // BF16_ROWS) * BF16_ROWS

    for l in range(depth):
        wint = jnp.swapaxes(w_in[l], 0, 1).astype(bf16)
        wup = jnp.pad(w_alpha_up[l], ((0, LANES - GLA_RANK), (0, 0))).astype(bf16)
        wr = jnp.pad(w_router[l], ((0, 0), (0, LANES - N_EXPERTS)))
        wr = wr.astype(bf16)
        br = jnp.pad(b_router[l], (0, LANES - N_EXPERTS)).reshape(1, LANES)
        wconv = jnp.pad(w_conv[l], ((0, 8 - CONV_K), (0, 0)))

        x1, h2, logits = _mixer(
            x2.reshape(batch, seq, D_MODEL), g_mix[l].reshape(1, D_MODEL), wint, wup,
            b_alpha[l].reshape(1, QK_W), g_gla[l].reshape(1, V_W), wconv,
            w_branch_a[l], w_branch_b[l], w_out[l],
            g_ffn[l].reshape(1, D_MODEL), wr, br)

        meta, counts = _route(logits, min(MIX_TS, seq))
        plan = _plan(counts, nt)
        xs = _dispatch(plan, h2, meta, rows_total)
        os = _ffn(plan["gstart"], plan["gsize"], plan["used"], xs,
                  w_gate_up[l], b_gate_up[l].reshape(N_EXPERTS, 1, 2 * D_FF),
                  w_down[l], b_down[l].reshape(N_EXPERTS, 1, D_MODEL))
        x2 = _combine(plan, os, meta, x1, g_final.reshape(1, D_MODEL), l == depth - 1)
    return x2.reshape(batch, seq, D_MODEL)
```

```python
import functools

import jax
import jax.numpy as jnp
from jax import lax
from jax.experimental import pallas as pl
from jax.experimental.pallas import tpu as pltpu

D_MODEL = 1024
GLA_HEADS = 4
GLA_DK = 128
GLA_DV = 256
GLA_RANK = 16
GLA_TAU = 16.0
GLA_CHUNK = 64
_CHUNK_SHIFT = GLA_CHUNK.bit_length() - 1
CONV_K = 3
N_EXPERTS = 32
TOP_K = 4
D_FF = 1024
SWIGLU_LIMIT = 7.0
SWIGLU_ALPHA = 1.702
EPS = 1e-6

QK_W = GLA_HEADS * GLA_DK
V_W = GLA_HEADS * GLA_DV

LANES = 128
BF16_ROWS = 16
VMEM_LIMIT = 56 * 1024 * 1024

MIX_TS = 256
MIX_ROWS = 2
ROUTE_ROWS = 1024
_CHUNK = 256
FFN_TM = 1024
_META_EXPERT, _META_GATE, _META_RANK = 0, TOP_K, 2 * TOP_K

_C_Q, _C_K, _C_V, _C_R = 0, QK_W, 2 * QK_W, 2 * QK_W + V_W
_C_CB = _C_R + V_W
_C_CC = _C_CB + D_MODEL
_C_CX = _C_CC + D_MODEL
_C_GA = _C_CX + D_MODEL
_C_GB = _C_GA + D_MODEL

f32 = jnp.float32
bf16 = jnp.bfloat16


def _rms(x, g):
    return x * lax.rsqrt(jnp.mean(x * x, axis=-1, keepdims=True) + EPS) * g


def _dot(a, b):
    return jnp.dot(a, b, preferred_element_type=f32)


def _split_bf16(a):
    hi = a.astype(bf16)
    lo = (a - hi.astype(f32)).astype(bf16)
    return hi, lo


def _mixer_body(x_ref, gmix_ref, wint_ref, wup_ref, balpha_ref, ggla_ref, wconv_ref,
                wa_ref, wb_ref, wo_ref, gffn_ref, wr_ref, br_ref,
                x1_ref, h2_ref, logit_ref,
                state_ref, ubuf_ref, obuf_ref):
    nb, ts = x_ref.shape[0], x_ref.shape[1]
    m = nb * ts
    nchunk = ts // GLA_CHUNK
    seqs = [slice(ch * ts, (ch + 1) * ts) for ch in range(nb)]

    @pl.when(pl.program_id(1) == 0)
    def _():
        state_ref[...] = jnp.zeros_like(state_ref)
        ubuf_ref[:, 0:8, :] = jnp.zeros((nb, 8, D_MODEL), f32)

    x = x_ref[...].reshape(m, D_MODEL)
    hb = _rms(x, gmix_ref[...]).astype(bf16)

    def dot_t(a, w_rows):
        return lax.dot_general(a, w_rows, (((1,), (1,)), ((), ())), preferred_element_type=f32)

    def proj(c0, width):
        r0 = c0 if c0 < _C_CB else c0 + GLA_RANK
        return dot_t(hb, wint_ref[r0:r0 + width, :])

    q = proj(_C_Q, QK_W)
    k = proj(_C_K, QK_W)
    vb = proj(_C_V, V_W).astype(bf16)

    a_lr = dot_t(hb, wint_ref[_C_CB:_C_CB + LANES, :])
    pre = _dot(a_lr.astype(bf16), wup_ref[...]) + balpha_ref[...]
    log_a = -(jnp.maximum(-pre, 0.0) + jnp.log1p(jnp.exp(-jnp.abs(pre)))) * (1.0 / GLA_TAU)
    ri = lax.broadcasted_iota(jnp.int32, (ts, ts), 0)
    ci = lax.broadcasted_iota(jnp.int32, (ts, ts), 1)
    same_chunk = (ri >> _CHUNK_SHIFT) == (ci >> _CHUNK_SHIFT)
    causal = jnp.logical_and(same_chunk, ci <= ri)
    tri = jnp.where(causal, 1.0, 0.0).astype(bf16)
    la_hi, la_lo = _split_bf16(log_a)
    b = jnp.concatenate([_dot(tri, la_hi[sq]) + _dot(tri, la_lo[sq]) for sq in seqs], axis=0)
    lasts = [b[(c + 1) * GLA_CHUNK - 1:(c + 1) * GLA_CHUNK, :] for c in range(nb * nchunk)]
    b_tot = jnp.concatenate([jnp.broadcast_to(r, (GLA_CHUNK, QK_W)) for r in lasts], axis=0)
    e_pos = jnp.exp(b)
    e_neg = jnp.exp(-b)
    e_tail = jnp.exp(b_tot - b)
    e_tot = [jnp.exp(r) for r in lasts]

    q_dec = (q * (GLA_DK ** -0.5) * e_pos).astype(bf16)
    k_inv = (k * e_neg).astype(bf16)
    k_tail = (k * e_tail).astype(bf16)

    for ch, sq in enumerate(seqs):
        for hh in range(GLA_HEADS):
            qs = slice(hh * GLA_DK, (hh + 1) * GLA_DK)
            vs = slice(hh * GLA_DV, (hh + 1) * GLA_DV)
            sc = lax.dot_general(q_dec[sq, qs], k_inv[sq, qs], (((1,), (1,)), ((), ())),
                                 preferred_element_type=f32)
            sc = jnp.where(causal, sc, 0.0).astype(bf16)
            obuf_ref[sq, vs] = _dot(sc, vb[sq, vs])
            st = state_ref[ch, hh]
            for c in range(nchunk):
                rs = slice(ch * ts + c * GLA_CHUNK, ch * ts + (c + 1) * GLA_CHUNK)
                o_inter = lax.dot_general(q_dec[rs, qs], st.astype(bf16), (((1,), (1,)), ((), ())),
                                          preferred_element_type=f32)
                obuf_ref[rs, vs] += o_inter
                upd = lax.dot_general(vb[rs, vs], k_tail[rs, qs], (((0,), (0,)), ((), ())),
                                      preferred_element_type=f32)
                st = st * e_tot[ch * nchunk + c][:, qs] + upd
            state_ref[ch, hh] = st

    r = proj(_C_R, V_W)
    ggla = ggla_ref[...]
    o_parts = []
    for hh in range(GLA_HEADS):
        vs = slice(hh * GLA_DV, (hh + 1) * GLA_DV)
        o_h = _rms(obuf_ref[:, vs], ggla[:, vs])
        r_h = r[:, vs]
        o_parts.append((o_h * (r_h * jax.nn.sigmoid(r_h))).astype(bf16))
    y_a = _dot(jnp.concatenate(o_parts, axis=1), wa_ref[...].astype(bf16))

    u = proj(_C_CC, D_MODEL) * proj(_C_CX, D_MODEL)
    wc = wconv_ref[...]
    y_convs = []
    for ch, sq in enumerate(seqs):
        ubuf_ref[ch, 8:8 + ts, :] = u[sq]
        y_convs.append(wc[0:1, :] * ubuf_ref[ch, 6:6 + ts, :] + wc[1:2, :] * ubuf_ref[ch, 7:7 + ts, :]
                       + wc[2:3, :] * u[sq])
        ubuf_ref[ch, 0:8, :] = ubuf_ref[ch, ts:ts + 8, :]
    y_conv = jnp.concatenate(y_convs, axis=0)
    y_b = _dot((proj(_C_CB, D_MODEL) * y_conv).astype(bf16), wb_ref[...].astype(bf16))

    mixed = (jax.nn.sigmoid(proj(_C_GA, D_MODEL)) * y_a
             + jax.nn.sigmoid(proj(_C_GB, D_MODEL)) * y_b)
    x1 = x + _dot(mixed.astype(bf16), wo_ref[...].astype(bf16))
    x1_ref[...] = x1.reshape(nb, ts, D_MODEL)

    h2 = _rms(x1, gffn_ref[...]).astype(bf16)
    h2_ref[...] = h2.reshape(nb, ts, D_MODEL)
    logits = _dot(h2, wr_ref[...]) + br_ref[...]
    logit_ref[...] = logits.reshape(nb, ts, LANES)


def _mixer(x3, gmix, wint, wup, balpha, ggla, wconv, wa, wb, wo, gffn, wr, br):
    batch, seq, _ = x3.shape
    ts = min(MIX_TS, seq)
    ns = seq // ts
    tokens = batch * seq
    nb = MIX_ROWS if batch % MIX_ROWS == 0 else 1

    def const(shape):
        return pl.BlockSpec(shape, lambda b, s: (0,) * len(shape), pipeline_mode=pl.Buffered(1))

    def row(width):
        return pl.BlockSpec((nb, ts, width), lambda b, s: (b, s, 0))

    x1, h2, logits = pl.pallas_call(
        _mixer_body,
        grid=(batch // nb, ns),
        in_specs=[row(D_MODEL), const(gmix.shape), const(wint.shape),
                  const(wup.shape), const(balpha.shape), const(ggla.shape), const(wconv.shape),
                  const(wa.shape), const(wb.shape), const(wo.shape), const(gffn.shape),
                  const(wr.shape), const(br.shape)],
        out_specs=[row(D_MODEL), row(D_MODEL), row(LANES)],
        out_shape=[jax.ShapeDtypeStruct((batch, seq, D_MODEL), f32),
                   jax.ShapeDtypeStruct((batch, seq, D_MODEL), bf16),
                   jax.ShapeDtypeStruct((batch, seq, LANES), f32)],
        scratch_shapes=[pltpu.VMEM((nb, GLA_HEADS, GLA_DV, GLA_DK), f32),
                        pltpu.VMEM((nb, ts + 8, D_MODEL), f32),
                        pltpu.VMEM((nb * ts, V_W), f32)],
        compiler_params=pltpu.CompilerParams(
            dimension_semantics=("parallel", "arbitrary"), vmem_limit_bytes=VMEM_LIMIT),
        name="mixer",
    )(x3, gmix, wint, wup, balpha, ggla, wconv, wa, wb, wo, gffn, wr, br)
    return (x1.reshape(tokens, D_MODEL), h2.reshape(tokens, D_MODEL),
            logits.reshape(tokens, LANES))


def _route_body(tile, logit_ref, meta_ref, cnt_ref):
    rows = logit_ref.shape[0]
    lane = lax.broadcasted_iota(jnp.int32, (rows, LANES), 1)
    lane_f = lane.astype(f32)
    lg = jnp.where(lane < N_EXPERTS, logit_ref[...], -jnp.inf)
    sels, tops, firsts = [], [], []
    for _ in range(TOP_K):
        m = jnp.max(lg, axis=-1, keepdims=True)
        first = jnp.min(jnp.where(lg == m, lane_f, float(LANES)), axis=-1, keepdims=True)
        sel = lane_f == first
        sels.append(sel)
        tops.append(m)
        firsts.append(first)
        lg = jnp.where(sel, -jnp.inf, lg)
    ps = [jnp.exp(t - tops[0]) for t in tops]
    denom = ps[0] + ps[1] + ps[2] + ps[3]
    onehot = jnp.zeros((rows, LANES), f32)
    for sel in sels:
        onehot = onehot + jnp.where(sel, 1.0, 0.0)
    onehot_b = onehot.astype(bf16)
    ri = lax.broadcasted_iota(jnp.int32, (tile, tile), 0)
    ci = lax.broadcasted_iota(jnp.int32, (tile, tile), 1)
    strict_lower = jnp.where(ci < ri, 1.0, 0.0).astype(bf16)
    ones = jnp.ones((8, tile), bf16)
    ranks = []
    for j in range(rows // tile):
        part = onehot_b[j * tile:(j + 1) * tile, :]
        ranks.append(_dot(strict_lower, part))
        cnt_ref[j * 8:(j + 1) * 8, :] = _dot(ones, part).astype(jnp.int32)
    rank = jnp.concatenate(ranks, axis=0)
    meta = jnp.zeros((rows, LANES), f32)
    all_ones = jnp.ones((LANES, LANES), bf16)
    for kk in range(TOP_K):
        meta = jnp.where(lane == _META_EXPERT + kk, firsts[kk], meta)
        meta = jnp.where(lane == _META_GATE + kk, ps[kk] / denom, meta)
        rank_k = _dot(jnp.where(sels[kk], rank, 0.0).astype(bf16), all_ones)
        meta = jnp.where(lane == _META_RANK + kk, rank_k, meta)
    meta_ref[...] = meta


def _route(logits, tile):
    tokens = logits.shape[0]
    rows = min(ROUTE_ROWS, tokens)
    steps = tokens // rows
    per_step = rows // tile
    return pl.pallas_call(
        functools.partial(_route_body, tile),
        grid=(steps,),
        in_specs=[pl.BlockSpec((rows, LANES), lambda i: (i, 0))],
        out_specs=[pl.BlockSpec((rows, LANES), lambda i: (i, 0)),
                   pl.BlockSpec((per_step * 8, LANES), lambda i: (i, 0))],
        out_shape=[jax.ShapeDtypeStruct((tokens, LANES), f32),
                   jax.ShapeDtypeStruct((tokens // tile * 8, LANES), jnp.int32)],
        compiler_params=pltpu.CompilerParams(
            dimension_semantics=("parallel",), vmem_limit_bytes=VMEM_LIMIT),
        name="route",
    )(logits)


def _lmax(tm):
    full = tm * TOP_K + N_EXPERTS * 2 * (BF16_ROWS - 1)
    return -(-full // 256) * 256


def _token_slots(meta, base_row):
    lane_f = lax.broadcasted_iota(jnp.int32, meta.shape, 1).astype(f32)
    slots = []
    for kk in range(TOP_K):
        sel = lane_f == meta[:, _META_EXPERT + kk:_META_EXPERT + kk + 1]
        first = jnp.sum(jnp.where(sel, base_row, 0.0), axis=-1, keepdims=True)
        slots.append(first + meta[:, _META_RANK + kk:_META_RANK + kk + 1])
    return slots


def _segment_copy(src, dst, sem, src_off, dst_off, rows):
    src_off = pl.multiple_of(src_off, BF16_ROWS)
    dst_off = pl.multiple_of(dst_off, BF16_ROWS)
    rows = pl.multiple_of(rows, BF16_ROWS)
    return pltpu.make_async_copy(src.at[pl.ds(src_off, rows)], dst.at[pl.ds(dst_off, rows)], sem)


def _wait_rows(src, dst, sem, total):
    @pl.when(total > 0)
    def _():
        _segment_copy(src, dst, sem, 0, 0, total).wait()


def _zero_rows(zero_ref, dst_ref, sem, first, start):
    tile = zero_ref.shape[0]
    n = dst_ref.shape[0] - first
    n_full = n // tile

    def full(j):
        return _segment_copy(zero_ref, dst_ref, sem, 0, first + j * tile, tile)

    def rest():
        return _segment_copy(zero_ref, dst_ref, sem, 0, first + n_full * tile, n - n_full * tile)

    def step(j, carry):
        full(j).start() if start else full(j).wait()
        return carry

    lax.fori_loop(0, n_full, step, 0)

    @pl.when(n - n_full * tile > 0)
    def _():
        rest().start() if start else rest().wait()


def _dispatch_body(voff_ref, coff_ref, keep_ref, row0_ref, wlen_ref, wtot_ref, used_ref,
                   h2_ref, meta_ref, base_ref, meta_next_ref, base_next_ref, xs_ref,
                   xc0_ref, xc1_ref, pt0_ref, pt1_ref, carry_ref, zero_ref, sem, zsem):
    i = pl.program_id(0)
    nt = pl.num_programs(0)
    tm = h2_ref.shape[0]
    lmax = xc0_ref.shape[0]
    slot = i & 1

    def slot_rows(m_ref, b_ref):
        meta = m_ref[...]
        lane = lax.broadcasted_iota(jnp.int32, meta.shape, 1)
        slots = jnp.zeros(meta.shape, f32)
        for kk, s in enumerate(_token_slots(meta, b_ref[0:1, :])):
            slots = jnp.where(lane == kk, s, slots)
        return slots.T[0:8, :].astype(jnp.int32)

    def one_hot_rows(slots_t, c):
        row = lax.broadcasted_iota(jnp.int32, (_CHUNK, tm), 0) + c * _CHUNK
        pt = jnp.zeros((_CHUNK, tm), f32)
        for kk in range(TOP_K):
            pt = jnp.where(row == slots_t[kk:kk + 1, :], 1.0, pt)
        return pt.astype(bf16)

    @pl.when(i == 0)
    def _():
        carry_ref[...] = jnp.zeros_like(carry_ref)
        zero_ref[...] = jnp.zeros_like(zero_ref)
        _zero_rows(zero_ref, xs_ref, zsem, used_ref[0], start=True)
        first = slot_rows(meta_ref, base_ref)
        for c in range(lmax // _CHUNK):
            pt0_ref[c * _CHUNK:(c + 1) * _CHUNK, :] = one_hot_rows(first, c)

    @pl.when(i >= 2)
    def _():
        _wait_rows(xc0_ref, xs_ref, sem.at[slot], wtot_ref[jnp.maximum(i - 2, 0)])

    def merge_carry(t, buf):
        def block(off):
            return buf.at[pl.ds(pl.multiple_of(off, BF16_ROWS), BF16_ROWS)]

        for e in range(N_EXPERTS):
            blk = block(voff_ref[t * N_EXPERTS + e])
            blk[...] = blk[...] + carry_ref[e]
        for e in range(N_EXPERTS):
            j = t * N_EXPERTS + e
            tail = block(coff_ref[j])[...]
            carry_ref[e] = jnp.where(keep_ref[j] > 0, tail, jnp.zeros_like(tail))

    def copy_out(t, buf):
        for e in range(N_EXPERTS):
            j = t * N_EXPERTS + e
            _segment_copy(buf, xs_ref, sem.at[slot], voff_ref[j], row0_ref[j], wlen_ref[j]).start()

    def compact(pt_now_ref, pt_next_ref, xc_now_ref):
        h2 = h2_ref[...]
        nxt = slot_rows(meta_next_ref, base_next_ref)
        for c in range(lmax // _CHUNK):
            rows = slice(c * _CHUNK, (c + 1) * _CHUNK)
            xc_now_ref[rows, :] = _dot(pt_now_ref[rows, :], h2).astype(bf16)
            pt_next_ref[rows, :] = one_hot_rows(nxt, c)
        merge_carry(i, xc_now_ref)
        copy_out(i, xc_now_ref)

    @pl.when(slot == 0)
    def _():
        compact(pt0_ref, pt1_ref, xc0_ref)

    @pl.when(slot == 1)
    def _():
        compact(pt1_ref, pt0_ref, xc1_ref)

    @pl.when(i == nt - 1)
    def _():
        @pl.when(i >= 1)
        def _():
            _wait_rows(xc0_ref, xs_ref, sem.at[1 - slot], wtot_ref[jnp.maximum(i - 1, 0)])
        _wait_rows(xc0_ref, xs_ref, sem.at[slot], wtot_ref[i])
        _zero_rows(zero_ref, xs_ref, zsem, used_ref[0], start=False)


def _dispatch(plan, h2, meta, rows_total):
    tokens = h2.shape[0]
    tm = min(MIX_TS, tokens)
    nt = tokens // tm
    scalars = (plan["voff"], plan["coff"], plan["keep"], plan["row0"], plan["wlen"], plan["wtot"],
               plan["used"])
    return pl.pallas_call(
        _dispatch_body,
        grid_spec=pltpu.PrefetchScalarGridSpec(
            num_scalar_prefetch=len(scalars),
            grid=(nt,),
            in_specs=[pl.BlockSpec((tm, D_MODEL), lambda i, *_: (i, 0)),
                      pl.BlockSpec((tm, LANES), lambda i, *_: (i, 0)),
                      pl.BlockSpec((8, LANES), lambda i, *_: (i, 0)),
                      pl.BlockSpec((tm, LANES), lambda i, *_: (jnp.minimum(i + 1, nt - 1), 0)),
                      pl.BlockSpec((8, LANES), lambda i, *_: (jnp.minimum(i + 1, nt - 1), 0))],
            out_specs=pl.BlockSpec(memory_space=pl.ANY),
            scratch_shapes=[pltpu.VMEM((_lmax(tm), D_MODEL), bf16),
                            pltpu.VMEM((_lmax(tm), D_MODEL), bf16),
                            pltpu.VMEM((_lmax(tm), tm), bf16),
                            pltpu.VMEM((_lmax(tm), tm), bf16),
                            pltpu.VMEM((N_EXPERTS, BF16_ROWS, D_MODEL), bf16),
                            pltpu.VMEM((FFN_TM, D_MODEL), bf16),
                            pltpu.SemaphoreType.DMA((2,)), pltpu.SemaphoreType.DMA],
        ),
        out_shape=jax.ShapeDtypeStruct((rows_total, D_MODEL), bf16),
        compiler_params=pltpu.CompilerParams(
            dimension_semantics=("arbitrary",), vmem_limit_bytes=VMEM_LIMIT),
        name="dispatch",
    )(*scalars, h2, meta, plan["base"], meta, plan["base"])


def _ffn_body(gstart_ref, gsize_ref, used_ref, xs_ref, wgu_ref, bgu_ref, wd_ref, bd_ref, os_ref,
              xbuf_ref, obuf_ref, zero_ref, state_ref, isem, osem, zsem):
    e = pl.program_id(0)
    ne = pl.num_programs(0)
    tile = xbuf_ref.shape[1]
    base = gstart_ref[e]
    n = gsize_ref[e]
    n_tiles = (n + tile - 1) // tile

    def rows_of(size, t):
        return jnp.minimum(tile, size - t * tile)

    def in_copy(row0, rows, slot):
        return pltpu.make_async_copy(xs_ref.at[pl.ds(row0, rows)],
                                     xbuf_ref.at[slot, pl.ds(0, rows)], isem.at[slot])

    def out_copy(row0, rows, slot):
        return pltpu.make_async_copy(obuf_ref.at[slot, pl.ds(0, rows)],
                                     os_ref.at[pl.ds(row0, rows)], osem.at[slot])

    def start_tile(copy, row0, rows, slot):
        copy(pl.multiple_of(row0, BF16_ROWS), pl.multiple_of(rows, BF16_ROWS), slot).start()

    def wait_tile(copy, rows, slot):
        copy(0, pl.multiple_of(rows, BF16_ROWS), slot).wait()

    @pl.when(e == 0)
    def _():
        xbuf_ref[...] = jnp.zeros_like(xbuf_ref)
        zero_ref[...] = jnp.zeros_like(zero_ref)
        _zero_rows(zero_ref, os_ref, zsem, used_ref[0], start=True)
        state_ref[0] = 0
        state_ref[1] = 0
        state_ref[2] = 0
        state_ref[3] = 0

    done = state_ref[0]

    @pl.when(jnp.logical_and(n_tiles > 0, state_ref[1] == 0))
    def _():
        start_tile(in_copy, base, rows_of(n, 0), done & 1)

    nxt = jnp.minimum(e + 1, ne - 1)
    has_next = jnp.logical_and(e + 1 < ne, gsize_ref[nxt] > 0)

    def step(t, carry):
        slot = (done + t) & 1

        @pl.when(t + 1 < n_tiles)
        def _():
            start_tile(in_copy, base + (t + 1) * tile, rows_of(n, t + 1), 1 - slot)

        @pl.when(jnp.logical_and(t + 1 == n_tiles, has_next))
        def _():
            start_tile(in_copy, gstart_ref[nxt], rows_of(gsize_ref[nxt], 0), 1 - slot)

        rows = rows_of(n, t)
        wait_tile(in_copy, rows, slot)

        @pl.when(state_ref[2 + slot] > 0)
        def _():
            wait_tile(out_copy, state_ref[2 + slot], slot)

        def ffn(r0, m):
            gu = _dot(xbuf_ref[slot, r0:r0 + m, :], wgu_ref[...].astype(bf16)) + bgu_ref[...]
            gate = jnp.minimum(gu[:, :D_FF], SWIGLU_LIMIT)
            lin = jnp.clip(gu[:, D_FF:], -SWIGLU_LIMIT, SWIGLU_LIMIT)
            act = (lin + 1.0) * (gate * jax.nn.sigmoid(SWIGLU_ALPHA * gate))
            out = _dot(act.astype(bf16), wd_ref[...].astype(bf16)) + bd_ref[...]
            obuf_ref[slot, r0:r0 + m, :] = out.astype(bf16)

        half, quarter, eighth = tile // 2, tile // 4, tile // 8
        for lo, hi, chains in ((half + quarter, tile, ((0, half), (half, half))),
                               (half + eighth, half + quarter, ((0, half), (half, quarter))),
                               (half, half + eighth, ((0, half), (half, eighth))),
                               (quarter, half, ((0, half),)),
                               (eighth, quarter, ((0, quarter),)),
                               (0, eighth, ((0, eighth),))):
            @pl.when(jnp.logical_and(rows > lo, rows <= hi))
            def _():
                for r0, m in chains:
                    ffn(r0, m)

        start_tile(out_copy, base + t * tile, rows, slot)
        state_ref[2 + slot] = rows
        return carry

    lax.fori_loop(0, n_tiles, step, 0)

    @pl.when(n_tiles > 0)
    def _():
        state_ref[0] = done + n_tiles
        state_ref[1] = has_next.astype(jnp.int32)

    @pl.when(e == ne - 1)
    def _():
        for slot in range(2):
            @pl.when(state_ref[2 + slot] > 0)
            def _():
                wait_tile(out_copy, state_ref[2 + slot], slot)
        _zero_rows(zero_ref, os_ref, zsem, used_ref[0], start=False)


def _ffn(gstart, gsize, used, xs, wgu, bgu, wd, bd):
    rows_total = xs.shape[0]
    return pl.pallas_call(
        _ffn_body,
        grid_spec=pltpu.PrefetchScalarGridSpec(
            num_scalar_prefetch=3,
            grid=(N_EXPERTS,),
            in_specs=[pl.BlockSpec(memory_space=pl.ANY),
                      pl.BlockSpec((None, D_MODEL, 2 * D_FF), lambda e, *_: (e, 0, 0)),
                      pl.BlockSpec((None, 1, 2 * D_FF), lambda e, *_: (e, 0, 0)),
                      pl.BlockSpec((None, D_FF, D_MODEL), lambda e, *_: (e, 0, 0)),
                      pl.BlockSpec((None, 1, D_MODEL), lambda e, *_: (e, 0, 0))],
            out_specs=pl.BlockSpec(memory_space=pl.ANY),
            scratch_shapes=[pltpu.VMEM((2, FFN_TM, D_MODEL), bf16),
                            pltpu.VMEM((2, FFN_TM, D_MODEL), bf16),
                            pltpu.VMEM((FFN_TM, D_MODEL), bf16),
                            pltpu.SMEM((4,), jnp.int32),
                            pltpu.SemaphoreType.DMA((2,)), pltpu.SemaphoreType.DMA((2,)),
                            pltpu.SemaphoreType.DMA],
        ),
        out_shape=jax.ShapeDtypeStruct((rows_total, D_MODEL), bf16),
        compiler_params=pltpu.CompilerParams(
            dimension_semantics=("arbitrary",), vmem_limit_bytes=VMEM_LIMIT),
        name="experts",
    )(gstart, gsize, used, xs, wgu, bgu, wd, bd)


def _combine_body(final_norm, voff_ref, row0_ref, rlen_ref, rtot_ref,
                  os_ref, meta_ref, base_ref, meta_next_ref, base_next_ref, x1_ref, gfin_ref,
                  out_ref, oc0_ref, oc1_ref, p0_ref, p1_ref, sem):
    i = pl.program_id(0)
    nt = pl.num_programs(0)
    tm = x1_ref.shape[0]
    lmax = oc0_ref.shape[0]
    slot = i & 1

    def slots_and_gates(m_ref, b_ref):
        meta = m_ref[...]
        slots = [s.astype(jnp.int32) for s in _token_slots(meta, b_ref[0:1, :])]
        gates = [meta[:, _META_GATE + kk:_META_GATE + kk + 1] for kk in range(TOP_K)]
        return slots, gates

    def gate_cols(slots, gates, c):
        col = lax.broadcasted_iota(jnp.int32, (tm, _CHUNK), 1) + c * _CHUNK
        p = jnp.zeros((tm, _CHUNK), f32)
        for s, g in zip(slots, gates):
            p = jnp.where(col == s, g, p)
        return p.astype(bf16)

    def fetch(tile, into_ref, sem_slot, rows_of):
        for e in range(N_EXPERTS):
            j = tile * N_EXPERTS + e
            _segment_copy(os_ref, into_ref, sem.at[sem_slot], row0_ref[j], voff_ref[j],
                          rows_of(rlen_ref[j])).start()

    @pl.when(i == 0)
    def _():
        oc0_ref[...] = jnp.zeros_like(oc0_ref)
        oc1_ref[...] = jnp.zeros_like(oc1_ref)
        fetch(0, oc0_ref, 0, lambda rows: rows)
        slots, gates = slots_and_gates(meta_ref, base_ref)
        for c in range(lmax // _CHUNK):
            p0_ref[:, c * _CHUNK:(c + 1) * _CHUNK] = gate_cols(slots, gates, c)

    _wait_rows(os_ref, oc0_ref, sem.at[slot], rtot_ref[i])

    def request_next(oc_next_ref):
        fetch(jnp.minimum(i + 1, nt - 1), oc_next_ref, 1 - slot,
              lambda rows: jnp.where(i + 1 < nt, rows, 0))

    @pl.when(slot == 0)
    def _():
        request_next(oc1_ref)

    @pl.when(slot == 1)
    def _():
        request_next(oc0_ref)

    def restore(p_now_ref, p_next_ref, oc_now_ref):
        slots, gates = slots_and_gates(meta_next_ref, base_next_ref)
        p_now = p_now_ref[...]
        n_build = lmax // _CHUNK
        n_out = D_MODEL // _CHUNK
        ssq = jnp.zeros((tm, 1), f32)
        for n in range(n_out):
            cols = slice(n * _CHUNK, (n + 1) * _CHUNK)
            xo = x1_ref[:, cols] + _dot(p_now, oc_now_ref[:, cols])
            out_ref[:, cols] = xo
            ssq = ssq + jnp.sum(xo * xo, axis=-1, keepdims=True)
            for c in range(n * n_build // n_out, (n + 1) * n_build // n_out):
                p_next_ref[:, c * _CHUNK:(c + 1) * _CHUNK] = gate_cols(slots, gates, c)
        if final_norm:
            out_ref[...] = out_ref[...] * lax.rsqrt(ssq * (1.0 / D_MODEL) + EPS) * gfin_ref[...]

    @pl.when(slot == 0)
    def _():
        restore(p0_ref, p1_ref, oc0_ref)

    @pl.when(slot == 1)
    def _():
        restore(p1_ref, p0_ref, oc1_ref)


def _combine(plan, os, meta, x1, gfin, final_norm):
    tokens = x1.shape[0]
    tm = min(MIX_TS, tokens)
    nt = tokens // tm
    scalars = (plan["voff"], plan["row0"], plan["rlen"], plan["rtot"])
    return pl.pallas_call(
        functools.partial(_combine_body, final_norm),
        grid_spec=pltpu.PrefetchScalarGridSpec(
            num_scalar_prefetch=len(scalars),
            grid=(nt,),
            in_specs=[pl.BlockSpec(memory_space=pl.ANY),
                      pl.BlockSpec((tm, LANES), lambda i, *_: (i, 0)),
                      pl.BlockSpec((8, LANES), lambda i, *_: (i, 0)),
                      pl.BlockSpec((tm, LANES), lambda i, *_: (jnp.minimum(i + 1, nt - 1), 0)),
                      pl.BlockSpec((8, LANES), lambda i, *_: (jnp.minimum(i + 1, nt - 1), 0)),
                      pl.BlockSpec((tm, D_MODEL), lambda i, *_: (i, 0)),
                      pl.BlockSpec((1, D_MODEL), lambda i, *_: (0, 0))],
            out_specs=pl.BlockSpec((tm, D_MODEL), lambda i, *_: (i, 0)),
            scratch_shapes=[pltpu.VMEM((_lmax(tm), D_MODEL), bf16),
                            pltpu.VMEM((_lmax(tm), D_MODEL), bf16),
                            pltpu.VMEM((tm, _lmax(tm)), bf16),
                            pltpu.VMEM((tm, _lmax(tm)), bf16),
                            pltpu.SemaphoreType.DMA((2,))],
        ),
        out_shape=jax.ShapeDtypeStruct((tokens, D_MODEL), f32),
        compiler_params=pltpu.CompilerParams(
            dimension_semantics=("arbitrary",), vmem_limit_bytes=VMEM_LIMIT),
        name="combine",
    )(*scalars, os, meta, plan["base"], meta, plan["base"], x1, gfin)


def _plan(counts, nt):
    up = lambda v: (v + (BF16_ROWS - 1)) // BF16_ROWS * BF16_ROWS
    down = lambda v: v // BF16_ROWS * BF16_ROWS
    c = counts.reshape(nt, 8, LANES)[:, 0, :N_EXPERTS]
    gsize = up(jnp.sum(c, axis=0))
    gstart = jnp.cumsum(gsize) - gsize
    before = jnp.cumsum(c, axis=0) - c
    a = before % BF16_ROWS
    length = a + c
    voff = jnp.cumsum(up(length), axis=1) - up(length)
    last = jnp.arange(nt)[:, None] == nt - 1
    wlen = jnp.where(last, up(length), down(length))
    rlen = jnp.where(c > 0, up(length), 0)
    base = jnp.zeros((nt, 8, LANES), jnp.float32)
    base = base.at[:, :, :N_EXPERTS].set((voff + a).astype(jnp.float32)[:, None, :])
    i32 = lambda v: v.reshape(-1).astype(jnp.int32)
    return dict(
        voff=i32(voff), coff=i32(voff + down(length)), keep=i32(length % BF16_ROWS),
        row0=i32(gstart[None, :] + before - a), wlen=i32(wlen), wtot=i32(jnp.sum(wlen, axis=1)),
        rlen=i32(rlen), rtot=i32(jnp.sum(rlen, axis=1)), base=base.reshape(nt * 8, LANES),
        gstart=i32(gstart), gsize=i32(gsize), used=i32(jnp.sum(gsize, keepdims=True)))


def kernel(x, g_mix, w_in, w_alpha_up, b_alpha, g_gla, w_conv, w_branch_a, w_branch_b, w_out, g_ffn, w_router, b_router, w_gate_up, b_gate_up, w_down, b_down, g_final):
    batch, seq, _ = x.shape
    tokens = batch * seq
    depth = w_in.shape[0]
    x2 = x.reshape(tokens, D_MODEL)
    nt = tokens // min(MIX_TS, seq)
    rows_total = tokens * TOP_K + N_EXPERTS * (BF16_ROWS - 1)
    rows_total = -(-rows_total // BF16_ROWS) * BF16_ROWS

    for l in range(depth):
        wint = jnp.swapaxes(w_in[l], 0, 1).astype(bf16)
        wup = jnp.pad(w_alpha_up[l], ((0, LANES - GLA_RANK), (0, 0))).astype(bf16)
        wr = jnp.pad(w_router[l], ((0, 0), (0, LANES - N_EXPERTS)))
        wr = wr.astype(bf16)
        br = jnp.pad(b_router[l], (0, LANES - N_EXPERTS)).reshape(1, LANES)
        wconv = jnp.pad(w_conv[l], ((0, 8 - CONV_K), (0, 0)))

        x1, h2, logits = _mixer(
            x2.reshape(batch, seq, D_MODEL), g_mix[l].reshape(1, D_MODEL), wint, wup,
            b_alpha[l].reshape(1, QK_W), g_gla[l].reshape(1, V_W), wconv,
            w_branch_a[l], w_branch_b[l], w_out[l],
            g_ffn[l].reshape(1, D_MODEL), wr, br)

        meta, counts = _route(logits, min(MIX_TS, seq))
        plan = _plan(counts, nt)
        xs = _dispatch(plan, h2, meta, rows_total)
        os = _ffn(plan["gstart"], plan["gsize"], plan["used"], xs,
                  w_gate_up[l], b_gate_up[l].reshape(N_EXPERTS, 1, 2 * D_FF),
                  w_down[l], b_down[l].reshape(N_EXPERTS, 1, D_MODEL))
        x2 = _combine(plan, os, meta, x1, g_final.reshape(1, D_MODEL), l == depth - 1)
    return x2.reshape(batch, seq, D_MODEL)
```
